```python
import math
import jax
import jax.numpy as jnp
from jax import lax
import numpy as np

D_MODEL = 1024
BATCH = 8
SEQ = 16384
DEPTH = 2

N_HEADS = 16
HEAD_DIM = D_MODEL // N_HEADS
D_FF = 4 * D_MODEL
ROPE_THETA = 500000.0
ROPE_DIM = HEAD_DIM // 4
BLOCK = 128
DILATED_BRANCHES = ((128, 1), (512, 4), (2048, 16))
N_MIXERS = 2
DEEPNORM_ALPHA = (2 * DEPTH) ** 0.25
DEEPNORM_BETA = (8 * DEPTH) ** -0.25
LN_EPS = 1e-5

kernel_name = "hybrid_stickbreak_dilated_deepnorm"


def layer_norm(x, g, b):
    xf = x.astype(jnp.float32)
    mu = jnp.mean(xf, axis=-1, keepdims=True)
    var = jnp.mean(jnp.square(xf - mu), axis=-1, keepdims=True)
    y = (xf - mu) * lax.rsqrt(var + LN_EPS)
    return (y * g.astype(jnp.float32) + b.astype(jnp.float32)).astype(x.dtype)


def rotary_partial(x, positions):
    half = ROPE_DIM // 2
    inv_freq = ROPE_THETA ** (-jnp.arange(half, dtype=jnp.float32) / half)
    ang = positions.astype(jnp.float32)[:, None] * inv_freq[None, :]
    cos = jnp.cos(ang)[None, :, None, :]
    sin = jnp.sin(ang)[None, :, None, :]
    xf = x.astype(jnp.float32)
    x1 = xf[..., :half]
    x2 = xf[..., half:ROPE_DIM]
    return jnp.concatenate([x1 * cos - x2 * sin, x2 * cos + x1 * sin, xf[..., ROPE_DIM:]], axis=-1)


def stick_breaking_attention(q, k, v):
    B, H, S, hd = q.shape
    nq = S // BLOCK
    scale = 1.0 / math.sqrt(hd)
    ar = jnp.arange(BLOCK)
    tri_incl = (ar[:, None] >= ar[None, :]).astype(jnp.float32)
    outs = []
    for blk in range(nq):
        n_k = blk + 1
        end = n_k * BLOCK
        q_blk = q[:, :, blk * BLOCK:end]
        k_blk = k[:, :, :end]
        v_blk = v[:, :, :end]
        z = jnp.einsum('bhqd,bhkd->bhqk', q_blk, k_blk) * scale
        causal = jnp.arange(end)[None, :] < (blk * BLOCK + ar)[:, None]
        log_fail = jnp.where(causal, -jax.nn.softplus(z), 0.0)
        lf = log_fail.reshape(B, H, BLOCK, n_k, BLOCK)
        r_in = jnp.einsum('bhqcj,js->bhqcs', lf, tri_incl)
        totals = jnp.sum(lf, axis=-1)
        ac = jnp.arange(n_k)
        later = (ac[:, None] > ac[None, :]).astype(jnp.float32)
        r_cross = jnp.einsum('bhqe,ec->bhqc', totals, later)
        log_w = z.reshape(B, H, BLOCK, n_k, BLOCK) + r_in + r_cross[..., None]
        w = jnp.where(causal, jnp.exp(log_w).reshape(B, H, BLOCK, end), 0.0)
        outs.append(jnp.einsum('bhqk,bhkd->bhqd', w, v_blk))
    out = jnp.concatenate(outs, axis=2)
    return out.transpose(0, 2, 1, 3)


def dilated_branch(q, k, v, window, dilation):
    B, H, S, hd = q.shape
    w_sub = window // dilation
    span = dilation * BLOCK
    s_pad = -(-S // span) * span
    L = s_pad // dilation
    nb = L // BLOCK

    def to_sub(t):
        t = jnp.pad(t, ((0, 0), (0, 0), (0, s_pad - S), (0, 0)))
        t = t.reshape(B, H, L, dilation, hd).transpose(0, 1, 3, 2, 4)
        return t.reshape(B, H, dilation, nb, BLOCK, hd)

    def with_prev(t):
        prev = jnp.pad(t, ((0, 0), (0, 0), (0, 0), (1, 0), (0, 0), (0, 0)))[:, :, :, :-1]
        return jnp.concatenate([prev, t], axis=4)

    qs = to_sub(q)
    kk = with_prev(to_sub(k))
    vv = with_prev(to_sub(v))
    s = jnp.einsum('bhrnqd,bhrnkd->bhrnqk', qs, kk)
    qi = jnp.arange(BLOCK)[:, None]
    kj = jnp.arange(2 * BLOCK)[None, :]
    dist = BLOCK + qi - kj
    blk = jnp.arange(nb)[:, None, None]
    valid = (dist >= 0) & (dist <= w_sub) & ((blk > 0) | (kj >= BLOCK))
    s = jnp.where(valid, s, -jnp.inf)
    m = jnp.max(s, axis=-1, keepdims=True)
    p = jnp.exp(s - m)
    den = jnp.sum(p, axis=-1, keepdims=True)
    num = jnp.einsum('bhrnqk,bhrnkd->bhrnqd', p, vv)

    def from_sub(t):
        c = t.shape[-1]
        t = t.reshape(B, H, dilation, L, c).transpose(0, 1, 3, 2, 4).reshape(B, H, s_pad, c)
        return t[:, :, :S]

    return from_sub(m), from_sub(den), from_sub(num)


def dilated_attention(q, k, v):
    hd = q.shape[-1]
    qs = q * (1.0 / math.sqrt(hd))
    parts = [dilated_branch(qs, k, v, w, d) for (w, d) in DILATED_BRANCHES]
    m_all = jnp.max(jnp.concatenate([p[0] for p in parts], axis=-1), axis=-1, keepdims=True)
    num = sum(jnp.exp(p[0] - m_all) * p[2] for p in parts)
    den = sum(jnp.exp(p[0] - m_all) * p[1] for p in parts)
    out = num / den
    return out.transpose(0, 2, 1, 3)


def _fwd_setup_inputs(seed: int = 0) -> dict:
    key = jax.random.key(seed)
    keys = jax.random.split(key, 1 + 9 * DEPTH)
    inputs = {"x": jax.random.normal(keys[0], (BATCH, SEQ, D_MODEL), jnp.float32)}
    for i in range(DEPTH):
        k = keys[1 + 9 * i: 10 + 9 * i]
        w_qk = jax.random.normal(k[0], (D_MODEL, 2 * D_MODEL), jnp.float32) * D_MODEL ** -0.5
        w_v = jax.random.normal(k[1], (D_MODEL, D_MODEL), jnp.float32) * (D_MODEL ** -0.5 * DEEPNORM_BETA)
        inputs[f"w_qkv_{i}"] = jnp.concatenate([w_qk, w_v], axis=1)
        inputs[f"w_o_{i}"] = jax.random.normal(k[2], (D_MODEL, D_MODEL), jnp.float32) * (D_MODEL ** -0.5 * DEEPNORM_BETA)
        inputs[f"ln1_g_{i}"] = 1.0 + 0.02 * jax.random.normal(k[3], (D_MODEL,), jnp.float32)
        inputs[f"ln1_b_{i}"] = 0.02 * jax.random.normal(k[4], (D_MODEL,), jnp.float32)
        inputs[f"w_ff1_{i}"] = jax.random.normal(k[5], (D_MODEL, D_FF), jnp.float32) * (D_MODEL ** -0.5 * DEEPNORM_BETA)
        inputs[f"w_ff2_{i}"] = jax.random.normal(k[6], (D_FF, D_MODEL), jnp.float32) * (D_FF ** -0.5 * DEEPNORM_BETA)
        inputs[f"ln2_g_{i}"] = 1.0 + 0.02 * jax.random.normal(k[7], (D_MODEL,), jnp.float32)
        inputs[f"ln2_b_{i}"] = 0.02 * jax.random.normal(k[8], (D_MODEL,), jnp.float32)
    return inputs


def _fwd_reference(x,
              w_qkv_0, w_o_0, ln1_g_0, ln1_b_0, w_ff1_0, w_ff2_0, ln2_g_0, ln2_b_0,
              w_qkv_1, w_o_1, ln1_g_1, ln1_b_1, w_ff1_1, w_ff2_1, ln2_g_1, ln2_b_1):
    layers = (
        (w_qkv_0, w_o_0, ln1_g_0, ln1_b_0, w_ff1_0, w_ff2_0, ln2_g_0, ln2_b_0),
        (w_qkv_1, w_o_1, ln1_g_1, ln1_b_1, w_ff1_1, w_ff2_1, ln2_g_1, ln2_b_1),
    )
    B, S, D = x.shape
    positions = jnp.arange(S, dtype=jnp.int32)
    for i in range(DEPTH):
        w_qkv, w_o, g1, b1, w1, w2, g2, b2 = layers[i]
        qkv = jnp.einsum('bsd,de->bse', x, w_qkv).reshape(B, S, 3, N_HEADS, HEAD_DIM)
        q, k, v = qkv[:, :, 0], qkv[:, :, 1], qkv[:, :, 2]
        if i % N_MIXERS == 0:
            qh = q.astype(jnp.float32).transpose(0, 2, 1, 3)
            kh = k.astype(jnp.float32).transpose(0, 2, 1, 3)
            vh = v.astype(jnp.float32).transpose(0, 2, 1, 3)
            o = stick_breaking_attention(qh, kh, vh)
        else:
            qh = rotary_partial(q, positions).transpose(0, 2, 1, 3)
            kh = rotary_partial(k, positions).transpose(0, 2, 1, 3)
            vh = v.astype(jnp.float32).transpose(0, 2, 1, 3)
            o = dilated_attention(qh, kh, vh)
        mix = jnp.einsum('bse,ed->bsd', o.reshape(B, S, D).astype(x.dtype), w_o)
        x = layer_norm(DEEPNORM_ALPHA * x + mix, g1, b1)
        h = jnp.square(jax.nn.relu(jnp.einsum('bsd,df->bsf', x, w1)))
        x = layer_norm(DEEPNORM_ALPHA * x + jnp.einsum('bsf,fd->bsd', h, w2), g2, b2)
    return x


import jax as _jax
import jax.numpy as _jnp

TWIN_FORMAT = 'train_step'
FWD_PARAMS = ['x', 'w_qkv_0', 'w_o_0', 'ln1_g_0', 'ln1_b_0', 'w_ff1_0', 'w_ff2_0', 'ln2_g_0', 'ln2_b_0', 'w_qkv_1', 'w_o_1', 'ln1_g_1', 'ln1_b_1', 'w_ff1_1', 'w_ff2_1', 'ln2_g_1', 'ln2_b_1']
TWIN_WEIGHTS = ['w_qkv_0', 'w_o_0', 'ln1_g_0', 'ln1_b_0', 'w_ff1_0', 'w_ff2_0', 'ln2_g_0', 'ln2_b_0', 'w_qkv_1', 'w_o_1', 'ln1_g_1', 'ln1_b_1', 'w_ff1_1', 'w_ff2_1', 'ln2_g_1', 'ln2_b_1']
TWIN_DIFF_INPUT = 'x'
TWIN_INPUTS = ['x', 'w_qkv_0', 'w_o_0', 'ln1_g_0', 'ln1_b_0', 'w_ff1_0', 'w_ff2_0', 'ln2_g_0', 'ln2_b_0', 'w_qkv_1', 'w_o_1', 'ln1_g_1', 'ln1_b_1', 'w_ff1_1', 'w_ff2_1', 'ln2_g_1', 'ln2_b_1', 'loss_target', 'm_w_qkv_0', 'm_w_o_0', 'm_ln1_g_0', 'm_ln1_b_0', 'm_w_ff1_0', 'm_w_ff2_0', 'm_ln2_g_0', 'm_ln2_b_0', 'm_w_qkv_1', 'm_w_o_1', 'm_ln1_g_1', 'm_ln1_b_1', 'm_w_ff1_1', 'm_w_ff2_1', 'm_ln2_g_1', 'm_ln2_b_1', 'v_w_qkv_0', 'v_w_o_0', 'v_ln1_g_0', 'v_ln1_b_0', 'v_w_ff1_0', 'v_w_ff2_0', 'v_ln2_g_0', 'v_ln2_b_0', 'v_w_qkv_1', 'v_w_o_1', 'v_ln1_g_1', 'v_ln1_b_1', 'v_w_ff1_1', 'v_w_ff2_1', 'v_ln2_g_1', 'v_ln2_b_1']
TWIN_OUTPUTS = ['loss', 'grad_x', 'grad_w_qkv_0', 'grad_w_o_0', 'grad_ln1_g_0', 'grad_ln1_b_0', 'grad_w_ff1_0', 'grad_w_ff2_0', 'grad_ln2_g_0', 'grad_ln2_b_0', 'grad_w_qkv_1', 'grad_w_o_1', 'grad_ln1_g_1', 'grad_ln1_b_1', 'grad_w_ff1_1', 'grad_w_ff2_1', 'grad_ln2_g_1', 'grad_ln2_b_1', 'delta_w_qkv_0', 'delta_w_o_0', 'delta_ln1_g_0', 'delta_ln1_b_0', 'delta_w_ff1_0', 'delta_w_ff2_0', 'delta_ln2_g_0', 'delta_ln2_b_0', 'delta_w_qkv_1', 'delta_w_o_1', 'delta_ln1_g_1', 'delta_ln1_b_1', 'delta_w_ff1_1', 'delta_w_ff2_1', 'delta_ln2_g_1', 'delta_ln2_b_1', 'new_m_w_qkv_0', 'new_m_w_o_0', 'new_m_ln1_g_0', 'new_m_ln1_b_0', 'new_m_w_ff1_0', 'new_m_w_ff2_0', 'new_m_ln2_g_0', 'new_m_ln2_b_0', 'new_m_w_qkv_1', 'new_m_w_o_1', 'new_m_ln1_g_1', 'new_m_ln1_b_1', 'new_m_w_ff1_1', 'new_m_w_ff2_1', 'new_m_ln2_g_1', 'new_m_ln2_b_1', 'new_v_w_qkv_0', 'new_v_w_o_0', 'new_v_ln1_g_0', 'new_v_ln1_b_0', 'new_v_w_ff1_0', 'new_v_w_ff2_0', 'new_v_ln2_g_0', 'new_v_ln2_b_0', 'new_v_w_qkv_1', 'new_v_w_o_1', 'new_v_ln1_g_1', 'new_v_ln1_b_1', 'new_v_w_ff1_1', 'new_v_w_ff2_1', 'new_v_ln2_g_1', 'new_v_ln2_b_1']
TWIN_LEAF_KINDS = {'loss': 'loss', 'grad_x': 'grad_x', 'grad_w_qkv_0': 'grad_w', 'grad_w_o_0': 'grad_w', 'grad_ln1_g_0': 'grad_w', 'grad_ln1_b_0': 'grad_w', 'grad_w_ff1_0': 'grad_w', 'grad_w_ff2_0': 'grad_w', 'grad_ln2_g_0': 'grad_w', 'grad_ln2_b_0': 'grad_w', 'grad_w_qkv_1': 'grad_w', 'grad_w_o_1': 'grad_w', 'grad_ln1_g_1': 'grad_w', 'grad_ln1_b_1': 'grad_w', 'grad_w_ff1_1': 'grad_w', 'grad_w_ff2_1': 'grad_w', 'grad_ln2_g_1': 'grad_w', 'grad_ln2_b_1': 'grad_w', 'delta_w_qkv_0': 'delta_w', 'delta_w_o_0': 'delta_w', 'delta_ln1_g_0': 'delta_w', 'delta_ln1_b_0': 'delta_w', 'delta_w_ff1_0': 'delta_w', 'delta_w_ff2_0': 'delta_w', 'delta_ln2_g_0': 'delta_w', 'delta_ln2_b_0': 'delta_w', 'delta_w_qkv_1': 'delta_w', 'delta_w_o_1': 'delta_w', 'delta_ln1_g_1': 'delta_w', 'delta_ln1_b_1': 'delta_w', 'delta_w_ff1_1': 'delta_w', 'delta_w_ff2_1': 'delta_w', 'delta_ln2_g_1': 'delta_w', 'delta_ln2_b_1': 'delta_w', 'new_m_w_qkv_0': 'new_m', 'new_m_w_o_0': 'new_m', 'new_m_ln1_g_0': 'new_m', 'new_m_ln1_b_0': 'new_m', 'new_m_w_ff1_0': 'new_m', 'new_m_w_ff2_0': 'new_m', 'new_m_ln2_g_0': 'new_m', 'new_m_ln2_b_0': 'new_m', 'new_m_w_qkv_1': 'new_m', 'new_m_w_o_1': 'new_m', 'new_m_ln1_g_1': 'new_m', 'new_m_ln1_b_1': 'new_m', 'new_m_w_ff1_1': 'new_m', 'new_m_w_ff2_1': 'new_m', 'new_m_ln2_g_1': 'new_m', 'new_m_ln2_b_1': 'new_m', 'new_v_w_qkv_0': 'new_v', 'new_v_w_o_0': 'new_v', 'new_v_ln1_g_0': 'new_v', 'new_v_ln1_b_0': 'new_v', 'new_v_w_ff1_0': 'new_v', 'new_v_w_ff2_0': 'new_v', 'new_v_ln2_g_0': 'new_v', 'new_v_ln2_b_0': 'new_v', 'new_v_w_qkv_1': 'new_v', 'new_v_w_o_1': 'new_v', 'new_v_ln1_g_1': 'new_v', 'new_v_ln1_b_1': 'new_v', 'new_v_w_ff1_1': 'new_v', 'new_v_w_ff2_1': 'new_v', 'new_v_ln2_g_1': 'new_v', 'new_v_ln2_b_1': 'new_v'}


def _forward(args):
    return _fwd_reference(*[args[k] for k in FWD_PARAMS])


def _output_shape():
    def fwd():
        inp = _fwd_setup_inputs(0)
        return _fwd_reference(*[inp[k] for k in FWD_PARAMS])
    out = _jax.eval_shape(fwd)
    return out.shape, out.dtype

N_MICROBATCH = 1
ADAM_LR = 0.001
ADAM_B1 = 0.9
ADAM_B2 = 0.999
ADAM_EPS = 1e-08
ADAM_WD = 0.01
ADAM_STEP = 10
PER_EXAMPLE_BATCH_AXIS = {'x': 0, 'loss_target': 0}
SHARED_INPUTS = []
_WEIGHT_DTYPES = {'w_qkv_0': _jnp.float32, 'w_o_0': _jnp.float32, 'ln1_g_0': _jnp.float32, 'ln1_b_0': _jnp.float32, 'w_ff1_0': _jnp.float32, 'w_ff2_0': _jnp.float32, 'ln2_g_0': _jnp.float32, 'ln2_b_0': _jnp.float32, 'w_qkv_1': _jnp.float32, 'w_o_1': _jnp.float32, 'ln1_g_1': _jnp.float32, 'ln1_b_1': _jnp.float32, 'w_ff1_1': _jnp.float32, 'w_ff2_1': _jnp.float32, 'ln2_g_1': _jnp.float32, 'ln2_b_1': _jnp.float32}
MOMENT_SCALE = {'w_qkv_0': 4.881466e-02, 'w_o_0': 8.059950e-02, 'ln1_g_0': 4.232909e+00, 'ln1_b_0': 1.881110e+00, 'w_ff1_0': 4.426195e-02, 'w_ff2_0': 1.340055e-01, 'ln2_g_0': 4.303001e+00, 'ln2_b_0': 1.874823e+00, 'w_qkv_1': 2.015346e-02, 'w_o_1': 3.084944e-02, 'ln1_g_1': 4.304326e+00, 'ln1_b_1': 1.864182e+00, 'w_ff1_1': 4.463618e-02, 'w_ff2_1': 1.339882e-01, 'ln2_g_1': 1.282946e+02, 'ln2_b_1': 8.802159e+00}


def _to_microbatches(a, axis):
    t = _jnp.moveaxis(a, axis, 0)
    t = t.reshape((N_MICROBATCH, t.shape[0] // N_MICROBATCH) + t.shape[1:])
    return _jnp.moveaxis(t, 1, axis + 1)


def setup_inputs(seed: int = 0) -> dict:
    inp = _fwd_setup_inputs(seed)
    key = _jax.random.fold_in(_jax.random.key(seed), 7919)
    shape, _ = _output_shape()
    out = dict(inp)
    out["loss_target"] = _jax.random.normal(_jax.random.fold_in(key, 0), shape, _jnp.float32)
    for i, name in enumerate(TWIN_WEIGHTS):
        w = inp[name].astype(_jnp.float32)
        if MOMENT_SCALE is None:
            s = _jnp.sqrt(_jnp.mean(_jnp.square(w)) + 1e-30)
        else:
            s = MOMENT_SCALE[name]
        km, kv = _jax.random.split(_jax.random.fold_in(key, i + 1))
        out[name] = w
        out["m_" + name] = s * _jax.random.normal(km, w.shape, _jnp.float32)
        out["v_" + name] = (s * s) * _jax.random.uniform(kv, w.shape, _jnp.float32, 0.5, 1.5)
    if N_MICROBATCH > 1:
        for name, axis in PER_EXAMPLE_BATCH_AXIS.items():
            out[name] = _to_microbatches(out[name], axis)
    return {'x': out['x'], 'w_qkv_0': out['w_qkv_0'], 'w_o_0': out['w_o_0'], 'ln1_g_0': out['ln1_g_0'], 'ln1_b_0': out['ln1_b_0'], 'w_ff1_0': out['w_ff1_0'], 'w_ff2_0': out['w_ff2_0'], 'ln2_g_0': out['ln2_g_0'], 'ln2_b_0': out['ln2_b_0'], 'w_qkv_1': out['w_qkv_1'], 'w_o_1': out['w_o_1'], 'ln1_g_1': out['ln1_g_1'], 'ln1_b_1': out['ln1_b_1'], 'w_ff1_1': out['w_ff1_1'], 'w_ff2_1': out['w_ff2_1'], 'ln2_g_1': out['ln2_g_1'], 'ln2_b_1': out['ln2_b_1'], 'loss_target': out['loss_target'], 'm_w_qkv_0': out['m_w_qkv_0'], 'm_w_o_0': out['m_w_o_0'], 'm_ln1_g_0': out['m_ln1_g_0'], 'm_ln1_b_0': out['m_ln1_b_0'], 'm_w_ff1_0': out['m_w_ff1_0'], 'm_w_ff2_0': out['m_w_ff2_0'], 'm_ln2_g_0': out['m_ln2_g_0'], 'm_ln2_b_0': out['m_ln2_b_0'], 'm_w_qkv_1': out['m_w_qkv_1'], 'm_w_o_1': out['m_w_o_1'], 'm_ln1_g_1': out['m_ln1_g_1'], 'm_ln1_b_1': out['m_ln1_b_1'], 'm_w_ff1_1': out['m_w_ff1_1'], 'm_w_ff2_1': out['m_w_ff2_1'], 'm_ln2_g_1': out['m_ln2_g_1'], 'm_ln2_b_1': out['m_ln2_b_1'], 'v_w_qkv_0': out['v_w_qkv_0'], 'v_w_o_0': out['v_w_o_0'], 'v_ln1_g_0': out['v_ln1_g_0'], 'v_ln1_b_0': out['v_ln1_b_0'], 'v_w_ff1_0': out['v_w_ff1_0'], 'v_w_ff2_0': out['v_w_ff2_0'], 'v_ln2_g_0': out['v_ln2_g_0'], 'v_ln2_b_0': out['v_ln2_b_0'], 'v_w_qkv_1': out['v_w_qkv_1'], 'v_w_o_1': out['v_w_o_1'], 'v_ln1_g_1': out['v_ln1_g_1'], 'v_ln1_b_1': out['v_ln1_b_1'], 'v_w_ff1_1': out['v_w_ff1_1'], 'v_w_ff2_1': out['v_w_ff2_1'], 'v_ln2_g_1': out['v_ln2_g_1'], 'v_ln2_b_1': out['v_ln2_b_1']}


def _loss(weights, diff, rest, loss_target):
    with _jax.named_scope("forward"):
        args = {**rest, TWIN_DIFF_INPUT: diff, **{k: w.astype(_WEIGHT_DTYPES[k]) for k, w in weights.items()}}
        y = _forward(args)
    with _jax.named_scope("loss_head"):
        err = _jnp.square(y.astype(_jnp.float32) - loss_target)
        return 0.5 * _jnp.sum(_jnp.mean(err, axis=-1)) if err.ndim else 0.5 * err


def _adamw(w, g, m, v):
    m = ADAM_B1 * m + (1.0 - ADAM_B1) * g
    v = ADAM_B2 * v + (1.0 - ADAM_B2) * _jnp.square(g)
    m_hat = m / (1.0 - ADAM_B1 ** ADAM_STEP)
    v_hat = v / (1.0 - ADAM_B2 ** ADAM_STEP)
    delta = -ADAM_LR * (m_hat / (_jnp.sqrt(v_hat) + ADAM_EPS) + ADAM_WD * w)
    return delta, m, v


def reference(x, w_qkv_0, w_o_0, ln1_g_0, ln1_b_0, w_ff1_0, w_ff2_0, ln2_g_0, ln2_b_0, w_qkv_1, w_o_1, ln1_g_1, ln1_b_1, w_ff1_1, w_ff2_1, ln2_g_1, ln2_b_1, loss_target, m_w_qkv_0, m_w_o_0, m_ln1_g_0, m_ln1_b_0, m_w_ff1_0, m_w_ff2_0, m_ln2_g_0, m_ln2_b_0, m_w_qkv_1, m_w_o_1, m_ln1_g_1, m_ln1_b_1, m_w_ff1_1, m_w_ff2_1, m_ln2_g_1, m_ln2_b_1, v_w_qkv_0, v_w_o_0, v_ln1_g_0, v_ln1_b_0, v_w_ff1_0, v_w_ff2_0, v_ln2_g_0, v_ln2_b_0, v_w_qkv_1, v_w_o_1, v_ln1_g_1, v_ln1_b_1, v_w_ff1_1, v_w_ff2_1, v_ln2_g_1, v_ln2_b_1):
    given = dict(x=x, w_qkv_0=w_qkv_0, w_o_0=w_o_0, ln1_g_0=ln1_g_0, ln1_b_0=ln1_b_0, w_ff1_0=w_ff1_0, w_ff2_0=w_ff2_0, ln2_g_0=ln2_g_0, ln2_b_0=ln2_b_0, w_qkv_1=w_qkv_1, w_o_1=w_o_1, ln1_g_1=ln1_g_1, ln1_b_1=ln1_b_1, w_ff1_1=w_ff1_1, w_ff2_1=w_ff2_1, ln2_g_1=ln2_g_1, ln2_b_1=ln2_b_1, loss_target=loss_target, m_w_qkv_0=m_w_qkv_0, m_w_o_0=m_w_o_0, m_ln1_g_0=m_ln1_g_0, m_ln1_b_0=m_ln1_b_0, m_w_ff1_0=m_w_ff1_0, m_w_ff2_0=m_w_ff2_0, m_ln2_g_0=m_ln2_g_0, m_ln2_b_0=m_ln2_b_0, m_w_qkv_1=m_w_qkv_1, m_w_o_1=m_w_o_1, m_ln1_g_1=m_ln1_g_1, m_ln1_b_1=m_ln1_b_1, m_w_ff1_1=m_w_ff1_1, m_w_ff2_1=m_w_ff2_1, m_ln2_g_1=m_ln2_g_1, m_ln2_b_1=m_ln2_b_1, v_w_qkv_0=v_w_qkv_0, v_w_o_0=v_w_o_0, v_ln1_g_0=v_ln1_g_0, v_ln1_b_0=v_ln1_b_0, v_w_ff1_0=v_w_ff1_0, v_w_ff2_0=v_w_ff2_0, v_ln2_g_0=v_ln2_g_0, v_ln2_b_0=v_ln2_b_0, v_w_qkv_1=v_w_qkv_1, v_w_o_1=v_w_o_1, v_ln1_g_1=v_ln1_g_1, v_ln1_b_1=v_ln1_b_1, v_w_ff1_1=v_w_ff1_1, v_w_ff2_1=v_w_ff2_1, v_ln2_g_1=v_ln2_g_1, v_ln2_b_1=v_ln2_b_1)
    weights = {n: given[n] for n in TWIN_WEIGHTS}
    shared = {n: given[n] for n in SHARED_INPUTS}
    per_example = {n: given[n] for n in ['x']}
    grad_fn = _jax.value_and_grad(_loss, argnums=(0, 1))

    def one_microbatch(ex, loss_target):
        ex = dict(ex)
        diff = ex.pop(TWIN_DIFF_INPUT)
        return grad_fn(weights, diff, {**shared, **ex}, loss_target)

    if N_MICROBATCH == 1:
        loss, (grad_w, grad_x) = one_microbatch(per_example, given["loss_target"])
    else:
        def body(carry, xs):
            loss_sum, grad_sum = carry
            l_k, (gw_k, gx_k) = one_microbatch(xs[0], xs[1])
            with _jax.named_scope("update"):
                return (loss_sum + l_k, _jax.tree.map(_jnp.add, grad_sum, gw_k)), gx_k

        init = (_jnp.zeros((), _jnp.float32), _jax.tree.map(_jnp.zeros_like, weights))
        (loss, grad_w), grad_x = _jax.lax.scan(body, init, (per_example, given["loss_target"]))
    with _jax.named_scope("update"):
        delta_w, new_m, new_v = {}, {}, {}
        for n in TWIN_WEIGHTS:
            delta_w[n], new_m[n], new_v[n] = _adamw(weights[n], grad_w[n], given["m_" + n], given["v_" + n])
    return (loss, grad_x, *[grad_w[n] for n in TWIN_WEIGHTS], *[delta_w[n] for n in TWIN_WEIGHTS],
            *[new_m[n] for n in TWIN_WEIGHTS], *[new_v[n] for n in TWIN_WEIGHTS])
```

```python
import functools
import math

import jax
import jax.numpy as jnp
import numpy as np
from jax import lax
from jax.experimental import pallas as pl
from jax.experimental.pallas import tpu as pltpu

F32 = jnp.float32
BF16 = jnp.bfloat16

D_MODEL = 1024
N_HEADS = 16
HEAD_DIM = 64
D_FF = 4096
N_DEV = 8
N_LAYERS = 2
ROPE_THETA = 500000.0
ROPE_DIM = 16
DILATED_BRANCHES = ((128, 1), (512, 4), (2048, 16))
ALPHA = (2 * N_LAYERS) ** 0.25
LN_EPS = 1e-5
Q_SCALE = 1.0 / math.sqrt(HEAD_DIM)
ADAM_LR, ADAM_B1, ADAM_B2, ADAM_EPS, ADAM_WD, ADAM_STEP = 0.001, 0.9, 0.999, 1e-08, 0.01, 10

LANES = 128
HEAD_PAIRS = D_MODEL // LANES
SB_TQ = 256
SB_CH = 256
DIL_BLK = 128
VMEM_BIG = 56 * 2 ** 20
MESH_AXES = ("x", "y", "c")

SHARD_ROWS = (384, 128, 512, 512)
LAYER_ROWS = sum(SHARD_ROWS)
ALL_ROWS = N_LAYERS * LAYER_ROWS


def _params(sem=None, vmem=None):
    kw = {}
    if sem is not None:
        kw["dimension_semantics"] = sem
    if vmem is not None:
        kw["vmem_limit_bytes"] = vmem
    return pltpu.CompilerParams(**kw)


def _dot(a, b):
    return jnp.dot(a, b, preferred_element_type=F32)


def _dot_nt(a, b):
    return lax.dot_general(a, b, (((1,), (1,)), ((), ())), preferred_element_type=F32)


def _dot_tn(a, b):
    return lax.dot_general(a, b, (((0,), (0,)), ((), ())), preferred_element_type=F32)


def _split3(p):
    hi = p.astype(BF16)
    r1 = p - hi.astype(F32)
    mid = r1.astype(BF16)
    lo = (r1 - mid.astype(F32)).astype(BF16)
    return hi, mid, lo


def _dot3(p, e):
    hi, mid, lo = _split3(p)
    return _dot(hi, e) + _dot(mid, e) + _dot(lo, e)


def _rope_apply(a, c, s1, s2, sign):
    return a * c + sign * (pltpu.roll(a, 8, 1) * s1 + pltpu.roll(a, LANES - 8, 1) * s2)


def _qkv_proj(xb, w_blk, rope, name):
    S = xb.shape[0]
    tm = 512
    n_rope = 0 if rope is None else 3

    def body(*refs):
        x_ref, w_ref = refs[:2]
        tabs = [r[...] for r in refs[2:2 + n_rope]]
        o_ref = refs[2 + n_rope]
        x = x_ref[...]
        for j in range(N_DEV):
            acc = _dot(x, w_ref[j])
            for g in range(3):
                col = j * 384 + g * LANES
                a = acc[:, g * LANES:(g + 1) * LANES]
                if n_rope and col < 2 * D_MODEL:
                    a = _rope_apply(a, *tabs, 1.0)
                if col < D_MODEL:
                    a = a * Q_SCALE
                o_ref[:, col:col + LANES] = a.astype(BF16)

    tab_specs = [pl.BlockSpec((tm, LANES), lambda i: (i, 0))] * n_rope
    return pl.pallas_call(
        body, name=name, grid=(S // tm,),
        in_specs=[pl.BlockSpec((tm, D_MODEL), lambda i: (i, 0)),
                  pl.BlockSpec((N_DEV, D_MODEL, 384), lambda i: (0, 0, 0))] + tab_specs,
        out_specs=pl.BlockSpec((tm, 3 * D_MODEL), lambda i: (i, 0)),
        out_shape=jax.ShapeDtypeStruct((S, 3 * D_MODEL), BF16),
        compiler_params=_params(("parallel",), VMEM_BIG),
    )(xb, w_blk, *(rope or ()))


def _layer_norm_rows(y, g, b):
    mu = jnp.mean(y, axis=-1, keepdims=True)
    yc = y - mu
    var = jnp.mean(yc * yc, axis=-1, keepdims=True)
    return yc * lax.rsqrt(var + LN_EPS) * g + b


def _mm_res_ln(a, xres, w, g, b, name):
    S, K = a.shape
    tm = 512 if K <= 1024 else 256

    def body(a_ref, x_ref, w_ref, g_ref, b_ref, y_ref, xn_ref, xb_ref):
        y = ALPHA * x_ref[...] + _dot(a_ref[...], w_ref[...])
        xn = _layer_norm_rows(y, g_ref[...], b_ref[...])
        y_ref[...] = y
        xn_ref[...] = xn
        xb_ref[...] = xn.astype(BF16)

    row = lambda i: (i, 0)
    fix = lambda i: (0, 0)
    return pl.pallas_call(
        body, name=name, grid=(S // tm,),
        in_specs=[pl.BlockSpec((tm, K), row), pl.BlockSpec((tm, D_MODEL), row),
                  pl.BlockSpec((K, D_MODEL), fix), pl.BlockSpec((1, D_MODEL), fix),
                  pl.BlockSpec((1, D_MODEL), fix)],
        out_specs=[pl.BlockSpec((tm, D_MODEL), row)] * 3,
        out_shape=[jax.ShapeDtypeStruct((S, D_MODEL), F32), jax.ShapeDtypeStruct((S, D_MODEL), F32),
                   jax.ShapeDtypeStruct((S, D_MODEL), BF16)],
        compiler_params=_params(("parallel",), VMEM_BIG),
    )(a, xres, w, g, b)


def _ff1(xb, w_blk, name):
    S = xb.shape[0]
    tm = 256

    def body(x_ref, w_ref, hp_ref, h_ref):
        x = x_ref[...]
        for j in range(N_DEV):
            acc = _dot(x, w_ref[j])
            r = jnp.maximum(acc, 0.0)
            hp_ref[:, j * 512:(j + 1) * 512] = acc
            h_ref[:, j * 512:(j + 1) * 512] = (r * r).astype(BF16)

    return pl.pallas_call(
        body, name=name, grid=(S // tm,),
        in_specs=[pl.BlockSpec((tm, D_MODEL), lambda i: (i, 0)),
                  pl.BlockSpec((N_DEV, D_MODEL, 512), lambda i: (0, 0, 0))],
        out_specs=[pl.BlockSpec((tm, D_FF), lambda i: (i, 0))] * 2,
        out_shape=[jax.ShapeDtypeStruct((S, D_FF), F32), jax.ShapeDtypeStruct((S, D_FF), BF16)],
        compiler_params=_params(("parallel",), VMEM_BIG),
    )(xb, w_blk)


def _loss_grad(y, target, name):
    S = y.shape[0]
    tm = 512

    def body(y_ref, t_ref, dy_ref, l_ref):
        @pl.when(pl.program_id(0) == 0)
        def _():
            l_ref[...] = jnp.zeros_like(l_ref)

        err = y_ref[...] - t_ref[...]
        dy_ref[...] = err * (1.0 / D_MODEL)
        sq = err * err
        rows = sq[0:8]
        for r in range(1, tm // 8):
            rows = rows + sq[r * 8:(r + 1) * 8]
        acc = rows[:, 0:LANES]
        for g in range(1, D_MODEL // LANES):
            acc = acc + rows[:, g * LANES:(g + 1) * LANES]
        l_ref[...] += acc * (0.5 / D_MODEL)

    return pl.pallas_call(
        body, name=name, grid=(S // tm,),
        in_specs=[pl.BlockSpec((tm, D_MODEL), lambda i: (i, 0))] * 2,
        out_specs=[pl.BlockSpec((tm, D_MODEL), lambda i: (i, 0)), pl.BlockSpec((8, LANES), lambda i: (0, 0))],
        out_shape=[jax.ShapeDtypeStruct((S, D_MODEL), F32), jax.ShapeDtypeStruct((8, LANES), F32)],
        compiler_params=_params(("arbitrary",)),
    )(y, target)


def _ln_bwd(dout, y, g, name):
    S = y.shape[0]
    tm = 512
    steps = S // tm

    def body(d_ref, y_ref, g_ref, dy_ref, dyb_ref, gb_ref, acc_g, acc_b):
        i = pl.program_id(0)

        @pl.when(i == 0)
        def _():
            acc_g[...] = jnp.zeros_like(acc_g)
            acc_b[...] = jnp.zeros_like(acc_b)

        d = d_ref[...]
        yv = y_ref[...]
        mu = jnp.mean(yv, axis=-1, keepdims=True)
        yc = yv - mu
        var = jnp.mean(yc * yc, axis=-1, keepdims=True)
        rstd = lax.rsqrt(var + LN_EPS)
        xhat = yc * rstd
        dxh = d * g_ref[...]
        m1 = jnp.mean(dxh, axis=-1, keepdims=True)
        m2 = jnp.mean(dxh * xhat, axis=-1, keepdims=True)
        dy = rstd * (dxh - m1 - xhat * m2)
        dy_ref[...] = dy
        dyb_ref[...] = dy.astype(BF16)
        pg = d * xhat
        sg = pg[0:8]
        sb = d[0:8]
        for r in range(1, tm // 8):
            sg = sg + pg[r * 8:(r + 1) * 8]
            sb = sb + d[r * 8:(r + 1) * 8]
        acc_g[...] += sg
        acc_b[...] += sb

        @pl.when(i == steps - 1)
        def _():
            gb_ref[0:1, :] = jnp.sum(acc_g[...], axis=0, keepdims=True)
            gb_ref[1:2, :] = jnp.sum(acc_b[...], axis=0, keepdims=True)

    row = lambda i: (i, 0)
    fix = lambda i: (0, 0)
    return pl.pallas_call(
        body, name=name, grid=(steps,),
        in_specs=[pl.BlockSpec((tm, D_MODEL), row), pl.BlockSpec((tm, D_MODEL), row), pl.BlockSpec((1, D_MODEL), fix)],
        out_specs=[pl.BlockSpec((tm, D_MODEL), row), pl.BlockSpec((tm, D_MODEL), row), pl.BlockSpec((2, D_MODEL), fix)],
        out_shape=[jax.ShapeDtypeStruct((S, D_MODEL), F32), jax.ShapeDtypeStruct((S, D_MODEL), BF16),
                   jax.ShapeDtypeStruct((2, D_MODEL), F32)],
        scratch_shapes=[pltpu.VMEM((8, D_MODEL), F32), pltpu.VMEM((8, D_MODEL), F32)],
        compiler_params=_params(("arbitrary",)),
    )(dout, y, g)


def _dh(dyb, w2, hpre, name):
    S = dyb.shape[0]
    tm = 256
    tn = 512

    def body(dy_ref, w_ref, hp_ref, o_ref):
        dy = dy_ref[...]
        for n in range(0, D_FF, tn):
            dh = _dot_nt(dy, w_ref[n:n + tn, :])
            o_ref[:, n:n + tn] = (dh * (2.0 * jnp.maximum(hp_ref[:, n:n + tn], 0.0))).astype(BF16)

    return pl.pallas_call(
        body, name=name, grid=(S // tm,),
        in_specs=[pl.BlockSpec((tm, D_MODEL), lambda i: (i, 0)), pl.BlockSpec((D_FF, D_MODEL), lambda i: (0, 0)),
                  pl.BlockSpec((tm, D_FF), lambda i: (i, 0))],
        out_specs=pl.BlockSpec((tm, D_FF), lambda i: (i, 0)),
        out_shape=jax.ShapeDtypeStruct((S, D_FF), BF16),
        compiler_params=_params(("parallel",), VMEM_BIG),
    )(dyb, w2, hpre)


def _dx_blk(dres, dz, w_blk, name):
    S, N = dz.shape
    bw = w_blk.shape[2]
    tm = 256

    def body(r_ref, z_ref, w_ref, o_ref):
        acc = ALPHA * r_ref[...]
        for j in range(N_DEV):
            acc = acc + _dot_nt(z_ref[:, j * bw:(j + 1) * bw], w_ref[j])
        o_ref[...] = acc

    return pl.pallas_call(
        body, name=name, grid=(S // tm,),
        in_specs=[pl.BlockSpec((tm, D_MODEL), lambda i: (i, 0)), pl.BlockSpec((tm, N), lambda i: (i, 0)),
                  pl.BlockSpec((N_DEV, D_MODEL, bw), lambda i: (0, 0, 0))],
        out_specs=pl.BlockSpec((tm, D_MODEL), lambda i: (i, 0)),
        out_shape=jax.ShapeDtypeStruct((S, D_MODEL), F32),
        compiler_params=_params(("parallel",), VMEM_BIG),
    )(dres, dz, w_blk)


def _mm_nt_plain(a, w, name):
    S = a.shape[0]
    tm = 512

    def body(a_ref, w_ref, o_ref):
        o_ref[...] = _dot_nt(a_ref[...], w_ref[...])

    return pl.pallas_call(
        body, name=name, grid=(S // tm,),
        in_specs=[pl.BlockSpec((tm, D_MODEL), lambda i: (i, 0)), pl.BlockSpec((D_MODEL, D_MODEL), lambda i: (0, 0))],
        out_specs=pl.BlockSpec((tm, D_MODEL), lambda i: (i, 0)),
        out_shape=jax.ShapeDtypeStruct((S, D_MODEL), F32),
        compiler_params=_params(("parallel",)),
    )(a, w)


def _mm_tn(a, b, ta, tb, blocked, name):
    S, Ka = a.shape
    Nb = b.shape[1]
    ts = 512

    def body(a_ref, b_ref, o_ref):
        @pl.when(pl.program_id(2) == 0)
        def _():
            o_ref[...] = jnp.zeros_like(o_ref)

        o_ref[...] += _dot_tn(a_ref[...], b_ref[...])

    if blocked:
        out_spec = pl.BlockSpec((None, ta, tb), lambda i, j, s: (j, i, 0))
        out_shape = jax.ShapeDtypeStruct((Nb // tb, Ka, tb), F32)
    else:
        out_spec = pl.BlockSpec((ta, tb), lambda i, j, s: (i, j))
        out_shape = jax.ShapeDtypeStruct((Ka, Nb), F32)
    return pl.pallas_call(
        body, name=name, grid=(Ka // ta, Nb // tb, S // ts),
        in_specs=[pl.BlockSpec((ts, ta), lambda i, j, s: (s, i)), pl.BlockSpec((ts, tb), lambda i, j, s: (s, j))],
        out_specs=out_spec, out_shape=out_shape,
        compiler_params=_params(("parallel", "parallel", "arbitrary"), VMEM_BIG),
    )(a, b)


def _head_sums(do, o, name):
    S = do.shape[0]
    tm = 512
    sel = (np.arange(D_MODEL)[:, None] // HEAD_DIM == np.arange(LANES)[None, :]).astype(np.float32)

    def body(d_ref, o_ref, e_ref, out_ref):
        out_ref[...] = _dot3(d_ref[...] * o_ref[...], e_ref[...])

    return pl.pallas_call(
        body, name=name, grid=(S // tm,),
        in_specs=[pl.BlockSpec((tm, D_MODEL), lambda i: (i, 0))] * 2 + [pl.BlockSpec((D_MODEL, LANES), lambda i: (0, 0))],
        out_specs=pl.BlockSpec((tm, LANES), lambda i: (i, 0)),
        out_shape=jax.ShapeDtypeStruct((S, LANES), F32),
        compiler_params=_params(("parallel",)),
    )(do, o, jnp.asarray(sel, BF16))


def _sb_tmat(later):
    r = np.arange(SB_CH)
    t = (r[None, :] > r[:, None]) if later else (r[None, :] <= r[:, None])
    return jnp.asarray(np.concatenate([t.astype(np.float32), np.ones((8, SB_CH), np.float32)], axis=0), BF16)


def _sb_gates(zT):
    l1 = jnp.log(1.0 + jnp.exp(-jnp.abs(zT)))
    a = jnp.minimum(zT, 0.0) - l1
    return a, a - zT


def _head_masks(x2):
    lane = lax.broadcasted_iota(jnp.int32, x2.shape, 1)
    zero = jnp.zeros_like(x2)
    return jnp.where(lane < HEAD_DIM, x2, zero), jnp.where(lane >= HEAD_DIM, x2, zero)


def _sb_fwd(qkv, vT3, tmat, name):
    S = qkv.shape[0]
    nq = S // SB_TQ
    nch = S // SB_CH

    def body(q_ref, k_ref, vT_ref, t_ref, o_ref, r_ref):
        i = pl.program_id(1)
        qm = _head_masks(q_ref[...])
        tm_ = t_ref[...]
        causal = (lax.broadcasted_iota(jnp.int32, (SB_CH, SB_TQ), 0)
                  < lax.broadcasted_iota(jnp.int32, (SB_CH, SB_TQ), 1))

        def tile(c, carry, masked):
            kc = k_ref[pl.ds(pl.multiple_of(c * SB_CH, SB_CH), SB_CH), :]
            vTc = vT_ref[c]
            out = []
            for h in range(2):
                R, oT = carry[h]
                r_ref[h, pl.ds(c, 1), :] = R
                a, lf = _sb_gates(_dot_nt(kc, qm[h]))
                if masked:
                    lf = jnp.where(causal, lf, 0.0)
                cum = _dot(tm_, lf.astype(BF16))
                w = jnp.exp(a + cum[:SB_CH] + R)
                if masked:
                    w = jnp.where(causal, w, 0.0)
                oT = oT + _dot(vTc[h * HEAD_DIM:(h + 1) * HEAD_DIM, :], w.astype(BF16))
                out.append((R + cum[SB_CH:SB_CH + 1], oT))
            return tuple(out)

        z1 = jnp.zeros((1, SB_TQ), F32)
        z64 = jnp.zeros((HEAD_DIM, SB_TQ), F32)
        carry = tile(i, ((z1, z64), (z1, z64)), True)
        carry = lax.fori_loop(0, i, lambda j, cr: tile(i - 1 - j, cr, False), carry)
        o_ref[...] = jnp.concatenate([carry[0][1], carry[1][1]], axis=0).T

    return pl.pallas_call(
        body, name=name, grid=(HEAD_PAIRS, nq),
        in_specs=[pl.BlockSpec((SB_TQ, LANES), lambda hp, i: (i, hp)),
                  pl.BlockSpec((S, LANES), lambda hp, i: (0, HEAD_PAIRS + hp)),
                  pl.BlockSpec((None, nch, LANES, SB_CH), lambda hp, i: (hp, 0, 0, 0)),
                  pl.BlockSpec((SB_CH + 8, SB_CH), lambda hp, i: (0, 0))],
        out_specs=[pl.BlockSpec((SB_TQ, LANES), lambda hp, i: (i, hp)),
                   pl.BlockSpec((2, nch, SB_TQ), lambda hp, i: (hp, 0, i))],
        out_shape=[jax.ShapeDtypeStruct((S, D_MODEL), F32), jax.ShapeDtypeStruct((N_HEADS, nch, S), F32)],
        compiler_params=_params(("parallel", "arbitrary"), VMEM_BIG),
    )(qkv, qkv, vT3, tmat)


def _sb_bwd(qkv, kT3, do, rsum, tmat_l, tmat_g, name):
    S = qkv.shape[0]
    nq = S // SB_TQ
    nch = S // SB_CH

    def body(q_ref, do_ref, r_ref, k_ref, v_ref, kT_ref, tl_ref, tg_ref, dq_ref, dk_hbm, dv_hbm, dk_acc, dv_acc, sems):
        hp = pl.program_id(0)
        i = pl.program_id(1)

        @pl.when(i == 0)
        def _():
            dk_acc[...] = jnp.zeros_like(dk_acc)
            dv_acc[...] = jnp.zeros_like(dv_acc)

        qm = _head_masks(q_ref[...])
        dom = _head_masks(do_ref[...].astype(BF16))
        tl_ = tl_ref[...]
        tg_ = tg_ref[...]
        causal = (lax.broadcasted_iota(jnp.int32, (SB_CH, SB_TQ), 0)
                  < lax.broadcasted_iota(jnp.int32, (SB_CH, SB_TQ), 1))

        def tile(c, carry, masked):
            rows = pl.ds(pl.multiple_of(c * SB_CH, SB_CH), SB_CH)
            kc = k_ref[rows, :]
            vc = v_ref[rows, :]
            kTc = kT_ref[c]
            out = []
            for h in range(2):
                Gs, dqT = carry[h]
                a, lf = _sb_gates(_dot_nt(kc, qm[h]))
                if masked:
                    lf = jnp.where(causal, lf, 0.0)
                w = jnp.exp(a + _dot(tl_, lf.astype(BF16))[:SB_CH] + r_ref[h, pl.ds(c, 1), :])
                if masked:
                    w = jnp.where(causal, w, 0.0)
                g = w * _dot_nt(vc, dom[h])
                cumg = _dot(tg_, g.astype(BF16))
                dz = g - jnp.exp(a) * (Gs + cumg[:SB_CH])
                if masked:
                    dz = jnp.where(causal, dz, 0.0)
                dzb = dz.astype(BF16)
                dk_h = _dot(dzb, qm[h])
                dv_h = _dot(w.astype(BF16), dom[h])
                dk_c = dk_h if h == 0 else dk_c + dk_h
                dv_c = dv_h if h == 0 else dv_c + dv_h
                dqT = dqT + _dot(kTc[h * HEAD_DIM:(h + 1) * HEAD_DIM, :], dzb)
                out.append((Gs + cumg[SB_CH:SB_CH + 1], dqT))
            dk_acc[rows, :] += dk_c
            dv_acc[rows, :] += dv_c
            return tuple(out)

        z1 = jnp.zeros((1, SB_TQ), F32)
        z64 = jnp.zeros((HEAD_DIM, SB_TQ), F32)
        carry = lax.fori_loop(0, i, lambda c, cr: tile(c, cr, False), ((z1, z64), (z1, z64)))
        carry = tile(i, carry, True)
        dq_ref[...] = jnp.concatenate([carry[0][1], carry[1][1]], axis=0).T * Q_SCALE

        @pl.when(i == nq - 1)
        def _():
            cols = pl.ds(pl.multiple_of(hp * LANES, LANES), LANES)
            ck = pltpu.make_async_copy(dk_acc, dk_hbm.at[:, cols], sems.at[0])
            cv = pltpu.make_async_copy(dv_acc, dv_hbm.at[:, cols], sems.at[1])
            ck.start()
            cv.start()
            ck.wait()
            cv.wait()

    blk = lambda hp, i: (i, hp)
    return pl.pallas_call(
        body, name=name, grid=(HEAD_PAIRS, nq),
        in_specs=[pl.BlockSpec((SB_TQ, LANES), blk),
                  pl.BlockSpec((SB_TQ, LANES), blk),
                  pl.BlockSpec((2, nch, SB_TQ), lambda hp, i: (hp, 0, i)),
                  pl.BlockSpec((S, LANES), lambda hp, i: (0, HEAD_PAIRS + hp)),
                  pl.BlockSpec((S, LANES), lambda hp, i: (0, 2 * HEAD_PAIRS + hp)),
                  pl.BlockSpec((None, nch, LANES, SB_CH), lambda hp, i: (hp, 0, 0, 0)),
                  pl.BlockSpec((SB_CH + 8, SB_CH), lambda hp, i: (0, 0)),
                  pl.BlockSpec((SB_CH + 8, SB_CH), lambda hp, i: (0, 0))],
        out_specs=[pl.BlockSpec((SB_TQ, LANES), blk), pl.BlockSpec(memory_space=pl.ANY),
                   pl.BlockSpec(memory_space=pl.ANY)],
        out_shape=[jax.ShapeDtypeStruct((S, D_MODEL), F32)] * 3,
        scratch_shapes=[pltpu.VMEM((S, LANES), F32), pltpu.VMEM((S, LANES), F32), pltpu.SemaphoreType.DMA((2,))],
        compiler_params=_params(("arbitrary", "arbitrary"), VMEM_BIG),
    )(qkv, do, rsum, qkv, qkv, kT3, tmat_l, tmat_g)


def _dil_valid(n):
    qi = lax.broadcasted_iota(jnp.int32, (DIL_BLK, 2 * DIL_BLK), 0)
    kj = lax.broadcasted_iota(jnp.int32, (DIL_BLK, 2 * DIL_BLK), 1)
    dist = DIL_BLK + qi - kj
    return (dist >= 0) & (dist <= DIL_BLK) & ((n > 0) | (kj >= DIL_BLK))


def _lane_pick(tile, idx):
    lane = lax.broadcasted_iota(jnp.int32, tile.shape, 1)
    return jnp.sum(jnp.where(lane == idx, tile, 0.0), axis=-1, keepdims=True)


def _dil_specs(d, width):
    cur = pl.BlockSpec((DIL_BLK, width), lambda r, n: (n, r))
    prev = pl.BlockSpec((DIL_BLK, width), lambda r, n: (jnp.maximum(n - 1, 0), r))
    return cur, prev


def _dil_fwd(q, k, v, d, name):
    S = q.shape[0]
    L = S // d
    nb = L // DIL_BLK

    def body(q_ref, kc_ref, kp_ref, vc_ref, vp_ref, o_ref, lse_ref):
        valid = _dil_valid(pl.program_id(1))
        lane = lax.broadcasted_iota(jnp.int32, (DIL_BLK, LANES), 1)
        lse_t = jnp.zeros((DIL_BLK, LANES), F32)
        for hp in range(HEAD_PAIRS):
            cols = slice(hp * LANES, (hp + 1) * LANES)
            qm = _head_masks(q_ref[:, cols])
            kk = jnp.concatenate([kp_ref[:, cols], kc_ref[:, cols]], axis=0)
            vm = _head_masks(jnp.concatenate([vp_ref[:, cols], vc_ref[:, cols]], axis=0))
            o2 = None
            for h in range(2):
                s = jnp.where(valid, _dot_nt(qm[h], kk), -1e30)
                m = jnp.max(s, axis=-1, keepdims=True)
                p = jnp.exp(s - m)
                den = jnp.sum(p, axis=-1, keepdims=True)
                oh = _dot(p.astype(BF16), vm[h]) / den
                o2 = oh if o2 is None else o2 + oh
                lse_t = jnp.where(lane == 2 * hp + h, m + jnp.log(den), lse_t)
            o_ref[:, cols] = o2
        lse_ref[...] = lse_t

    cur, prev = _dil_specs(d, D_MODEL)
    lcur, _ = _dil_specs(d, LANES)
    view = lambda a: a.reshape(L, d * a.shape[1])
    o, lse = pl.pallas_call(
        body, name=name, grid=(d, nb),
        in_specs=[cur, cur, prev, cur, prev],
        out_specs=[cur, lcur],
        out_shape=[jax.ShapeDtypeStruct((L, d * D_MODEL), F32), jax.ShapeDtypeStruct((L, d * LANES), F32)],
        compiler_params=_params(("parallel", "parallel")),
    )(view(q), view(k), view(k), view(v), view(v))
    return o.reshape(S, D_MODEL), lse.reshape(S, LANES)


def _head_expand():
    return jnp.asarray((np.arange(LANES)[:, None] == np.arange(D_MODEL)[None, :] // HEAD_DIM).astype(np.float32), BF16)


def _dil_merge(os_, lses, name):
    S = os_[0].shape[0]
    tm = 256
    nbr = len(os_)

    def body(*refs):
        o_refs, l_refs, e_ref = refs[:nbr], refs[nbr:2 * nbr], refs[2 * nbr]
        out_ref, outb_ref, lse_ref = refs[2 * nbr + 1:]
        ls = [r[...] for r in l_refs]
        m = ls[0]
        for l in ls[1:]:
            m = jnp.maximum(m, l)
        tot = jnp.exp(ls[0] - m)
        for l in ls[1:]:
            tot = tot + jnp.exp(l - m)
        lse = m + jnp.log(tot)
        acc = None
        for o_r, l in zip(o_refs, ls):
            wt = _dot3(jnp.exp(l - lse), e_ref[...])
            term = wt * o_r[...]
            acc = term if acc is None else acc + term
        out_ref[...] = acc
        outb_ref[...] = acc.astype(BF16)
        lse_ref[...] = lse

    row = lambda i: (i, 0)
    return pl.pallas_call(
        body, name=name, grid=(S // tm,),
        in_specs=[pl.BlockSpec((tm, D_MODEL), row)] * nbr + [pl.BlockSpec((tm, LANES), row)] * nbr
        + [pl.BlockSpec((LANES, D_MODEL), lambda i: (0, 0))],
        out_specs=[pl.BlockSpec((tm, D_MODEL), row), pl.BlockSpec((tm, D_MODEL), row), pl.BlockSpec((tm, LANES), row)],
        out_shape=[jax.ShapeDtypeStruct((S, D_MODEL), F32), jax.ShapeDtypeStruct((S, D_MODEL), BF16),
                   jax.ShapeDtypeStruct((S, LANES), F32)],
        compiler_params=_params(("parallel",)),
    )(*os_, *lses, _head_expand())


def _dil_bwd(q, k, v, do, lse, dlt, d, name):
    S = q.shape[0]
    L = S // d
    nb = L // DIL_BLK

    def body(q_ref, kc_ref, kp_ref, vc_ref, vp_ref, do_ref, lse_ref, dl_ref,
             dq_ref, dka_ref, dkb_ref, dva_ref, dvb_ref):
        valid = _dil_valid(pl.program_id(1))
        lse_t = lse_ref[...]
        dl_t = dl_ref[...]
        for hp in range(HEAD_PAIRS):
            cols = slice(hp * LANES, (hp + 1) * LANES)
            qm = _head_masks(q_ref[:, cols])
            dom = _head_masks(do_ref[:, cols].astype(BF16))
            kk = jnp.concatenate([kp_ref[:, cols], kc_ref[:, cols]], axis=0)
            vv = jnp.concatenate([vp_ref[:, cols], vc_ref[:, cols]], axis=0)
            km = _head_masks(kk)
            dq2 = dkk = dvv = None
            for h in range(2):
                s = _dot_nt(qm[h], kk)
                p = jnp.where(valid, jnp.exp(s - _lane_pick(lse_t, 2 * hp + h)), 0.0)
                ds = (p * (_dot_nt(dom[h], vv) - _lane_pick(dl_t, 2 * hp + h))).astype(BF16)
                t_q = _dot(ds, km[h])
                t_k = _dot_tn(ds, qm[h])
                t_v = _dot_tn(p.astype(BF16), dom[h])
                dq2 = t_q if dq2 is None else dq2 + t_q
                dkk = t_k if dkk is None else dkk + t_k
                dvv = t_v if dvv is None else dvv + t_v
            dq_ref[:, cols] = dq2
            dkb_ref[:, cols] = dkk[:DIL_BLK]
            dka_ref[:, cols] = dkk[DIL_BLK:]
            dvb_ref[:, cols] = dvv[:DIL_BLK]
            dva_ref[:, cols] = dvv[DIL_BLK:]

    cur, prev = _dil_specs(d, D_MODEL)
    lcur, _ = _dil_specs(d, LANES)
    view = lambda a: a.reshape(L, d * a.shape[1])
    outs = pl.pallas_call(
        body, name=name, grid=(d, nb),
        in_specs=[cur, cur, prev, cur, prev, cur, lcur, lcur],
        out_specs=[cur] * 5,
        out_shape=[jax.ShapeDtypeStruct((L, d * D_MODEL), F32)] * 5,
        compiler_params=_params(("parallel", "parallel"), VMEM_BIG),
    )(view(q), view(k), view(k), view(v), view(v), view(do), view(lse), view(dlt))
    return [o.reshape(S, D_MODEL) for o in outs]


def _dil_combine(parts, rope, name):
    S = parts[0][0].shape[0]
    tm = DIL_BLK
    nblk = S // tm
    dils = [d for _, d in DILATED_BRANCHES]

    def body(*refs):
        ins = refs[:5 * len(dils)]
        c_ref, s1_ref, s2_ref, o_ref = refs[5 * len(dils):]
        i = pl.program_id(0)
        tabs = (c_ref[...], s1_ref[...], s2_ref[...])
        dq = dk = dv = None
        for b, d in enumerate(dils):
            dq_r, dka_r, dkb_r, dva_r, dvb_r = ins[5 * b:5 * b + 5]
            live = (i + d < nblk).astype(F32)
            tq = dq_r[...]
            tk = dka_r[...] + live * dkb_r[...]
            tv = dva_r[...] + live * dvb_r[...]
            dq = tq if dq is None else dq + tq
            dk = tk if dk is None else dk + tk
            dv = tv if dv is None else dv + tv
        dq = dq * Q_SCALE
        for g in range(HEAD_PAIRS):
            cols = slice(g * LANES, (g + 1) * LANES)
            o_ref[:, g * LANES:(g + 1) * LANES] = _rope_apply(dq[:, cols], *tabs, -1.0).astype(BF16)
            o_ref[:, D_MODEL + g * LANES:D_MODEL + (g + 1) * LANES] = _rope_apply(dk[:, cols], *tabs, -1.0).astype(BF16)
        o_ref[:, 2 * D_MODEL:] = dv.astype(BF16)

    row = pl.BlockSpec((tm, D_MODEL), lambda i: (i, 0))
    in_specs = []
    args = []
    for (dq_b, dka, dkb, dva, dvb), d in zip(parts, dils):
        ahead = pl.BlockSpec((tm, D_MODEL), lambda i, d=d: (jnp.minimum(i + d, nblk - 1), 0))
        in_specs += [row, row, ahead, row, ahead]
        args += [dq_b, dka, dkb, dva, dvb]
    in_specs += [pl.BlockSpec((tm, LANES), lambda i: (i, 0))] * 3
    return pl.pallas_call(
        body, name=name, grid=(nblk,),
        in_specs=in_specs,
        out_specs=pl.BlockSpec((tm, 3 * D_MODEL), lambda i: (i, 0)),
        out_shape=jax.ShapeDtypeStruct((S, 3 * D_MODEL), BF16),
        compiler_params=_params(("parallel",), VMEM_BIG),
    )(*args, *rope)


def _mesh_pos():
    return lax.axis_index("x"), lax.axis_index("y"), lax.axis_index("c")


def _all_gather(shard, name):
    R, C = shard.shape

    def body(x_ref, out_ref, send_sems, recv_sems, local_sem):
        x, y, c = _mesh_pos()
        me, sibling = (x, y, c), (x, y, 1 - c)
        chips = [(1 - x, y), (x, 1 - y), (1 - x, 1 - y)]

        def blk(p):
            return out_ref.at[4 * p[0] + 2 * p[1] + p[2]]

        def copy(k, block, to, src=None):
            return pltpu.make_async_remote_copy(
                src_ref=blk(block) if src is None else src, dst_ref=blk(block),
                send_sem=send_sems.at[k], recv_sem=recv_sems.at[k],
                device_id=to, device_id_type=pl.DeviceIdType.MESH)

        mine = pltpu.make_async_copy(x_ref, blk(me), local_sem)
        mine.start()
        first = [copy(0, me, sibling, src=x_ref)]
        first += [copy(1 + j, me, (*chip, c), src=x_ref) for j, chip in enumerate(chips)]
        for cp in first:
            cp.start()
        passed = [copy(4 + j, (*chip, c), sibling) for j, chip in enumerate(chips)]
        for j, chip in enumerate(chips):
            copy(1 + j, (*chip, c), me).wait_recv()
            passed[j].start()
        copy(0, sibling, me).wait_recv()
        for j, chip in enumerate(chips):
            copy(4 + j, (*chip, 1 - c), me).wait_recv()
        for cp in first + passed:
            cp.wait_send()
        mine.wait()

    return pl.pallas_call(
        body, name=name,
        in_specs=[pl.BlockSpec(memory_space=pl.ANY)],
        out_specs=pl.BlockSpec(memory_space=pl.ANY),
        out_shape=jax.ShapeDtypeStruct((N_DEV, R, C), shard.dtype),
        scratch_shapes=[pltpu.SemaphoreType.DMA((7,)), pltpu.SemaphoreType.DMA((7,)), pltpu.SemaphoreType.DMA],
    )(shard)


def _rs_pair(g, name):
    _, R, C = g.shape

    def body(g_ref, out_ref, send_sems, recv_sems):
        x, y, c = _mesh_pos()
        sibling = (x, y, 1 - c)
        cps = []
        for chip in range(4):
            cps.append(pltpu.make_async_remote_copy(
                src_ref=g_ref.at[2 * chip + (1 - c)], dst_ref=out_ref.at[chip],
                send_sem=send_sems.at[chip], recv_sem=recv_sems.at[chip],
                device_id=sibling, device_id_type=pl.DeviceIdType.MESH))
        for cp in cps:
            cp.start()
        for cp in cps:
            cp.wait_recv()
        for cp in cps:
            cp.wait_send()

    return pl.pallas_call(
        body, name=name,
        in_specs=[pl.BlockSpec(memory_space=pl.ANY)],
        out_specs=pl.BlockSpec(memory_space=pl.ANY),
        out_shape=jax.ShapeDtypeStruct((4, R, C), g.dtype),
        scratch_shapes=[pltpu.SemaphoreType.DMA((4,)), pltpu.SemaphoreType.DMA((4,))],
    )(g)


def _pair_add(g, got, cidx, name):
    _, R, C = g.shape
    tr = 256

    def body(c_ref, g_ref, r_ref, o_ref):
        o_ref[...] = g_ref[...] + r_ref[...]

    return pl.pallas_call(
        body, name=name,
        grid_spec=pltpu.PrefetchScalarGridSpec(
            num_scalar_prefetch=1, grid=(4, R // tr),
            in_specs=[pl.BlockSpec((None, tr, C), lambda k, i, c: (2 * k + c[0], i, 0)),
                      pl.BlockSpec((None, tr, C), lambda k, i, c: (k, i, 0))],
            out_specs=pl.BlockSpec((None, tr, C), lambda k, i, c: (k, i, 0))),
        out_shape=jax.ShapeDtypeStruct((4, R, C), g.dtype),
        compiler_params=_params(("parallel", "parallel")),
    )(cidx, g, got)


def _rs_chips(p, name):
    _, R, C = p.shape

    def body(p_ref, out_ref, send_sems, recv_sems):
        x, y, c = _mesh_pos()
        chips = [(1 - x, y), (x, 1 - y), (1 - x, 1 - y)]
        cps = []
        for j, (cx, cy) in enumerate(chips):
            cps.append(pltpu.make_async_remote_copy(
                src_ref=p_ref.at[2 * cx + cy], dst_ref=out_ref.at[j],
                send_sem=send_sems.at[j], recv_sem=recv_sems.at[j],
                device_id=(cx, cy, c), device_id_type=pl.DeviceIdType.MESH))
        for cp in cps:
            cp.start()
        for cp in cps:
            cp.wait_recv()
        for cp in cps:
            cp.wait_send()

    return pl.pallas_call(
        body, name=name,
        in_specs=[pl.BlockSpec(memory_space=pl.ANY)],
        out_specs=pl.BlockSpec(memory_space=pl.ANY),
        out_shape=jax.ShapeDtypeStruct((3, R, C), p.dtype),
        scratch_shapes=[pltpu.SemaphoreType.DMA((3,)), pltpu.SemaphoreType.DMA((3,))],
    )(p)


def _adamw_math(w, g, m, v):
    m2 = ADAM_B1 * m + (1.0 - ADAM_B1) * g
    v2 = ADAM_B2 * v + (1.0 - ADAM_B2) * (g * g)
    m_hat = m2 / (1.0 - ADAM_B1 ** ADAM_STEP)
    v_hat = v2 / (1.0 - ADAM_B2 ** ADAM_STEP)
    delta = -ADAM_LR * (m_hat / (jnp.sqrt(v_hat) + ADAM_EPS) + ADAM_WD * w)
    return delta, m2, v2


def _adamw_shard(p, got, chip_idx, w, m, v, name):
    R, C = w.shape
    tr = 256

    def body(k_ref, p_ref, r_ref, w_ref, m_ref, v_ref, g_out, d_out, m_out, v_out):
        g = ((p_ref[...] + r_ref[0]) + r_ref[1]) + r_ref[2]
        delta, m2, v2 = _adamw_math(w_ref[...], g, m_ref[...], v_ref[...])
        g_out[...] = g
        d_out[...] = delta
        m_out[...] = m2
        v_out[...] = v2

    row = pl.BlockSpec((tr, C), lambda i, k: (i, 0))
    return pl.pallas_call(
        body, name=name,
        grid_spec=pltpu.PrefetchScalarGridSpec(
            num_scalar_prefetch=1, grid=(R // tr,),
            in_specs=[pl.BlockSpec((None, tr, C), lambda i, k: (k[0], i, 0)),
                      pl.BlockSpec((3, tr, C), lambda i, k: (0, i, 0)), row, row, row],
            out_specs=[row] * 4),
        out_shape=[jax.ShapeDtypeStruct((R, C), F32)] * 4,
        compiler_params=_params(("parallel",)),
    )(chip_idx, p, got, w, m, v)


def _adamw_small(gathered, w, m, v, name):
    _, R, C = gathered.shape

    def body(a_ref, w_ref, m_ref, v_ref, g_out, d_out, m_out, v_out):
        g = a_ref[0]
        for k in range(1, N_DEV):
            g = g + a_ref[k]
        delta, m2, v2 = _adamw_math(w_ref[...], g, m_ref[...], v_ref[...])
        g_out[...] = g
        d_out[...] = delta
        m_out[...] = m2
        v_out[...] = v2

    return pl.pallas_call(
        body, name=name, out_shape=[jax.ShapeDtypeStruct((R, C), F32)] * 4,
    )(gathered, w, m, v)


def _rope_tables(S):
    half = ROPE_DIM // 2
    inv_freq = ROPE_THETA ** (-jnp.arange(half, dtype=F32) / half)
    ang = jnp.arange(S, dtype=jnp.int32).astype(F32)[:, None] * inv_freq[None, :]
    cos, sin = jnp.cos(ang), jnp.sin(ang)
    ones = jnp.ones((S, HEAD_DIM - ROPE_DIM), F32)
    zeros = jnp.zeros((S, HEAD_DIM - ROPE_DIM), F32)
    zh = jnp.zeros((S, half), F32)
    c = jnp.concatenate([cos, cos, ones], axis=1)
    s1 = jnp.concatenate([zh, sin, zeros], axis=1)
    s2 = jnp.concatenate([-sin, zh, zeros], axis=1)
    two = lambda t: jnp.concatenate([t, t], axis=1)
    return two(c), two(s1), two(s2)


def _chunk_transposed(a, S):
    return a.reshape(S // SB_CH, SB_CH, HEAD_PAIRS, LANES).transpose(2, 0, 3, 1)


def _flat_shards(ws):
    return jnp.concatenate([w.reshape(-1, D_MODEL) for layer in ws for w in layer], axis=0)


def kernel(x, w_qkv_0, w_o_0, ln1_g_0, ln1_b_0, w_ff1_0, w_ff2_0, ln2_g_0, ln2_b_0, w_qkv_1, w_o_1, ln1_g_1, ln1_b_1, w_ff1_1, w_ff2_1, ln2_g_1, ln2_b_1, loss_target, m_w_qkv_0, m_w_o_0, m_ln1_g_0, m_ln1_b_0, m_w_ff1_0, m_w_ff2_0, m_ln2_g_0, m_ln2_b_0, m_w_qkv_1, m_w_o_1, m_ln1_g_1, m_ln1_b_1, m_w_ff1_1, m_w_ff2_1, m_ln2_g_1, m_ln2_b_1, v_w_qkv_0, v_w_o_0, v_ln1_g_0, v_ln1_b_0, v_w_ff1_0, v_w_ff2_0, v_ln2_g_0, v_ln2_b_0, v_w_qkv_1, v_w_o_1, v_ln1_g_1, v_ln1_b_1, v_w_ff1_1, v_w_ff2_1, v_ln2_g_1, v_ln2_b_1):
    S = x.shape[1]
    x0 = x.reshape(S, D_MODEL)
    target = loss_target.reshape(S, D_MODEL)
    mats = ((w_qkv_0, w_o_0, w_ff1_0, w_ff2_0), (w_qkv_1, w_o_1, w_ff1_1, w_ff2_1))
    mats_m = ((m_w_qkv_0, m_w_o_0, m_w_ff1_0, m_w_ff2_0), (m_w_qkv_1, m_w_o_1, m_w_ff1_1, m_w_ff2_1))
    mats_v = ((v_w_qkv_0, v_w_o_0, v_w_ff1_0, v_w_ff2_0), (v_w_qkv_1, v_w_o_1, v_w_ff1_1, v_w_ff2_1))
    vecs = (ln1_g_0, ln1_b_0, ln2_g_0, ln2_b_0, ln1_g_1, ln1_b_1, ln2_g_1, ln2_b_1)
    vecs_m = (m_ln1_g_0, m_ln1_b_0, m_ln2_g_0, m_ln2_b_0, m_ln1_g_1, m_ln1_b_1, m_ln2_g_1, m_ln2_b_1)
    vecs_v = (v_ln1_g_0, v_ln1_b_0, v_ln2_g_0, v_ln2_b_0, v_ln1_g_1, v_ln1_b_1, v_ln2_g_1, v_ln2_b_1)

    w_flat = _flat_shards(mats)
    w_all = _all_gather(w_flat.astype(BF16), "ag_weights")
    layers = []
    for l in range(N_LAYERS):
        base = l * LAYER_ROWS
        r0, r1, r2, r3 = np.cumsum((0,) + SHARD_ROWS)[:4] + base
        layers.append(dict(
            qkv=w_all[:, r0:r0 + 384].reshape(N_DEV, D_MODEL, 384),
            o=w_all[:, r1:r1 + 128].reshape(D_MODEL, D_MODEL),
            ff1=w_all[:, r2:r2 + 512].reshape(N_DEV, D_MODEL, 512),
            ff2=w_all[:, r3:r3 + 512].reshape(D_FF, D_MODEL),
            g1=vecs[4 * l].reshape(1, D_MODEL), b1=vecs[4 * l + 1].reshape(1, D_MODEL),
            g2=vecs[4 * l + 2].reshape(1, D_MODEL), b2=vecs[4 * l + 3].reshape(1, D_MODEL)))

    rope = _rope_tables(S)
    tmat_later = _sb_tmat(True)
    tmat_upto = _sb_tmat(False)

    saved = []
    xin, xinb = x0, x0.astype(BF16)
    for l, W in enumerate(layers):
        sv = dict(xin=xin, xinb=xinb)
        qkv = _qkv_proj(xinb, W["qkv"], rope if l == 1 else None, f"qkv_proj_{l}")
        sv["qkv"] = qkv
        if l == 0:
            vT3 = _chunk_transposed(qkv[:, 2 * D_MODEL:], S)
            o, rsum = _sb_fwd(qkv, vT3, tmat_later, "sb_fwd")
            ob = o.astype(BF16)
            sv["rsum"] = rsum
        else:
            q, k, v = qkv[:, :D_MODEL], qkv[:, D_MODEL:2 * D_MODEL], qkv[:, 2 * D_MODEL:]
            outs = [_dil_fwd(q, k, v, d, f"dil_fwd_{d}") for _, d in DILATED_BRANCHES]
            o, ob, lse = _dil_merge([t[0] for t in outs], [t[1] for t in outs], "dil_merge")
            sv.update(q=q, k=k, v=v, lse=lse)
        sv.update(o=o, ob=ob)
        y1, x1, x1b = _mm_res_ln(ob, xin, W["o"], W["g1"], W["b1"], f"attn_out_ln_{l}")
        hpre, h = _ff1(x1b, W["ff1"], f"ff1_{l}")
        y2, x2, x2b = _mm_res_ln(h, x1, W["ff2"], W["g2"], W["b2"], f"ff2_ln_{l}")
        sv.update(y1=y1, x1=x1, x1b=x1b, hpre=hpre, h=h, y2=y2)
        saved.append(sv)
        xin, xinb = x2, x2b

    dout, loss_parts = _loss_grad(xin, target, "loss_grad")
    loss = lax.psum(jnp.sum(loss_parts), MESH_AXES)

    gmats = [None] * N_LAYERS
    gvecs = [None] * (4 * N_LAYERS)
    for l in reversed(range(N_LAYERS)):
        W, sv = layers[l], saved[l]
        dy2, dy2b, gb2 = _ln_bwd(dout, sv["y2"], W["g2"], f"ln2_bwd_{l}")
        dhp = _dh(dy2b, W["ff2"], sv["hpre"], f"dh_{l}")
        g_ff2 = _mm_tn(sv["h"], dy2b, 512, D_MODEL, False, f"dw_ff2_{l}")
        dx1 = _dx_blk(dy2, dhp, W["ff1"], f"dx_ff1_{l}")
        g_ff1 = _mm_tn(sv["x1b"], dhp, D_MODEL, 512, True, f"dw_ff1_{l}")
        dy1, dy1b, gb1 = _ln_bwd(dx1, sv["y1"], W["g1"], f"ln1_bwd_{l}")
        do = _mm_nt_plain(dy1b, W["o"], f"do_{l}")
        g_o = _mm_tn(sv["ob"], dy1b, 512, D_MODEL, False, f"dw_o_{l}")
        if l == 0:
            kT3 = _chunk_transposed(sv["qkv"][:, D_MODEL:2 * D_MODEL], S)
            dq, dk, dv = _sb_bwd(sv["qkv"], kT3, do, sv["rsum"], tmat_later, tmat_upto, "sb_bwd")
            dqkv = jnp.concatenate([dq, dk, dv], axis=1).astype(BF16)
        else:
            dlt = _head_sums(do, sv["o"], "head_sums")
            parts = [_dil_bwd(sv["q"], sv["k"], sv["v"], do, sv["lse"], dlt, d, f"dil_bwd_{d}")
                     for _, d in DILATED_BRANCHES]
            dqkv = _dil_combine(parts, rope, "dil_combine")
        dout = _dx_blk(dy1, dqkv, W["qkv"], f"dx_qkv_{l}")
        g_qkv = _mm_tn(sv["xinb"], dqkv, D_MODEL, 384, True, f"dw_qkv_{l}")
        gmats[l] = (g_qkv.reshape(N_DEV, 384, D_MODEL), g_o.reshape(N_DEV, 128, D_MODEL),
                    g_ff1.reshape(N_DEV, 512, D_MODEL), g_ff2.reshape(N_DEV, 512, D_MODEL))
        gvecs[4 * l:4 * l + 4] = [gb1[0], gb1[1], gb2[0], gb2[1]]
    grad_x = dout.reshape(1, S, D_MODEL)

    cx, cy, cc = _mesh_pos()
    g_all = jnp.concatenate([g for layer in gmats for g in layer], axis=1)
    got_pair = _rs_pair(g_all, "rs_pair")
    chip_part = _pair_add(g_all, got_pair, cc.astype(jnp.int32).reshape(1), "rs_pair_add")
    got_chips = _rs_chips(chip_part, "rs_chips")
    chip_idx = (2 * cx + cy).astype(jnp.int32).reshape(1)
    g_sh, d_sh, m_sh, v_sh = _adamw_shard(chip_part, got_chips, chip_idx, w_flat, _flat_shards(mats_m),
                                          _flat_shards(mats_v), "adamw_mats")

    def unflat(a):
        out, pos = [], 0
        for layer in mats:
            for w in layer:
                n = w.size // D_MODEL
                out.append(a[pos:pos + n].reshape(w.shape))
                pos += n
        return out

    gv_all = _all_gather(jnp.stack(gvecs), "ag_vec_grads")
    g_v, d_v, m_v, v_v = _adamw_small(gv_all, jnp.stack(vecs), jnp.stack(vecs_m), jnp.stack(vecs_v), "adamw_vecs")

    def interleave(mat_list, vec_arr):
        out = []
        for l in range(N_LAYERS):
            qkv_, o_, ff1_, ff2_ = mat_list[4 * l:4 * l + 4]
            out += [qkv_, o_, vec_arr[4 * l], vec_arr[4 * l + 1], ff1_, ff2_, vec_arr[4 * l + 2], vec_arr[4 * l + 3]]
        return out

    return (loss, grad_x, *interleave(unflat(g_sh), g_v), *interleave(unflat(d_sh), d_v),
            *interleave(unflat(m_sh), m_v), *interleave(unflat(v_sh), v_v))
```

```python
import functools
import math

import jax
import jax.numpy as jnp
import numpy as np
from jax import lax
from jax.experimental import pallas as pl
from jax.experimental.pallas import tpu as pltpu

F32 = jnp.float32
BF16 = jnp.bfloat16

D_MODEL = 1024
N_HEADS = 16
HEAD_DIM = 64
D_FF = 4096
N_DEV = 8
N_LAYERS = 2
ROPE_THETA = 500000.0
ROPE_DIM = 16
DILATED_BRANCHES = ((128, 1), (512, 4), (2048, 16))
ALPHA = (2 * N_LAYERS) ** 0.25
LN_EPS = 1e-5
Q_SCALE = 1.0 / math.sqrt(HEAD_DIM)
LOG2E = math.log2(math.e)
LN2 = math.log(2.0)
ADAM_LR, ADAM_B1, ADAM_B2, ADAM_EPS, ADAM_WD, ADAM_STEP = 0.001, 0.9, 0.999, 1e-08, 0.01, 10

LANES = 128
HEAD_PAIRS = D_MODEL // LANES
SB_TQ = 256
SB_CH = 256
DIL_BLK = 128
VMEM_BIG = 56 * 2 ** 20
MESH_AXES = ("x", "y", "c")

SHARD_ROWS = (384, 128, 512, 512)
LAYER_ROWS = sum(SHARD_ROWS)
ALL_ROWS = N_LAYERS * LAYER_ROWS


def _params(sem=None, vmem=None):
    kw = {}
    if sem is not None:
        kw["dimension_semantics"] = sem
    if vmem is not None:
        kw["vmem_limit_bytes"] = vmem
    return pltpu.CompilerParams(**kw)


def _dot(a, b):
    return jnp.dot(a, b, preferred_element_type=F32)


def _dot_nt(a, b):
    return lax.dot_general(a, b, (((1,), (1,)), ((), ())), preferred_element_type=F32)


def _dot_tn(a, b):
    return lax.dot_general(a, b, (((0,), (0,)), ((), ())), preferred_element_type=F32)


def _split3(p):
    hi = p.astype(BF16)
    r1 = p - hi.astype(F32)
    mid = r1.astype(BF16)
    lo = (r1 - mid.astype(F32)).astype(BF16)
    return hi, mid, lo


def _dot3(p, e):
    hi, mid, lo = _split3(p)
    return _dot(hi, e) + _dot(mid, e) + _dot(lo, e)


def _rope_apply(a, c, s1, s2, sign):
    return a * c + sign * (pltpu.roll(a, 8, 1) * s1 + pltpu.roll(a, LANES - 8, 1) * s2)


def _qkv_proj(xb, w_blk, rope, q_mult, name):
    S = xb.shape[0]
    tm = 512
    n_rope = 0 if rope is None else 3

    def body(*refs):
        x_ref, w_ref = refs[:2]
        tabs = [r[...] for r in refs[2:2 + n_rope]]
        o_ref = refs[2 + n_rope]
        x = x_ref[...]
        for j in range(N_DEV):
            acc = _dot(x, w_ref[j])
            for g in range(3):
                col = j * 384 + g * LANES
                a = acc[:, g * LANES:(g + 1) * LANES]
                if n_rope and col < 2 * D_MODEL:
                    a = _rope_apply(a, *tabs, 1.0)
                if col < D_MODEL:
                    a = a * q_mult
                o_ref[:, col:col + LANES] = a.astype(BF16)

    tab_specs = [pl.BlockSpec((tm, LANES), lambda i: (i, 0))] * n_rope
    return pl.pallas_call(
        body, name=name, grid=(S // tm,),
        in_specs=[pl.BlockSpec((tm, D_MODEL), lambda i: (i, 0)),
                  pl.BlockSpec((N_DEV, D_MODEL, 384), lambda i: (0, 0, 0))] + tab_specs,
        out_specs=pl.BlockSpec((tm, 3 * D_MODEL), lambda i: (i, 0)),
        out_shape=jax.ShapeDtypeStruct((S, 3 * D_MODEL), BF16),
        compiler_params=_params(("parallel",), VMEM_BIG),
    )(xb, w_blk, *(rope or ()))


def _layer_norm_rows(y, g, b):
    mu = jnp.mean(y, axis=-1, keepdims=True)
    yc = y - mu
    var = jnp.mean(yc * yc, axis=-1, keepdims=True)
    return yc * lax.rsqrt(var + LN_EPS) * g + b


def _mm_res_ln(a, xres, w, g, b, name):
    S, K = a.shape
    tm = 512 if K <= 1024 else 256

    def body(a_ref, x_ref, w_ref, g_ref, b_ref, y_ref, xn_ref, xb_ref):
        y = ALPHA * x_ref[...] + _dot(a_ref[...], w_ref[...])
        xn = _layer_norm_rows(y, g_ref[...], b_ref[...])
        y_ref[...] = y
        xn_ref[...] = xn
        xb_ref[...] = xn.astype(BF16)

    row = lambda i: (i, 0)
    fix = lambda i: (0, 0)
    return pl.pallas_call(
        body, name=name, grid=(S // tm,),
        in_specs=[pl.BlockSpec((tm, K), row), pl.BlockSpec((tm, D_MODEL), row),
                  pl.BlockSpec((K, D_MODEL), fix), pl.BlockSpec((1, D_MODEL), fix),
                  pl.BlockSpec((1, D_MODEL), fix)],
        out_specs=[pl.BlockSpec((tm, D_MODEL), row)] * 3,
        out_shape=[jax.ShapeDtypeStruct((S, D_MODEL), F32), jax.ShapeDtypeStruct((S, D_MODEL), F32),
                   jax.ShapeDtypeStruct((S, D_MODEL), BF16)],
        compiler_params=_params(("parallel",), VMEM_BIG),
    )(a, xres, w, g, b)


def _ff1(xb, w_blk, name):
    S = xb.shape[0]
    tm = 256

    def body(x_ref, w_ref, hp_ref, h_ref):
        x = x_ref[...]
        for j in range(N_DEV):
            acc = _dot(x, w_ref[j])
            r = jnp.maximum(acc, 0.0)
            hp_ref[:, j * 512:(j + 1) * 512] = acc
            h_ref[:, j * 512:(j + 1) * 512] = (r * r).astype(BF16)

    return pl.pallas_call(
        body, name=name, grid=(S // tm,),
        in_specs=[pl.BlockSpec((tm, D_MODEL), lambda i: (i, 0)),
                  pl.BlockSpec((N_DEV, D_MODEL, 512), lambda i: (0, 0, 0))],
        out_specs=[pl.BlockSpec((tm, D_FF), lambda i: (i, 0))] * 2,
        out_shape=[jax.ShapeDtypeStruct((S, D_FF), F32), jax.ShapeDtypeStruct((S, D_FF), BF16)],
        compiler_params=_params(("parallel",), VMEM_BIG),
    )(xb, w_blk)


def _loss_grad(y, target, name):
    S = y.shape[0]
    tm = 512

    def body(y_ref, t_ref, dy_ref, l_ref):
        @pl.when(pl.program_id(0) == 0)
        def _():
            l_ref[...] = jnp.zeros_like(l_ref)

        err = y_ref[...] - t_ref[...]
        dy_ref[...] = err * (1.0 / D_MODEL)
        sq = err * err
        rows = sq[0:8]
        for r in range(1, tm // 8):
            rows = rows + sq[r * 8:(r + 1) * 8]
        acc = rows[:, 0:LANES]
        for g in range(1, D_MODEL // LANES):
            acc = acc + rows[:, g * LANES:(g + 1) * LANES]
        l_ref[...] += acc * (0.5 / D_MODEL)

    return pl.pallas_call(
        body, name=name, grid=(S // tm,),
        in_specs=[pl.BlockSpec((tm, D_MODEL), lambda i: (i, 0))] * 2,
        out_specs=[pl.BlockSpec((tm, D_MODEL), lambda i: (i, 0)), pl.BlockSpec((8, LANES), lambda i: (0, 0))],
        out_shape=[jax.ShapeDtypeStruct((S, D_MODEL), F32), jax.ShapeDtypeStruct((8, LANES), F32)],
        compiler_params=_params(("arbitrary",)),
    )(y, target)


def _ln_bwd(dout, y, g, name):
    S = y.shape[0]
    tm = 512
    steps = S // tm

    def body(d_ref, y_ref, g_ref, dy_ref, dyb_ref, gb_ref, acc_g, acc_b):
        i = pl.program_id(0)

        @pl.when(i == 0)
        def _():
            acc_g[...] = jnp.zeros_like(acc_g)
            acc_b[...] = jnp.zeros_like(acc_b)

        d = d_ref[...]
        yv = y_ref[...]
        mu = jnp.mean(yv, axis=-1, keepdims=True)
        yc = yv - mu
        var = jnp.mean(yc * yc, axis=-1, keepdims=True)
        rstd = lax.rsqrt(var + LN_EPS)
        xhat = yc * rstd
        dxh = d * g_ref[...]
        m1 = jnp.mean(dxh, axis=-1, keepdims=True)
        m2 = jnp.mean(dxh * xhat, axis=-1, keepdims=True)
        dy = rstd * (dxh - m1 - xhat * m2)
        dy_ref[...] = dy
        dyb_ref[...] = dy.astype(BF16)
        pg = d * xhat
        sg = pg[0:8]
        sb = d[0:8]
        for r in range(1, tm // 8):
            sg = sg + pg[r * 8:(r + 1) * 8]
            sb = sb + d[r * 8:(r + 1) * 8]
        acc_g[...] += sg
        acc_b[...] += sb

        @pl.when(i == steps - 1)
        def _():
            gb_ref[0:1, :] = jnp.sum(acc_g[...], axis=0, keepdims=True)
            gb_ref[1:2, :] = jnp.sum(acc_b[...], axis=0, keepdims=True)

    row = lambda i: (i, 0)
    fix = lambda i: (0, 0)
    return pl.pallas_call(
        body, name=name, grid=(steps,),
        in_specs=[pl.BlockSpec((tm, D_MODEL), row), pl.BlockSpec((tm, D_MODEL), row), pl.BlockSpec((1, D_MODEL), fix)],
        out_specs=[pl.BlockSpec((tm, D_MODEL), row), pl.BlockSpec((tm, D_MODEL), row), pl.BlockSpec((2, D_MODEL), fix)],
        out_shape=[jax.ShapeDtypeStruct((S, D_MODEL), F32), jax.ShapeDtypeStruct((S, D_MODEL), BF16),
                   jax.ShapeDtypeStruct((2, D_MODEL), F32)],
        scratch_shapes=[pltpu.VMEM((8, D_MODEL), F32), pltpu.VMEM((8, D_MODEL), F32)],
        compiler_params=_params(("arbitrary",)),
    )(dout, y, g)


def _dh(dyb, w2, hpre, name):
    S = dyb.shape[0]
    tm = 256
    tn = 512

    def body(dy_ref, w_ref, hp_ref, o_ref):
        dy = dy_ref[...]
        for n in range(0, D_FF, tn):
            dh = _dot_nt(dy, w_ref[n:n + tn, :])
            o_ref[:, n:n + tn] = (dh * (2.0 * jnp.maximum(hp_ref[:, n:n + tn], 0.0))).astype(BF16)

    return pl.pallas_call(
        body, name=name, grid=(S // tm,),
        in_specs=[pl.BlockSpec((tm, D_MODEL), lambda i: (i, 0)), pl.BlockSpec((D_FF, D_MODEL), lambda i: (0, 0)),
                  pl.BlockSpec((tm, D_FF), lambda i: (i, 0))],
        out_specs=pl.BlockSpec((tm, D_FF), lambda i: (i, 0)),
        out_shape=jax.ShapeDtypeStruct((S, D_FF), BF16),
        compiler_params=_params(("parallel",), VMEM_BIG),
    )(dyb, w2, hpre)


def _dx_blk(dres, dz, w_blk, name):
    S, N = dz.shape
    bw = w_blk.shape[2]
    tm = 256

    def body(r_ref, z_ref, w_ref, o_ref):
        acc = ALPHA * r_ref[...]
        for j in range(N_DEV):
            acc = acc + _dot_nt(z_ref[:, j * bw:(j + 1) * bw], w_ref[j])
        o_ref[...] = acc

    return pl.pallas_call(
        body, name=name, grid=(S // tm,),
        in_specs=[pl.BlockSpec((tm, D_MODEL), lambda i: (i, 0)), pl.BlockSpec((tm, N), lambda i: (i, 0)),
                  pl.BlockSpec((N_DEV, D_MODEL, bw), lambda i: (0, 0, 0))],
        out_specs=pl.BlockSpec((tm, D_MODEL), lambda i: (i, 0)),
        out_shape=jax.ShapeDtypeStruct((S, D_MODEL), F32),
        compiler_params=_params(("parallel",), VMEM_BIG),
    )(dres, dz, w_blk)


def _mm_nt_plain(a, w, name):
    S = a.shape[0]
    tm = 512

    def body(a_ref, w_ref, o_ref):
        o_ref[...] = _dot_nt(a_ref[...], w_ref[...])

    return pl.pallas_call(
        body, name=name, grid=(S // tm,),
        in_specs=[pl.BlockSpec((tm, D_MODEL), lambda i: (i, 0)), pl.BlockSpec((D_MODEL, D_MODEL), lambda i: (0, 0))],
        out_specs=pl.BlockSpec((tm, D_MODEL), lambda i: (i, 0)),
        out_shape=jax.ShapeDtypeStruct((S, D_MODEL), F32),
        compiler_params=_params(("parallel",)),
    )(a, w)


def _mm_tn(a, b, ta, tb, blocked, name):
    S, Ka = a.shape
    Nb = b.shape[1]
    ts = 512

    def body(a_ref, b_ref, o_ref):
        @pl.when(pl.program_id(2) == 0)
        def _():
            o_ref[...] = jnp.zeros_like(o_ref)

        o_ref[...] += _dot_tn(a_ref[...], b_ref[...])

    if blocked:
        out_spec = pl.BlockSpec((None, ta, tb), lambda i, j, s: (j, i, 0))
        out_shape = jax.ShapeDtypeStruct((Nb // tb, Ka, tb), F32)
    else:
        out_spec = pl.BlockSpec((ta, tb), lambda i, j, s: (i, j))
        out_shape = jax.ShapeDtypeStruct((Ka, Nb), F32)
    return pl.pallas_call(
        body, name=name, grid=(Ka // ta, Nb // tb, S // ts),
        in_specs=[pl.BlockSpec((ts, ta), lambda i, j, s: (s, i)), pl.BlockSpec((ts, tb), lambda i, j, s: (s, j))],
        out_specs=out_spec, out_shape=out_shape,
        compiler_params=_params(("parallel", "parallel", "arbitrary"), VMEM_BIG),
    )(a, b)


def _head_sums(do, o, name):
    S = do.shape[0]
    tm = 512
    sel = (np.arange(D_MODEL)[:, None] // HEAD_DIM == np.arange(LANES)[None, :]).astype(np.float32)

    def body(d_ref, o_ref, e_ref, out_ref):
        out_ref[...] = _dot3(d_ref[...] * o_ref[...], e_ref[...])

    return pl.pallas_call(
        body, name=name, grid=(S // tm,),
        in_specs=[pl.BlockSpec((tm, D_MODEL), lambda i: (i, 0))] * 2 + [pl.BlockSpec((D_MODEL, LANES), lambda i: (0, 0))],
        out_specs=pl.BlockSpec((tm, LANES), lambda i: (i, 0)),
        out_shape=jax.ShapeDtypeStruct((S, LANES), F32),
        compiler_params=_params(("parallel",)),
    )(do, o, jnp.asarray(sel, BF16))


def _sb_tmat(later):
    r = np.arange(SB_CH)
    t = (r[None, :] > r[:, None]) if later else (r[None, :] <= r[:, None])
    return jnp.asarray(np.concatenate([t.astype(np.float32), np.ones((8, SB_CH), np.float32)], axis=0), BF16)


def _sb_gates(z2):
    neg_abs = lax.bitcast_convert_type(lax.bitcast_convert_type(z2, jnp.uint32) | jnp.uint32(0x80000000), F32)
    l1 = jnp.log2(1.0 + jnp.exp2(neg_abs))
    a = jnp.minimum(z2, 0.0) - l1
    return a, a - z2


def _head_masks(x2):
    lane = lax.broadcasted_iota(jnp.int32, x2.shape, 1)
    zero = jnp.zeros_like(x2)
    return jnp.where(lane < HEAD_DIM, x2, zero), jnp.where(lane >= HEAD_DIM, x2, zero)


def _sb_fwd(qkv, vT3, tmat, name):
    S = qkv.shape[0]
    nq = S // SB_TQ
    nch = S // SB_CH

    def body(q_ref, k_ref, vT_ref, t_ref, o_ref, r_ref, z_scr, a_scr, cum_scr, oT_scr):
        i = pl.program_id(1)
        qm = _head_masks(q_ref[...])
        causal = (lax.broadcasted_iota(jnp.int32, (SB_CH, SB_TQ), 0)
                  < lax.broadcasted_iota(jnp.int32, (SB_CH, SB_TQ), 1))

        def head_rows(vTc, h):
            return vTc[h * HEAD_DIM:(h + 1) * HEAD_DIM, :]

        @pl.when(jnp.logical_and(pl.program_id(0) == 0, i == 0))
        def _():
            z_scr[...] = jnp.zeros_like(z_scr)
            a_scr[...] = jnp.zeros_like(a_scr)
            cum_scr[...] = jnp.zeros_like(cum_scr)

        kc = k_ref[pl.ds(pl.multiple_of(i * SB_CH, SB_CH), SB_CH), :]
        vTc = vT_ref[i]
        R0 = []
        for h in range(2):
            a, lf = _sb_gates(_dot_nt(kc, qm[h]))
            cum = _dot(t_ref[...], jnp.where(causal, lf, 0.0).astype(BF16))
            w = jnp.where(causal, jnp.exp2(a + cum[:SB_CH]), 0.0)
            r_ref[h, pl.ds(i, 1), :] = jnp.zeros((1, SB_TQ), F32)
            oT_scr[h] = _dot(head_rows(vTc, h), w.astype(BF16))
            R0.append(cum[SB_CH:SB_CH + 1])

        def step(t, p, R):
            cA = jnp.maximum(i - 1 - t, 0)
            kA = k_ref[pl.ds(pl.multiple_of(cA * SB_CH, SB_CH), SB_CH), :]
            valid = jnp.logical_and(t >= 2, t - 2 < i)
            cC = jnp.clip(i + 1 - t, 0, nch - 1)
            row = jnp.where(valid, cC, nch)
            vC = vT_ref[cC]
            out = []
            for h in range(2):
                z_scr[p, h] = _dot_nt(kA, qm[h])
                a, lf = _sb_gates(z_scr[1 - p, h])
                a_scr[1 - p, h] = a
                cum_scr[1 - p, h] = _dot(t_ref[...], lf.astype(BF16))
                w = jnp.exp2(a_scr[p, h] + cum_scr[p, h, :SB_CH, :] + R[h])
                r_ref[h, pl.ds(row, 1), :] = R[h]
                contrib = _dot(head_rows(vC, h), w.astype(BF16))
                oT_scr[h] += jnp.where(valid, contrib, 0.0)
                out.append(R[h] + jnp.where(valid, cum_scr[p, h, SB_CH:SB_CH + 1, :], 0.0))
            return tuple(out)

        def two_steps(tt, R):
            return step(2 * tt + 1, 1, step(2 * tt, 0, R))

        lax.fori_loop(0, (i + 3) // 2, two_steps, tuple(R0))
        o_ref[...] = jnp.concatenate([oT_scr[0], oT_scr[1]], axis=0).T

    return pl.pallas_call(
        body, name=name, grid=(HEAD_PAIRS, nq),
        in_specs=[pl.BlockSpec((SB_TQ, LANES), lambda hp, i: (i, hp)),
                  pl.BlockSpec((S, LANES), lambda hp, i: (0, HEAD_PAIRS + hp)),
                  pl.BlockSpec((None, nch, LANES, SB_CH), lambda hp, i: (hp, 0, 0, 0)),
                  pl.BlockSpec((SB_CH + 8, SB_CH), lambda hp, i: (0, 0))],
        out_specs=[pl.BlockSpec((SB_TQ, LANES), lambda hp, i: (i, hp)),
                   pl.BlockSpec((2, nch + 8, SB_TQ), lambda hp, i: (hp, 0, i))],
        out_shape=[jax.ShapeDtypeStruct((S, D_MODEL), F32), jax.ShapeDtypeStruct((N_HEADS, nch + 8, S), F32)],
        scratch_shapes=[pltpu.VMEM((2, 2, SB_CH, SB_TQ), F32), pltpu.VMEM((2, 2, SB_CH, SB_TQ), F32),
                        pltpu.VMEM((2, 2, SB_CH + 8, SB_TQ), F32), pltpu.VMEM((2, HEAD_DIM, SB_TQ), F32)],
        compiler_params=_params(("arbitrary", "arbitrary"), VMEM_BIG),
    )(qkv, qkv, vT3, tmat)


def _sb_bwd(qkv, kT3, do, rsum, tmat_l, tmat_g, name):
    S = qkv.shape[0]
    nq = S // SB_TQ
    nch = S // SB_CH

    def body(q_ref, do_ref, r_ref, k_ref, v_ref, kT_ref, tl_ref, tg_ref, dq_ref, dk_hbm, dv_hbm, dk_acc, dv_acc, sems,
             z_scr, dwv_scr, a_scr, cum_scr, g_scr, sig_scr, cumg_scr, dqT_scr):
        hp = pl.program_id(0)
        i = pl.program_id(1)

        @pl.when(i == 0)
        def _():
            dk_acc[...] = jnp.zeros_like(dk_acc)
            dv_acc[...] = jnp.zeros_like(dv_acc)

        @pl.when(jnp.logical_and(hp == 0, i == 0))
        def _():
            for scr in (z_scr, dwv_scr, a_scr, cum_scr, g_scr, sig_scr, cumg_scr):
                scr[...] = jnp.zeros_like(scr)

        dqT_scr[...] = jnp.zeros_like(dqT_scr)
        qm = _head_masks(q_ref[...])
        dom = _head_masks(do_ref[...].astype(BF16))
        last = jnp.maximum(i - 1, 0)

        def rows_of(c):
            return pl.ds(pl.multiple_of(c * SB_CH, SB_CH), SB_CH)

        def head_rows(kTc, h):
            return kTc[h * HEAD_DIM:(h + 1) * HEAD_DIM, :]

        def step(t, p, Gs):
            q = 1 - p
            valid_c = jnp.logical_and(t >= 2, t - 2 < i)
            valid_d = jnp.logical_and(t >= 3, t - 3 < i)
            c_c = jnp.clip(t - 2, 0, i)
            c_d = jnp.clip(t - 3, 0, last)
            kA = k_ref[rows_of(jnp.minimum(t, last)), :]
            vB = v_ref[rows_of(jnp.clip(t - 1, 0, last)), :]
            kTd = kT_ref[c_d]
            out = []
            for h in range(2):
                z_scr[p, h] = _dot_nt(kA, qm[h])

                a, lf = _sb_gates(z_scr[q, h])
                a_scr[q, h] = a
                cum_scr[q, h] = _dot(tl_ref[...], lf.astype(BF16))
                dwv_scr[q, h] = _dot_nt(vB, dom[h])

                a_c = a_scr[p, h]
                w = jnp.exp2(a_c + cum_scr[p, h, :SB_CH, :] + r_ref[h, pl.ds(c_c, 1), :])
                g = w * dwv_scr[p, h]
                g_scr[p, h] = g
                sig_scr[p, h] = jnp.exp2(a_c)
                cumg_scr[p, h] = _dot(tg_ref[...], g.astype(BF16))
                dv_h = _dot(w.astype(BF16), dom[h])

                dzb = (g_scr[q, h] - sig_scr[q, h] * (Gs[h] + cumg_scr[q, h, :SB_CH, :])).astype(BF16)
                dk_h = _dot(dzb, qm[h])
                dqT_scr[h] += jnp.where(valid_d, _dot(head_rows(kTd, h), dzb), 0.0)
                out.append(Gs[h] + jnp.where(valid_d, cumg_scr[q, h, SB_CH:SB_CH + 1, :], 0.0))
                dk_c = dk_h if h == 0 else dk_c + dk_h
                dv_c = dv_h if h == 0 else dv_c + dv_h
            dv_acc[rows_of(c_c), :] += jnp.where(valid_c, dv_c, 0.0)
            dk_acc[rows_of(c_d), :] += jnp.where(valid_d, dk_c, 0.0)
            return tuple(out)

        def two_steps(tt, Gs):
            return step(2 * tt + 1, 1, step(2 * tt, 0, Gs))

        z1 = jnp.zeros((1, SB_TQ), F32)
        Gs = lax.fori_loop(0, (i + 4) // 2, two_steps, (z1, z1))

        causal = (lax.broadcasted_iota(jnp.int32, (SB_CH, SB_TQ), 0)
                  < lax.broadcasted_iota(jnp.int32, (SB_CH, SB_TQ), 1))
        rows = rows_of(i)
        kc = k_ref[rows, :]
        vc = v_ref[rows, :]
        kTc = kT_ref[i]
        for h in range(2):
            a, lf = _sb_gates(_dot_nt(kc, qm[h]))
            cum = _dot(tl_ref[...], jnp.where(causal, lf, 0.0).astype(BF16))
            w = jnp.where(causal, jnp.exp2(a + cum[:SB_CH]), 0.0)
            g = w * _dot_nt(vc, dom[h])
            cumg = _dot(tg_ref[...], g.astype(BF16))
            dz = jnp.where(causal, g - jnp.exp2(a) * (Gs[h] + cumg[:SB_CH]), 0.0)
            dzb = dz.astype(BF16)
            dk_h = _dot(dzb, qm[h])
            dv_h = _dot(w.astype(BF16), dom[h])
            dqT_scr[h] += _dot(head_rows(kTc, h), dzb)
            dk_c = dk_h if h == 0 else dk_c + dk_h
            dv_c = dv_h if h == 0 else dv_c + dv_h
        dk_acc[rows, :] += dk_c
        dv_acc[rows, :] += dv_c
        dq_ref[...] = jnp.concatenate([dqT_scr[0], dqT_scr[1]], axis=0).T * Q_SCALE

        @pl.when(i == nq - 1)
        def _():
            dk_acc[...] = dk_acc[...] * LN2
            cols = pl.ds(pl.multiple_of(hp * LANES, LANES), LANES)
            ck = pltpu.make_async_copy(dk_acc, dk_hbm.at[:, cols], sems.at[0])
            cv = pltpu.make_async_copy(dv_acc, dv_hbm.at[:, cols], sems.at[1])
            ck.start()
            cv.start()
            ck.wait()
            cv.wait()

    blk = lambda hp, i: (i, hp)
    return pl.pallas_call(
        body, name=name, grid=(HEAD_PAIRS, nq),
        in_specs=[pl.BlockSpec((SB_TQ, LANES), blk),
                  pl.BlockSpec((SB_TQ, LANES), blk),
                  pl.BlockSpec((2, nch + 8, SB_TQ), lambda hp, i: (hp, 0, i)),
                  pl.BlockSpec((S, LANES), lambda hp, i: (0, HEAD_PAIRS + hp)),
                  pl.BlockSpec((S, LANES), lambda hp, i: (0, 2 * HEAD_PAIRS + hp)),
                  pl.BlockSpec((None, nch, LANES, SB_CH), lambda hp, i: (hp, 0, 0, 0)),
                  pl.BlockSpec((SB_CH + 8, SB_CH), lambda hp, i: (0, 0)),
                  pl.BlockSpec((SB_CH + 8, SB_CH), lambda hp, i: (0, 0))],
        out_specs=[pl.BlockSpec((SB_TQ, LANES), blk), pl.BlockSpec(memory_space=pl.ANY),
                   pl.BlockSpec(memory_space=pl.ANY)],
        out_shape=[jax.ShapeDtypeStruct((S, D_MODEL), F32)] * 3,
        scratch_shapes=[pltpu.VMEM((S, LANES), F32), pltpu.VMEM((S, LANES), F32), pltpu.SemaphoreType.DMA((2,))]
        + [pltpu.VMEM((2, 2, SB_CH, SB_TQ), F32)] * 2
        + [pltpu.VMEM((2, 2, SB_CH, SB_TQ), F32), pltpu.VMEM((2, 2, SB_CH + 8, SB_TQ), F32)]
        + [pltpu.VMEM((2, 2, SB_CH, SB_TQ), F32)] * 2
        + [pltpu.VMEM((2, 2, SB_CH + 8, SB_TQ), F32), pltpu.VMEM((2, HEAD_DIM, SB_TQ), F32)],
        compiler_params=_params(("arbitrary", "arbitrary"), VMEM_BIG),
    )(qkv, do, rsum, qkv, qkv, kT3, tmat_l, tmat_g)


def _dil_valid(n):
    qi = lax.broadcasted_iota(jnp.int32, (DIL_BLK, 2 * DIL_BLK), 0)
    kj = lax.broadcasted_iota(jnp.int32, (DIL_BLK, 2 * DIL_BLK), 1)
    dist = DIL_BLK + qi - kj
    return (dist >= 0) & (dist <= DIL_BLK) & ((n > 0) | (kj >= DIL_BLK))


def _lane_pick(tile, idx):
    lane = lax.broadcasted_iota(jnp.int32, tile.shape, 1)
    return jnp.sum(jnp.where(lane == idx, tile, 0.0), axis=-1, keepdims=True)


def _dil_specs(d, width):
    cur = pl.BlockSpec((DIL_BLK, width), lambda r, n: (n, r))
    prev = pl.BlockSpec((DIL_BLK, width), lambda r, n: (jnp.maximum(n - 1, 0), r))
    return cur, prev


def _dil_fwd(q, k, v, d, name):
    S = q.shape[0]
    L = S // d
    nb = L // DIL_BLK

    def body(q_ref, kc_ref, kp_ref, vc_ref, vp_ref, o_ref, lse_ref):
        valid = _dil_valid(pl.program_id(1))
        lane = lax.broadcasted_iota(jnp.int32, (DIL_BLK, LANES), 1)
        lse_t = jnp.zeros((DIL_BLK, LANES), F32)
        for hp in range(HEAD_PAIRS):
            cols = slice(hp * LANES, (hp + 1) * LANES)
            qm = _head_masks(q_ref[:, cols])
            kk = jnp.concatenate([kp_ref[:, cols], kc_ref[:, cols]], axis=0)
            vm = _head_masks(jnp.concatenate([vp_ref[:, cols], vc_ref[:, cols]], axis=0))
            o2 = None
            for h in range(2):
                s = jnp.where(valid, _dot_nt(qm[h], kk), -1e30)
                m = jnp.max(s, axis=-1, keepdims=True)
                p = jnp.exp(s - m)
                den = jnp.sum(p, axis=-1, keepdims=True)
                oh = _dot(p.astype(BF16), vm[h]) / den
                o2 = oh if o2 is None else o2 + oh
                lse_t = jnp.where(lane == 2 * hp + h, m + jnp.log(den), lse_t)
            o_ref[:, cols] = o2
        lse_ref[...] = lse_t

    cur, prev = _dil_specs(d, D_MODEL)
    lcur, _ = _dil_specs(d, LANES)
    view = lambda a: a.reshape(L, d * a.shape[1])
    o, lse = pl.pallas_call(
        body, name=name, grid=(d, nb),
        in_specs=[cur, cur, prev, cur, prev],
        out_specs=[cur, lcur],
        out_shape=[jax.ShapeDtypeStruct((L, d * D_MODEL), F32), jax.ShapeDtypeStruct((L, d * LANES), F32)],
        compiler_params=_params(("parallel", "parallel")),
    )(view(q), view(k), view(k), view(v), view(v))
    return o.reshape(S, D_MODEL), lse.reshape(S, LANES)


def _head_expand():
    return jnp.asarray((np.arange(LANES)[:, None] == np.arange(D_MODEL)[None, :] // HEAD_DIM).astype(np.float32), BF16)


def _dil_merge(os_, lses, name):
    S = os_[0].shape[0]
    tm = 256
    nbr = len(os_)

    def body(*refs):
        o_refs, l_refs, e_ref = refs[:nbr], refs[nbr:2 * nbr], refs[2 * nbr]
        out_ref, outb_ref, lse_ref = refs[2 * nbr + 1:]
        ls = [r[...] for r in l_refs]
        m = ls[0]
        for l in ls[1:]:
            m = jnp.maximum(m, l)
        tot = jnp.exp(ls[0] - m)
        for l in ls[1:]:
            tot = tot + jnp.exp(l - m)
        lse = m + jnp.log(tot)
        acc = None
        for o_r, l in zip(o_refs, ls):
            wt = _dot3(jnp.exp(l - lse), e_ref[...])
            term = wt * o_r[...]
            acc = term if acc is None else acc + term
        out_ref[...] = acc
        outb_ref[...] = acc.astype(BF16)
        lse_ref[...] = lse

    row = lambda i: (i, 0)
    return pl.pallas_call(
        body, name=name, grid=(S // tm,),
        in_specs=[pl.BlockSpec((tm, D_MODEL), row)] * nbr + [pl.BlockSpec((tm, LANES), row)] * nbr
        + [pl.BlockSpec((LANES, D_MODEL), lambda i: (0, 0))],
        out_specs=[pl.BlockSpec((tm, D_MODEL), row), pl.BlockSpec((tm, D_MODEL), row), pl.BlockSpec((tm, LANES), row)],
        out_shape=[jax.ShapeDtypeStruct((S, D_MODEL), F32), jax.ShapeDtypeStruct((S, D_MODEL), BF16),
                   jax.ShapeDtypeStruct((S, LANES), F32)],
        compiler_params=_params(("parallel",)),
    )(*os_, *lses, _head_expand())


def _dil_bwd(q, k, v, do, lse, dlt, d, name):
    S = q.shape[0]
    L = S // d
    nb = L // DIL_BLK

    def body(q_ref, kc_ref, kp_ref, vc_ref, vp_ref, do_ref, lse_ref, dl_ref,
             dq_ref, dka_ref, dkb_ref, dva_ref, dvb_ref):
        valid = _dil_valid(pl.program_id(1))
        lse_t = lse_ref[...]
        dl_t = dl_ref[...]
        for hp in range(HEAD_PAIRS):
            cols = slice(hp * LANES, (hp + 1) * LANES)
            qm = _head_masks(q_ref[:, cols])
            dom = _head_masks(do_ref[:, cols].astype(BF16))
            kk = jnp.concatenate([kp_ref[:, cols], kc_ref[:, cols]], axis=0)
            vv = jnp.concatenate([vp_ref[:, cols], vc_ref[:, cols]], axis=0)
            km = _head_masks(kk)
            dq2 = dkk = dvv = None
            for h in range(2):
                s = _dot_nt(qm[h], kk)
                p = jnp.where(valid, jnp.exp(s - _lane_pick(lse_t, 2 * hp + h)), 0.0)
                ds = (p * (_dot_nt(dom[h], vv) - _lane_pick(dl_t, 2 * hp + h))).astype(BF16)
                t_q = _dot(ds, km[h])
                t_k = _dot_tn(ds, qm[h])
                t_v = _dot_tn(p.astype(BF16), dom[h])
                dq2 = t_q if dq2 is None else dq2 + t_q
                dkk = t_k if dkk is None else dkk + t_k
                dvv = t_v if dvv is None else dvv + t_v
            dq_ref[:, cols] = dq2
            dkb_ref[:, cols] = dkk[:DIL_BLK]
            dka_ref[:, cols] = dkk[DIL_BLK:]
            dvb_ref[:, cols] = dvv[:DIL_BLK]
            dva_ref[:, cols] = dvv[DIL_BLK:]

    cur, prev = _dil_specs(d, D_MODEL)
    lcur, _ = _dil_specs(d, LANES)
    view = lambda a: a.reshape(L, d * a.shape[1])
    outs = pl.pallas_call(
        body, name=name, grid=(d, nb),
        in_specs=[cur, cur, prev, cur, prev, cur, lcur, lcur],
        out_specs=[cur] * 5,
        out_shape=[jax.ShapeDtypeStruct((L, d * D_MODEL), F32)] * 5,
        compiler_params=_params(("parallel", "parallel"), VMEM_BIG),
    )(view(q), view(k), view(k), view(v), view(v), view(do), view(lse), view(dlt))
    return [o.reshape(S, D_MODEL) for o in outs]


def _dil_combine(parts, rope, name):
    S = parts[0][0].shape[0]
    tm = DIL_BLK
    nblk = S // tm
    dils = [d for _, d in DILATED_BRANCHES]

    def body(*refs):
        ins = refs[:5 * len(dils)]
        c_ref, s1_ref, s2_ref, o_ref = refs[5 * len(dils):]
        i = pl.program_id(0)
        tabs = (c_ref[...], s1_ref[...], s2_ref[...])
        dq = dk = dv = None
        for b, d in enumerate(dils):
            dq_r, dka_r, dkb_r, dva_r, dvb_r = ins[5 * b:5 * b + 5]
            live = (i + d < nblk).astype(F32)
            tq = dq_r[...]
            tk = dka_r[...] + live * dkb_r[...]
            tv = dva_r[...] + live * dvb_r[...]
            dq = tq if dq is None else dq + tq
            dk = tk if dk is None else dk + tk
            dv = tv if dv is None else dv + tv
        dq = dq * Q_SCALE
        for g in range(HEAD_PAIRS):
            cols = slice(g * LANES, (g + 1) * LANES)
            o_ref[:, g * LANES:(g + 1) * LANES] = _rope_apply(dq[:, cols], *tabs, -1.0).astype(BF16)
            o_ref[:, D_MODEL + g * LANES:D_MODEL + (g + 1) * LANES] = _rope_apply(dk[:, cols], *tabs, -1.0).astype(BF16)
        o_ref[:, 2 * D_MODEL:] = dv.astype(BF16)

    row = pl.BlockSpec((tm, D_MODEL), lambda i: (i, 0))
    in_specs = []
    args = []
    for (dq_b, dka, dkb, dva, dvb), d in zip(parts, dils):
        ahead = pl.BlockSpec((tm, D_MODEL), lambda i, d=d: (jnp.minimum(i + d, nblk - 1), 0))
        in_specs += [row, row, ahead, row, ahead]
        args += [dq_b, dka, dkb, dva, dvb]
    in_specs += [pl.BlockSpec((tm, LANES), lambda i: (i, 0))] * 3
    return pl.pallas_call(
        body, name=name, grid=(nblk,),
        in_specs=in_specs,
        out_specs=pl.BlockSpec((tm, 3 * D_MODEL), lambda i: (i, 0)),
        out_shape=jax.ShapeDtypeStruct((S, 3 * D_MODEL), BF16),
        compiler_params=_params(("parallel",), VMEM_BIG),
    )(*args, *rope)


def _mesh_pos():
    return lax.axis_index("x"), lax.axis_index("y"), lax.axis_index("c")


def _all_gather(shard, name):
    R, C = shard.shape

    def body(x_ref, out_ref, send_sems, recv_sems, local_sem):
        x, y, c = _mesh_pos()
        me, sibling = (x, y, c), (x, y, 1 - c)
        chips = [(1 - x, y), (x, 1 - y), (1 - x, 1 - y)]

        def blk(p):
            return out_ref.at[4 * p[0] + 2 * p[1] + p[2]]

        def copy(k, block, to, src=None):
            return pltpu.make_async_remote_copy(
                src_ref=blk(block) if src is None else src, dst_ref=blk(block),
                send_sem=send_sems.at[k], recv_sem=recv_sems.at[k],
                device_id=to, device_id_type=pl.DeviceIdType.MESH)

        mine = pltpu.make_async_copy(x_ref, blk(me), local_sem)
        mine.start()
        first = [copy(0, me, sibling, src=x_ref)]
        first += [copy(1 + j, me, (*chip, c), src=x_ref) for j, chip in enumerate(chips)]
        for cp in first:
            cp.start()
        passed = [copy(4 + j, (*chip, c), sibling) for j, chip in enumerate(chips)]
        for j, chip in enumerate(chips):
            copy(1 + j, (*chip, c), me).wait_recv()
            passed[j].start()
        copy(0, sibling, me).wait_recv()
        for j, chip in enumerate(chips):
            copy(4 + j, (*chip, 1 - c), me).wait_recv()
        for cp in first + passed:
            cp.wait_send()
        mine.wait()

    return pl.pallas_call(
        body, name=name,
        in_specs=[pl.BlockSpec(memory_space=pl.ANY)],
        out_specs=pl.BlockSpec(memory_space=pl.ANY),
        out_shape=jax.ShapeDtypeStruct((N_DEV, R, C), shard.dtype),
        scratch_shapes=[pltpu.SemaphoreType.DMA((7,)), pltpu.SemaphoreType.DMA((7,)), pltpu.SemaphoreType.DMA],
    )(shard)


def _rs_pair(g, name):
    _, R, C = g.shape

    def body(g_ref, out_ref, send_sems, recv_sems):
        x, y, c = _mesh_pos()
        sibling = (x, y, 1 - c)
        cps = []
        for chip in range(4):
            cps.append(pltpu.make_async_remote_copy(
                src_ref=g_ref.at[2 * chip + (1 - c)], dst_ref=out_ref.at[chip],
                send_sem=send_sems.at[chip], recv_sem=recv_sems.at[chip],
                device_id=sibling, device_id_type=pl.DeviceIdType.MESH))
        for cp in cps:
            cp.start()
        for cp in cps:
            cp.wait_recv()
        for cp in cps:
            cp.wait_send()

    return pl.pallas_call(
        body, name=name,
        in_specs=[pl.BlockSpec(memory_space=pl.ANY)],
        out_specs=pl.BlockSpec(memory_space=pl.ANY),
        out_shape=jax.ShapeDtypeStruct((4, R, C), g.dtype),
        scratch_shapes=[pltpu.SemaphoreType.DMA((4,)), pltpu.SemaphoreType.DMA((4,))],
    )(g)


def _pair_add(g, got, cidx, name):
    _, R, C = g.shape
    tr = 256

    def body(c_ref, g_ref, r_ref, o_ref):
        o_ref[...] = g_ref[...] + r_ref[...]

    return pl.pallas_call(
        body, name=name,
        grid_spec=pltpu.PrefetchScalarGridSpec(
            num_scalar_prefetch=1, grid=(4, R // tr),
            in_specs=[pl.BlockSpec((None, tr, C), lambda k, i, c: (2 * k + c[0], i, 0)),
                      pl.BlockSpec((None, tr, C), lambda k, i, c: (k, i, 0))],
            out_specs=pl.BlockSpec((None, tr, C), lambda k, i, c: (k, i, 0))),
        out_shape=jax.ShapeDtypeStruct((4, R, C), g.dtype),
        compiler_params=_params(("parallel", "parallel")),
    )(cidx, g, got)


def _rs_chips(p, name):
    _, R, C = p.shape

    def body(p_ref, out_ref, send_sems, recv_sems):
        x, y, c = _mesh_pos()
        chips = [(1 - x, y), (x, 1 - y), (1 - x, 1 - y)]
        cps = []
        for j, (cx, cy) in enumerate(chips):
            cps.append(pltpu.make_async_remote_copy(
                src_ref=p_ref.at[2 * cx + cy], dst_ref=out_ref.at[j],
                send_sem=send_sems.at[j], recv_sem=recv_sems.at[j],
                device_id=(cx, cy, c), device_id_type=pl.DeviceIdType.MESH))
        for cp in cps:
            cp.start()
        for cp in cps:
            cp.wait_recv()
        for cp in cps:
            cp.wait_send()

    return pl.pallas_call(
        body, name=name,
        in_specs=[pl.BlockSpec(memory_space=pl.ANY)],
        out_specs=pl.BlockSpec(memory_space=pl.ANY),
        out_shape=jax.ShapeDtypeStruct((3, R, C), p.dtype),
        scratch_shapes=[pltpu.SemaphoreType.DMA((3,)), pltpu.SemaphoreType.DMA((3,))],
    )(p)


def _adamw_math(w, g, m, v):
    m2 = ADAM_B1 * m + (1.0 - ADAM_B1) * g
    v2 = ADAM_B2 * v + (1.0 - ADAM_B2) * (g * g)
    m_hat = m2 / (1.0 - ADAM_B1 ** ADAM_STEP)
    v_hat = v2 / (1.0 - ADAM_B2 ** ADAM_STEP)
    delta = -ADAM_LR * (m_hat / (jnp.sqrt(v_hat) + ADAM_EPS) + ADAM_WD * w)
    return delta, m2, v2


def _adamw_shard(p, got, chip_idx, w, m, v, name):
    R, C = w.shape
    tr = 256

    def body(k_ref, p_ref, r_ref, w_ref, m_ref, v_ref, g_out, d_out, m_out, v_out):
        g = ((p_ref[...] + r_ref[0]) + r_ref[1]) + r_ref[2]
        delta, m2, v2 = _adamw_math(w_ref[...], g, m_ref[...], v_ref[...])
        g_out[...] = g
        d_out[...] = delta
        m_out[...] = m2
        v_out[...] = v2

    row = pl.BlockSpec((tr, C), lambda i, k: (i, 0))
    return pl.pallas_call(
        body, name=name,
        grid_spec=pltpu.PrefetchScalarGridSpec(
            num_scalar_prefetch=1, grid=(R // tr,),
            in_specs=[pl.BlockSpec((None, tr, C), lambda i, k: (k[0], i, 0)),
                      pl.BlockSpec((3, tr, C), lambda i, k: (0, i, 0)), row, row, row],
            out_specs=[row] * 4),
        out_shape=[jax.ShapeDtypeStruct((R, C), F32)] * 4,
        compiler_params=_params(("parallel",)),
    )(chip_idx, p, got, w, m, v)


def _adamw_small(gathered, w, m, v, name):
    _, R, C = gathered.shape

    def body(a_ref, w_ref, m_ref, v_ref, g_out, d_out, m_out, v_out):
        g = a_ref[0]
        for k in range(1, N_DEV):
            g = g + a_ref[k]
        delta, m2, v2 = _adamw_math(w_ref[...], g, m_ref[...], v_ref[...])
        g_out[...] = g
        d_out[...] = delta
        m_out[...] = m2
        v_out[...] = v2

    return pl.pallas_call(
        body, name=name, out_shape=[jax.ShapeDtypeStruct((R, C), F32)] * 4,
    )(gathered, w, m, v)


def _rope_tables(S):
    half = ROPE_DIM // 2
    inv_freq = ROPE_THETA ** (-jnp.arange(half, dtype=F32) / half)
    ang = jnp.arange(S, dtype=jnp.int32).astype(F32)[:, None] * inv_freq[None, :]
    cos, sin = jnp.cos(ang), jnp.sin(ang)
    ones = jnp.ones((S, HEAD_DIM - ROPE_DIM), F32)
    zeros = jnp.zeros((S, HEAD_DIM - ROPE_DIM), F32)
    zh = jnp.zeros((S, half), F32)
    c = jnp.concatenate([cos, cos, ones], axis=1)
    s1 = jnp.concatenate([zh, sin, zeros], axis=1)
    s2 = jnp.concatenate([-sin, zh, zeros], axis=1)
    two = lambda t: jnp.concatenate([t, t], axis=1)
    return two(c), two(s1), two(s2)


def _chunk_transposed(a, S):
    return a.reshape(S // SB_CH, SB_CH, HEAD_PAIRS, LANES).transpose(2, 0, 3, 1)


def _flat_shards(ws):
    return jnp.concatenate([w.reshape(-1, D_MODEL) for layer in ws for w in layer], axis=0)


def kernel(x, w_qkv_0, w_o_0, ln1_g_0, ln1_b_0, w_ff1_0, w_ff2_0, ln2_g_0, ln2_b_0, w_qkv_1, w_o_1, ln1_g_1, ln1_b_1, w_ff1_1, w_ff2_1, ln2_g_1, ln2_b_1, loss_target, m_w_qkv_0, m_w_o_0, m_ln1_g_0, m_ln1_b_0, m_w_ff1_0, m_w_ff2_0, m_ln2_g_0, m_ln2_b_0, m_w_qkv_1, m_w_o_1, m_ln1_g_1, m_ln1_b_1, m_w_ff1_1, m_w_ff2_1, m_ln2_g_1, m_ln2_b_1, v_w_qkv_0, v_w_o_0, v_ln1_g_0, v_ln1_b_0, v_w_ff1_0, v_w_ff2_0, v_ln2_g_0, v_ln2_b_0, v_w_qkv_1, v_w_o_1, v_ln1_g_1, v_ln1_b_1, v_w_ff1_1, v_w_ff2_1, v_ln2_g_1, v_ln2_b_1):
    S = x.shape[1]
    x0 = x.reshape(S, D_MODEL)
    target = loss_target.reshape(S, D_MODEL)
    mats = ((w_qkv_0, w_o_0, w_ff1_0, w_ff2_0), (w_qkv_1, w_o_1, w_ff1_1, w_ff2_1))
    mats_m = ((m_w_qkv_0, m_w_o_0, m_w_ff1_0, m_w_ff2_0), (m_w_qkv_1, m_w_o_1, m_w_ff1_1, m_w_ff2_1))
    mats_v = ((v_w_qkv_0, v_w_o_0, v_w_ff1_0, v_w_ff2_0), (v_w_qkv_1, v_w_o_1, v_w_ff1_1, v_w_ff2_1))
    vecs = (ln1_g_0, ln1_b_0, ln2_g_0, ln2_b_0, ln1_g_1, ln1_b_1, ln2_g_1, ln2_b_1)
    vecs_m = (m_ln1_g_0, m_ln1_b_0, m_ln2_g_0, m_ln2_b_0, m_ln1_g_1, m_ln1_b_1, m_ln2_g_1, m_ln2_b_1)
    vecs_v = (v_ln1_g_0, v_ln1_b_0, v_ln2_g_0, v_ln2_b_0, v_ln1_g_1, v_ln1_b_1, v_ln2_g_1, v_ln2_b_1)

    w_flat = _flat_shards(mats)
    w_all = _all_gather(w_flat.astype(BF16), "ag_weights")
    layers = []
    for l in range(N_LAYERS):
        base = l * LAYER_ROWS
        r0, r1, r2, r3 = np.cumsum((0,) + SHARD_ROWS)[:4] + base
        layers.append(dict(
            qkv=w_all[:, r0:r0 + 384].reshape(N_DEV, D_MODEL, 384),
            o=w_all[:, r1:r1 + 128].reshape(D_MODEL, D_MODEL),
            ff1=w_all[:, r2:r2 + 512].reshape(N_DEV, D_MODEL, 512),
            ff2=w_all[:, r3:r3 + 512].reshape(D_FF, D_MODEL),
            g1=vecs[4 * l].reshape(1, D_MODEL), b1=vecs[4 * l + 1].reshape(1, D_MODEL),
            g2=vecs[4 * l + 2].reshape(1, D_MODEL), b2=vecs[4 * l + 3].reshape(1, D_MODEL)))

    rope = _rope_tables(S)
    tmat_later = _sb_tmat(True)
    tmat_upto = _sb_tmat(False)

    saved = []
    xin, xinb = x0, x0.astype(BF16)
    for l, W in enumerate(layers):
        sv = dict(xin=xin, xinb=xinb)
        qkv = _qkv_proj(xinb, W["qkv"], rope if l == 1 else None, Q_SCALE * LOG2E if l == 0 else Q_SCALE,
                        f"qkv_proj_{l}")
        sv["qkv"] = qkv
        if l == 0:
            vT3 = _chunk_transposed(qkv[:, 2 * D_MODEL:], S)
            o, rsum = _sb_fwd(qkv, vT3, tmat_later, "sb_fwd")
            ob = o.astype(BF16)
            sv["rsum"] = rsum
        else:
            q, k, v = qkv[:, :D_MODEL], qkv[:, D_MODEL:2 * D_MODEL], qkv[:, 2 * D_MODEL:]
            outs = [_dil_fwd(q, k, v, d, f"dil_fwd_{d}") for _, d in DILATED_BRANCHES]
            o, ob, lse = _dil_merge([t[0] for t in outs], [t[1] for t in outs], "dil_merge")
            sv.update(q=q, k=k, v=v, lse=lse)
        sv.update(o=o, ob=ob)
        y1, x1, x1b = _mm_res_ln(ob, xin, W["o"], W["g1"], W["b1"], f"attn_out_ln_{l}")
        hpre, h = _ff1(x1b, W["ff1"], f"ff1_{l}")
        y2, x2, x2b = _mm_res_ln(h, x1, W["ff2"], W["g2"], W["b2"], f"ff2_ln_{l}")
        sv.update(y1=y1, x1=x1, x1b=x1b, hpre=hpre, h=h, y2=y2)
        saved.append(sv)
        xin, xinb = x2, x2b

    dout, loss_parts = _loss_grad(xin, target, "loss_grad")
    loss = lax.psum(jnp.sum(loss_parts), MESH_AXES)

    gmats = [None] * N_LAYERS
    gvecs = [None] * (4 * N_LAYERS)
    for l in reversed(range(N_LAYERS)):
        W, sv = layers[l], saved[l]
        dy2, dy2b, gb2 = _ln_bwd(dout, sv["y2"], W["g2"], f"ln2_bwd_{l}")
        dhp = _dh(dy2b, W["ff2"], sv["hpre"], f"dh_{l}")
        g_ff2 = _mm_tn(sv["h"], dy2b, 512, D_MODEL, False, f"dw_ff2_{l}")
        dx1 = _dx_blk(dy2, dhp, W["ff1"], f"dx_ff1_{l}")
        g_ff1 = _mm_tn(sv["x1b"], dhp, D_MODEL, 512, True, f"dw_ff1_{l}")
        dy1, dy1b, gb1 = _ln_bwd(dx1, sv["y1"], W["g1"], f"ln1_bwd_{l}")
        do = _mm_nt_plain(dy1b, W["o"], f"do_{l}")
        g_o = _mm_tn(sv["ob"], dy1b, 512, D_MODEL, False, f"dw_o_{l}")
        if l == 0:
            kT3 = _chunk_transposed(sv["qkv"][:, D_MODEL:2 * D_MODEL], S)
            dq, dk, dv = _sb_bwd(sv["qkv"], kT3, do, sv["rsum"], tmat_later, tmat_upto, "sb_bwd")
            dqkv = jnp.concatenate([dq, dk, dv], axis=1).astype(BF16)
        else:
            dlt = _head_sums(do, sv["o"], "head_sums")
            parts = [_dil_bwd(sv["q"], sv["k"], sv["v"], do, sv["lse"], dlt, d, f"dil_bwd_{d}")
                     for _, d in DILATED_BRANCHES]
            dqkv = _dil_combine(parts, rope, "dil_combine")
        dout = _dx_blk(dy1, dqkv, W["qkv"], f"dx_qkv_{l}")
        g_qkv = _mm_tn(sv["xinb"], dqkv, D_MODEL, 384, True, f"dw_qkv_{l}")
        gmats[l] = (g_qkv.reshape(N_DEV, 384, D_MODEL), g_o.reshape(N_DEV, 128, D_MODEL),
                    g_ff1.reshape(N_DEV, 512, D_MODEL), g_ff2.reshape(N_DEV, 512, D_MODEL))
        gvecs[4 * l:4 * l + 4] = [gb1[0], gb1[1], gb2[0], gb2[1]]
    grad_x = dout.reshape(1, S, D_MODEL)

    cx, cy, cc = _mesh_pos()
    g_all = jnp.concatenate([g for layer in gmats for g in layer], axis=1)
    got_pair = _rs_pair(g_all, "rs_pair")
    chip_part = _pair_add(g_all, got_pair, cc.astype(jnp.int32).reshape(1), "rs_pair_add")
    got_chips = _rs_chips(chip_part, "rs_chips")
    chip_idx = (2 * cx + cy).astype(jnp.int32).reshape(1)
    g_sh, d_sh, m_sh, v_sh = _adamw_shard(chip_part, got_chips, chip_idx, w_flat, _flat_shards(mats_m),
                                          _flat_shards(mats_v), "adamw_mats")

    def unflat(a):
        out, pos = [], 0
        for layer in mats:
            for w in layer:
                n = w.size // D_MODEL
                out.append(a[pos:pos + n].reshape(w.shape))
                pos += n
        return out

    gv_all = _all_gather(jnp.stack(gvecs), "ag_vec_grads")
    g_v, d_v, m_v, v_v = _adamw_small(gv_all, jnp.stack(vecs), jnp.stack(vecs_m), jnp.stack(vecs_v), "adamw_vecs")

    def interleave(mat_list, vec_arr):
        out = []
        for l in range(N_LAYERS):
            qkv_, o_, ff1_, ff2_ = mat_list[4 * l:4 * l + 4]
            out += [qkv_, o_, vec_arr[4 * l], vec_arr[4 * l + 1], ff1_, ff2_, vec_arr[4 * l + 2], vec_arr[4 * l + 3]]
        return out

    return (loss, grad_x, *interleave(unflat(g_sh), g_v), *interleave(unflat(d_sh), d_v),
            *interleave(unflat(m_sh), m_v), *interleave(unflat(v_sh), v_v))
```

```python
import functools
import math

import jax
import jax.numpy as jnp
import numpy as np
from jax import lax
from jax.experimental import pallas as pl
from jax.experimental.pallas import tpu as pltpu

F32 = jnp.float32
BF16 = jnp.bfloat16

D_MODEL = 1024
N_HEADS = 16
HEAD_DIM = 64
D_FF = 4096
N_DEV = 8
N_LAYERS = 2
ROPE_THETA = 500000.0
ROPE_DIM = 16
DILATED_BRANCHES = ((128, 1), (512, 4), (2048, 16))
ALPHA = (2 * N_LAYERS) ** 0.25
LN_EPS = 1e-5
Q_SCALE = 1.0 / math.sqrt(HEAD_DIM)
LOG2E = math.log2(math.e)
LN2 = math.log(2.0)
ADAM_LR, ADAM_B1, ADAM_B2, ADAM_EPS, ADAM_WD, ADAM_STEP = 0.001, 0.9, 0.999, 1e-08, 0.01, 10

LANES = 128
HEAD_PAIRS = D_MODEL // LANES
SB_TQ = 256
SB_CH = 256
SB_SAVE_SLOTS = 4
SB_LOAD_SLOTS = 8
SB_LOAD_AHEAD = SB_LOAD_SLOTS - 3
DIL_BLK = 128
VMEM_BIG = 56 * 2 ** 20
MESH_AXES = ("x", "y", "c")

SHARD_ROWS = (384, 128, 512, 512)
LAYER_ROWS = sum(SHARD_ROWS)
ALL_ROWS = N_LAYERS * LAYER_ROWS


def _params(sem=None, vmem=None):
    kw = {}
    if sem is not None:
        kw["dimension_semantics"] = sem
    if vmem is not None:
        kw["vmem_limit_bytes"] = vmem
    return pltpu.CompilerParams(**kw)


def _dot(a, b):
    return jnp.dot(a, b, preferred_element_type=F32)


def _dot_nt(a, b):
    return lax.dot_general(a, b, (((1,), (1,)), ((), ())), preferred_element_type=F32)


def _dot_tn(a, b):
    return lax.dot_general(a, b, (((0,), (0,)), ((), ())), preferred_element_type=F32)


def _split3(p):
    hi = p.astype(BF16)
    r1 = p - hi.astype(F32)
    mid = r1.astype(BF16)
    lo = (r1 - mid.astype(F32)).astype(BF16)
    return hi, mid, lo


def _dot3(p, e):
    hi, mid, lo = _split3(p)
    return _dot(hi, e) + _dot(mid, e) + _dot(lo, e)


def _rope_apply(a, c, s1, s2, sign):
    return a * c + sign * (pltpu.roll(a, 8, 1) * s1 + pltpu.roll(a, LANES - 8, 1) * s2)


def _qkv_proj(xb, w_blk, rope, q_mult, name):
    S = xb.shape[0]
    tm = 512
    n_rope = 0 if rope is None else 3

    def body(*refs):
        x_ref, w_ref = refs[:2]
        tabs = [r[...] for r in refs[2:2 + n_rope]]
        o_ref = refs[2 + n_rope]
        x = x_ref[...]
        for j in range(N_DEV):
            acc = _dot(x, w_ref[j])
            for g in range(3):
                col = j * 384 + g * LANES
                a = acc[:, g * LANES:(g + 1) * LANES]
                if n_rope and col < 2 * D_MODEL:
                    a = _rope_apply(a, *tabs, 1.0)
                if col < D_MODEL:
                    a = a * q_mult
                o_ref[:, col:col + LANES] = a.astype(BF16)

    tab_specs = [pl.BlockSpec((tm, LANES), lambda i: (i, 0))] * n_rope
    return pl.pallas_call(
        body, name=name, grid=(S // tm,),
        in_specs=[pl.BlockSpec((tm, D_MODEL), lambda i: (i, 0)),
                  pl.BlockSpec((N_DEV, D_MODEL, 384), lambda i: (0, 0, 0))] + tab_specs,
        out_specs=pl.BlockSpec((tm, 3 * D_MODEL), lambda i: (i, 0)),
        out_shape=jax.ShapeDtypeStruct((S, 3 * D_MODEL), BF16),
        compiler_params=_params(("parallel",), VMEM_BIG),
    )(xb, w_blk, *(rope or ()))


def _layer_norm_rows(y, g, b):
    mu = jnp.mean(y, axis=-1, keepdims=True)
    yc = y - mu
    var = jnp.mean(yc * yc, axis=-1, keepdims=True)
    return yc * lax.rsqrt(var + LN_EPS) * g + b


def _mm_res_ln(a, xres, w, g, b, name):
    S, K = a.shape
    tm = 512 if K <= 1024 else 256

    def body(a_ref, x_ref, w_ref, g_ref, b_ref, y_ref, xn_ref, xb_ref):
        y = ALPHA * x_ref[...] + _dot(a_ref[...], w_ref[...])
        xn = _layer_norm_rows(y, g_ref[...], b_ref[...])
        y_ref[...] = y
        xn_ref[...] = xn
        xb_ref[...] = xn.astype(BF16)

    row = lambda i: (i, 0)
    fix = lambda i: (0, 0)
    return pl.pallas_call(
        body, name=name, grid=(S // tm,),
        in_specs=[pl.BlockSpec((tm, K), row), pl.BlockSpec((tm, D_MODEL), row),
                  pl.BlockSpec((K, D_MODEL), fix), pl.BlockSpec((1, D_MODEL), fix),
                  pl.BlockSpec((1, D_MODEL), fix)],
        out_specs=[pl.BlockSpec((tm, D_MODEL), row)] * 3,
        out_shape=[jax.ShapeDtypeStruct((S, D_MODEL), F32), jax.ShapeDtypeStruct((S, D_MODEL), F32),
                   jax.ShapeDtypeStruct((S, D_MODEL), BF16)],
        compiler_params=_params(("parallel",), VMEM_BIG),
    )(a, xres, w, g, b)


def _ff1(xb, w_blk, name):
    S = xb.shape[0]
    tm = 256

    def body(x_ref, w_ref, hp_ref, h_ref):
        x = x_ref[...]
        for j in range(N_DEV):
            acc = _dot(x, w_ref[j])
            r = jnp.maximum(acc, 0.0)
            hp_ref[:, j * 512:(j + 1) * 512] = acc
            h_ref[:, j * 512:(j + 1) * 512] = (r * r).astype(BF16)

    return pl.pallas_call(
        body, name=name, grid=(S // tm,),
        in_specs=[pl.BlockSpec((tm, D_MODEL), lambda i: (i, 0)),
                  pl.BlockSpec((N_DEV, D_MODEL, 512), lambda i: (0, 0, 0))],
        out_specs=[pl.BlockSpec((tm, D_FF), lambda i: (i, 0))] * 2,
        out_shape=[jax.ShapeDtypeStruct((S, D_FF), F32), jax.ShapeDtypeStruct((S, D_FF), BF16)],
        compiler_params=_params(("parallel",), VMEM_BIG),
    )(xb, w_blk)


def _loss_grad(y, target, name):
    S = y.shape[0]
    tm = 512

    def body(y_ref, t_ref, dy_ref, l_ref):
        @pl.when(pl.program_id(0) == 0)
        def _():
            l_ref[...] = jnp.zeros_like(l_ref)

        err = y_ref[...] - t_ref[...]
        dy_ref[...] = err * (1.0 / D_MODEL)
        sq = err * err
        rows = sq[0:8]
        for r in range(1, tm // 8):
            rows = rows + sq[r * 8:(r + 1) * 8]
        acc = rows[:, 0:LANES]
        for g in range(1, D_MODEL // LANES):
            acc = acc + rows[:, g * LANES:(g + 1) * LANES]
        l_ref[...] += acc * (0.5 / D_MODEL)

    return pl.pallas_call(
        body, name=name, grid=(S // tm,),
        in_specs=[pl.BlockSpec((tm, D_MODEL), lambda i: (i, 0))] * 2,
        out_specs=[pl.BlockSpec((tm, D_MODEL), lambda i: (i, 0)), pl.BlockSpec((8, LANES), lambda i: (0, 0))],
        out_shape=[jax.ShapeDtypeStruct((S, D_MODEL), F32), jax.ShapeDtypeStruct((8, LANES), F32)],
        compiler_params=_params(("arbitrary",)),
    )(y, target)


def _ln_bwd(dout, y, g, name):
    S = y.shape[0]
    tm = 512
    steps = S // tm

    def body(d_ref, y_ref, g_ref, dy_ref, dyb_ref, gb_ref, acc_g, acc_b):
        i = pl.program_id(0)

        @pl.when(i == 0)
        def _():
            acc_g[...] = jnp.zeros_like(acc_g)
            acc_b[...] = jnp.zeros_like(acc_b)

        d = d_ref[...]
        yv = y_ref[...]
        mu = jnp.mean(yv, axis=-1, keepdims=True)
        yc = yv - mu
        var = jnp.mean(yc * yc, axis=-1, keepdims=True)
        rstd = lax.rsqrt(var + LN_EPS)
        xhat = yc * rstd
        dxh = d * g_ref[...]
        m1 = jnp.mean(dxh, axis=-1, keepdims=True)
        m2 = jnp.mean(dxh * xhat, axis=-1, keepdims=True)
        dy = rstd * (dxh - m1 - xhat * m2)
        dy_ref[...] = dy
        dyb_ref[...] = dy.astype(BF16)
        pg = d * xhat
        sg = pg[0:8]
        sb = d[0:8]
        for r in range(1, tm // 8):
            sg = sg + pg[r * 8:(r + 1) * 8]
            sb = sb + d[r * 8:(r + 1) * 8]
        acc_g[...] += sg
        acc_b[...] += sb

        @pl.when(i == steps - 1)
        def _():
            gb_ref[0:1, :] = jnp.sum(acc_g[...], axis=0, keepdims=True)
            gb_ref[1:2, :] = jnp.sum(acc_b[...], axis=0, keepdims=True)

    row = lambda i: (i, 0)
    fix = lambda i: (0, 0)
    return pl.pallas_call(
        body, name=name, grid=(steps,),
        in_specs=[pl.BlockSpec((tm, D_MODEL), row), pl.BlockSpec((tm, D_MODEL), row), pl.BlockSpec((1, D_MODEL), fix)],
        out_specs=[pl.BlockSpec((tm, D_MODEL), row), pl.BlockSpec((tm, D_MODEL), row), pl.BlockSpec((2, D_MODEL), fix)],
        out_shape=[jax.ShapeDtypeStruct((S, D_MODEL), F32), jax.ShapeDtypeStruct((S, D_MODEL), BF16),
                   jax.ShapeDtypeStruct((2, D_MODEL), F32)],
        scratch_shapes=[pltpu.VMEM((8, D_MODEL), F32), pltpu.VMEM((8, D_MODEL), F32)],
        compiler_params=_params(("arbitrary",)),
    )(dout, y, g)


def _dh(dyb, w2, hpre, name):
    S = dyb.shape[0]
    tm = 256
    tn = 512

    def body(dy_ref, w_ref, hp_ref, o_ref):
        dy = dy_ref[...]
        for n in range(0, D_FF, tn):
            dh = _dot_nt(dy, w_ref[n:n + tn, :])
            o_ref[:, n:n + tn] = (dh * (2.0 * jnp.maximum(hp_ref[:, n:n + tn], 0.0))).astype(BF16)

    return pl.pallas_call(
        body, name=name, grid=(S // tm,),
        in_specs=[pl.BlockSpec((tm, D_MODEL), lambda i: (i, 0)), pl.BlockSpec((D_FF, D_MODEL), lambda i: (0, 0)),
                  pl.BlockSpec((tm, D_FF), lambda i: (i, 0))],
        out_specs=pl.BlockSpec((tm, D_FF), lambda i: (i, 0)),
        out_shape=jax.ShapeDtypeStruct((S, D_FF), BF16),
        compiler_params=_params(("parallel",), VMEM_BIG),
    )(dyb, w2, hpre)


def _dx_blk(dres, dz, w_blk, name):
    S, N = dz.shape
    bw = w_blk.shape[2]
    tm = 256

    def body(r_ref, z_ref, w_ref, o_ref):
        acc = ALPHA * r_ref[...]
        for j in range(N_DEV):
            acc = acc + _dot_nt(z_ref[:, j * bw:(j + 1) * bw], w_ref[j])
        o_ref[...] = acc

    return pl.pallas_call(
        body, name=name, grid=(S // tm,),
        in_specs=[pl.BlockSpec((tm, D_MODEL), lambda i: (i, 0)), pl.BlockSpec((tm, N), lambda i: (i, 0)),
                  pl.BlockSpec((N_DEV, D_MODEL, bw), lambda i: (0, 0, 0))],
        out_specs=pl.BlockSpec((tm, D_MODEL), lambda i: (i, 0)),
        out_shape=jax.ShapeDtypeStruct((S, D_MODEL), F32),
        compiler_params=_params(("parallel",), VMEM_BIG),
    )(dres, dz, w_blk)


def _mm_nt_plain(a, w, name):
    S = a.shape[0]
    tm = 512

    def body(a_ref, w_ref, o_ref):
        o_ref[...] = _dot_nt(a_ref[...], w_ref[...])

    return pl.pallas_call(
        body, name=name, grid=(S // tm,),
        in_specs=[pl.BlockSpec((tm, D_MODEL), lambda i: (i, 0)), pl.BlockSpec((D_MODEL, D_MODEL), lambda i: (0, 0))],
        out_specs=pl.BlockSpec((tm, D_MODEL), lambda i: (i, 0)),
        out_shape=jax.ShapeDtypeStruct((S, D_MODEL), F32),
        compiler_params=_params(("parallel",)),
    )(a, w)


def _mm_tn(a, b, ta, tb, blocked, name):
    S, Ka = a.shape
    Nb = b.shape[1]
    ts = 512

    def body(a_ref, b_ref, o_ref):
        @pl.when(pl.program_id(2) == 0)
        def _():
            o_ref[...] = jnp.zeros_like(o_ref)

        o_ref[...] += _dot_tn(a_ref[...], b_ref[...])

    if blocked:
        out_spec = pl.BlockSpec((None, ta, tb), lambda i, j, s: (j, i, 0))
        out_shape = jax.ShapeDtypeStruct((Nb // tb, Ka, tb), F32)
    else:
        out_spec = pl.BlockSpec((ta, tb), lambda i, j, s: (i, j))
        out_shape = jax.ShapeDtypeStruct((Ka, Nb), F32)
    return pl.pallas_call(
        body, name=name, grid=(Ka // ta, Nb // tb, S // ts),
        in_specs=[pl.BlockSpec((ts, ta), lambda i, j, s: (s, i)), pl.BlockSpec((ts, tb), lambda i, j, s: (s, j))],
        out_specs=out_spec, out_shape=out_shape,
        compiler_params=_params(("parallel", "parallel", "arbitrary"), VMEM_BIG),
    )(a, b)


def _head_sums(do, o, name):
    S = do.shape[0]
    tm = 512
    sel = (np.arange(D_MODEL)[:, None] // HEAD_DIM == np.arange(LANES)[None, :]).astype(np.float32)

    def body(d_ref, o_ref, e_ref, out_ref):
        out_ref[...] = _dot3(d_ref[...] * o_ref[...], e_ref[...])

    return pl.pallas_call(
        body, name=name, grid=(S // tm,),
        in_specs=[pl.BlockSpec((tm, D_MODEL), lambda i: (i, 0))] * 2 + [pl.BlockSpec((D_MODEL, LANES), lambda i: (0, 0))],
        out_specs=pl.BlockSpec((tm, LANES), lambda i: (i, 0)),
        out_shape=jax.ShapeDtypeStruct((S, LANES), F32),
        compiler_params=_params(("parallel",)),
    )(do, o, jnp.asarray(sel, BF16))


def _sb_tmat(later):
    r = np.arange(SB_CH)
    t = (r[None, :] > r[:, None]) if later else (r[None, :] <= r[:, None])
    return jnp.asarray(np.concatenate([t.astype(np.float32), np.ones((8, SB_CH), np.float32)], axis=0), BF16)


def _sb_gates(z2):
    neg_abs = lax.bitcast_convert_type(lax.bitcast_convert_type(z2, jnp.uint32) | jnp.uint32(0x80000000), F32)
    l1 = jnp.log2(1.0 + jnp.exp2(neg_abs))
    a = jnp.minimum(z2, 0.0) - l1
    return a, a - z2


def _head_masks(x2):
    lane = lax.broadcasted_iota(jnp.int32, x2.shape, 1)
    zero = jnp.zeros_like(x2)
    return jnp.where(lane < HEAD_DIM, x2, zero), jnp.where(lane >= HEAD_DIM, x2, zero)


def _sb_fwd(qkv, vT3, tmat, name):
    S = qkv.shape[0]
    nq = S // SB_TQ
    nch = S // SB_CH
    ns = SB_SAVE_SLOTS

    def body(q_ref, k_ref, vT_ref, t_ref, o_ref, ws_hbm, z_scr, a_scr, cum_scr, oT_scr, stage, stage_d, sems):
        hp = pl.program_id(0)
        i = pl.program_id(1)
        base = (i * (i + 1)) // 2
        qm = _head_masks(q_ref[...])

        def save(src, sem, c):
            return pltpu.make_async_copy(src, ws_hbm.at[hp, base + c], sem)

        def c_valid(t):
            return jnp.logical_and(t >= 2, t - 2 < i)
        causal = (lax.broadcasted_iota(jnp.int32, (SB_CH, SB_TQ), 0)
                  < lax.broadcasted_iota(jnp.int32, (SB_CH, SB_TQ), 1))

        def head_rows(vTc, h):
            return vTc[h * HEAD_DIM:(h + 1) * HEAD_DIM, :]

        @pl.when(jnp.logical_and(hp == 0, i == 0))
        def _():
            z_scr[...] = jnp.zeros_like(z_scr)
            a_scr[...] = jnp.zeros_like(a_scr)
            cum_scr[...] = jnp.zeros_like(cum_scr)

        kc = k_ref[pl.ds(pl.multiple_of(i * SB_CH, SB_CH), SB_CH), :]
        vTc = vT_ref[i]
        R0 = []
        for h in range(2):
            a, lf = _sb_gates(_dot_nt(kc, qm[h]))
            cum = _dot(t_ref[...], jnp.where(causal, lf, 0.0).astype(BF16))
            wb = jnp.where(causal, jnp.exp2(a + cum[:SB_CH]), 0.0).astype(BF16)
            stage_d[2 * h] = wb
            stage_d[2 * h + 1] = a.astype(BF16)
            oT_scr[h] = _dot(head_rows(vTc, h), wb)
            R0.append(cum[SB_CH:SB_CH + 1])
        save(stage_d, sems.at[ns], i).start()

        def c_chunk(t):
            return jnp.clip(i + 1 - t, 0, nch - 1)

        def step(t, p, R):
            slot = lax.rem(t, ns)
            cA = jnp.maximum(i - 1 - t, 0)
            kA = k_ref[pl.ds(pl.multiple_of(cA * SB_CH, SB_CH), SB_CH), :]
            valid = c_valid(t)
            vC = vT_ref[c_chunk(t)]
            out = []
            for h in range(2):
                z_scr[p, h] = _dot_nt(kA, qm[h])
                a, lf = _sb_gates(z_scr[1 - p, h])
                a_scr[1 - p, h] = a
                cum_scr[1 - p, h] = _dot(t_ref[...], lf.astype(BF16))
                a_c = a_scr[p, h]
                wb = jnp.exp2(a_c + cum_scr[p, h, :SB_CH, :] + R[h]).astype(BF16)
                stage[slot, 2 * h] = wb
                stage[slot, 2 * h + 1] = a_c.astype(BF16)
                oT_scr[h] += jnp.where(valid, _dot(head_rows(vC, h), wb), 0.0)
                out.append(R[h] + jnp.where(valid, cum_scr[p, h, SB_CH:SB_CH + 1, :], 0.0))
            return tuple(out)

        def two_steps(tt, R):
            for p in range(2):
                t = 2 * tt + p

                @pl.when(c_valid(t - ns))
                def _():
                    slot = lax.rem(t, ns)
                    save(stage.at[slot], sems.at[slot], 0).wait()

            R = step(2 * tt + 1, 1, step(2 * tt, 0, R))
            for p in range(2):
                t = 2 * tt + p

                @pl.when(c_valid(t))
                def _():
                    slot = lax.rem(t, ns)
                    save(stage.at[slot], sems.at[slot], c_chunk(t)).start()

            return R

        trips = (i + 3) // 2
        lax.fori_loop(0, trips, two_steps, tuple(R0))
        for back in range(1, ns + 1):
            t_last = 2 * trips - back

            @pl.when(c_valid(t_last))
            def _():
                slot = lax.rem(t_last, ns)
                save(stage.at[slot], sems.at[slot], 0).wait()

        save(stage_d, sems.at[ns], i).wait()
        o_ref[...] = jnp.concatenate([oT_scr[0], oT_scr[1]], axis=0).T

    ntile = nq * (nq + 1) // 2
    return pl.pallas_call(
        body, name=name, grid=(HEAD_PAIRS, nq),
        in_specs=[pl.BlockSpec((SB_TQ, LANES), lambda hp, i: (i, hp)),
                  pl.BlockSpec((S, LANES), lambda hp, i: (0, HEAD_PAIRS + hp)),
                  pl.BlockSpec((None, nch, LANES, SB_CH), lambda hp, i: (hp, 0, 0, 0)),
                  pl.BlockSpec((SB_CH + 8, SB_CH), lambda hp, i: (0, 0))],
        out_specs=[pl.BlockSpec((SB_TQ, LANES), lambda hp, i: (i, hp)), pl.BlockSpec(memory_space=pl.ANY)],
        out_shape=[jax.ShapeDtypeStruct((S, D_MODEL), F32),
                   jax.ShapeDtypeStruct((HEAD_PAIRS, ntile, 4, SB_CH, SB_TQ), BF16)],
        scratch_shapes=[pltpu.VMEM((2, 2, SB_CH, SB_TQ), F32), pltpu.VMEM((2, 2, SB_CH, SB_TQ), F32),
                        pltpu.VMEM((2, 2, SB_CH + 8, SB_TQ), F32), pltpu.VMEM((2, HEAD_DIM, SB_TQ), F32),
                        pltpu.VMEM((ns, 4, SB_CH, SB_TQ), BF16), pltpu.VMEM((4, SB_CH, SB_TQ), BF16),
                        pltpu.SemaphoreType.DMA((ns + 1,))],
        compiler_params=_params(("arbitrary", "arbitrary"), VMEM_BIG),
    )(qkv, qkv, vT3, tmat)


def _sb_bwd(qkv, kT3, do, ws, tmat_g, name):
    S = qkv.shape[0]
    nq = S // SB_TQ
    nch = S // SB_CH
    nl = SB_LOAD_SLOTS
    ahead = SB_LOAD_AHEAD

    def body(q_ref, do_ref, v_ref, kT_ref, tg_ref, ws_hbm, dq_ref, dk_hbm, dv_hbm, dk_acc, dv_acc, sems,
             dwv_scr, g_scr, sig_scr, cumg_scr, dqT_scr, ring, ring_sems):
        hp = pl.program_id(0)
        i = pl.program_id(1)
        base = (i * (i + 1)) // 2

        @pl.when(i == 0)
        def _():
            dk_acc[...] = jnp.zeros_like(dk_acc)
            dv_acc[...] = jnp.zeros_like(dv_acc)

        @pl.when(jnp.logical_and(hp == 0, i == 0))
        def _():
            for scr in (dwv_scr, g_scr, sig_scr, cumg_scr, ring):
                scr[...] = jnp.zeros_like(scr)

        def load(u):
            slot = lax.rem(u, nl)
            return pltpu.make_async_copy(ws_hbm.at[hp, base + u], ring.at[slot], ring_sems.at[slot])

        for u in range(ahead):
            @pl.when(u <= i)
            def _():
                load(u).start()

        dqT_scr[...] = jnp.zeros_like(dqT_scr)
        qm = _head_masks(q_ref[...])
        dom = _head_masks(do_ref[...].astype(BF16))
        last = jnp.maximum(i - 1, 0)

        def rows_of(c):
            return pl.ds(pl.multiple_of(c * SB_CH, SB_CH), SB_CH)

        def head_rows(kTc, h):
            return kTc[h * HEAD_DIM:(h + 1) * HEAD_DIM, :]

        def step(t, p, Gs):
            q = 1 - p
            valid_b = jnp.logical_and(t >= 1, t - 1 < i)
            valid_c = jnp.logical_and(t >= 2, t - 2 < i)
            c_b = jnp.clip(t - 1, 0, last)
            c_c = jnp.clip(t - 2, 0, last)
            slot = jnp.where(valid_b, lax.rem(jnp.maximum(t - 1, 0), nl), nl)
            vA = v_ref[rows_of(jnp.minimum(t, last)), :]
            kTc = kT_ref[c_c]
            out = []
            for h in range(2):
                dwv_scr[p, h] = _dot_nt(vA, dom[h])

                wb = ring[slot, 2 * h]
                g = wb.astype(F32) * dwv_scr[q, h]
                g_scr[q, h] = g
                sig_scr[q, h] = jnp.exp2(ring[slot, 2 * h + 1].astype(F32))
                cumg_scr[q, h] = _dot(tg_ref[...], g.astype(BF16))
                dv_h = _dot(wb, dom[h])

                dzb = (g_scr[p, h] - sig_scr[p, h] * (Gs[h] + cumg_scr[p, h, :SB_CH, :])).astype(BF16)
                dk_h = _dot(dzb, qm[h])
                dqT_scr[h] += jnp.where(valid_c, _dot(head_rows(kTc, h), dzb), 0.0)
                out.append(Gs[h] + jnp.where(valid_c, cumg_scr[p, h, SB_CH:SB_CH + 1, :], 0.0))
                dk_c = dk_h if h == 0 else dk_c + dk_h
                dv_c = dv_h if h == 0 else dv_c + dv_h
            dv_acc[rows_of(c_b), :] += jnp.where(valid_b, dv_c, 0.0)
            dk_acc[rows_of(c_c), :] += jnp.where(valid_c, dk_c, 0.0)
            return tuple(out)

        def two_steps(tt, Gs):
            for p in range(2):
                t = 2 * tt + p

                @pl.when(jnp.logical_and(t >= 1, t - 1 < i))
                def _():
                    load(t - 1).wait()

            for p in range(2):
                t = 2 * tt + p

                @pl.when(t + ahead <= i)
                def _():
                    load(t + ahead).start()

            return step(2 * tt + 1, 1, step(2 * tt, 0, Gs))

        z1 = jnp.zeros((1, SB_TQ), F32)
        Gs = lax.fori_loop(0, (i + 3) // 2, two_steps, (z1, z1))

        causal = (lax.broadcasted_iota(jnp.int32, (SB_CH, SB_TQ), 0)
                  < lax.broadcasted_iota(jnp.int32, (SB_CH, SB_TQ), 1))
        rows = rows_of(i)
        vc = v_ref[rows, :]
        kTc = kT_ref[i]
        load(i).wait()
        slot = lax.rem(i, nl)
        for h in range(2):
            wb = ring[slot, 2 * h]
            g = wb.astype(F32) * _dot_nt(vc, dom[h])
            cumg = _dot(tg_ref[...], g.astype(BF16))
            sig = jnp.exp2(ring[slot, 2 * h + 1].astype(F32))
            dz = jnp.where(causal, g - sig * (Gs[h] + cumg[:SB_CH]), 0.0)
            dzb = dz.astype(BF16)
            dk_h = _dot(dzb, qm[h])
            dv_h = _dot(wb, dom[h])
            dqT_scr[h] += _dot(head_rows(kTc, h), dzb)
            dk_c = dk_h if h == 0 else dk_c + dk_h
            dv_c = dv_h if h == 0 else dv_c + dv_h
        dk_acc[rows, :] += dk_c
        dv_acc[rows, :] += dv_c
        dq_ref[...] = jnp.concatenate([dqT_scr[0], dqT_scr[1]], axis=0).T * Q_SCALE

        @pl.when(i == nq - 1)
        def _():
            dk_acc[...] = dk_acc[...] * LN2
            cols = pl.ds(pl.multiple_of(hp * LANES, LANES), LANES)
            ck = pltpu.make_async_copy(dk_acc, dk_hbm.at[:, cols], sems.at[0])
            cv = pltpu.make_async_copy(dv_acc, dv_hbm.at[:, cols], sems.at[1])
            ck.start()
            cv.start()
            ck.wait()
            cv.wait()

    blk = lambda hp, i: (i, hp)
    return pl.pallas_call(
        body, name=name, grid=(HEAD_PAIRS, nq),
        in_specs=[pl.BlockSpec((SB_TQ, LANES), blk),
                  pl.BlockSpec((SB_TQ, LANES), blk),
                  pl.BlockSpec((S, LANES), lambda hp, i: (0, 2 * HEAD_PAIRS + hp)),
                  pl.BlockSpec((None, nch, LANES, SB_CH), lambda hp, i: (hp, 0, 0, 0)),
                  pl.BlockSpec((SB_CH + 8, SB_CH), lambda hp, i: (0, 0)),
                  pl.BlockSpec(memory_space=pl.ANY)],
        out_specs=[pl.BlockSpec((SB_TQ, LANES), blk), pl.BlockSpec(memory_space=pl.ANY),
                   pl.BlockSpec(memory_space=pl.ANY)],
        out_shape=[jax.ShapeDtypeStruct((S, D_MODEL), F32)] * 3,
        scratch_shapes=[pltpu.VMEM((S, LANES), F32), pltpu.VMEM((S, LANES), F32), pltpu.SemaphoreType.DMA((2,))]
        + [pltpu.VMEM((2, 2, SB_CH, SB_TQ), F32)] * 3
        + [pltpu.VMEM((2, 2, SB_CH + 8, SB_TQ), F32), pltpu.VMEM((2, HEAD_DIM, SB_TQ), F32)]
        + [pltpu.VMEM((nl + 1, 4, SB_CH, SB_TQ), BF16), pltpu.SemaphoreType.DMA((nl,))],
        compiler_params=_params(("arbitrary", "arbitrary"), VMEM_BIG),
    )(qkv, do, qkv, kT3, tmat_g, ws)


def _dil_valid(n):
    qi = lax.broadcasted_iota(jnp.int32, (DIL_BLK, 2 * DIL_BLK), 0)
    kj = lax.broadcasted_iota(jnp.int32, (DIL_BLK, 2 * DIL_BLK), 1)
    dist = DIL_BLK + qi - kj
    return (dist >= 0) & (dist <= DIL_BLK) & ((n > 0) | (kj >= DIL_BLK))


def _lane_pick(tile, idx):
    lane = lax.broadcasted_iota(jnp.int32, tile.shape, 1)
    return jnp.sum(jnp.where(lane == idx, tile, 0.0), axis=-1, keepdims=True)


def _dil_specs(d, width):
    cur = pl.BlockSpec((DIL_BLK, width), lambda r, n: (n, r))
    prev = pl.BlockSpec((DIL_BLK, width), lambda r, n: (jnp.maximum(n - 1, 0), r))
    return cur, prev


def _dil_fwd(q, k, v, d, name):
    S = q.shape[0]
    L = S // d
    nb = L // DIL_BLK

    def body(q_ref, kc_ref, kp_ref, vc_ref, vp_ref, o_ref, lse_ref):
        valid = _dil_valid(pl.program_id(1))
        lane = lax.broadcasted_iota(jnp.int32, (DIL_BLK, LANES), 1)
        lse_t = jnp.zeros((DIL_BLK, LANES), F32)
        for hp in range(HEAD_PAIRS):
            cols = slice(hp * LANES, (hp + 1) * LANES)
            qm = _head_masks(q_ref[:, cols])
            kk = jnp.concatenate([kp_ref[:, cols], kc_ref[:, cols]], axis=0)
            vm = _head_masks(jnp.concatenate([vp_ref[:, cols], vc_ref[:, cols]], axis=0))
            o2 = None
            for h in range(2):
                s = jnp.where(valid, _dot_nt(qm[h], kk), -1e30)
                m = jnp.max(s, axis=-1, keepdims=True)
                p = jnp.exp(s - m)
                den = jnp.sum(p, axis=-1, keepdims=True)
                oh = _dot(p.astype(BF16), vm[h]) / den
                o2 = oh if o2 is None else o2 + oh
                lse_t = jnp.where(lane == 2 * hp + h, m + jnp.log(den), lse_t)
            o_ref[:, cols] = o2
        lse_ref[...] = lse_t

    cur, prev = _dil_specs(d, D_MODEL)
    lcur, _ = _dil_specs(d, LANES)
    view = lambda a: a.reshape(L, d * a.shape[1])
    o, lse = pl.pallas_call(
        body, name=name, grid=(d, nb),
        in_specs=[cur, cur, prev, cur, prev],
        out_specs=[cur, lcur],
        out_shape=[jax.ShapeDtypeStruct((L, d * D_MODEL), F32), jax.ShapeDtypeStruct((L, d * LANES), F32)],
        compiler_params=_params(("parallel", "parallel")),
    )(view(q), view(k), view(k), view(v), view(v))
    return o.reshape(S, D_MODEL), lse.reshape(S, LANES)


def _head_expand():
    return jnp.asarray((np.arange(LANES)[:, None] == np.arange(D_MODEL)[None, :] // HEAD_DIM).astype(np.float32), BF16)


def _dil_merge(os_, lses, name):
    S = os_[0].shape[0]
    tm = 256
    nbr = len(os_)

    def body(*refs):
        o_refs, l_refs, e_ref = refs[:nbr], refs[nbr:2 * nbr], refs[2 * nbr]
        out_ref, outb_ref, lse_ref = refs[2 * nbr + 1:]
        ls = [r[...] for r in l_refs]
        m = ls[0]
        for l in ls[1:]:
            m = jnp.maximum(m, l)
        tot = jnp.exp(ls[0] - m)
        for l in ls[1:]:
            tot = tot + jnp.exp(l - m)
        lse = m + jnp.log(tot)
        acc = None
        for o_r, l in zip(o_refs, ls):
            wt = _dot3(jnp.exp(l - lse), e_ref[...])
            term = wt * o_r[...]
            acc = term if acc is None else acc + term
        out_ref[...] = acc
        outb_ref[...] = acc.astype(BF16)
        lse_ref[...] = lse

    row = lambda i: (i, 0)
    return pl.pallas_call(
        body, name=name, grid=(S // tm,),
        in_specs=[pl.BlockSpec((tm, D_MODEL), row)] * nbr + [pl.BlockSpec((tm, LANES), row)] * nbr
        + [pl.BlockSpec((LANES, D_MODEL), lambda i: (0, 0))],
        out_specs=[pl.BlockSpec((tm, D_MODEL), row), pl.BlockSpec((tm, D_MODEL), row), pl.BlockSpec((tm, LANES), row)],
        out_shape=[jax.ShapeDtypeStruct((S, D_MODEL), F32), jax.ShapeDtypeStruct((S, D_MODEL), BF16),
                   jax.ShapeDtypeStruct((S, LANES), F32)],
        compiler_params=_params(("parallel",)),
    )(*os_, *lses, _head_expand())


def _dil_bwd(q, k, v, do, lse, dlt, d, name):
    S = q.shape[0]
    L = S // d
    nb = L // DIL_BLK

    def body(q_ref, kc_ref, kp_ref, vc_ref, vp_ref, do_ref, lse_ref, dl_ref,
             dq_ref, dka_ref, dkb_ref, dva_ref, dvb_ref):
        valid = _dil_valid(pl.program_id(1))
        lse_t = lse_ref[...]
        dl_t = dl_ref[...]
        for hp in range(HEAD_PAIRS):
            cols = slice(hp * LANES, (hp + 1) * LANES)
            qm = _head_masks(q_ref[:, cols])
            dom = _head_masks(do_ref[:, cols].astype(BF16))
            kk = jnp.concatenate([kp_ref[:, cols], kc_ref[:, cols]], axis=0)
            vv = jnp.concatenate([vp_ref[:, cols], vc_ref[:, cols]], axis=0)
            km = _head_masks(kk)
            dq2 = dkk = dvv = None
            for h in range(2):
                s = _dot_nt(qm[h], kk)
                p = jnp.where(valid, jnp.exp(s - _lane_pick(lse_t, 2 * hp + h)), 0.0)
                ds = (p * (_dot_nt(dom[h], vv) - _lane_pick(dl_t, 2 * hp + h))).astype(BF16)
                t_q = _dot(ds, km[h])
                t_k = _dot_tn(ds, qm[h])
                t_v = _dot_tn(p.astype(BF16), dom[h])
                dq2 = t_q if dq2 is None else dq2 + t_q
                dkk = t_k if dkk is None else dkk + t_k
                dvv = t_v if dvv is None else dvv + t_v
            dq_ref[:, cols] = dq2
            dkb_ref[:, cols] = dkk[:DIL_BLK]
            dka_ref[:, cols] = dkk[DIL_BLK:]
            dvb_ref[:, cols] = dvv[:DIL_BLK]
            dva_ref[:, cols] = dvv[DIL_BLK:]

    cur, prev = _dil_specs(d, D_MODEL)
    lcur, _ = _dil_specs(d, LANES)
    view = lambda a: a.reshape(L, d * a.shape[1])
    outs = pl.pallas_call(
        body, name=name, grid=(d, nb),
        in_specs=[cur, cur, prev, cur, prev, cur, lcur, lcur],
        out_specs=[cur] * 5,
        out_shape=[jax.ShapeDtypeStruct((L, d * D_MODEL), F32)] * 5,
        compiler_params=_params(("parallel", "parallel"), VMEM_BIG),
    )(view(q), view(k), view(k), view(v), view(v), view(do), view(lse), view(dlt))
    return [o.reshape(S, D_MODEL) for o in outs]


def _dil_combine(parts, rope, name):
    S = parts[0][0].shape[0]
    tm = DIL_BLK
    nblk = S // tm
    dils = [d for _, d in DILATED_BRANCHES]

    def body(*refs):
        ins = refs[:5 * len(dils)]
        c_ref, s1_ref, s2_ref, o_ref = refs[5 * len(dils):]
        i = pl.program_id(0)
        tabs = (c_ref[...], s1_ref[...], s2_ref[...])
        dq = dk = dv = None
        for b, d in enumerate(dils):
            dq_r, dka_r, dkb_r, dva_r, dvb_r = ins[5 * b:5 * b + 5]
            live = (i + d < nblk).astype(F32)
            tq = dq_r[...]
            tk = dka_r[...] + live * dkb_r[...]
            tv = dva_r[...] + live * dvb_r[...]
            dq = tq if dq is None else dq + tq
            dk = tk if dk is None else dk + tk
            dv = tv if dv is None else dv + tv
        dq = dq * Q_SCALE
        for g in range(HEAD_PAIRS):
            cols = slice(g * LANES, (g + 1) * LANES)
            o_ref[:, g * LANES:(g + 1) * LANES] = _rope_apply(dq[:, cols], *tabs, -1.0).astype(BF16)
            o_ref[:, D_MODEL + g * LANES:D_MODEL + (g + 1) * LANES] = _rope_apply(dk[:, cols], *tabs, -1.0).astype(BF16)
        o_ref[:, 2 * D_MODEL:] = dv.astype(BF16)

    row = pl.BlockSpec((tm, D_MODEL), lambda i: (i, 0))
    in_specs = []
    args = []
    for (dq_b, dka, dkb, dva, dvb), d in zip(parts, dils):
        ahead = pl.BlockSpec((tm, D_MODEL), lambda i, d=d: (jnp.minimum(i + d, nblk - 1), 0))
        in_specs += [row, row, ahead, row, ahead]
        args += [dq_b, dka, dkb, dva, dvb]
    in_specs += [pl.BlockSpec((tm, LANES), lambda i: (i, 0))] * 3
    return pl.pallas_call(
        body, name=name, grid=(nblk,),
        in_specs=in_specs,
        out_specs=pl.BlockSpec((tm, 3 * D_MODEL), lambda i: (i, 0)),
        out_shape=jax.ShapeDtypeStruct((S, 3 * D_MODEL), BF16),
        compiler_params=_params(("parallel",), VMEM_BIG),
    )(*args, *rope)


def _mesh_pos():
    return lax.axis_index("x"), lax.axis_index("y"), lax.axis_index("c")


def _all_gather(shard, name):
    R, C = shard.shape

    def body(x_ref, out_ref, send_sems, recv_sems, local_sem):
        x, y, c = _mesh_pos()
        me, sibling = (x, y, c), (x, y, 1 - c)
        chips = [(1 - x, y), (x, 1 - y), (1 - x, 1 - y)]

        def blk(p):
            return out_ref.at[4 * p[0] + 2 * p[1] + p[2]]

        def copy(k, block, to, src=None):
            return pltpu.make_async_remote_copy(
                src_ref=blk(block) if src is None else src, dst_ref=blk(block),
                send_sem=send_sems.at[k], recv_sem=recv_sems.at[k],
                device_id=to, device_id_type=pl.DeviceIdType.MESH)

        mine = pltpu.make_async_copy(x_ref, blk(me), local_sem)
        mine.start()
        first = [copy(0, me, sibling, src=x_ref)]
        first += [copy(1 + j, me, (*chip, c), src=x_ref) for j, chip in enumerate(chips)]
        for cp in first:
            cp.start()
        passed = [copy(4 + j, (*chip, c), sibling) for j, chip in enumerate(chips)]
        for j, chip in enumerate(chips):
            copy(1 + j, (*chip, c), me).wait_recv()
            passed[j].start()
        copy(0, sibling, me).wait_recv()
        for j, chip in enumerate(chips):
            copy(4 + j, (*chip, 1 - c), me).wait_recv()
        for cp in first + passed:
            cp.wait_send()
        mine.wait()

    return pl.pallas_call(
        body, name=name,
        in_specs=[pl.BlockSpec(memory_space=pl.ANY)],
        out_specs=pl.BlockSpec(memory_space=pl.ANY),
        out_shape=jax.ShapeDtypeStruct((N_DEV, R, C), shard.dtype),
        scratch_shapes=[pltpu.SemaphoreType.DMA((7,)), pltpu.SemaphoreType.DMA((7,)), pltpu.SemaphoreType.DMA],
    )(shard)


def _rs_pair(g, name):
    _, R, C = g.shape

    def body(g_ref, out_ref, send_sems, recv_sems):
        x, y, c = _mesh_pos()
        sibling = (x, y, 1 - c)
        cps = []
        for chip in range(4):
            cps.append(pltpu.make_async_remote_copy(
                src_ref=g_ref.at[2 * chip + (1 - c)], dst_ref=out_ref.at[chip],
                send_sem=send_sems.at[chip], recv_sem=recv_sems.at[chip],
                device_id=sibling, device_id_type=pl.DeviceIdType.MESH))
        for cp in cps:
            cp.start()
        for cp in cps:
            cp.wait_recv()
        for cp in cps:
            cp.wait_send()

    return pl.pallas_call(
        body, name=name,
        in_specs=[pl.BlockSpec(memory_space=pl.ANY)],
        out_specs=pl.BlockSpec(memory_space=pl.ANY),
        out_shape=jax.ShapeDtypeStruct((4, R, C), g.dtype),
        scratch_shapes=[pltpu.SemaphoreType.DMA((4,)), pltpu.SemaphoreType.DMA((4,))],
    )(g)


def _pair_add(g, got, cidx, name):
    _, R, C = g.shape
    tr = 256

    def body(c_ref, g_ref, r_ref, o_ref):
        o_ref[...] = g_ref[...] + r_ref[...]

    return pl.pallas_call(
        body, name=name,
        grid_spec=pltpu.PrefetchScalarGridSpec(
            num_scalar_prefetch=1, grid=(4, R // tr),
            in_specs=[pl.BlockSpec((None, tr, C), lambda k, i, c: (2 * k + c[0], i, 0)),
                      pl.BlockSpec((None, tr, C), lambda k, i, c: (k, i, 0))],
            out_specs=pl.BlockSpec((None, tr, C), lambda k, i, c: (k, i, 0))),
        out_shape=jax.ShapeDtypeStruct((4, R, C), g.dtype),
        compiler_params=_params(("parallel", "parallel")),
    )(cidx, g, got)


def _rs_chips(p, name):
    _, R, C = p.shape

    def body(p_ref, out_ref, send_sems, recv_sems):
        x, y, c = _mesh_pos()
        chips = [(1 - x, y), (x, 1 - y), (1 - x, 1 - y)]
        cps = []
        for j, (cx, cy) in enumerate(chips):
            cps.append(pltpu.make_async_remote_copy(
                src_ref=p_ref.at[2 * cx + cy], dst_ref=out_ref.at[j],
                send_sem=send_sems.at[j], recv_sem=recv_sems.at[j],
                device_id=(cx, cy, c), device_id_type=pl.DeviceIdType.MESH))
        for cp in cps:
            cp.start()
        for cp in cps:
            cp.wait_recv()
        for cp in cps:
            cp.wait_send()

    return pl.pallas_call(
        body, name=name,
        in_specs=[pl.BlockSpec(memory_space=pl.ANY)],
        out_specs=pl.BlockSpec(memory_space=pl.ANY),
        out_shape=jax.ShapeDtypeStruct((3, R, C), p.dtype),
        scratch_shapes=[pltpu.SemaphoreType.DMA((3,)), pltpu.SemaphoreType.DMA((3,))],
    )(p)


def _adamw_math(w, g, m, v):
    m2 = ADAM_B1 * m + (1.0 - ADAM_B1) * g
    v2 = ADAM_B2 * v + (1.0 - ADAM_B2) * (g * g)
    m_hat = m2 / (1.0 - ADAM_B1 ** ADAM_STEP)
    v_hat = v2 / (1.0 - ADAM_B2 ** ADAM_STEP)
    delta = -ADAM_LR * (m_hat / (jnp.sqrt(v_hat) + ADAM_EPS) + ADAM_WD * w)
    return delta, m2, v2


def _adamw_shard(p, got, chip_idx, w, m, v, name):
    R, C = w.shape
    tr = 256

    def body(k_ref, p_ref, r_ref, w_ref, m_ref, v_ref, g_out, d_out, m_out, v_out):
        g = ((p_ref[...] + r_ref[0]) + r_ref[1]) + r_ref[2]
        delta, m2, v2 = _adamw_math(w_ref[...], g, m_ref[...], v_ref[...])
        g_out[...] = g
        d_out[...] = delta
        m_out[...] = m2
        v_out[...] = v2

    row = pl.BlockSpec((tr, C), lambda i, k: (i, 0))
    return pl.pallas_call(
        body, name=name,
        grid_spec=pltpu.PrefetchScalarGridSpec(
            num_scalar_prefetch=1, grid=(R // tr,),
            in_specs=[pl.BlockSpec((None, tr, C), lambda i, k: (k[0], i, 0)),
                      pl.BlockSpec((3, tr, C), lambda i, k: (0, i, 0)), row, row, row],
            out_specs=[row] * 4),
        out_shape=[jax.ShapeDtypeStruct((R, C), F32)] * 4,
        compiler_params=_params(("parallel",)),
    )(chip_idx, p, got, w, m, v)


def _adamw_small(gathered, w, m, v, name):
    _, R, C = gathered.shape

    def body(a_ref, w_ref, m_ref, v_ref, g_out, d_out, m_out, v_out):
        g = a_ref[0]
        for k in range(1, N_DEV):
            g = g + a_ref[k]
        delta, m2, v2 = _adamw_math(w_ref[...], g, m_ref[...], v_ref[...])
        g_out[...] = g
        d_out[...] = delta
        m_out[...] = m2
        v_out[...] = v2

    return pl.pallas_call(
        body, name=name, out_shape=[jax.ShapeDtypeStruct((R, C), F32)] * 4,
    )(gathered, w, m, v)


def _rope_tables(S):
    half = ROPE_DIM // 2
    inv_freq = ROPE_THETA ** (-jnp.arange(half, dtype=F32) / half)
    ang = jnp.arange(S, dtype=jnp.int32).astype(F32)[:, None] * inv_freq[None, :]
    cos, sin = jnp.cos(ang), jnp.sin(ang)
    ones = jnp.ones((S, HEAD_DIM - ROPE_DIM), F32)
    zeros = jnp.zeros((S, HEAD_DIM - ROPE_DIM), F32)
    zh = jnp.zeros((S, half), F32)
    c = jnp.concatenate([cos, cos, ones], axis=1)
    s1 = jnp.concatenate([zh, sin, zeros], axis=1)
    s2 = jnp.concatenate([-sin, zh, zeros], axis=1)
    two = lambda t: jnp.concatenate([t, t], axis=1)
    return two(c), two(s1), two(s2)


def _chunk_transposed(a, S):
    return a.reshape(S // SB_CH, SB_CH, HEAD_PAIRS, LANES).transpose(2, 0, 3, 1)


def _flat_shards(ws):
    return jnp.concatenate([w.reshape(-1, D_MODEL) for layer in ws for w in layer], axis=0)


def kernel(x, w_qkv_0, w_o_0, ln1_g_0, ln1_b_0, w_ff1_0, w_ff2_0, ln2_g_0, ln2_b_0, w_qkv_1, w_o_1, ln1_g_1, ln1_b_1, w_ff1_1, w_ff2_1, ln2_g_1, ln2_b_1, loss_target, m_w_qkv_0, m_w_o_0, m_ln1_g_0, m_ln1_b_0, m_w_ff1_0, m_w_ff2_0, m_ln2_g_0, m_ln2_b_0, m_w_qkv_1, m_w_o_1, m_ln1_g_1, m_ln1_b_1, m_w_ff1_1, m_w_ff2_1, m_ln2_g_1, m_ln2_b_1, v_w_qkv_0, v_w_o_0, v_ln1_g_0, v_ln1_b_0, v_w_ff1_0, v_w_ff2_0, v_ln2_g_0, v_ln2_b_0, v_w_qkv_1, v_w_o_1, v_ln1_g_1, v_ln1_b_1, v_w_ff1_1, v_w_ff2_1, v_ln2_g_1, v_ln2_b_1):
    S = x.shape[1]
    x0 = x.reshape(S, D_MODEL)
    target = loss_target.reshape(S, D_MODEL)
    mats = ((w_qkv_0, w_o_0, w_ff1_0, w_ff2_0), (w_qkv_1, w_o_1, w_ff1_1, w_ff2_1))
    mats_m = ((m_w_qkv_0, m_w_o_0, m_w_ff1_0, m_w_ff2_0), (m_w_qkv_1, m_w_o_1, m_w_ff1_1, m_w_ff2_1))
    mats_v = ((v_w_qkv_0, v_w_o_0, v_w_ff1_0, v_w_ff2_0), (v_w_qkv_1, v_w_o_1, v_w_ff1_1, v_w_ff2_1))
    vecs = (ln1_g_0, ln1_b_0, ln2_g_0, ln2_b_0, ln1_g_1, ln1_b_1, ln2_g_1, ln2_b_1)
    vecs_m = (m_ln1_g_0, m_ln1_b_0, m_ln2_g_0, m_ln2_b_0, m_ln1_g_1, m_ln1_b_1, m_ln2_g_1, m_ln2_b_1)
    vecs_v = (v_ln1_g_0, v_ln1_b_0, v_ln2_g_0, v_ln2_b_0, v_ln1_g_1, v_ln1_b_1, v_ln2_g_1, v_ln2_b_1)

    w_flat = _flat_shards(mats)
    w_all = _all_gather(w_flat.astype(BF16), "ag_weights")
    layers = []
    for l in range(N_LAYERS):
        base = l * LAYER_ROWS
        r0, r1, r2, r3 = np.cumsum((0,) + SHARD_ROWS)[:4] + base
        layers.append(dict(
            qkv=w_all[:, r0:r0 + 384].reshape(N_DEV, D_MODEL, 384),
            o=w_all[:, r1:r1 + 128].reshape(D_MODEL, D_MODEL),
            ff1=w_all[:, r2:r2 + 512].reshape(N_DEV, D_MODEL, 512),
            ff2=w_all[:, r3:r3 + 512].reshape(D_FF, D_MODEL),
            g1=vecs[4 * l].reshape(1, D_MODEL), b1=vecs[4 * l + 1].reshape(1, D_MODEL),
            g2=vecs[4 * l + 2].reshape(1, D_MODEL), b2=vecs[4 * l + 3].reshape(1, D_MODEL)))

    rope = _rope_tables(S)
    tmat_later = _sb_tmat(True)
    tmat_upto = _sb_tmat(False)

    saved = []
    xin, xinb = x0, x0.astype(BF16)
    for l, W in enumerate(layers):
        sv = dict(xin=xin, xinb=xinb)
        qkv = _qkv_proj(xinb, W["qkv"], rope if l == 1 else None, Q_SCALE * LOG2E if l == 0 else Q_SCALE,
                        f"qkv_proj_{l}")
        sv["qkv"] = qkv
        if l == 0:
            vT3 = _chunk_transposed(qkv[:, 2 * D_MODEL:], S)
            o, sb_tiles = _sb_fwd(qkv, vT3, tmat_later, "sb_fwd")
            ob = o.astype(BF16)
            sv["sb_tiles"] = sb_tiles
        else:
            q, k, v = qkv[:, :D_MODEL], qkv[:, D_MODEL:2 * D_MODEL], qkv[:, 2 * D_MODEL:]
            outs = [_dil_fwd(q, k, v, d, f"dil_fwd_{d}") for _, d in DILATED_BRANCHES]
            o, ob, lse = _dil_merge([t[0] for t in outs], [t[1] for t in outs], "dil_merge")
            sv.update(q=q, k=k, v=v, lse=lse)
        sv.update(o=o, ob=ob)
        y1, x1, x1b = _mm_res_ln(ob, xin, W["o"], W["g1"], W["b1"], f"attn_out_ln_{l}")
        hpre, h = _ff1(x1b, W["ff1"], f"ff1_{l}")
        y2, x2, x2b = _mm_res_ln(h, x1, W["ff2"], W["g2"], W["b2"], f"ff2_ln_{l}")
        sv.update(y1=y1, x1=x1, x1b=x1b, hpre=hpre, h=h, y2=y2)
        saved.append(sv)
        xin, xinb = x2, x2b

    dout, loss_parts = _loss_grad(xin, target, "loss_grad")
    loss = lax.psum(jnp.sum(loss_parts), MESH_AXES)

    gmats = [None] * N_LAYERS
    gvecs = [None] * (4 * N_LAYERS)
    for l in reversed(range(N_LAYERS)):
        W, sv = layers[l], saved[l]
        dy2, dy2b, gb2 = _ln_bwd(dout, sv["y2"], W["g2"], f"ln2_bwd_{l}")
        dhp = _dh(dy2b, W["ff2"], sv["hpre"], f"dh_{l}")
        g_ff2 = _mm_tn(sv["h"], dy2b, 512, D_MODEL, False, f"dw_ff2_{l}")
        dx1 = _dx_blk(dy2, dhp, W["ff1"], f"dx_ff1_{l}")
        g_ff1 = _mm_tn(sv["x1b"], dhp, D_MODEL, 512, True, f"dw_ff1_{l}")
        dy1, dy1b, gb1 = _ln_bwd(dx1, sv["y1"], W["g1"], f"ln1_bwd_{l}")
        do = _mm_nt_plain(dy1b, W["o"], f"do_{l}")
        g_o = _mm_tn(sv["ob"], dy1b, 512, D_MODEL, False, f"dw_o_{l}")
        if l == 0:
            kT3 = _chunk_transposed(sv["qkv"][:, D_MODEL:2 * D_MODEL], S)
            dq, dk, dv = _sb_bwd(sv["qkv"], kT3, do, sv["sb_tiles"], tmat_upto, "sb_bwd")
            dqkv = jnp.concatenate([dq, dk, dv], axis=1).astype(BF16)
        else:
            dlt = _head_sums(do, sv["o"], "head_sums")
            parts = [_dil_bwd(sv["q"], sv["k"], sv["v"], do, sv["lse"], dlt, d, f"dil_bwd_{d}")
                     for _, d in DILATED_BRANCHES]
            dqkv = _dil_combine(parts, rope, "dil_combine")
        dout = _dx_blk(dy1, dqkv, W["qkv"], f"dx_qkv_{l}")
        g_qkv = _mm_tn(sv["xinb"], dqkv, D_MODEL, 384, True, f"dw_qkv_{l}")
        gmats[l] = (g_qkv.reshape(N_DEV, 384, D_MODEL), g_o.reshape(N_DEV, 128, D_MODEL),
                    g_ff1.reshape(N_DEV, 512, D_MODEL), g_ff2.reshape(N_DEV, 512, D_MODEL))
        gvecs[4 * l:4 * l + 4] = [gb1[0], gb1[1], gb2[0], gb2[1]]
    grad_x = dout.reshape(1, S, D_MODEL)

    cx, cy, cc = _mesh_pos()
    g_all = jnp.concatenate([g for layer in gmats for g in layer], axis=1)
    got_pair = _rs_pair(g_all, "rs_pair")
    chip_part = _pair_add(g_all, got_pair, cc.astype(jnp.int32).reshape(1), "rs_pair_add")
    got_chips = _rs_chips(chip_part, "rs_chips")
    chip_idx = (2 * cx + cy).astype(jnp.int32).reshape(1)
    g_sh, d_sh, m_sh, v_sh = _adamw_shard(chip_part, got_chips, chip_idx, w_flat, _flat_shards(mats_m),
                                          _flat_shards(mats_v), "adamw_mats")

    def unflat(a):
        out, pos = [], 0
        for layer in mats:
            for w in layer:
                n = w.size // D_MODEL
                out.append(a[pos:pos + n].reshape(w.shape))
                pos += n
        return out

    gv_all = _all_gather(jnp.stack(gvecs), "ag_vec_grads")
    g_v, d_v, m_v, v_v = _adamw_small(gv_all, jnp.stack(vecs), jnp.stack(vecs_m), jnp.stack(vecs_v), "adamw_vecs")

    def interleave(mat_list, vec_arr):
        out = []
        for l in range(N_LAYERS):
            qkv_, o_, ff1_, ff2_ = mat_list[4 * l:4 * l + 4]
            out += [qkv_, o_, vec_arr[4 * l], vec_arr[4 * l + 1], ff1_, ff2_, vec_arr[4 * l + 2], vec_arr[4 * l + 3]]
        return out

    return (loss, grad_x, *interleave(unflat(g_sh), g_v), *interleave(unflat(d_sh), d_v),
            *interleave(unflat(m_sh), m_v), *interleave(unflat(v_sh), v_v))
```

```python
import functools
import math

import jax
import jax.numpy as jnp
import numpy as np
from jax import lax
from jax.experimental import pallas as pl
from jax.experimental.pallas import tpu as pltpu

F32 = jnp.float32
BF16 = jnp.bfloat16

D_MODEL = 1024
N_HEADS = 16
HEAD_DIM = 64
D_FF = 4096
N_DEV = 8
N_LAYERS = 2
ROPE_THETA = 500000.0
ROPE_DIM = 16
DILATED_BRANCHES = ((128, 1), (512, 4), (2048, 16))
ALPHA = (2 * N_LAYERS) ** 0.25
LN_EPS = 1e-5
Q_SCALE = 1.0 / math.sqrt(HEAD_DIM)
LOG2E = math.log2(math.e)
LN2 = math.log(2.0)
ADAM_LR, ADAM_B1, ADAM_B2, ADAM_EPS, ADAM_WD, ADAM_STEP = 0.001, 0.9, 0.999, 1e-08, 0.01, 10

LANES = 128
HEAD_PAIRS = D_MODEL // LANES
SB_TQ = 256
SB_CH = 256
SB_STEPS = 4
SB_SAVE_SLOTS = SB_STEPS
SB_LOAD_SLOTS = 12
SB_LOAD_AHEAD = SB_LOAD_SLOTS - SB_STEPS - 1
DIL_BLK = 128
VMEM_BIG = 56 * 2 ** 20
MESH_AXES = ("x", "y", "c")

SHARD_ROWS = (384, 128, 512, 512)
LAYER_ROWS = sum(SHARD_ROWS)
ALL_ROWS = N_LAYERS * LAYER_ROWS


def _params(sem=None, vmem=None):
    kw = {}
    if sem is not None:
        kw["dimension_semantics"] = sem
    if vmem is not None:
        kw["vmem_limit_bytes"] = vmem
    return pltpu.CompilerParams(**kw)


def _dot(a, b):
    return jnp.dot(a, b, preferred_element_type=F32)


def _dot_nt(a, b):
    return lax.dot_general(a, b, (((1,), (1,)), ((), ())), preferred_element_type=F32)


def _dot_tn(a, b):
    return lax.dot_general(a, b, (((0,), (0,)), ((), ())), preferred_element_type=F32)


def _split3(p):
    hi = p.astype(BF16)
    r1 = p - hi.astype(F32)
    mid = r1.astype(BF16)
    lo = (r1 - mid.astype(F32)).astype(BF16)
    return hi, mid, lo


def _dot3(p, e):
    hi, mid, lo = _split3(p)
    return _dot(hi, e) + _dot(mid, e) + _dot(lo, e)


def _rope_apply(a, c, s1, s2, sign):
    return a * c + sign * (pltpu.roll(a, 8, 1) * s1 + pltpu.roll(a, LANES - 8, 1) * s2)


def _qkv_proj(xb, w_blk, rope, q_mult, name):
    S = xb.shape[0]
    tm = 512
    n_rope = 0 if rope is None else 3

    def body(*refs):
        x_ref, w_ref = refs[:2]
        tabs = [r[...] for r in refs[2:2 + n_rope]]
        o_ref = refs[2 + n_rope]
        x = x_ref[...]
        for j in range(N_DEV):
            acc = _dot(x, w_ref[j])
            for g in range(3):
                col = j * 384 + g * LANES
                a = acc[:, g * LANES:(g + 1) * LANES]
                if n_rope and col < 2 * D_MODEL:
                    a = _rope_apply(a, *tabs, 1.0)
                if col < D_MODEL:
                    a = a * q_mult
                o_ref[:, col:col + LANES] = a.astype(BF16)

    tab_specs = [pl.BlockSpec((tm, LANES), lambda i: (i, 0))] * n_rope
    return pl.pallas_call(
        body, name=name, grid=(S // tm,),
        in_specs=[pl.BlockSpec((tm, D_MODEL), lambda i: (i, 0)),
                  pl.BlockSpec((N_DEV, D_MODEL, 384), lambda i: (0, 0, 0))] + tab_specs,
        out_specs=pl.BlockSpec((tm, 3 * D_MODEL), lambda i: (i, 0)),
        out_shape=jax.ShapeDtypeStruct((S, 3 * D_MODEL), BF16),
        compiler_params=_params(("parallel",), VMEM_BIG),
    )(xb, w_blk, *(rope or ()))


def _layer_norm_rows(y, g, b):
    mu = jnp.mean(y, axis=-1, keepdims=True)
    yc = y - mu
    var = jnp.mean(yc * yc, axis=-1, keepdims=True)
    return yc * lax.rsqrt(var + LN_EPS) * g + b


def _mm_res_ln(a, xres, w, g, b, name):
    S, K = a.shape
    tm = 512 if K <= 1024 else 256

    def body(a_ref, x_ref, w_ref, g_ref, b_ref, y_ref, xn_ref, xb_ref):
        y = ALPHA * x_ref[...] + _dot(a_ref[...], w_ref[...])
        xn = _layer_norm_rows(y, g_ref[...], b_ref[...])
        y_ref[...] = y
        xn_ref[...] = xn
        xb_ref[...] = xn.astype(BF16)

    row = lambda i: (i, 0)
    fix = lambda i: (0, 0)
    return pl.pallas_call(
        body, name=name, grid=(S // tm,),
        in_specs=[pl.BlockSpec((tm, K), row), pl.BlockSpec((tm, D_MODEL), row),
                  pl.BlockSpec((K, D_MODEL), fix), pl.BlockSpec((1, D_MODEL), fix),
                  pl.BlockSpec((1, D_MODEL), fix)],
        out_specs=[pl.BlockSpec((tm, D_MODEL), row)] * 3,
        out_shape=[jax.ShapeDtypeStruct((S, D_MODEL), F32), jax.ShapeDtypeStruct((S, D_MODEL), F32),
                   jax.ShapeDtypeStruct((S, D_MODEL), BF16)],
        compiler_params=_params(("parallel",), VMEM_BIG),
    )(a, xres, w, g, b)


def _ff1(xb, w_blk, name):
    S = xb.shape[0]
    tm = 256

    def body(x_ref, w_ref, hp_ref, h_ref):
        x = x_ref[...]
        for j in range(N_DEV):
            acc = _dot(x, w_ref[j])
            r = jnp.maximum(acc, 0.0)
            hp_ref[:, j * 512:(j + 1) * 512] = acc
            h_ref[:, j * 512:(j + 1) * 512] = (r * r).astype(BF16)

    return pl.pallas_call(
        body, name=name, grid=(S // tm,),
        in_specs=[pl.BlockSpec((tm, D_MODEL), lambda i: (i, 0)),
                  pl.BlockSpec((N_DEV, D_MODEL, 512), lambda i: (0, 0, 0))],
        out_specs=[pl.BlockSpec((tm, D_FF), lambda i: (i, 0))] * 2,
        out_shape=[jax.ShapeDtypeStruct((S, D_FF), F32), jax.ShapeDtypeStruct((S, D_FF), BF16)],
        compiler_params=_params(("parallel",), VMEM_BIG),
    )(xb, w_blk)


def _loss_grad(y, target, name):
    S = y.shape[0]
    tm = 512

    def body(y_ref, t_ref, dy_ref, l_ref):
        @pl.when(pl.program_id(0) == 0)
        def _():
            l_ref[...] = jnp.zeros_like(l_ref)

        err = y_ref[...] - t_ref[...]
        dy_ref[...] = err * (1.0 / D_MODEL)
        sq = err * err
        rows = sq[0:8]
        for r in range(1, tm // 8):
            rows = rows + sq[r * 8:(r + 1) * 8]
        acc = rows[:, 0:LANES]
        for g in range(1, D_MODEL // LANES):
            acc = acc + rows[:, g * LANES:(g + 1) * LANES]
        l_ref[...] += acc * (0.5 / D_MODEL)

    return pl.pallas_call(
        body, name=name, grid=(S // tm,),
        in_specs=[pl.BlockSpec((tm, D_MODEL), lambda i: (i, 0))] * 2,
        out_specs=[pl.BlockSpec((tm, D_MODEL), lambda i: (i, 0)), pl.BlockSpec((8, LANES), lambda i: (0, 0))],
        out_shape=[jax.ShapeDtypeStruct((S, D_MODEL), F32), jax.ShapeDtypeStruct((8, LANES), F32)],
        compiler_params=_params(("arbitrary",)),
    )(y, target)


def _ln_bwd(dout, y, g, name):
    S = y.shape[0]
    tm = 512
    steps = S // tm

    def body(d_ref, y_ref, g_ref, dy_ref, dyb_ref, gb_ref, acc_g, acc_b):
        i = pl.program_id(0)

        @pl.when(i == 0)
        def _():
            acc_g[...] = jnp.zeros_like(acc_g)
            acc_b[...] = jnp.zeros_like(acc_b)

        d = d_ref[...]
        yv = y_ref[...]
        mu = jnp.mean(yv, axis=-1, keepdims=True)
        yc = yv - mu
        var = jnp.mean(yc * yc, axis=-1, keepdims=True)
        rstd = lax.rsqrt(var + LN_EPS)
        xhat = yc * rstd
        dxh = d * g_ref[...]
        m1 = jnp.mean(dxh, axis=-1, keepdims=True)
        m2 = jnp.mean(dxh * xhat, axis=-1, keepdims=True)
        dy = rstd * (dxh - m1 - xhat * m2)
        dy_ref[...] = dy
        dyb_ref[...] = dy.astype(BF16)
        pg = d * xhat
        sg = pg[0:8]
        sb = d[0:8]
        for r in range(1, tm // 8):
            sg = sg + pg[r * 8:(r + 1) * 8]
            sb = sb + d[r * 8:(r + 1) * 8]
        acc_g[...] += sg
        acc_b[...] += sb

        @pl.when(i == steps - 1)
        def _():
            gb_ref[0:1, :] = jnp.sum(acc_g[...], axis=0, keepdims=True)
            gb_ref[1:2, :] = jnp.sum(acc_b[...], axis=0, keepdims=True)

    row = lambda i: (i, 0)
    fix = lambda i: (0, 0)
    return pl.pallas_call(
        body, name=name, grid=(steps,),
        in_specs=[pl.BlockSpec((tm, D_MODEL), row), pl.BlockSpec((tm, D_MODEL), row), pl.BlockSpec((1, D_MODEL), fix)],
        out_specs=[pl.BlockSpec((tm, D_MODEL), row), pl.BlockSpec((tm, D_MODEL), row), pl.BlockSpec((2, D_MODEL), fix)],
        out_shape=[jax.ShapeDtypeStruct((S, D_MODEL), F32), jax.ShapeDtypeStruct((S, D_MODEL), BF16),
                   jax.ShapeDtypeStruct((2, D_MODEL), F32)],
        scratch_shapes=[pltpu.VMEM((8, D_MODEL), F32), pltpu.VMEM((8, D_MODEL), F32)],
        compiler_params=_params(("arbitrary",)),
    )(dout, y, g)


def _dh(dyb, w2, hpre, name):
    S = dyb.shape[0]
    tm = 256
    tn = 512

    def body(dy_ref, w_ref, hp_ref, o_ref):
        dy = dy_ref[...]
        for n in range(0, D_FF, tn):
            dh = _dot_nt(dy, w_ref[n:n + tn, :])
            o_ref[:, n:n + tn] = (dh * (2.0 * jnp.maximum(hp_ref[:, n:n + tn], 0.0))).astype(BF16)

    return pl.pallas_call(
        body, name=name, grid=(S // tm,),
        in_specs=[pl.BlockSpec((tm, D_MODEL), lambda i: (i, 0)), pl.BlockSpec((D_FF, D_MODEL), lambda i: (0, 0)),
                  pl.BlockSpec((tm, D_FF), lambda i: (i, 0))],
        out_specs=pl.BlockSpec((tm, D_FF), lambda i: (i, 0)),
        out_shape=jax.ShapeDtypeStruct((S, D_FF), BF16),
        compiler_params=_params(("parallel",), VMEM_BIG),
    )(dyb, w2, hpre)


def _dx_blk(dres, dz, w_blk, name):
    S, N = dz.shape
    bw = w_blk.shape[2]
    tm = 256

    def body(r_ref, z_ref, w_ref, o_ref):
        acc = ALPHA * r_ref[...]
        for j in range(N_DEV):
            acc = acc + _dot_nt(z_ref[:, j * bw:(j + 1) * bw], w_ref[j])
        o_ref[...] = acc

    return pl.pallas_call(
        body, name=name, grid=(S // tm,),
        in_specs=[pl.BlockSpec((tm, D_MODEL), lambda i: (i, 0)), pl.BlockSpec((tm, N), lambda i: (i, 0)),
                  pl.BlockSpec((N_DEV, D_MODEL, bw), lambda i: (0, 0, 0))],
        out_specs=pl.BlockSpec((tm, D_MODEL), lambda i: (i, 0)),
        out_shape=jax.ShapeDtypeStruct((S, D_MODEL), F32),
        compiler_params=_params(("parallel",), VMEM_BIG),
    )(dres, dz, w_blk)


def _mm_nt_plain(a, w, name):
    S = a.shape[0]
    tm = 512

    def body(a_ref, w_ref, o_ref):
        o_ref[...] = _dot_nt(a_ref[...], w_ref[...])

    return pl.pallas_call(
        body, name=name, grid=(S // tm,),
        in_specs=[pl.BlockSpec((tm, D_MODEL), lambda i: (i, 0)), pl.BlockSpec((D_MODEL, D_MODEL), lambda i: (0, 0))],
        out_specs=pl.BlockSpec((tm, D_MODEL), lambda i: (i, 0)),
        out_shape=jax.ShapeDtypeStruct((S, D_MODEL), F32),
        compiler_params=_params(("parallel",)),
    )(a, w)


def _mm_tn(a, b, ta, tb, blocked, name):
    S, Ka = a.shape
    Nb = b.shape[1]
    ts = 2048

    def body(a_ref, b_ref, o_ref):
        @pl.when(pl.program_id(2) == 0)
        def _():
            o_ref[...] = jnp.zeros_like(o_ref)

        o_ref[...] += _dot_tn(a_ref[...], b_ref[...])

    if blocked:
        out_spec = pl.BlockSpec((None, ta, tb), lambda i, j, s: (j, i, 0))
        out_shape = jax.ShapeDtypeStruct((Nb // tb, Ka, tb), F32)
    else:
        out_spec = pl.BlockSpec((ta, tb), lambda i, j, s: (i, j))
        out_shape = jax.ShapeDtypeStruct((Ka, Nb), F32)
    return pl.pallas_call(
        body, name=name, grid=(Ka // ta, Nb // tb, S // ts),
        in_specs=[pl.BlockSpec((ts, ta), lambda i, j, s: (s, i)), pl.BlockSpec((ts, tb), lambda i, j, s: (s, j))],
        out_specs=out_spec, out_shape=out_shape,
        compiler_params=_params(("parallel", "parallel", "arbitrary"), VMEM_BIG),
    )(a, b)


def _head_sums(do, o, name):
    S = do.shape[0]
    tm = 512
    sel = (np.arange(D_MODEL)[:, None] // HEAD_DIM == np.arange(LANES)[None, :]).astype(np.float32)

    def body(d_ref, o_ref, e_ref, out_ref):
        out_ref[...] = _dot3(d_ref[...] * o_ref[...], e_ref[...])

    return pl.pallas_call(
        body, name=name, grid=(S // tm,),
        in_specs=[pl.BlockSpec((tm, D_MODEL), lambda i: (i, 0))] * 2 + [pl.BlockSpec((D_MODEL, LANES), lambda i: (0, 0))],
        out_specs=pl.BlockSpec((tm, LANES), lambda i: (i, 0)),
        out_shape=jax.ShapeDtypeStruct((S, LANES), F32),
        compiler_params=_params(("parallel",)),
    )(do, o, jnp.asarray(sel, BF16))


def _sb_tmat(later):
    r = np.arange(SB_CH)
    t = (r[None, :] > r[:, None]) if later else (r[None, :] <= r[:, None])
    return jnp.asarray(np.concatenate([t.astype(np.float32), np.ones((8, SB_CH), np.float32)], axis=0), BF16)


def _sb_gates(z2):
    neg_abs = lax.bitcast_convert_type(lax.bitcast_convert_type(z2, jnp.uint32) | jnp.uint32(0x80000000), F32)
    l1 = jnp.log2(1.0 + jnp.exp2(neg_abs))
    a = jnp.minimum(z2, 0.0) - l1
    return a, a - z2


def _head_masks(x2):
    lane = lax.broadcasted_iota(jnp.int32, x2.shape, 1)
    zero = jnp.zeros_like(x2)
    return jnp.where(lane < HEAD_DIM, x2, zero), jnp.where(lane >= HEAD_DIM, x2, zero)


def _sb_fwd(qkv, vT3, tmat, name):
    S = qkv.shape[0]
    nq = S // SB_TQ
    nch = S // SB_CH
    ns = SB_SAVE_SLOTS

    def body(q_ref, k_ref, vT_ref, t_ref, o_ref, ws_hbm, z_scr, a_scr, cum_scr, oT_scr, stage, sems):
        hp = pl.program_id(0)
        i = pl.program_id(1)
        base = (i * (i + 1)) // 2
        qm = _head_masks(q_ref[...])

        def save(src, sem, c):
            return pltpu.make_async_copy(src, ws_hbm.at[hp, base + c], sem)

        causal = (lax.broadcasted_iota(jnp.int32, (SB_CH, SB_TQ), 0)
                  < lax.broadcasted_iota(jnp.int32, (SB_CH, SB_TQ), 1))

        def head_rows(vTc, h):
            return vTc[h * HEAD_DIM:(h + 1) * HEAD_DIM, :]

        @pl.when(jnp.logical_and(hp == 0, i == 0))
        def _():
            z_scr[...] = jnp.zeros_like(z_scr)
            a_scr[...] = jnp.zeros_like(a_scr)
            cum_scr[...] = jnp.zeros_like(cum_scr)

        oT_scr[...] = jnp.zeros_like(oT_scr)

        def c_valid(t):
            return jnp.logical_and(t >= 2, t - 2 <= i)

        def c_chunk(t):
            return jnp.clip(i + 2 - t, 0, nch - 1)

        def step(t, slot, R, own_b, own_c):
            p = slot % 2
            cA = jnp.maximum(i - t, 0)
            kA = k_ref[pl.ds(pl.multiple_of(cA * SB_CH, SB_CH), SB_CH), :]
            valid = c_valid(t)
            vC = vT_ref[c_chunk(t)]
            out = []
            for h in range(2):
                z_scr[p, h] = _dot_nt(kA, qm[h])
                a, lf = _sb_gates(z_scr[1 - p, h])
                if own_b:
                    lf = jnp.where(causal, lf, 0.0)
                a_scr[1 - p, h] = a
                cum_scr[1 - p, h] = _dot(t_ref[...], lf.astype(BF16))
                a_c = a_scr[p, h]
                w = jnp.exp2(a_c + cum_scr[p, h, :SB_CH, :] + R[h])
                if own_c:
                    w = jnp.where(causal, w, 0.0)
                wb = w.astype(BF16)
                stage[slot, 2 * h] = wb
                stage[slot, 2 * h + 1] = a_c.astype(BF16)
                oT_scr[h] += jnp.where(valid, _dot(head_rows(vC, h), wb), 0.0)
                out.append(R[h] + jnp.where(valid, cum_scr[p, h, SB_CH:SB_CH + 1, :], 0.0))
            return tuple(out)

        def trip(tt, R, first):
            if not first:
                for j in range(SB_STEPS):
                    @pl.when(c_valid(SB_STEPS * tt + j - ns))
                    def _():
                        save(stage.at[j], sems.at[j], 0).wait()

            for j in range(SB_STEPS):
                R = step(SB_STEPS * tt + j, j, R, first and j == 1, first and j == 2)
            for j in range(SB_STEPS):
                t = SB_STEPS * tt + j

                @pl.when(c_valid(t))
                def _():
                    save(stage.at[j], sems.at[j], c_chunk(t)).start()

            return R

        z1 = jnp.zeros((1, SB_TQ), F32)
        trips = (i + 3 + SB_STEPS - 1) // SB_STEPS
        lax.fori_loop(1, trips, lambda tt, R: trip(tt, R, False), trip(0, (z1, z1), True))
        for j in range(SB_STEPS):
            @pl.when(c_valid(SB_STEPS * (trips - 1) + j))
            def _():
                save(stage.at[j], sems.at[j], 0).wait()

        o_ref[...] = jnp.concatenate([oT_scr[0], oT_scr[1]], axis=0).T

    ntile = nq * (nq + 1) // 2
    return pl.pallas_call(
        body, name=name, grid=(HEAD_PAIRS, nq),
        in_specs=[pl.BlockSpec((SB_TQ, LANES), lambda hp, i: (i, hp)),
                  pl.BlockSpec((S, LANES), lambda hp, i: (0, HEAD_PAIRS + hp)),
                  pl.BlockSpec((None, nch, LANES, SB_CH), lambda hp, i: (hp, 0, 0, 0)),
                  pl.BlockSpec((SB_CH + 8, SB_CH), lambda hp, i: (0, 0))],
        out_specs=[pl.BlockSpec((SB_TQ, LANES), lambda hp, i: (i, hp)), pl.BlockSpec(memory_space=pl.ANY)],
        out_shape=[jax.ShapeDtypeStruct((S, D_MODEL), F32),
                   jax.ShapeDtypeStruct((HEAD_PAIRS, ntile, 4, SB_CH, SB_TQ), BF16)],
        scratch_shapes=[pltpu.VMEM((2, 2, SB_CH, SB_TQ), F32), pltpu.VMEM((2, 2, SB_CH, SB_TQ), F32),
                        pltpu.VMEM((2, 2, SB_CH + 8, SB_TQ), F32), pltpu.VMEM((2, HEAD_DIM, SB_TQ), F32),
                        pltpu.VMEM((ns, 4, SB_CH, SB_TQ), BF16), pltpu.SemaphoreType.DMA((ns,))],
        compiler_params=_params(("arbitrary", "arbitrary"), VMEM_BIG),
    )(qkv, qkv, vT3, tmat)


def _sb_bwd(qkv, kT3, do, ws, tmat_g, name):
    S = qkv.shape[0]
    nq = S // SB_TQ
    nch = S // SB_CH
    nl = SB_LOAD_SLOTS
    ahead = SB_LOAD_AHEAD

    def body(q_ref, do_ref, v_ref, kT_ref, tg_ref, ws_hbm, dq_ref, dk_hbm, dv_hbm, dk_acc, dv_acc, sems,
             dwv_scr, g_scr, sig_scr, cumg_scr, dqT_scr, ring, ring_sems):
        hp = pl.program_id(0)
        i = pl.program_id(1)
        base = (i * (i + 1)) // 2

        @pl.when(i == 0)
        def _():
            dk_acc[...] = jnp.zeros_like(dk_acc)
            dv_acc[...] = jnp.zeros_like(dv_acc)

        @pl.when(jnp.logical_and(hp == 0, i == 0))
        def _():
            for scr in (dwv_scr, g_scr, sig_scr, cumg_scr, ring):
                scr[...] = jnp.zeros_like(scr)

        def load(u):
            slot = lax.rem(u, nl)
            return pltpu.make_async_copy(ws_hbm.at[hp, base + u], ring.at[slot], ring_sems.at[slot])

        for u in range(ahead):
            @pl.when(u <= i)
            def _():
                load(u).start()

        dqT_scr[...] = jnp.zeros_like(dqT_scr)
        qm = _head_masks(q_ref[...])
        dom = _head_masks(do_ref[...].astype(BF16))
        causal = (lax.broadcasted_iota(jnp.int32, (SB_CH, SB_TQ), 0)
                  < lax.broadcasted_iota(jnp.int32, (SB_CH, SB_TQ), 1))

        def rows_of(c):
            return pl.ds(pl.multiple_of(c * SB_CH, SB_CH), SB_CH)

        def head_rows(kTc, h):
            return kTc[h * HEAD_DIM:(h + 1) * HEAD_DIM, :]

        def step(t, p, Gs):
            q = 1 - p
            valid_b = jnp.logical_and(t >= 1, t - 1 <= i)
            valid_c = jnp.logical_and(t >= 2, t - 2 <= i)
            c_b = jnp.clip(t - 1, 0, i)
            c_c = jnp.clip(t - 2, 0, i)
            slot = jnp.where(valid_b, lax.rem(jnp.maximum(t - 1, 0), nl), nl)
            vA = v_ref[rows_of(jnp.minimum(t, i)), :]
            kTc = kT_ref[c_c]
            keep = jnp.logical_or(causal, t - 2 != i)
            out = []
            for h in range(2):
                dwv_scr[p, h] = _dot_nt(vA, dom[h])

                wb = ring[slot, 2 * h]
                g = wb.astype(F32) * dwv_scr[q, h]
                g_scr[q, h] = g
                sig_scr[q, h] = jnp.exp2(ring[slot, 2 * h + 1].astype(F32))
                cumg_scr[q, h] = _dot(tg_ref[...], g.astype(BF16))
                dv_h = _dot(wb, dom[h])

                dz = g_scr[p, h] - sig_scr[p, h] * (Gs[h] + cumg_scr[p, h, :SB_CH, :])
                dzb = jnp.where(keep, dz, 0.0).astype(BF16)
                dk_h = _dot(dzb, qm[h])
                dqT_scr[h] += jnp.where(valid_c, _dot(head_rows(kTc, h), dzb), 0.0)
                out.append(Gs[h] + jnp.where(valid_c, cumg_scr[p, h, SB_CH:SB_CH + 1, :], 0.0))
                dk_c = dk_h if h == 0 else dk_c + dk_h
                dv_c = dv_h if h == 0 else dv_c + dv_h
            dv_acc[rows_of(c_b), :] += jnp.where(valid_b, dv_c, 0.0)
            dk_acc[rows_of(c_c), :] += jnp.where(valid_c, dk_c, 0.0)
            return tuple(out)

        def trip(tt, Gs):
            for j in range(SB_STEPS):
                t = SB_STEPS * tt + j

                @pl.when(jnp.logical_and(t >= 1, t - 1 <= i))
                def _():
                    load(t - 1).wait()

            for j in range(SB_STEPS):
                t = SB_STEPS * tt + j

                @pl.when(t + ahead <= i)
                def _():
                    load(t + ahead).start()

            for j in range(SB_STEPS):
                Gs = step(SB_STEPS * tt + j, j % 2, Gs)
            return Gs

        z1 = jnp.zeros((1, SB_TQ), F32)
        lax.fori_loop(0, (i + 3 + SB_STEPS - 1) // SB_STEPS, trip, (z1, z1))
        dq_ref[...] = jnp.concatenate([dqT_scr[0], dqT_scr[1]], axis=0).T * Q_SCALE

        @pl.when(i == nq - 1)
        def _():
            dk_acc[...] = dk_acc[...] * LN2
            cols = pl.ds(pl.multiple_of(hp * LANES, LANES), LANES)
            ck = pltpu.make_async_copy(dk_acc, dk_hbm.at[:, cols], sems.at[0])
            cv = pltpu.make_async_copy(dv_acc, dv_hbm.at[:, cols], sems.at[1])
            ck.start()
            cv.start()
            ck.wait()
            cv.wait()

    blk = lambda hp, i: (i, hp)
    return pl.pallas_call(
        body, name=name, grid=(HEAD_PAIRS, nq),
        in_specs=[pl.BlockSpec((SB_TQ, LANES), blk),
                  pl.BlockSpec((SB_TQ, LANES), blk),
                  pl.BlockSpec((S, LANES), lambda hp, i: (0, 2 * HEAD_PAIRS + hp)),
                  pl.BlockSpec((None, nch, LANES, SB_CH), lambda hp, i: (hp, 0, 0, 0)),
                  pl.BlockSpec((SB_CH + 8, SB_CH), lambda hp, i: (0, 0)),
                  pl.BlockSpec(memory_space=pl.ANY)],
        out_specs=[pl.BlockSpec((SB_TQ, LANES), blk), pl.BlockSpec(memory_space=pl.ANY),
                   pl.BlockSpec(memory_space=pl.ANY)],
        out_shape=[jax.ShapeDtypeStruct((S, D_MODEL), F32)] * 3,
        scratch_shapes=[pltpu.VMEM((S, LANES), F32), pltpu.VMEM((S, LANES), F32), pltpu.SemaphoreType.DMA((2,))]
        + [pltpu.VMEM((2, 2, SB_CH, SB_TQ), F32)] * 3
        + [pltpu.VMEM((2, 2, SB_CH + 8, SB_TQ), F32), pltpu.VMEM((2, HEAD_DIM, SB_TQ), F32)]
        + [pltpu.VMEM((nl + 1, 4, SB_CH, SB_TQ), BF16), pltpu.SemaphoreType.DMA((nl,))],
        compiler_params=_params(("arbitrary", "arbitrary"), VMEM_BIG),
    )(qkv, do, qkv, kT3, tmat_g, ws)


def _dil_valid(n):
    qi = lax.broadcasted_iota(jnp.int32, (DIL_BLK, 2 * DIL_BLK), 0)
    kj = lax.broadcasted_iota(jnp.int32, (DIL_BLK, 2 * DIL_BLK), 1)
    dist = DIL_BLK + qi - kj
    return (dist >= 0) & (dist <= DIL_BLK) & ((n > 0) | (kj >= DIL_BLK))


def _lane_pick(tile, idx):
    lane = lax.broadcasted_iota(jnp.int32, tile.shape, 1)
    return jnp.sum(jnp.where(lane == idx, tile, 0.0), axis=-1, keepdims=True)


def _dil_specs(d, width):
    cur = pl.BlockSpec((DIL_BLK, width), lambda r, n: (n, r))
    prev = pl.BlockSpec((DIL_BLK, width), lambda r, n: (jnp.maximum(n - 1, 0), r))
    return cur, prev


def _dil_fwd(q, k, v, d, name):
    S = q.shape[0]
    L = S // d
    nb = L // DIL_BLK

    def body(q_ref, kc_ref, kp_ref, vc_ref, vp_ref, o_ref, lse_ref):
        valid = _dil_valid(pl.program_id(1))
        lane = lax.broadcasted_iota(jnp.int32, (DIL_BLK, LANES), 1)
        lse_t = jnp.zeros((DIL_BLK, LANES), F32)
        for hp in range(HEAD_PAIRS):
            cols = slice(hp * LANES, (hp + 1) * LANES)
            qm = _head_masks(q_ref[:, cols])
            kk = jnp.concatenate([kp_ref[:, cols], kc_ref[:, cols]], axis=0)
            vm = _head_masks(jnp.concatenate([vp_ref[:, cols], vc_ref[:, cols]], axis=0))
            o2 = None
            for h in range(2):
                s = jnp.where(valid, _dot_nt(qm[h], kk), -1e30)
                m = jnp.max(s, axis=-1, keepdims=True)
                p = jnp.exp(s - m)
                den = jnp.sum(p, axis=-1, keepdims=True)
                oh = _dot(p.astype(BF16), vm[h]) / den
                o2 = oh if o2 is None else o2 + oh
                lse_t = jnp.where(lane == 2 * hp + h, m + jnp.log(den), lse_t)
            o_ref[:, cols] = o2
        lse_ref[...] = lse_t

    cur, prev = _dil_specs(d, D_MODEL)
    lcur, _ = _dil_specs(d, LANES)
    view = lambda a: a.reshape(L, d * a.shape[1])
    o, lse = pl.pallas_call(
        body, name=name, grid=(d, nb),
        in_specs=[cur, cur, prev, cur, prev],
        out_specs=[cur, lcur],
        out_shape=[jax.ShapeDtypeStruct((L, d * D_MODEL), F32), jax.ShapeDtypeStruct((L, d * LANES), F32)],
        compiler_params=_params(("parallel", "parallel")),
    )(view(q), view(k), view(k), view(v), view(v))
    return o.reshape(S, D_MODEL), lse.reshape(S, LANES)


def _head_expand():
    return jnp.asarray((np.arange(LANES)[:, None] == np.arange(D_MODEL)[None, :] // HEAD_DIM).astype(np.float32), BF16)


def _dil_merge(os_, lses, name):
    S = os_[0].shape[0]
    tm = 256
    nbr = len(os_)

    def body(*refs):
        o_refs, l_refs, e_ref = refs[:nbr], refs[nbr:2 * nbr], refs[2 * nbr]
        out_ref, outb_ref, lse_ref = refs[2 * nbr + 1:]
        ls = [r[...] for r in l_refs]
        m = ls[0]
        for l in ls[1:]:
            m = jnp.maximum(m, l)
        tot = jnp.exp(ls[0] - m)
        for l in ls[1:]:
            tot = tot + jnp.exp(l - m)
        lse = m + jnp.log(tot)
        acc = None
        for o_r, l in zip(o_refs, ls):
            wt = _dot3(jnp.exp(l - lse), e_ref[...])
            term = wt * o_r[...]
            acc = term if acc is None else acc + term
        out_ref[...] = acc
        outb_ref[...] = acc.astype(BF16)
        lse_ref[...] = lse

    row = lambda i: (i, 0)
    return pl.pallas_call(
        body, name=name, grid=(S // tm,),
        in_specs=[pl.BlockSpec((tm, D_MODEL), row)] * nbr + [pl.BlockSpec((tm, LANES), row)] * nbr
        + [pl.BlockSpec((LANES, D_MODEL), lambda i: (0, 0))],
        out_specs=[pl.BlockSpec((tm, D_MODEL), row), pl.BlockSpec((tm, D_MODEL), row), pl.BlockSpec((tm, LANES), row)],
        out_shape=[jax.ShapeDtypeStruct((S, D_MODEL), F32), jax.ShapeDtypeStruct((S, D_MODEL), BF16),
                   jax.ShapeDtypeStruct((S, LANES), F32)],
        compiler_params=_params(("parallel",)),
    )(*os_, *lses, _head_expand())


def _dil_bwd(q, k, v, do, lse, dlt, d, name):
    S = q.shape[0]
    L = S // d
    nb = L // DIL_BLK

    def body(q_ref, kc_ref, kp_ref, vc_ref, vp_ref, do_ref, lse_ref, dl_ref,
             dq_ref, dka_ref, dkb_ref, dva_ref, dvb_ref):
        valid = _dil_valid(pl.program_id(1))
        lse_t = lse_ref[...]
        dl_t = dl_ref[...]
        for hp in range(HEAD_PAIRS):
            cols = slice(hp * LANES, (hp + 1) * LANES)
            qm = _head_masks(q_ref[:, cols])
            dom = _head_masks(do_ref[:, cols].astype(BF16))
            kk = jnp.concatenate([kp_ref[:, cols], kc_ref[:, cols]], axis=0)
            vv = jnp.concatenate([vp_ref[:, cols], vc_ref[:, cols]], axis=0)
            km = _head_masks(kk)
            dq2 = dkk = dvv = None
            for h in range(2):
                s = _dot_nt(qm[h], kk)
                p = jnp.where(valid, jnp.exp(s - _lane_pick(lse_t, 2 * hp + h)), 0.0)
                ds = (p * (_dot_nt(dom[h], vv) - _lane_pick(dl_t, 2 * hp + h))).astype(BF16)
                t_q = _dot(ds, km[h])
                t_k = _dot_tn(ds, qm[h])
                t_v = _dot_tn(p.astype(BF16), dom[h])
                dq2 = t_q if dq2 is None else dq2 + t_q
                dkk = t_k if dkk is None else dkk + t_k
                dvv = t_v if dvv is None else dvv + t_v
            dq_ref[:, cols] = dq2
            dkb_ref[:, cols] = dkk[:DIL_BLK]
            dka_ref[:, cols] = dkk[DIL_BLK:]
            dvb_ref[:, cols] = dvv[:DIL_BLK]
            dva_ref[:, cols] = dvv[DIL_BLK:]

    cur, prev = _dil_specs(d, D_MODEL)
    lcur, _ = _dil_specs(d, LANES)
    view = lambda a: a.reshape(L, d * a.shape[1])
    outs = pl.pallas_call(
        body, name=name, grid=(d, nb),
        in_specs=[cur, cur, prev, cur, prev, cur, lcur, lcur],
        out_specs=[cur] * 5,
        out_shape=[jax.ShapeDtypeStruct((L, d * D_MODEL), F32)] * 5,
        compiler_params=_params(("parallel", "parallel"), VMEM_BIG),
    )(view(q), view(k), view(k), view(v), view(v), view(do), view(lse), view(dlt))
    return [o.reshape(S, D_MODEL) for o in outs]


def _dil_combine(parts, rope, name):
    S = parts[0][0].shape[0]
    tm = DIL_BLK
    nblk = S // tm
    dils = [d for _, d in DILATED_BRANCHES]

    def body(*refs):
        ins = refs[:5 * len(dils)]
        c_ref, s1_ref, s2_ref, o_ref = refs[5 * len(dils):]
        i = pl.program_id(0)
        tabs = (c_ref[...], s1_ref[...], s2_ref[...])
        dq = dk = dv = None
        for b, d in enumerate(dils):
            dq_r, dka_r, dkb_r, dva_r, dvb_r = ins[5 * b:5 * b + 5]
            live = (i + d < nblk).astype(F32)
            tq = dq_r[...]
            tk = dka_r[...] + live * dkb_r[...]
            tv = dva_r[...] + live * dvb_r[...]
            dq = tq if dq is None else dq + tq
            dk = tk if dk is None else dk + tk
            dv = tv if dv is None else dv + tv
        dq = dq * Q_SCALE
        for g in range(HEAD_PAIRS):
            cols = slice(g * LANES, (g + 1) * LANES)
            o_ref[:, g * LANES:(g + 1) * LANES] = _rope_apply(dq[:, cols], *tabs, -1.0).astype(BF16)
            o_ref[:, D_MODEL + g * LANES:D_MODEL + (g + 1) * LANES] = _rope_apply(dk[:, cols], *tabs, -1.0).astype(BF16)
        o_ref[:, 2 * D_MODEL:] = dv.astype(BF16)

    row = pl.BlockSpec((tm, D_MODEL), lambda i: (i, 0))
    in_specs = []
    args = []
    for (dq_b, dka, dkb, dva, dvb), d in zip(parts, dils):
        ahead = pl.BlockSpec((tm, D_MODEL), lambda i, d=d: (jnp.minimum(i + d, nblk - 1), 0))
        in_specs += [row, row, ahead, row, ahead]
        args += [dq_b, dka, dkb, dva, dvb]
    in_specs += [pl.BlockSpec((tm, LANES), lambda i: (i, 0))] * 3
    return pl.pallas_call(
        body, name=name, grid=(nblk,),
        in_specs=in_specs,
        out_specs=pl.BlockSpec((tm, 3 * D_MODEL), lambda i: (i, 0)),
        out_shape=jax.ShapeDtypeStruct((S, 3 * D_MODEL), BF16),
        compiler_params=_params(("parallel",), VMEM_BIG),
    )(*args, *rope)


def _mesh_pos():
    return lax.axis_index("x"), lax.axis_index("y"), lax.axis_index("c")


def _all_gather(shard, name):
    R, C = shard.shape

    def body(x_ref, out_ref, send_sems, recv_sems, local_sem):
        x, y, c = _mesh_pos()
        me, sibling = (x, y, c), (x, y, 1 - c)
        chips = [(1 - x, y), (x, 1 - y), (1 - x, 1 - y)]

        def blk(p):
            return out_ref.at[4 * p[0] + 2 * p[1] + p[2]]

        def copy(k, block, to, src=None):
            return pltpu.make_async_remote_copy(
                src_ref=blk(block) if src is None else src, dst_ref=blk(block),
                send_sem=send_sems.at[k], recv_sem=recv_sems.at[k],
                device_id=to, device_id_type=pl.DeviceIdType.MESH)

        mine = pltpu.make_async_copy(x_ref, blk(me), local_sem)
        mine.start()
        first = [copy(0, me, sibling, src=x_ref)]
        first += [copy(1 + j, me, (*chip, c), src=x_ref) for j, chip in enumerate(chips)]
        for cp in first:
            cp.start()
        passed = [copy(4 + j, (*chip, c), sibling) for j, chip in enumerate(chips)]
        for j, chip in enumerate(chips):
            copy(1 + j, (*chip, c), me).wait_recv()
            passed[j].start()
        copy(0, sibling, me).wait_recv()
        for j, chip in enumerate(chips):
            copy(4 + j, (*chip, 1 - c), me).wait_recv()
        for cp in first + passed:
            cp.wait_send()
        mine.wait()

    return pl.pallas_call(
        body, name=name,
        in_specs=[pl.BlockSpec(memory_space=pl.ANY)],
        out_specs=pl.BlockSpec(memory_space=pl.ANY),
        out_shape=jax.ShapeDtypeStruct((N_DEV, R, C), shard.dtype),
        scratch_shapes=[pltpu.SemaphoreType.DMA((7,)), pltpu.SemaphoreType.DMA((7,)), pltpu.SemaphoreType.DMA],
    )(shard)


def _rs_pair(g, name):
    _, R, C = g.shape

    def body(g_ref, out_ref, send_sems, recv_sems):
        x, y, c = _mesh_pos()
        sibling = (x, y, 1 - c)
        cps = []
        for chip in range(4):
            cps.append(pltpu.make_async_remote_copy(
                src_ref=g_ref.at[2 * chip + (1 - c)], dst_ref=out_ref.at[chip],
                send_sem=send_sems.at[chip], recv_sem=recv_sems.at[chip],
                device_id=sibling, device_id_type=pl.DeviceIdType.MESH))
        for cp in cps:
            cp.start()
        for cp in cps:
            cp.wait_recv()
        for cp in cps:
            cp.wait_send()

    return pl.pallas_call(
        body, name=name,
        in_specs=[pl.BlockSpec(memory_space=pl.ANY)],
        out_specs=pl.BlockSpec(memory_space=pl.ANY),
        out_shape=jax.ShapeDtypeStruct((4, R, C), g.dtype),
        scratch_shapes=[pltpu.SemaphoreType.DMA((4,)), pltpu.SemaphoreType.DMA((4,))],
    )(g)


def _pair_add(g, got, cidx, name):
    _, R, C = g.shape
    tr = 256

    def body(c_ref, g_ref, r_ref, o_ref):
        o_ref[...] = g_ref[...] + r_ref[...]

    return pl.pallas_call(
        body, name=name,
        grid_spec=pltpu.PrefetchScalarGridSpec(
            num_scalar_prefetch=1, grid=(4, R // tr),
            in_specs=[pl.BlockSpec((None, tr, C), lambda k, i, c: (2 * k + c[0], i, 0)),
                      pl.BlockSpec((None, tr, C), lambda k, i, c: (k, i, 0))],
            out_specs=pl.BlockSpec((None, tr, C), lambda k, i, c: (k, i, 0))),
        out_shape=jax.ShapeDtypeStruct((4, R, C), g.dtype),
        compiler_params=_params(("parallel", "parallel")),
    )(cidx, g, got)


def _rs_chips(p, name):
    _, R, C = p.shape

    def body(p_ref, out_ref, send_sems, recv_sems):
        x, y, c = _mesh_pos()
        chips = [(1 - x, y), (x, 1 - y), (1 - x, 1 - y)]
        cps = []
        for j, (cx, cy) in enumerate(chips):
            cps.append(pltpu.make_async_remote_copy(
                src_ref=p_ref.at[2 * cx + cy], dst_ref=out_ref.at[j],
                send_sem=send_sems.at[j], recv_sem=recv_sems.at[j],
                device_id=(cx, cy, c), device_id_type=pl.DeviceIdType.MESH))
        for cp in cps:
            cp.start()
        for cp in cps:
            cp.wait_recv()
        for cp in cps:
            cp.wait_send()

    return pl.pallas_call(
        body, name=name,
        in_specs=[pl.BlockSpec(memory_space=pl.ANY)],
        out_specs=pl.BlockSpec(memory_space=pl.ANY),
        out_shape=jax.ShapeDtypeStruct((3, R, C), p.dtype),
        scratch_shapes=[pltpu.SemaphoreType.DMA((3,)), pltpu.SemaphoreType.DMA((3,))],
    )(p)


def _adamw_math(w, g, m, v):
    m2 = ADAM_B1 * m + (1.0 - ADAM_B1) * g
    v2 = ADAM_B2 * v + (1.0 - ADAM_B2) * (g * g)
    m_hat = m2 / (1.0 - ADAM_B1 ** ADAM_STEP)
    v_hat = v2 / (1.0 - ADAM_B2 ** ADAM_STEP)
    delta = -ADAM_LR * (m_hat / (jnp.sqrt(v_hat) + ADAM_EPS) + ADAM_WD * w)
    return delta, m2, v2


def _adamw_shard(p, got, chip_idx, w, m, v, name):
    R, C = w.shape
    tr = 256

    def body(k_ref, p_ref, r_ref, w_ref, m_ref, v_ref, g_out, d_out, m_out, v_out):
        g = ((p_ref[...] + r_ref[0]) + r_ref[1]) + r_ref[2]
        delta, m2, v2 = _adamw_math(w_ref[...], g, m_ref[...], v_ref[...])
        g_out[...] = g
        d_out[...] = delta
        m_out[...] = m2
        v_out[...] = v2

    row = pl.BlockSpec((tr, C), lambda i, k: (i, 0))
    return pl.pallas_call(
        body, name=name,
        grid_spec=pltpu.PrefetchScalarGridSpec(
            num_scalar_prefetch=1, grid=(R // tr,),
            in_specs=[pl.BlockSpec((None, tr, C), lambda i, k: (k[0], i, 0)),
                      pl.BlockSpec((3, tr, C), lambda i, k: (0, i, 0)), row, row, row],
            out_specs=[row] * 4),
        out_shape=[jax.ShapeDtypeStruct((R, C), F32)] * 4,
        compiler_params=_params(("parallel",)),
    )(chip_idx, p, got, w, m, v)


def _adamw_small(gathered, w, m, v, name):
    _, R, C = gathered.shape

    def body(a_ref, w_ref, m_ref, v_ref, g_out, d_out, m_out, v_out):
        g = a_ref[0]
        for k in range(1, N_DEV):
            g = g + a_ref[k]
        delta, m2, v2 = _adamw_math(w_ref[...], g, m_ref[...], v_ref[...])
        g_out[...] = g
        d_out[...] = delta
        m_out[...] = m2
        v_out[...] = v2

    return pl.pallas_call(
        body, name=name, out_shape=[jax.ShapeDtypeStruct((R, C), F32)] * 4,
    )(gathered, w, m, v)


def _rope_tables(S):
    half = ROPE_DIM // 2
    inv_freq = ROPE_THETA ** (-jnp.arange(half, dtype=F32) / half)
    ang = jnp.arange(S, dtype=jnp.int32).astype(F32)[:, None] * inv_freq[None, :]
    cos, sin = jnp.cos(ang), jnp.sin(ang)
    ones = jnp.ones((S, HEAD_DIM - ROPE_DIM), F32)
    zeros = jnp.zeros((S, HEAD_DIM - ROPE_DIM), F32)
    zh = jnp.zeros((S, half), F32)
    c = jnp.concatenate([cos, cos, ones], axis=1)
    s1 = jnp.concatenate([zh, sin, zeros], axis=1)
    s2 = jnp.concatenate([-sin, zh, zeros], axis=1)
    two = lambda t: jnp.concatenate([t, t], axis=1)
    return two(c), two(s1), two(s2)


def _chunk_transposed(a, S):
    return a.reshape(S // SB_CH, SB_CH, HEAD_PAIRS, LANES).transpose(2, 0, 3, 1)


def _flat_shards(ws):
    return jnp.concatenate([w.reshape(-1, D_MODEL) for layer in ws for w in layer], axis=0)


def kernel(x, w_qkv_0, w_o_0, ln1_g_0, ln1_b_0, w_ff1_0, w_ff2_0, ln2_g_0, ln2_b_0, w_qkv_1, w_o_1, ln1_g_1, ln1_b_1, w_ff1_1, w_ff2_1, ln2_g_1, ln2_b_1, loss_target, m_w_qkv_0, m_w_o_0, m_ln1_g_0, m_ln1_b_0, m_w_ff1_0, m_w_ff2_0, m_ln2_g_0, m_ln2_b_0, m_w_qkv_1, m_w_o_1, m_ln1_g_1, m_ln1_b_1, m_w_ff1_1, m_w_ff2_1, m_ln2_g_1, m_ln2_b_1, v_w_qkv_0, v_w_o_0, v_ln1_g_0, v_ln1_b_0, v_w_ff1_0, v_w_ff2_0, v_ln2_g_0, v_ln2_b_0, v_w_qkv_1, v_w_o_1, v_ln1_g_1, v_ln1_b_1, v_w_ff1_1, v_w_ff2_1, v_ln2_g_1, v_ln2_b_1):
    S = x.shape[1]
    x0 = x.reshape(S, D_MODEL)
    target = loss_target.reshape(S, D_MODEL)
    mats = ((w_qkv_0, w_o_0, w_ff1_0, w_ff2_0), (w_qkv_1, w_o_1, w_ff1_1, w_ff2_1))
    mats_m = ((m_w_qkv_0, m_w_o_0, m_w_ff1_0, m_w_ff2_0), (m_w_qkv_1, m_w_o_1, m_w_ff1_1, m_w_ff2_1))
    mats_v = ((v_w_qkv_0, v_w_o_0, v_w_ff1_0, v_w_ff2_0), (v_w_qkv_1, v_w_o_1, v_w_ff1_1, v_w_ff2_1))
    vecs = (ln1_g_0, ln1_b_0, ln2_g_0, ln2_b_0, ln1_g_1, ln1_b_1, ln2_g_1, ln2_b_1)
    vecs_m = (m_ln1_g_0, m_ln1_b_0, m_ln2_g_0, m_ln2_b_0, m_ln1_g_1, m_ln1_b_1, m_ln2_g_1, m_ln2_b_1)
    vecs_v = (v_ln1_g_0, v_ln1_b_0, v_ln2_g_0, v_ln2_b_0, v_ln1_g_1, v_ln1_b_1, v_ln2_g_1, v_ln2_b_1)

    w_flat = _flat_shards(mats)
    w_all = _all_gather(w_flat.astype(BF16), "ag_weights")
    layers = []
    for l in range(N_LAYERS):
        base = l * LAYER_ROWS
        r0, r1, r2, r3 = np.cumsum((0,) + SHARD_ROWS)[:4] + base
        layers.append(dict(
            qkv=w_all[:, r0:r0 + 384].reshape(N_DEV, D_MODEL, 384),
            o=w_all[:, r1:r1 + 128].reshape(D_MODEL, D_MODEL),
            ff1=w_all[:, r2:r2 + 512].reshape(N_DEV, D_MODEL, 512),
            ff2=w_all[:, r3:r3 + 512].reshape(D_FF, D_MODEL),
            g1=vecs[4 * l].reshape(1, D_MODEL), b1=vecs[4 * l + 1].reshape(1, D_MODEL),
            g2=vecs[4 * l + 2].reshape(1, D_MODEL), b2=vecs[4 * l + 3].reshape(1, D_MODEL)))

    rope = _rope_tables(S)
    tmat_later = _sb_tmat(True)
    tmat_upto = _sb_tmat(False)

    saved = []
    xin, xinb = x0, x0.astype(BF16)
    for l, W in enumerate(layers):
        sv = dict(xin=xin, xinb=xinb)
        qkv = _qkv_proj(xinb, W["qkv"], rope if l == 1 else None, Q_SCALE * LOG2E if l == 0 else Q_SCALE,
                        f"qkv_proj_{l}")
        sv["qkv"] = qkv
        if l == 0:
            vT3 = _chunk_transposed(qkv[:, 2 * D_MODEL:], S)
            o, sb_tiles = _sb_fwd(qkv, vT3, tmat_later, "sb_fwd")
            ob = o.astype(BF16)
            sv["sb_tiles"] = sb_tiles
        else:
            q, k, v = qkv[:, :D_MODEL], qkv[:, D_MODEL:2 * D_MODEL], qkv[:, 2 * D_MODEL:]
            outs = [_dil_fwd(q, k, v, d, f"dil_fwd_{d}") for _, d in DILATED_BRANCHES]
            o, ob, lse = _dil_merge([t[0] for t in outs], [t[1] for t in outs], "dil_merge")
            sv.update(q=q, k=k, v=v, lse=lse)
        sv.update(o=o, ob=ob)
        y1, x1, x1b = _mm_res_ln(ob, xin, W["o"], W["g1"], W["b1"], f"attn_out_ln_{l}")
        hpre, h = _ff1(x1b, W["ff1"], f"ff1_{l}")
        y2, x2, x2b = _mm_res_ln(h, x1, W["ff2"], W["g2"], W["b2"], f"ff2_ln_{l}")
        sv.update(y1=y1, x1=x1, x1b=x1b, hpre=hpre, h=h, y2=y2)
        saved.append(sv)
        xin, xinb = x2, x2b

    dout, loss_parts = _loss_grad(xin, target, "loss_grad")
    loss = lax.psum(jnp.sum(loss_parts), MESH_AXES)

    gmats = [None] * N_LAYERS
    gvecs = [None] * (4 * N_LAYERS)
    for l in reversed(range(N_LAYERS)):
        W, sv = layers[l], saved[l]
        dy2, dy2b, gb2 = _ln_bwd(dout, sv["y2"], W["g2"], f"ln2_bwd_{l}")
        dhp = _dh(dy2b, W["ff2"], sv["hpre"], f"dh_{l}")
        g_ff2 = _mm_tn(sv["h"], dy2b, 512, D_MODEL, False, f"dw_ff2_{l}")
        dx1 = _dx_blk(dy2, dhp, W["ff1"], f"dx_ff1_{l}")
        g_ff1 = _mm_tn(sv["x1b"], dhp, D_MODEL, 512, True, f"dw_ff1_{l}")
        dy1, dy1b, gb1 = _ln_bwd(dx1, sv["y1"], W["g1"], f"ln1_bwd_{l}")
        do = _mm_nt_plain(dy1b, W["o"], f"do_{l}")
        g_o = _mm_tn(sv["ob"], dy1b, 512, D_MODEL, False, f"dw_o_{l}")
        if l == 0:
            kT3 = _chunk_transposed(sv["qkv"][:, D_MODEL:2 * D_MODEL], S)
            dq, dk, dv = _sb_bwd(sv["qkv"], kT3, do, sv["sb_tiles"], tmat_upto, "sb_bwd")
            dqkv = jnp.concatenate([dq, dk, dv], axis=1).astype(BF16)
        else:
            dlt = _head_sums(do, sv["o"], "head_sums")
            parts = [_dil_bwd(sv["q"], sv["k"], sv["v"], do, sv["lse"], dlt, d, f"dil_bwd_{d}")
                     for _, d in DILATED_BRANCHES]
            dqkv = _dil_combine(parts, rope, "dil_combine")
        dout = _dx_blk(dy1, dqkv, W["qkv"], f"dx_qkv_{l}")
        g_qkv = _mm_tn(sv["xinb"], dqkv, D_MODEL, 384, True, f"dw_qkv_{l}")
        gmats[l] = (g_qkv.reshape(N_DEV, 384, D_MODEL), g_o.reshape(N_DEV, 128, D_MODEL),
                    g_ff1.reshape(N_DEV, 512, D_MODEL), g_ff2.reshape(N_DEV, 512, D_MODEL))
        gvecs[4 * l:4 * l + 4] = [gb1[0], gb1[1], gb2[0], gb2[1]]
    grad_x = dout.reshape(1, S, D_MODEL)

    cx, cy, cc = _mesh_pos()
    g_all = jnp.concatenate([g for layer in gmats for g in layer], axis=1)
    got_pair = _rs_pair(g_all, "rs_pair")
    chip_part = _pair_add(g_all, got_pair, cc.astype(jnp.int32).reshape(1), "rs_pair_add")
    got_chips = _rs_chips(chip_part, "rs_chips")
    chip_idx = (2 * cx + cy).astype(jnp.int32).reshape(1)
    g_sh, d_sh, m_sh, v_sh = _adamw_shard(chip_part, got_chips, chip_idx, w_flat, _flat_shards(mats_m),
                                          _flat_shards(mats_v), "adamw_mats")

    def unflat(a):
        out, pos = [], 0
        for layer in mats:
            for w in layer:
                n = w.size // D_MODEL
                out.append(a[pos:pos + n].reshape(w.shape))
                pos += n
        return out

    gv_all = _all_gather(jnp.stack(gvecs), "ag_vec_grads")
    g_v, d_v, m_v, v_v = _adamw_small(gv_all, jnp.stack(vecs), jnp.stack(vecs_m), jnp.stack(vecs_v), "adamw_vecs")

    def interleave(mat_list, vec_arr):
        out = []
        for l in range(N_LAYERS):
            qkv_, o_, ff1_, ff2_ = mat_list[4 * l:4 * l + 4]
            out += [qkv_, o_, vec_arr[4 * l], vec_arr[4 * l + 1], ff1_, ff2_, vec_arr[4 * l + 2], vec_arr[4 * l + 3]]
        return out

    return (loss, grad_x, *interleave(unflat(g_sh), g_v), *interleave(unflat(d_sh), d_v),
            *interleave(unflat(m_sh), m_v), *interleave(unflat(v_sh), v_v))
```

```python
import functools
import math

import jax
import jax.numpy as jnp
import numpy as np
from jax import lax
from jax.experimental import pallas as pl
from jax.experimental.pallas import tpu as pltpu

F32 = jnp.float32
BF16 = jnp.bfloat16

D_MODEL = 1024
N_HEADS = 16
HEAD_DIM = 64
D_FF = 4096
N_DEV = 8
N_LAYERS = 2
ROPE_THETA = 500000.0
ROPE_DIM = 16
DILATED_BRANCHES = ((128, 1), (512, 4), (2048, 16))
ALPHA = (2 * N_LAYERS) ** 0.25
LN_EPS = 1e-5
Q_SCALE = 1.0 / math.sqrt(HEAD_DIM)
LOG2E = math.log2(math.e)
LN2 = math.log(2.0)
ADAM_LR, ADAM_B1, ADAM_B2, ADAM_EPS, ADAM_WD, ADAM_STEP = 0.001, 0.9, 0.999, 1e-08, 0.01, 10

LANES = 128
HEAD_PAIRS = D_MODEL // LANES
SB_TQ = 256
SB_CH = 256
SB_STEPS = 4
SB_SAVE_SLOTS = 2 * SB_STEPS
SB_LOAD_SLOTS = 12
SB_LOAD_AHEAD = SB_LOAD_SLOTS - SB_STEPS - 1
DIL_BLK = 128
VMEM_BIG = 56 * 2 ** 20
MESH_AXES = ("x", "y", "c")

SHARD_ROWS = (384, 128, 512, 512)
LAYER_ROWS = sum(SHARD_ROWS)
ALL_ROWS = N_LAYERS * LAYER_ROWS


def _params(sem=None, vmem=None):
    kw = {}
    if sem is not None:
        kw["dimension_semantics"] = sem
    if vmem is not None:
        kw["vmem_limit_bytes"] = vmem
    return pltpu.CompilerParams(**kw)


def _dot(a, b):
    return jnp.dot(a, b, preferred_element_type=F32)


def _dot_nt(a, b):
    return lax.dot_general(a, b, (((1,), (1,)), ((), ())), preferred_element_type=F32)


def _dot_tn(a, b):
    return lax.dot_general(a, b, (((0,), (0,)), ((), ())), preferred_element_type=F32)


def _split3(p):
    hi = p.astype(BF16)
    r1 = p - hi.astype(F32)
    mid = r1.astype(BF16)
    lo = (r1 - mid.astype(F32)).astype(BF16)
    return hi, mid, lo


def _dot3(p, e):
    hi, mid, lo = _split3(p)
    return _dot(hi, e) + _dot(mid, e) + _dot(lo, e)


def _rope_apply(a, c, s1, s2, sign):
    return a * c + sign * (pltpu.roll(a, 8, 1) * s1 + pltpu.roll(a, LANES - 8, 1) * s2)


def _qkv_proj(xb, w_blk, rope, q_mult, name):
    S = xb.shape[0]
    tm = 512
    n_rope = 0 if rope is None else 3

    def body(*refs):
        x_ref, w_ref = refs[:2]
        tabs = [r[...] for r in refs[2:2 + n_rope]]
        o_ref = refs[2 + n_rope]
        x = x_ref[...]
        for j in range(N_DEV):
            acc = _dot(x, w_ref[j])
            for g in range(3):
                col = j * 384 + g * LANES
                a = acc[:, g * LANES:(g + 1) * LANES]
                if n_rope and col < 2 * D_MODEL:
                    a = _rope_apply(a, *tabs, 1.0)
                if col < D_MODEL:
                    a = a * q_mult
                o_ref[:, col:col + LANES] = a.astype(BF16)

    tab_specs = [pl.BlockSpec((tm, LANES), lambda i: (i, 0))] * n_rope
    return pl.pallas_call(
        body, name=name, grid=(S // tm,),
        in_specs=[pl.BlockSpec((tm, D_MODEL), lambda i: (i, 0)),
                  pl.BlockSpec((N_DEV, D_MODEL, 384), lambda i: (0, 0, 0))] + tab_specs,
        out_specs=pl.BlockSpec((tm, 3 * D_MODEL), lambda i: (i, 0)),
        out_shape=jax.ShapeDtypeStruct((S, 3 * D_MODEL), BF16),
        compiler_params=_params(("parallel",), VMEM_BIG),
    )(xb, w_blk, *(rope or ()))


def _layer_norm_rows(y, g, b):
    mu = jnp.mean(y, axis=-1, keepdims=True)
    yc = y - mu
    var = jnp.mean(yc * yc, axis=-1, keepdims=True)
    return yc * lax.rsqrt(var + LN_EPS) * g + b


def _mm_res_ln(a, xres, w, g, b, name):
    S, K = a.shape
    tm = 512 if K <= 1024 else 256

    def body(a_ref, x_ref, w_ref, g_ref, b_ref, y_ref, xn_ref, xb_ref):
        y = ALPHA * x_ref[...] + _dot(a_ref[...], w_ref[...])
        xn = _layer_norm_rows(y, g_ref[...], b_ref[...])
        y_ref[...] = y
        xn_ref[...] = xn
        xb_ref[...] = xn.astype(BF16)

    row = lambda i: (i, 0)
    fix = lambda i: (0, 0)
    return pl.pallas_call(
        body, name=name, grid=(S // tm,),
        in_specs=[pl.BlockSpec((tm, K), row), pl.BlockSpec((tm, D_MODEL), row),
                  pl.BlockSpec((K, D_MODEL), fix), pl.BlockSpec((1, D_MODEL), fix),
                  pl.BlockSpec((1, D_MODEL), fix)],
        out_specs=[pl.BlockSpec((tm, D_MODEL), row)] * 3,
        out_shape=[jax.ShapeDtypeStruct((S, D_MODEL), F32), jax.ShapeDtypeStruct((S, D_MODEL), F32),
                   jax.ShapeDtypeStruct((S, D_MODEL), BF16)],
        compiler_params=_params(("parallel",), VMEM_BIG),
    )(a, xres, w, g, b)


def _ff1(xb, w_blk, name):
    S = xb.shape[0]
    tm = 256

    def body(x_ref, w_ref, hp_ref, h_ref):
        x = x_ref[...]
        for j in range(N_DEV):
            acc = _dot(x, w_ref[j])
            r = jnp.maximum(acc, 0.0)
            hp_ref[:, j * 512:(j + 1) * 512] = acc
            h_ref[:, j * 512:(j + 1) * 512] = (r * r).astype(BF16)

    return pl.pallas_call(
        body, name=name, grid=(S // tm,),
        in_specs=[pl.BlockSpec((tm, D_MODEL), lambda i: (i, 0)),
                  pl.BlockSpec((N_DEV, D_MODEL, 512), lambda i: (0, 0, 0))],
        out_specs=[pl.BlockSpec((tm, D_FF), lambda i: (i, 0))] * 2,
        out_shape=[jax.ShapeDtypeStruct((S, D_FF), F32), jax.ShapeDtypeStruct((S, D_FF), BF16)],
        compiler_params=_params(("parallel",), VMEM_BIG),
    )(xb, w_blk)


def _loss_grad(y, target, name):
    S = y.shape[0]
    tm = 512

    def body(y_ref, t_ref, dy_ref, l_ref):
        @pl.when(pl.program_id(0) == 0)
        def _():
            l_ref[...] = jnp.zeros_like(l_ref)

        err = y_ref[...] - t_ref[...]
        dy_ref[...] = err * (1.0 / D_MODEL)
        sq = err * err
        rows = sq[0:8]
        for r in range(1, tm // 8):
            rows = rows + sq[r * 8:(r + 1) * 8]
        acc = rows[:, 0:LANES]
        for g in range(1, D_MODEL // LANES):
            acc = acc + rows[:, g * LANES:(g + 1) * LANES]
        l_ref[...] += acc * (0.5 / D_MODEL)

    return pl.pallas_call(
        body, name=name, grid=(S // tm,),
        in_specs=[pl.BlockSpec((tm, D_MODEL), lambda i: (i, 0))] * 2,
        out_specs=[pl.BlockSpec((tm, D_MODEL), lambda i: (i, 0)), pl.BlockSpec((8, LANES), lambda i: (0, 0))],
        out_shape=[jax.ShapeDtypeStruct((S, D_MODEL), F32), jax.ShapeDtypeStruct((8, LANES), F32)],
        compiler_params=_params(("arbitrary",)),
    )(y, target)


def _ln_bwd(dout, y, g, name):
    S = y.shape[0]
    tm = 512
    steps = S // tm

    def body(d_ref, y_ref, g_ref, dy_ref, dyb_ref, gb_ref, acc_g, acc_b):
        i = pl.program_id(0)

        @pl.when(i == 0)
        def _():
            acc_g[...] = jnp.zeros_like(acc_g)
            acc_b[...] = jnp.zeros_like(acc_b)

        d = d_ref[...]
        yv = y_ref[...]
        mu = jnp.mean(yv, axis=-1, keepdims=True)
        yc = yv - mu
        var = jnp.mean(yc * yc, axis=-1, keepdims=True)
        rstd = lax.rsqrt(var + LN_EPS)
        xhat = yc * rstd
        dxh = d * g_ref[...]
        m1 = jnp.mean(dxh, axis=-1, keepdims=True)
        m2 = jnp.mean(dxh * xhat, axis=-1, keepdims=True)
        dy = rstd * (dxh - m1 - xhat * m2)
        dy_ref[...] = dy
        dyb_ref[...] = dy.astype(BF16)
        pg = d * xhat
        sg = pg[0:8]
        sb = d[0:8]
        for r in range(1, tm // 8):
            sg = sg + pg[r * 8:(r + 1) * 8]
            sb = sb + d[r * 8:(r + 1) * 8]
        acc_g[...] += sg
        acc_b[...] += sb

        @pl.when(i == steps - 1)
        def _():
            gb_ref[0:1, :] = jnp.sum(acc_g[...], axis=0, keepdims=True)
            gb_ref[1:2, :] = jnp.sum(acc_b[...], axis=0, keepdims=True)

    row = lambda i: (i, 0)
    fix = lambda i: (0, 0)
    return pl.pallas_call(
        body, name=name, grid=(steps,),
        in_specs=[pl.BlockSpec((tm, D_MODEL), row), pl.BlockSpec((tm, D_MODEL), row), pl.BlockSpec((1, D_MODEL), fix)],
        out_specs=[pl.BlockSpec((tm, D_MODEL), row), pl.BlockSpec((tm, D_MODEL), row), pl.BlockSpec((2, D_MODEL), fix)],
        out_shape=[jax.ShapeDtypeStruct((S, D_MODEL), F32), jax.ShapeDtypeStruct((S, D_MODEL), BF16),
                   jax.ShapeDtypeStruct((2, D_MODEL), F32)],
        scratch_shapes=[pltpu.VMEM((8, D_MODEL), F32), pltpu.VMEM((8, D_MODEL), F32)],
        compiler_params=_params(("arbitrary",)),
    )(dout, y, g)


def _dh(dyb, w2, hpre, name):
    S = dyb.shape[0]
    tm = 256
    tn = 512

    def body(dy_ref, w_ref, hp_ref, o_ref):
        dy = dy_ref[...]
        for n in range(0, D_FF, tn):
            dh = _dot_nt(dy, w_ref[n:n + tn, :])
            o_ref[:, n:n + tn] = (dh * (2.0 * jnp.maximum(hp_ref[:, n:n + tn], 0.0))).astype(BF16)

    return pl.pallas_call(
        body, name=name, grid=(S // tm,),
        in_specs=[pl.BlockSpec((tm, D_MODEL), lambda i: (i, 0)), pl.BlockSpec((D_FF, D_MODEL), lambda i: (0, 0)),
                  pl.BlockSpec((tm, D_FF), lambda i: (i, 0))],
        out_specs=pl.BlockSpec((tm, D_FF), lambda i: (i, 0)),
        out_shape=jax.ShapeDtypeStruct((S, D_FF), BF16),
        compiler_params=_params(("parallel",), VMEM_BIG),
    )(dyb, w2, hpre)


def _dx_blk(dres, dz, w_blk, name):
    S, N = dz.shape
    bw = w_blk.shape[2]
    tm = 256

    def body(r_ref, z_ref, w_ref, o_ref):
        acc = ALPHA * r_ref[...]
        for j in range(N_DEV):
            acc = acc + _dot_nt(z_ref[:, j * bw:(j + 1) * bw], w_ref[j])
        o_ref[...] = acc

    return pl.pallas_call(
        body, name=name, grid=(S // tm,),
        in_specs=[pl.BlockSpec((tm, D_MODEL), lambda i: (i, 0)), pl.BlockSpec((tm, N), lambda i: (i, 0)),
                  pl.BlockSpec((N_DEV, D_MODEL, bw), lambda i: (0, 0, 0))],
        out_specs=pl.BlockSpec((tm, D_MODEL), lambda i: (i, 0)),
        out_shape=jax.ShapeDtypeStruct((S, D_MODEL), F32),
        compiler_params=_params(("parallel",), VMEM_BIG),
    )(dres, dz, w_blk)


def _mm_nt_plain(a, w, name):
    S = a.shape[0]
    tm = 512

    def body(a_ref, w_ref, o_ref):
        o_ref[...] = _dot_nt(a_ref[...], w_ref[...])

    return pl.pallas_call(
        body, name=name, grid=(S // tm,),
        in_specs=[pl.BlockSpec((tm, D_MODEL), lambda i: (i, 0)), pl.BlockSpec((D_MODEL, D_MODEL), lambda i: (0, 0))],
        out_specs=pl.BlockSpec((tm, D_MODEL), lambda i: (i, 0)),
        out_shape=jax.ShapeDtypeStruct((S, D_MODEL), F32),
        compiler_params=_params(("parallel",)),
    )(a, w)


def _mm_tn(a, b, ta, tb, blocked, name):
    S, Ka = a.shape
    Nb = b.shape[1]
    ts = 2048

    def body(a_ref, b_ref, o_ref):
        @pl.when(pl.program_id(2) == 0)
        def _():
            o_ref[...] = jnp.zeros_like(o_ref)

        o_ref[...] += _dot_tn(a_ref[...], b_ref[...])

    if blocked:
        out_spec = pl.BlockSpec((None, ta, tb), lambda i, j, s: (j, i, 0))
        out_shape = jax.ShapeDtypeStruct((Nb // tb, Ka, tb), F32)
    else:
        out_spec = pl.BlockSpec((ta, tb), lambda i, j, s: (i, j))
        out_shape = jax.ShapeDtypeStruct((Ka, Nb), F32)
    return pl.pallas_call(
        body, name=name, grid=(Ka // ta, Nb // tb, S // ts),
        in_specs=[pl.BlockSpec((ts, ta), lambda i, j, s: (s, i)), pl.BlockSpec((ts, tb), lambda i, j, s: (s, j))],
        out_specs=out_spec, out_shape=out_shape,
        compiler_params=_params(("parallel", "parallel", "arbitrary"), VMEM_BIG),
    )(a, b)


def _head_sums(do, o, name):
    S = do.shape[0]
    tm = 512
    sel = (np.arange(D_MODEL)[:, None] // HEAD_DIM == np.arange(LANES)[None, :]).astype(np.float32)

    def body(d_ref, o_ref, e_ref, out_ref):
        out_ref[...] = _dot3(d_ref[...] * o_ref[...], e_ref[...])

    return pl.pallas_call(
        body, name=name, grid=(S // tm,),
        in_specs=[pl.BlockSpec((tm, D_MODEL), lambda i: (i, 0))] * 2 + [pl.BlockSpec((D_MODEL, LANES), lambda i: (0, 0))],
        out_specs=pl.BlockSpec((tm, LANES), lambda i: (i, 0)),
        out_shape=jax.ShapeDtypeStruct((S, LANES), F32),
        compiler_params=_params(("parallel",)),
    )(do, o, jnp.asarray(sel, BF16))


def _sb_tmat(later):
    r = np.arange(SB_CH)
    t = (r[None, :] > r[:, None]) if later else (r[None, :] <= r[:, None])
    return jnp.asarray(np.concatenate([t.astype(np.float32), np.ones((8, SB_CH), np.float32)], axis=0), BF16)


def _sb_gates(z2):
    neg_abs = lax.bitcast_convert_type(lax.bitcast_convert_type(z2, jnp.uint32) | jnp.uint32(0x80000000), F32)
    l1 = jnp.log2(1.0 + jnp.exp2(neg_abs))
    a = jnp.minimum(z2, 0.0) - l1
    return a, a - z2


def _head_masks(x2):
    lane = lax.broadcasted_iota(jnp.int32, x2.shape, 1)
    zero = jnp.zeros_like(x2)
    return jnp.where(lane < HEAD_DIM, x2, zero), jnp.where(lane >= HEAD_DIM, x2, zero)


def _sb_fwd(qkv, vT3, tmat, name):
    S = qkv.shape[0]
    nq = S // SB_TQ
    nch = S // SB_CH
    ns = SB_SAVE_SLOTS

    def body(q_ref, k_ref, vT_ref, t_ref, o_ref, ws_hbm, z_scr, a_scr, cum_scr, oT_scr, stage, sems):
        hp = pl.program_id(0)
        i = pl.program_id(1)
        base = (i * (i + 1)) // 2
        qm = _head_masks(q_ref[...])

        def save(src, sem, c):
            return pltpu.make_async_copy(src, ws_hbm.at[hp, base + c], sem)

        causal = (lax.broadcasted_iota(jnp.int32, (SB_CH, SB_TQ), 0)
                  < lax.broadcasted_iota(jnp.int32, (SB_CH, SB_TQ), 1))

        def head_rows(vTc, h):
            return vTc[h * HEAD_DIM:(h + 1) * HEAD_DIM, :]

        @pl.when(jnp.logical_and(hp == 0, i == 0))
        def _():
            z_scr[...] = jnp.zeros_like(z_scr)
            a_scr[...] = jnp.zeros_like(a_scr)
            cum_scr[...] = jnp.zeros_like(cum_scr)

        oT_scr[...] = jnp.zeros_like(oT_scr)

        def c_valid(t):
            return jnp.logical_and(t >= 2, t - 2 <= i)

        def c_chunk(t):
            return jnp.clip(i + 2 - t, 0, nch - 1)

        def step(t, p, slot, R, own_b, own_c):
            cA = jnp.maximum(i - t, 0)
            kA = k_ref[pl.ds(pl.multiple_of(cA * SB_CH, SB_CH), SB_CH), :]
            valid = c_valid(t)
            vC = vT_ref[c_chunk(t)]
            out = []
            for h in range(2):
                z_scr[p, h] = _dot_nt(kA, qm[h])
                a, lf = _sb_gates(z_scr[1 - p, h])
                if own_b:
                    lf = jnp.where(causal, lf, 0.0)
                a_scr[1 - p, h] = a
                cum_scr[1 - p, h] = _dot(t_ref[...], lf.astype(BF16))
                a_c = a_scr[p, h]
                w = jnp.exp2(a_c + cum_scr[p, h, :SB_CH, :] + R[h])
                if own_c:
                    w = jnp.where(causal, w, 0.0)
                wb = w.astype(BF16)
                stage[slot, 2 * h] = wb
                stage[slot, 2 * h + 1] = a_c.astype(BF16)
                oT_scr[h] += jnp.where(valid, _dot(head_rows(vC, h), wb), 0.0)
                out.append(R[h] + jnp.where(valid, cum_scr[p, h, SB_CH:SB_CH + 1, :], 0.0))
            return tuple(out)

        def trip(tt, R, first):
            half = 0 if first else lax.rem(tt, 2) * SB_STEPS
            if not first:
                for j in range(SB_STEPS):
                    @pl.when(c_valid(SB_STEPS * tt + j - ns))
                    def _():
                        save(stage.at[half + j], sems.at[half + j], 0).wait()

            for j in range(SB_STEPS):
                R = step(SB_STEPS * tt + j, j % 2, half + j, R, first and j == 1, first and j == 2)
            for j in range(SB_STEPS):
                t = SB_STEPS * tt + j

                @pl.when(c_valid(t))
                def _():
                    save(stage.at[half + j], sems.at[half + j], c_chunk(t)).start()

            return R

        z1 = jnp.zeros((1, SB_TQ), F32)
        trips = (i + 3 + SB_STEPS - 1) // SB_STEPS
        lax.fori_loop(1, trips, lambda tt, R: trip(tt, R, False), trip(0, (z1, z1), True))
        for back in (2, 1):
            tl = trips - back
            half = lax.rem(tl + 2, 2) * SB_STEPS
            for j in range(SB_STEPS):
                @pl.when(c_valid(SB_STEPS * tl + j))
                def _():
                    save(stage.at[half + j], sems.at[half + j], 0).wait()

        o_ref[...] = jnp.concatenate([oT_scr[0], oT_scr[1]], axis=0).T

    ntile = nq * (nq + 1) // 2
    return pl.pallas_call(
        body, name=name, grid=(HEAD_PAIRS, nq),
        in_specs=[pl.BlockSpec((SB_TQ, LANES), lambda hp, i: (i, hp)),
                  pl.BlockSpec((S, LANES), lambda hp, i: (0, HEAD_PAIRS + hp)),
                  pl.BlockSpec((None, nch, LANES, SB_CH), lambda hp, i: (hp, 0, 0, 0)),
                  pl.BlockSpec((SB_CH + 8, SB_CH), lambda hp, i: (0, 0))],
        out_specs=[pl.BlockSpec((SB_TQ, LANES), lambda hp, i: (i, hp)), pl.BlockSpec(memory_space=pl.ANY)],
        out_shape=[jax.ShapeDtypeStruct((S, D_MODEL), F32),
                   jax.ShapeDtypeStruct((HEAD_PAIRS, ntile, 4, SB_CH, SB_TQ), BF16)],
        scratch_shapes=[pltpu.VMEM((2, 2, SB_CH, SB_TQ), F32), pltpu.VMEM((2, 2, SB_CH, SB_TQ), F32),
                        pltpu.VMEM((2, 2, SB_CH + 8, SB_TQ), F32), pltpu.VMEM((2, HEAD_DIM, SB_TQ), F32),
                        pltpu.VMEM((ns, 4, SB_CH, SB_TQ), BF16), pltpu.SemaphoreType.DMA((ns,))],
        compiler_params=_params(("arbitrary", "arbitrary"), VMEM_BIG),
    )(qkv, qkv, vT3, tmat)


def _sb_bwd(qkv, kT3, do, ws, tmat_g, name):
    S = qkv.shape[0]
    nq = S // SB_TQ
    nch = S // SB_CH
    nl = SB_LOAD_SLOTS
    ahead = SB_LOAD_AHEAD

    def body(q_ref, do_ref, v_ref, kT_ref, tg_ref, ws_hbm, dq_ref, dk_hbm, dv_hbm, dk_acc, dv_acc, sems,
             dwv_scr, g_scr, sig_scr, cumg_scr, dqT_scr, ring, ring_sems):
        hp = pl.program_id(0)
        i = pl.program_id(1)
        base = (i * (i + 1)) // 2

        @pl.when(i == 0)
        def _():
            dk_acc[...] = jnp.zeros_like(dk_acc)
            dv_acc[...] = jnp.zeros_like(dv_acc)

        @pl.when(jnp.logical_and(hp == 0, i == 0))
        def _():
            for scr in (dwv_scr, g_scr, sig_scr, cumg_scr, ring):
                scr[...] = jnp.zeros_like(scr)

        def load(u):
            slot = lax.rem(u, nl)
            return pltpu.make_async_copy(ws_hbm.at[hp, base + u], ring.at[slot], ring_sems.at[slot])

        for u in range(ahead):
            @pl.when(u <= i)
            def _():
                load(u).start()

        dqT_scr[...] = jnp.zeros_like(dqT_scr)
        qm = _head_masks(q_ref[...])
        dom = _head_masks(do_ref[...].astype(BF16))
        causal = (lax.broadcasted_iota(jnp.int32, (SB_CH, SB_TQ), 0)
                  < lax.broadcasted_iota(jnp.int32, (SB_CH, SB_TQ), 1))

        def rows_of(c):
            return pl.ds(pl.multiple_of(c * SB_CH, SB_CH), SB_CH)

        def head_rows(kTc, h):
            return kTc[h * HEAD_DIM:(h + 1) * HEAD_DIM, :]

        def step(t, p, Gs):
            q = 1 - p
            valid_b = jnp.logical_and(t >= 1, t - 1 <= i)
            valid_c = jnp.logical_and(t >= 2, t - 2 <= i)
            c_b = jnp.clip(t - 1, 0, i)
            c_c = jnp.clip(t - 2, 0, i)
            slot = jnp.where(valid_b, lax.rem(jnp.maximum(t - 1, 0), nl), nl)
            vA = v_ref[rows_of(jnp.minimum(t, i)), :]
            kTc = kT_ref[c_c]
            keep = jnp.logical_or(causal, t - 2 != i)
            out = []
            for h in range(2):
                dwv_scr[p, h] = _dot_nt(vA, dom[h])

                wb = ring[slot, 2 * h]
                g = wb.astype(F32) * dwv_scr[q, h]
                g_scr[q, h] = g
                sig_scr[q, h] = jnp.exp2(ring[slot, 2 * h + 1].astype(F32))
                cumg_scr[q, h] = _dot(tg_ref[...], g.astype(BF16))
                dv_h = _dot(wb, dom[h])

                dz = g_scr[p, h] - sig_scr[p, h] * (Gs[h] + cumg_scr[p, h, :SB_CH, :])
                dzb = jnp.where(keep, dz, 0.0).astype(BF16)
                dk_h = _dot(dzb, qm[h])
                dqT_scr[h] += jnp.where(valid_c, _dot(head_rows(kTc, h), dzb), 0.0)
                out.append(Gs[h] + jnp.where(valid_c, cumg_scr[p, h, SB_CH:SB_CH + 1, :], 0.0))
                dk_c = dk_h if h == 0 else dk_c + dk_h
                dv_c = dv_h if h == 0 else dv_c + dv_h
            dv_acc[rows_of(c_b), :] += jnp.where(valid_b, dv_c, 0.0)
            dk_acc[rows_of(c_c), :] += jnp.where(valid_c, dk_c, 0.0)
            return tuple(out)

        def trip(tt, Gs):
            for j in range(SB_STEPS):
                t = SB_STEPS * tt + j

                @pl.when(jnp.logical_and(t >= 1, t - 1 <= i))
                def _():
                    load(t - 1).wait()

            for j in range(SB_STEPS):
                t = SB_STEPS * tt + j

                @pl.when(t + ahead <= i)
                def _():
                    load(t + ahead).start()

            for j in range(SB_STEPS):
                Gs = step(SB_STEPS * tt + j, j % 2, Gs)
            return Gs

        z1 = jnp.zeros((1, SB_TQ), F32)
        lax.fori_loop(0, (i + 3 + SB_STEPS - 1) // SB_STEPS, trip, (z1, z1))
        dq_ref[...] = jnp.concatenate([dqT_scr[0], dqT_scr[1]], axis=0).T * Q_SCALE

        @pl.when(i == nq - 1)
        def _():
            dk_acc[...] = dk_acc[...] * LN2
            cols = pl.ds(pl.multiple_of(hp * LANES, LANES), LANES)
            ck = pltpu.make_async_copy(dk_acc, dk_hbm.at[:, cols], sems.at[0])
            cv = pltpu.make_async_copy(dv_acc, dv_hbm.at[:, cols], sems.at[1])
            ck.start()
            cv.start()
            ck.wait()
            cv.wait()

    blk = lambda hp, i: (i, hp)
    return pl.pallas_call(
        body, name=name, grid=(HEAD_PAIRS, nq),
        in_specs=[pl.BlockSpec((SB_TQ, LANES), blk),
                  pl.BlockSpec((SB_TQ, LANES), blk),
                  pl.BlockSpec((S, LANES), lambda hp, i: (0, 2 * HEAD_PAIRS + hp)),
                  pl.BlockSpec((None, nch, LANES, SB_CH), lambda hp, i: (hp, 0, 0, 0)),
                  pl.BlockSpec((SB_CH + 8, SB_CH), lambda hp, i: (0, 0)),
                  pl.BlockSpec(memory_space=pl.ANY)],
        out_specs=[pl.BlockSpec((SB_TQ, LANES), blk), pl.BlockSpec(memory_space=pl.ANY),
                   pl.BlockSpec(memory_space=pl.ANY)],
        out_shape=[jax.ShapeDtypeStruct((S, D_MODEL), F32)] * 3,
        scratch_shapes=[pltpu.VMEM((S, LANES), F32), pltpu.VMEM((S, LANES), F32), pltpu.SemaphoreType.DMA((2,))]
        + [pltpu.VMEM((2, 2, SB_CH, SB_TQ), F32)] * 3
        + [pltpu.VMEM((2, 2, SB_CH + 8, SB_TQ), F32), pltpu.VMEM((2, HEAD_DIM, SB_TQ), F32)]
        + [pltpu.VMEM((nl + 1, 4, SB_CH, SB_TQ), BF16), pltpu.SemaphoreType.DMA((nl,))],
        compiler_params=_params(("arbitrary", "arbitrary"), VMEM_BIG),
    )(qkv, do, qkv, kT3, tmat_g, ws)


def _dil_valid(n):
    qi = lax.broadcasted_iota(jnp.int32, (DIL_BLK, 2 * DIL_BLK), 0)
    kj = lax.broadcasted_iota(jnp.int32, (DIL_BLK, 2 * DIL_BLK), 1)
    dist = DIL_BLK + qi - kj
    return (dist >= 0) & (dist <= DIL_BLK) & ((n > 0) | (kj >= DIL_BLK))


def _lane_pick(tile, idx):
    lane = lax.broadcasted_iota(jnp.int32, tile.shape, 1)
    return jnp.sum(jnp.where(lane == idx, tile, 0.0), axis=-1, keepdims=True)


def _dil_specs(d, width):
    cur = pl.BlockSpec((DIL_BLK, width), lambda r, n: (n, r))
    prev = pl.BlockSpec((DIL_BLK, width), lambda r, n: (jnp.maximum(n - 1, 0), r))
    return cur, prev


def _dil_fwd(q, k, v, d, name):
    S = q.shape[0]
    L = S // d
    nb = L // DIL_BLK

    def body(q_ref, kc_ref, kp_ref, vc_ref, vp_ref, o_ref, lse_ref):
        valid = _dil_valid(pl.program_id(1))
        lane = lax.broadcasted_iota(jnp.int32, (DIL_BLK, LANES), 1)
        lse_t = jnp.zeros((DIL_BLK, LANES), F32)
        for hp in range(HEAD_PAIRS):
            cols = slice(hp * LANES, (hp + 1) * LANES)
            qm = _head_masks(q_ref[:, cols])
            kk = jnp.concatenate([kp_ref[:, cols], kc_ref[:, cols]], axis=0)
            vm = _head_masks(jnp.concatenate([vp_ref[:, cols], vc_ref[:, cols]], axis=0))
            o2 = None
            for h in range(2):
                s = jnp.where(valid, _dot_nt(qm[h], kk), -1e30)
                m = jnp.max(s, axis=-1, keepdims=True)
                p = jnp.exp(s - m)
                den = jnp.sum(p, axis=-1, keepdims=True)
                oh = _dot(p.astype(BF16), vm[h]) / den
                o2 = oh if o2 is None else o2 + oh
                lse_t = jnp.where(lane == 2 * hp + h, m + jnp.log(den), lse_t)
            o_ref[:, cols] = o2
        lse_ref[...] = lse_t

    cur, prev = _dil_specs(d, D_MODEL)
    lcur, _ = _dil_specs(d, LANES)
    view = lambda a: a.reshape(L, d * a.shape[1])
    o, lse = pl.pallas_call(
        body, name=name, grid=(d, nb),
        in_specs=[cur, cur, prev, cur, prev],
        out_specs=[cur, lcur],
        out_shape=[jax.ShapeDtypeStruct((L, d * D_MODEL), F32), jax.ShapeDtypeStruct((L, d * LANES), F32)],
        compiler_params=_params(("parallel", "parallel")),
    )(view(q), view(k), view(k), view(v), view(v))
    return o.reshape(S, D_MODEL), lse.reshape(S, LANES)


def _head_expand():
    return jnp.asarray((np.arange(LANES)[:, None] == np.arange(D_MODEL)[None, :] // HEAD_DIM).astype(np.float32), BF16)


def _dil_merge(os_, lses, name):
    S = os_[0].shape[0]
    tm = 256
    nbr = len(os_)

    def body(*refs):
        o_refs, l_refs, e_ref = refs[:nbr], refs[nbr:2 * nbr], refs[2 * nbr]
        out_ref, outb_ref, lse_ref = refs[2 * nbr + 1:]
        ls = [r[...] for r in l_refs]
        m = ls[0]
        for l in ls[1:]:
            m = jnp.maximum(m, l)
        tot = jnp.exp(ls[0] - m)
        for l in ls[1:]:
            tot = tot + jnp.exp(l - m)
        lse = m + jnp.log(tot)
        acc = None
        for o_r, l in zip(o_refs, ls):
            wt = _dot3(jnp.exp(l - lse), e_ref[...])
            term = wt * o_r[...]
            acc = term if acc is None else acc + term
        out_ref[...] = acc
        outb_ref[...] = acc.astype(BF16)
        lse_ref[...] = lse

    row = lambda i: (i, 0)
    return pl.pallas_call(
        body, name=name, grid=(S // tm,),
        in_specs=[pl.BlockSpec((tm, D_MODEL), row)] * nbr + [pl.BlockSpec((tm, LANES), row)] * nbr
        + [pl.BlockSpec((LANES, D_MODEL), lambda i: (0, 0))],
        out_specs=[pl.BlockSpec((tm, D_MODEL), row), pl.BlockSpec((tm, D_MODEL), row), pl.BlockSpec((tm, LANES), row)],
        out_shape=[jax.ShapeDtypeStruct((S, D_MODEL), F32), jax.ShapeDtypeStruct((S, D_MODEL), BF16),
                   jax.ShapeDtypeStruct((S, LANES), F32)],
        compiler_params=_params(("parallel",)),
    )(*os_, *lses, _head_expand())


def _dil_bwd(q, k, v, do, lse, dlt, d, name):
    S = q.shape[0]
    L = S // d
    nb = L // DIL_BLK

    def body(q_ref, kc_ref, kp_ref, vc_ref, vp_ref, do_ref, lse_ref, dl_ref,
             dq_ref, dka_ref, dkb_ref, dva_ref, dvb_ref):
        valid = _dil_valid(pl.program_id(1))
        lse_t = lse_ref[...]
        dl_t = dl_ref[...]
        for hp in range(HEAD_PAIRS):
            cols = slice(hp * LANES, (hp + 1) * LANES)
            qm = _head_masks(q_ref[:, cols])
            dom = _head_masks(do_ref[:, cols].astype(BF16))
            kk = jnp.concatenate([kp_ref[:, cols], kc_ref[:, cols]], axis=0)
            vv = jnp.concatenate([vp_ref[:, cols], vc_ref[:, cols]], axis=0)
            km = _head_masks(kk)
            dq2 = dkk = dvv = None
            for h in range(2):
                s = _dot_nt(qm[h], kk)
                p = jnp.where(valid, jnp.exp(s - _lane_pick(lse_t, 2 * hp + h)), 0.0)
                ds = (p * (_dot_nt(dom[h], vv) - _lane_pick(dl_t, 2 * hp + h))).astype(BF16)
                t_q = _dot(ds, km[h])
                t_k = _dot_tn(ds, qm[h])
                t_v = _dot_tn(p.astype(BF16), dom[h])
                dq2 = t_q if dq2 is None else dq2 + t_q
                dkk = t_k if dkk is None else dkk + t_k
                dvv = t_v if dvv is None else dvv + t_v
            dq_ref[:, cols] = dq2
            dkb_ref[:, cols] = dkk[:DIL_BLK]
            dka_ref[:, cols] = dkk[DIL_BLK:]
            dvb_ref[:, cols] = dvv[:DIL_BLK]
            dva_ref[:, cols] = dvv[DIL_BLK:]

    cur, prev = _dil_specs(d, D_MODEL)
    lcur, _ = _dil_specs(d, LANES)
    view = lambda a: a.reshape(L, d * a.shape[1])
    outs = pl.pallas_call(
        body, name=name, grid=(d, nb),
        in_specs=[cur, cur, prev, cur, prev, cur, lcur, lcur],
        out_specs=[cur] * 5,
        out_shape=[jax.ShapeDtypeStruct((L, d * D_MODEL), F32)] * 5,
        compiler_params=_params(("parallel", "parallel"), VMEM_BIG),
    )(view(q), view(k), view(k), view(v), view(v), view(do), view(lse), view(dlt))
    return [o.reshape(S, D_MODEL) for o in outs]


def _dil_combine(parts, rope, name):
    S = parts[0][0].shape[0]
    tm = DIL_BLK
    nblk = S // tm
    dils = [d for _, d in DILATED_BRANCHES]

    def body(*refs):
        ins = refs[:5 * len(dils)]
        c_ref, s1_ref, s2_ref, o_ref = refs[5 * len(dils):]
        i = pl.program_id(0)
        tabs = (c_ref[...], s1_ref[...], s2_ref[...])
        dq = dk = dv = None
        for b, d in enumerate(dils):
            dq_r, dka_r, dkb_r, dva_r, dvb_r = ins[5 * b:5 * b + 5]
            live = (i + d < nblk).astype(F32)
            tq = dq_r[...]
            tk = dka_r[...] + live * dkb_r[...]
            tv = dva_r[...] + live * dvb_r[...]
            dq = tq if dq is None else dq + tq
            dk = tk if dk is None else dk + tk
            dv = tv if dv is None else dv + tv
        dq = dq * Q_SCALE
        for g in range(HEAD_PAIRS):
            cols = slice(g * LANES, (g + 1) * LANES)
            o_ref[:, g * LANES:(g + 1) * LANES] = _rope_apply(dq[:, cols], *tabs, -1.0).astype(BF16)
            o_ref[:, D_MODEL + g * LANES:D_MODEL + (g + 1) * LANES] = _rope_apply(dk[:, cols], *tabs, -1.0).astype(BF16)
        o_ref[:, 2 * D_MODEL:] = dv.astype(BF16)

    row = pl.BlockSpec((tm, D_MODEL), lambda i: (i, 0))
    in_specs = []
    args = []
    for (dq_b, dka, dkb, dva, dvb), d in zip(parts, dils):
        ahead = pl.BlockSpec((tm, D_MODEL), lambda i, d=d: (jnp.minimum(i + d, nblk - 1), 0))
        in_specs += [row, row, ahead, row, ahead]
        args += [dq_b, dka, dkb, dva, dvb]
    in_specs += [pl.BlockSpec((tm, LANES), lambda i: (i, 0))] * 3
    return pl.pallas_call(
        body, name=name, grid=(nblk,),
        in_specs=in_specs,
        out_specs=pl.BlockSpec((tm, 3 * D_MODEL), lambda i: (i, 0)),
        out_shape=jax.ShapeDtypeStruct((S, 3 * D_MODEL), BF16),
        compiler_params=_params(("parallel",), VMEM_BIG),
    )(*args, *rope)


def _mesh_pos():
    return lax.axis_index("x"), lax.axis_index("y"), lax.axis_index("c")


def _all_gather(shard, name):
    R, C = shard.shape

    def body(x_ref, out_ref, send_sems, recv_sems, local_sem):
        x, y, c = _mesh_pos()
        me, sibling = (x, y, c), (x, y, 1 - c)
        chips = [(1 - x, y), (x, 1 - y), (1 - x, 1 - y)]

        def blk(p):
            return out_ref.at[4 * p[0] + 2 * p[1] + p[2]]

        def copy(k, block, to, src=None):
            return pltpu.make_async_remote_copy(
                src_ref=blk(block) if src is None else src, dst_ref=blk(block),
                send_sem=send_sems.at[k], recv_sem=recv_sems.at[k],
                device_id=to, device_id_type=pl.DeviceIdType.MESH)

        mine = pltpu.make_async_copy(x_ref, blk(me), local_sem)
        mine.start()
        first = [copy(0, me, sibling, src=x_ref)]
        first += [copy(1 + j, me, (*chip, c), src=x_ref) for j, chip in enumerate(chips)]
        for cp in first:
            cp.start()
        passed = [copy(4 + j, (*chip, c), sibling) for j, chip in enumerate(chips)]
        for j, chip in enumerate(chips):
            copy(1 + j, (*chip, c), me).wait_recv()
            passed[j].start()
        copy(0, sibling, me).wait_recv()
        for j, chip in enumerate(chips):
            copy(4 + j, (*chip, 1 - c), me).wait_recv()
        for cp in first + passed:
            cp.wait_send()
        mine.wait()

    return pl.pallas_call(
        body, name=name,
        in_specs=[pl.BlockSpec(memory_space=pl.ANY)],
        out_specs=pl.BlockSpec(memory_space=pl.ANY),
        out_shape=jax.ShapeDtypeStruct((N_DEV, R, C), shard.dtype),
        scratch_shapes=[pltpu.SemaphoreType.DMA((7,)), pltpu.SemaphoreType.DMA((7,)), pltpu.SemaphoreType.DMA],
    )(shard)


def _rs_pair(g, name):
    _, R, C = g.shape

    def body(g_ref, out_ref, send_sems, recv_sems):
        x, y, c = _mesh_pos()
        sibling = (x, y, 1 - c)
        cps = []
        for chip in range(4):
            cps.append(pltpu.make_async_remote_copy(
                src_ref=g_ref.at[2 * chip + (1 - c)], dst_ref=out_ref.at[chip],
                send_sem=send_sems.at[chip], recv_sem=recv_sems.at[chip],
                device_id=sibling, device_id_type=pl.DeviceIdType.MESH))
        for cp in cps:
            cp.start()
        for cp in cps:
            cp.wait_recv()
        for cp in cps:
            cp.wait_send()

    return pl.pallas_call(
        body, name=name,
        in_specs=[pl.BlockSpec(memory_space=pl.ANY)],
        out_specs=pl.BlockSpec(memory_space=pl.ANY),
        out_shape=jax.ShapeDtypeStruct((4, R, C), g.dtype),
        scratch_shapes=[pltpu.SemaphoreType.DMA((4,)), pltpu.SemaphoreType.DMA((4,))],
    )(g)


def _pair_add(g, got, cidx, name):
    _, R, C = g.shape
    tr = 256

    def body(c_ref, g_ref, r_ref, o_ref):
        o_ref[...] = g_ref[...] + r_ref[...]

    return pl.pallas_call(
        body, name=name,
        grid_spec=pltpu.PrefetchScalarGridSpec(
            num_scalar_prefetch=1, grid=(4, R // tr),
            in_specs=[pl.BlockSpec((None, tr, C), lambda k, i, c: (2 * k + c[0], i, 0)),
                      pl.BlockSpec((None, tr, C), lambda k, i, c: (k, i, 0))],
            out_specs=pl.BlockSpec((None, tr, C), lambda k, i, c: (k, i, 0))),
        out_shape=jax.ShapeDtypeStruct((4, R, C), g.dtype),
        compiler_params=_params(("parallel", "parallel")),
    )(cidx, g, got)


def _rs_chips(p, name):
    _, R, C = p.shape

    def body(p_ref, out_ref, send_sems, recv_sems):
        x, y, c = _mesh_pos()
        chips = [(1 - x, y), (x, 1 - y), (1 - x, 1 - y)]
        cps = []
        for j, (cx, cy) in enumerate(chips):
            cps.append(pltpu.make_async_remote_copy(
                src_ref=p_ref.at[2 * cx + cy], dst_ref=out_ref.at[j],
                send_sem=send_sems.at[j], recv_sem=recv_sems.at[j],
                device_id=(cx, cy, c), device_id_type=pl.DeviceIdType.MESH))
        for cp in cps:
            cp.start()
        for cp in cps:
            cp.wait_recv()
        for cp in cps:
            cp.wait_send()

    return pl.pallas_call(
        body, name=name,
        in_specs=[pl.BlockSpec(memory_space=pl.ANY)],
        out_specs=pl.BlockSpec(memory_space=pl.ANY),
        out_shape=jax.ShapeDtypeStruct((3, R, C), p.dtype),
        scratch_shapes=[pltpu.SemaphoreType.DMA((3,)), pltpu.SemaphoreType.DMA((3,))],
    )(p)


def _adamw_math(w, g, m, v):
    m2 = ADAM_B1 * m + (1.0 - ADAM_B1) * g
    v2 = ADAM_B2 * v + (1.0 - ADAM_B2) * (g * g)
    m_hat = m2 / (1.0 - ADAM_B1 ** ADAM_STEP)
    v_hat = v2 / (1.0 - ADAM_B2 ** ADAM_STEP)
    delta = -ADAM_LR * (m_hat / (jnp.sqrt(v_hat) + ADAM_EPS) + ADAM_WD * w)
    return delta, m2, v2


def _adamw_shard(p, got, chip_idx, w, m, v, name):
    R, C = w.shape
    tr = 256

    def body(k_ref, p_ref, r_ref, w_ref, m_ref, v_ref, g_out, d_out, m_out, v_out):
        g = ((p_ref[...] + r_ref[0]) + r_ref[1]) + r_ref[2]
        delta, m2, v2 = _adamw_math(w_ref[...], g, m_ref[...], v_ref[...])
        g_out[...] = g
        d_out[...] = delta
        m_out[...] = m2
        v_out[...] = v2

    row = pl.BlockSpec((tr, C), lambda i, k: (i, 0))
    return pl.pallas_call(
        body, name=name,
        grid_spec=pltpu.PrefetchScalarGridSpec(
            num_scalar_prefetch=1, grid=(R // tr,),
            in_specs=[pl.BlockSpec((None, tr, C), lambda i, k: (k[0], i, 0)),
                      pl.BlockSpec((3, tr, C), lambda i, k: (0, i, 0)), row, row, row],
            out_specs=[row] * 4),
        out_shape=[jax.ShapeDtypeStruct((R, C), F32)] * 4,
        compiler_params=_params(("parallel",)),
    )(chip_idx, p, got, w, m, v)


def _adamw_small(gathered, w, m, v, name):
    _, R, C = gathered.shape

    def body(a_ref, w_ref, m_ref, v_ref, g_out, d_out, m_out, v_out):
        g = a_ref[0]
        for k in range(1, N_DEV):
            g = g + a_ref[k]
        delta, m2, v2 = _adamw_math(w_ref[...], g, m_ref[...], v_ref[...])
        g_out[...] = g
        d_out[...] = delta
        m_out[...] = m2
        v_out[...] = v2

    return pl.pallas_call(
        body, name=name, out_shape=[jax.ShapeDtypeStruct((R, C), F32)] * 4,
    )(gathered, w, m, v)


def _rope_tables(S):
    half = ROPE_DIM // 2
    inv_freq = ROPE_THETA ** (-jnp.arange(half, dtype=F32) / half)
    ang = jnp.arange(S, dtype=jnp.int32).astype(F32)[:, None] * inv_freq[None, :]
    cos, sin = jnp.cos(ang), jnp.sin(ang)
    ones = jnp.ones((S, HEAD_DIM - ROPE_DIM), F32)
    zeros = jnp.zeros((S, HEAD_DIM - ROPE_DIM), F32)
    zh = jnp.zeros((S, half), F32)
    c = jnp.concatenate([cos, cos, ones], axis=1)
    s1 = jnp.concatenate([zh, sin, zeros], axis=1)
    s2 = jnp.concatenate([-sin, zh, zeros], axis=1)
    two = lambda t: jnp.concatenate([t, t], axis=1)
    return two(c), two(s1), two(s2)


def _chunk_transposed(a, S):
    return a.reshape(S // SB_CH, SB_CH, HEAD_PAIRS, LANES).transpose(2, 0, 3, 1)


def _flat_shards(ws):
    return jnp.concatenate([w.reshape(-1, D_MODEL) for layer in ws for w in layer], axis=0)


def kernel(x, w_qkv_0, w_o_0, ln1_g_0, ln1_b_0, w_ff1_0, w_ff2_0, ln2_g_0, ln2_b_0, w_qkv_1, w_o_1, ln1_g_1, ln1_b_1, w_ff1_1, w_ff2_1, ln2_g_1, ln2_b_1, loss_target, m_w_qkv_0, m_w_o_0, m_ln1_g_0, m_ln1_b_0, m_w_ff1_0, m_w_ff2_0, m_ln2_g_0, m_ln2_b_0, m_w_qkv_1, m_w_o_1, m_ln1_g_1, m_ln1_b_1, m_w_ff1_1, m_w_ff2_1, m_ln2_g_1, m_ln2_b_1, v_w_qkv_0, v_w_o_0, v_ln1_g_0, v_ln1_b_0, v_w_ff1_0, v_w_ff2_0, v_ln2_g_0, v_ln2_b_0, v_w_qkv_1, v_w_o_1, v_ln1_g_1, v_ln1_b_1, v_w_ff1_1, v_w_ff2_1, v_ln2_g_1, v_ln2_b_1):
    S = x.shape[1]
    x0 = x.reshape(S, D_MODEL)
    target = loss_target.reshape(S, D_MODEL)
    mats = ((w_qkv_0, w_o_0, w_ff1_0, w_ff2_0), (w_qkv_1, w_o_1, w_ff1_1, w_ff2_1))
    mats_m = ((m_w_qkv_0, m_w_o_0, m_w_ff1_0, m_w_ff2_0), (m_w_qkv_1, m_w_o_1, m_w_ff1_1, m_w_ff2_1))
    mats_v = ((v_w_qkv_0, v_w_o_0, v_w_ff1_0, v_w_ff2_0), (v_w_qkv_1, v_w_o_1, v_w_ff1_1, v_w_ff2_1))
    vecs = (ln1_g_0, ln1_b_0, ln2_g_0, ln2_b_0, ln1_g_1, ln1_b_1, ln2_g_1, ln2_b_1)
    vecs_m = (m_ln1_g_0, m_ln1_b_0, m_ln2_g_0, m_ln2_b_0, m_ln1_g_1, m_ln1_b_1, m_ln2_g_1, m_ln2_b_1)
    vecs_v = (v_ln1_g_0, v_ln1_b_0, v_ln2_g_0, v_ln2_b_0, v_ln1_g_1, v_ln1_b_1, v_ln2_g_1, v_ln2_b_1)

    w_flat = _flat_shards(mats)
    w_all = _all_gather(w_flat.astype(BF16), "ag_weights")
    layers = []
    for l in range(N_LAYERS):
        base = l * LAYER_ROWS
        r0, r1, r2, r3 = np.cumsum((0,) + SHARD_ROWS)[:4] + base
        layers.append(dict(
            qkv=w_all[:, r0:r0 + 384].reshape(N_DEV, D_MODEL, 384),
            o=w_all[:, r1:r1 + 128].reshape(D_MODEL, D_MODEL),
            ff1=w_all[:, r2:r2 + 512].reshape(N_DEV, D_MODEL, 512),
            ff2=w_all[:, r3:r3 + 512].reshape(D_FF, D_MODEL),
            g1=vecs[4 * l].reshape(1, D_MODEL), b1=vecs[4 * l + 1].reshape(1, D_MODEL),
            g2=vecs[4 * l + 2].reshape(1, D_MODEL), b2=vecs[4 * l + 3].reshape(1, D_MODEL)))

    rope = _rope_tables(S)
    tmat_later = _sb_tmat(True)
    tmat_upto = _sb_tmat(False)

    saved = []
    xin, xinb = x0, x0.astype(BF16)
    for l, W in enumerate(layers):
        sv = dict(xin=xin, xinb=xinb)
        qkv = _qkv_proj(xinb, W["qkv"], rope if l == 1 else None, Q_SCALE * LOG2E if l == 0 else Q_SCALE,
                        f"qkv_proj_{l}")
        sv["qkv"] = qkv
        if l == 0:
            vT3 = _chunk_transposed(qkv[:, 2 * D_MODEL:], S)
            o, sb_tiles = _sb_fwd(qkv, vT3, tmat_later, "sb_fwd")
            ob = o.astype(BF16)
            sv["sb_tiles"] = sb_tiles
        else:
            q, k, v = qkv[:, :D_MODEL], qkv[:, D_MODEL:2 * D_MODEL], qkv[:, 2 * D_MODEL:]
            outs = [_dil_fwd(q, k, v, d, f"dil_fwd_{d}") for _, d in DILATED_BRANCHES]
            o, ob, lse = _dil_merge([t[0] for t in outs], [t[1] for t in outs], "dil_merge")
            sv.update(q=q, k=k, v=v, lse=lse)
        sv.update(o=o, ob=ob)
        y1, x1, x1b = _mm_res_ln(ob, xin, W["o"], W["g1"], W["b1"], f"attn_out_ln_{l}")
        hpre, h = _ff1(x1b, W["ff1"], f"ff1_{l}")
        y2, x2, x2b = _mm_res_ln(h, x1, W["ff2"], W["g2"], W["b2"], f"ff2_ln_{l}")
        sv.update(y1=y1, x1=x1, x1b=x1b, hpre=hpre, h=h, y2=y2)
        saved.append(sv)
        xin, xinb = x2, x2b

    dout, loss_parts = _loss_grad(xin, target, "loss_grad")
    loss = lax.psum(jnp.sum(loss_parts), MESH_AXES)

    gmats = [None] * N_LAYERS
    gvecs = [None] * (4 * N_LAYERS)
    for l in reversed(range(N_LAYERS)):
        W, sv = layers[l], saved[l]
        dy2, dy2b, gb2 = _ln_bwd(dout, sv["y2"], W["g2"], f"ln2_bwd_{l}")
        dhp = _dh(dy2b, W["ff2"], sv["hpre"], f"dh_{l}")
        g_ff2 = _mm_tn(sv["h"], dy2b, 512, D_MODEL, False, f"dw_ff2_{l}")
        dx1 = _dx_blk(dy2, dhp, W["ff1"], f"dx_ff1_{l}")
        g_ff1 = _mm_tn(sv["x1b"], dhp, D_MODEL, 512, True, f"dw_ff1_{l}")
        dy1, dy1b, gb1 = _ln_bwd(dx1, sv["y1"], W["g1"], f"ln1_bwd_{l}")
        do = _mm_nt_plain(dy1b, W["o"], f"do_{l}")
        g_o = _mm_tn(sv["ob"], dy1b, 512, D_MODEL, False, f"dw_o_{l}")
        if l == 0:
            kT3 = _chunk_transposed(sv["qkv"][:, D_MODEL:2 * D_MODEL], S)
            dq, dk, dv = _sb_bwd(sv["qkv"], kT3, do, sv["sb_tiles"], tmat_upto, "sb_bwd")
            dqkv = jnp.concatenate([dq, dk, dv], axis=1).astype(BF16)
        else:
            dlt = _head_sums(do, sv["o"], "head_sums")
            parts = [_dil_bwd(sv["q"], sv["k"], sv["v"], do, sv["lse"], dlt, d, f"dil_bwd_{d}")
                     for _, d in DILATED_BRANCHES]
            dqkv = _dil_combine(parts, rope, "dil_combine")
        dout = _dx_blk(dy1, dqkv, W["qkv"], f"dx_qkv_{l}")
        g_qkv = _mm_tn(sv["xinb"], dqkv, D_MODEL, 384, True, f"dw_qkv_{l}")
        gmats[l] = (g_qkv.reshape(N_DEV, 384, D_MODEL), g_o.reshape(N_DEV, 128, D_MODEL),
                    g_ff1.reshape(N_DEV, 512, D_MODEL), g_ff2.reshape(N_DEV, 512, D_MODEL))
        gvecs[4 * l:4 * l + 4] = [gb1[0], gb1[1], gb2[0], gb2[1]]
    grad_x = dout.reshape(1, S, D_MODEL)

    cx, cy, cc = _mesh_pos()
    g_all = jnp.concatenate([g for layer in gmats for g in layer], axis=1)
    got_pair = _rs_pair(g_all, "rs_pair")
    chip_part = _pair_add(g_all, got_pair, cc.astype(jnp.int32).reshape(1), "rs_pair_add")
    got_chips = _rs_chips(chip_part, "rs_chips")
    chip_idx = (2 * cx + cy).astype(jnp.int32).reshape(1)
    g_sh, d_sh, m_sh, v_sh = _adamw_shard(chip_part, got_chips, chip_idx, w_flat, _flat_shards(mats_m),
                                          _flat_shards(mats_v), "adamw_mats")

    def unflat(a):
        out, pos = [], 0
        for layer in mats:
            for w in layer:
                n = w.size // D_MODEL
                out.append(a[pos:pos + n].reshape(w.shape))
                pos += n
        return out

    gv_all = _all_gather(jnp.stack(gvecs), "ag_vec_grads")
    g_v, d_v, m_v, v_v = _adamw_small(gv_all, jnp.stack(vecs), jnp.stack(vecs_m), jnp.stack(vecs_v), "adamw_vecs")

    def interleave(mat_list, vec_arr):
        out = []
        for l in range(N_LAYERS):
            qkv_, o_, ff1_, ff2_ = mat_list[4 * l:4 * l + 4]
            out += [qkv_, o_, vec_arr[4 * l], vec_arr[4 * l + 1], ff1_, ff2_, vec_arr[4 * l + 2], vec_arr[4 * l + 3]]
        return out

    return (loss, grad_x, *interleave(unflat(g_sh), g_v), *interleave(unflat(d_sh), d_v),
            *interleave(unflat(m_sh), m_v), *interleave(unflat(v_sh), v_v))
```

```python
import functools
import math

import jax
import jax.numpy as jnp
import numpy as np
from jax import lax
from jax.experimental import pallas as pl
from jax.experimental.pallas import tpu as pltpu

F32 = jnp.float32
BF16 = jnp.bfloat16

D_MODEL = 1024
N_HEADS = 16
HEAD_DIM = 64
D_FF = 4096
N_DEV = 8
N_LAYERS = 2
ROPE_THETA = 500000.0
ROPE_DIM = 16
DILATED_BRANCHES = ((128, 1), (512, 4), (2048, 16))
ALPHA = (2 * N_LAYERS) ** 0.25
LN_EPS = 1e-5
Q_SCALE = 1.0 / math.sqrt(HEAD_DIM)
LOG2E = math.log2(math.e)
LN2 = math.log(2.0)
ADAM_LR, ADAM_B1, ADAM_B2, ADAM_EPS, ADAM_WD, ADAM_STEP = 0.001, 0.9, 0.999, 1e-08, 0.01, 10

LANES = 128
HEAD_PAIRS = D_MODEL // LANES
SB_TQ = 256
SB_CH = 256
SB_STEPS = 4
SB_SAVE_SLOTS = 2 * SB_STEPS
SB_LOAD_SLOTS = 12
SB_LOAD_AHEAD = SB_LOAD_SLOTS - SB_STEPS - 1
DIL_BLK = 128
VMEM_BIG = 56 * 2 ** 20
MESH_AXES = ("x", "y", "c")

SHARD_ROWS = (384, 128, 512, 512)
LAYER_ROWS = sum(SHARD_ROWS)
ALL_ROWS = N_LAYERS * LAYER_ROWS


def _params(sem=None, vmem=None):
    kw = {}
    if sem is not None:
        kw["dimension_semantics"] = sem
    if vmem is not None:
        kw["vmem_limit_bytes"] = vmem
    return pltpu.CompilerParams(**kw)


def _dot(a, b):
    return jnp.dot(a, b, preferred_element_type=F32)


def _dot_nt(a, b):
    return lax.dot_general(a, b, (((1,), (1,)), ((), ())), preferred_element_type=F32)


def _dot_tn(a, b):
    return lax.dot_general(a, b, (((0,), (0,)), ((), ())), preferred_element_type=F32)


def _split3(p):
    hi = p.astype(BF16)
    r1 = p - hi.astype(F32)
    mid = r1.astype(BF16)
    lo = (r1 - mid.astype(F32)).astype(BF16)
    return hi, mid, lo


def _dot3(p, e):
    hi, mid, lo = _split3(p)
    return _dot(hi, e) + _dot(mid, e) + _dot(lo, e)


def _rope_apply(a, c, s1, s2, sign):
    return a * c + sign * (pltpu.roll(a, 8, 1) * s1 + pltpu.roll(a, LANES - 8, 1) * s2)


def _qkv_proj(xb, w_blk, rope, q_mult, out_dtype, name):
    S = xb.shape[0]
    tm = 512
    n_rope = 0 if rope is None else 3

    def body(*refs):
        x_ref, w_ref = refs[:2]
        tabs = [r[...] for r in refs[2:2 + n_rope]]
        o_ref = refs[2 + n_rope]
        x = x_ref[...]
        for j in range(N_DEV):
            acc = _dot(x, w_ref[j])
            for g in range(3):
                col = j * 384 + g * LANES
                a = acc[:, g * LANES:(g + 1) * LANES]
                if n_rope and col < 2 * D_MODEL:
                    a = _rope_apply(a, *tabs, 1.0)
                if col < D_MODEL:
                    a = a * q_mult
                o_ref[:, col:col + LANES] = a.astype(out_dtype)

    tab_specs = [pl.BlockSpec((tm, LANES), lambda i: (i, 0))] * n_rope
    return pl.pallas_call(
        body, name=name, grid=(S // tm,),
        in_specs=[pl.BlockSpec((tm, D_MODEL), lambda i: (i, 0)),
                  pl.BlockSpec((N_DEV, D_MODEL, 384), lambda i: (0, 0, 0))] + tab_specs,
        out_specs=pl.BlockSpec((tm, 3 * D_MODEL), lambda i: (i, 0)),
        out_shape=jax.ShapeDtypeStruct((S, 3 * D_MODEL), out_dtype),
        compiler_params=_params(("parallel",), VMEM_BIG),
    )(xb, w_blk, *(rope or ()))


def _layer_norm_rows(y, g, b):
    mu = jnp.mean(y, axis=-1, keepdims=True)
    yc = y - mu
    var = jnp.mean(yc * yc, axis=-1, keepdims=True)
    return yc * lax.rsqrt(var + LN_EPS) * g + b


def _mm_res_ln(a, xres, w, g, b, name):
    S, K = a.shape
    tm = 512 if K <= 1024 else 256

    def body(a_ref, x_ref, w_ref, g_ref, b_ref, y_ref, xn_ref, xb_ref):
        y = ALPHA * x_ref[...] + _dot(a_ref[...], w_ref[...])
        xn = _layer_norm_rows(y, g_ref[...], b_ref[...])
        y_ref[...] = y
        xn_ref[...] = xn
        xb_ref[...] = xn.astype(BF16)

    row = lambda i: (i, 0)
    fix = lambda i: (0, 0)
    return pl.pallas_call(
        body, name=name, grid=(S // tm,),
        in_specs=[pl.BlockSpec((tm, K), row), pl.BlockSpec((tm, D_MODEL), row),
                  pl.BlockSpec((K, D_MODEL), fix), pl.BlockSpec((1, D_MODEL), fix),
                  pl.BlockSpec((1, D_MODEL), fix)],
        out_specs=[pl.BlockSpec((tm, D_MODEL), row)] * 3,
        out_shape=[jax.ShapeDtypeStruct((S, D_MODEL), F32), jax.ShapeDtypeStruct((S, D_MODEL), F32),
                   jax.ShapeDtypeStruct((S, D_MODEL), BF16)],
        compiler_params=_params(("parallel",), VMEM_BIG),
    )(a, xres, w, g, b)


def _ff1(xb, w_blk, name):
    S = xb.shape[0]
    tm = 256

    def body(x_ref, w_ref, hp_ref, h_ref):
        x = x_ref[...]
        for j in range(N_DEV):
            acc = _dot(x, w_ref[j])
            r = jnp.maximum(acc, 0.0)
            hp_ref[:, j * 512:(j + 1) * 512] = acc
            h_ref[:, j * 512:(j + 1) * 512] = (r * r).astype(BF16)

    return pl.pallas_call(
        body, name=name, grid=(S // tm,),
        in_specs=[pl.BlockSpec((tm, D_MODEL), lambda i: (i, 0)),
                  pl.BlockSpec((N_DEV, D_MODEL, 512), lambda i: (0, 0, 0))],
        out_specs=[pl.BlockSpec((tm, D_FF), lambda i: (i, 0))] * 2,
        out_shape=[jax.ShapeDtypeStruct((S, D_FF), F32), jax.ShapeDtypeStruct((S, D_FF), BF16)],
        compiler_params=_params(("parallel",), VMEM_BIG),
    )(xb, w_blk)


def _loss_grad(y, target, name):
    S = y.shape[0]
    tm = 512

    def body(y_ref, t_ref, dy_ref, l_ref):
        @pl.when(pl.program_id(0) == 0)
        def _():
            l_ref[...] = jnp.zeros_like(l_ref)

        err = y_ref[...] - t_ref[...]
        dy_ref[...] = err * (1.0 / D_MODEL)
        sq = err * err
        rows = sq[0:8]
        for r in range(1, tm // 8):
            rows = rows + sq[r * 8:(r + 1) * 8]
        acc = rows[:, 0:LANES]
        for g in range(1, D_MODEL // LANES):
            acc = acc + rows[:, g * LANES:(g + 1) * LANES]
        l_ref[...] += acc * (0.5 / D_MODEL)

    return pl.pallas_call(
        body, name=name, grid=(S // tm,),
        in_specs=[pl.BlockSpec((tm, D_MODEL), lambda i: (i, 0))] * 2,
        out_specs=[pl.BlockSpec((tm, D_MODEL), lambda i: (i, 0)), pl.BlockSpec((8, LANES), lambda i: (0, 0))],
        out_shape=[jax.ShapeDtypeStruct((S, D_MODEL), F32), jax.ShapeDtypeStruct((8, LANES), F32)],
        compiler_params=_params(("arbitrary",)),
    )(y, target)


def _ln_bwd(dout, y, g, name):
    S = y.shape[0]
    tm = 512
    steps = S // tm

    def body(d_ref, y_ref, g_ref, dy_ref, dyb_ref, gb_ref, acc_g, acc_b):
        i = pl.program_id(0)

        @pl.when(i == 0)
        def _():
            acc_g[...] = jnp.zeros_like(acc_g)
            acc_b[...] = jnp.zeros_like(acc_b)

        d = d_ref[...]
        yv = y_ref[...]
        mu = jnp.mean(yv, axis=-1, keepdims=True)
        yc = yv - mu
        var = jnp.mean(yc * yc, axis=-1, keepdims=True)
        rstd = lax.rsqrt(var + LN_EPS)
        xhat = yc * rstd
        dxh = d * g_ref[...]
        m1 = jnp.mean(dxh, axis=-1, keepdims=True)
        m2 = jnp.mean(dxh * xhat, axis=-1, keepdims=True)
        dy = rstd * (dxh - m1 - xhat * m2)
        dy_ref[...] = dy
        dyb_ref[...] = dy.astype(BF16)
        pg = d * xhat
        sg = pg[0:8]
        sb = d[0:8]
        for r in range(1, tm // 8):
            sg = sg + pg[r * 8:(r + 1) * 8]
            sb = sb + d[r * 8:(r + 1) * 8]
        acc_g[...] += sg
        acc_b[...] += sb

        @pl.when(i == steps - 1)
        def _():
            gb_ref[0:1, :] = jnp.sum(acc_g[...], axis=0, keepdims=True)
            gb_ref[1:2, :] = jnp.sum(acc_b[...], axis=0, keepdims=True)

    row = lambda i: (i, 0)
    fix = lambda i: (0, 0)
    return pl.pallas_call(
        body, name=name, grid=(steps,),
        in_specs=[pl.BlockSpec((tm, D_MODEL), row), pl.BlockSpec((tm, D_MODEL), row), pl.BlockSpec((1, D_MODEL), fix)],
        out_specs=[pl.BlockSpec((tm, D_MODEL), row), pl.BlockSpec((tm, D_MODEL), row), pl.BlockSpec((2, D_MODEL), fix)],
        out_shape=[jax.ShapeDtypeStruct((S, D_MODEL), F32), jax.ShapeDtypeStruct((S, D_MODEL), BF16),
                   jax.ShapeDtypeStruct((2, D_MODEL), F32)],
        scratch_shapes=[pltpu.VMEM((8, D_MODEL), F32), pltpu.VMEM((8, D_MODEL), F32)],
        compiler_params=_params(("arbitrary",)),
    )(dout, y, g)


def _dh(dyb, w2, hpre, name):
    S = dyb.shape[0]
    tm = 256
    tn = 512

    def body(dy_ref, w_ref, hp_ref, o_ref):
        dy = dy_ref[...]
        for n in range(0, D_FF, tn):
            dh = _dot_nt(dy, w_ref[n:n + tn, :])
            o_ref[:, n:n + tn] = (dh * (2.0 * jnp.maximum(hp_ref[:, n:n + tn], 0.0))).astype(BF16)

    return pl.pallas_call(
        body, name=name, grid=(S // tm,),
        in_specs=[pl.BlockSpec((tm, D_MODEL), lambda i: (i, 0)), pl.BlockSpec((D_FF, D_MODEL), lambda i: (0, 0)),
                  pl.BlockSpec((tm, D_FF), lambda i: (i, 0))],
        out_specs=pl.BlockSpec((tm, D_FF), lambda i: (i, 0)),
        out_shape=jax.ShapeDtypeStruct((S, D_FF), BF16),
        compiler_params=_params(("parallel",), VMEM_BIG),
    )(dyb, w2, hpre)


def _dx_blk(dres, dz, w_blk, name):
    S, N = dz.shape
    bw = w_blk.shape[2]
    tm = 256

    def body(r_ref, z_ref, w_ref, o_ref):
        acc = ALPHA * r_ref[...]
        for j in range(N_DEV):
            acc = acc + _dot_nt(z_ref[:, j * bw:(j + 1) * bw], w_ref[j])
        o_ref[...] = acc

    return pl.pallas_call(
        body, name=name, grid=(S // tm,),
        in_specs=[pl.BlockSpec((tm, D_MODEL), lambda i: (i, 0)), pl.BlockSpec((tm, N), lambda i: (i, 0)),
                  pl.BlockSpec((N_DEV, D_MODEL, bw), lambda i: (0, 0, 0))],
        out_specs=pl.BlockSpec((tm, D_MODEL), lambda i: (i, 0)),
        out_shape=jax.ShapeDtypeStruct((S, D_MODEL), F32),
        compiler_params=_params(("parallel",), VMEM_BIG),
    )(dres, dz, w_blk)


def _mm_nt_plain(a, w, name):
    S = a.shape[0]
    tm = 512

    def body(a_ref, w_ref, o_ref):
        o_ref[...] = _dot_nt(a_ref[...], w_ref[...])

    return pl.pallas_call(
        body, name=name, grid=(S // tm,),
        in_specs=[pl.BlockSpec((tm, D_MODEL), lambda i: (i, 0)), pl.BlockSpec((D_MODEL, D_MODEL), lambda i: (0, 0))],
        out_specs=pl.BlockSpec((tm, D_MODEL), lambda i: (i, 0)),
        out_shape=jax.ShapeDtypeStruct((S, D_MODEL), F32),
        compiler_params=_params(("parallel",)),
    )(a, w)


def _mm_tn(a, b, ta, tb, blocked, name):
    S, Ka = a.shape
    Nb = b.shape[1]
    ts = 2048

    def body(a_ref, b_ref, o_ref):
        @pl.when(pl.program_id(2) == 0)
        def _():
            o_ref[...] = jnp.zeros_like(o_ref)

        o_ref[...] += _dot_tn(a_ref[...], b_ref[...])

    if blocked:
        out_spec = pl.BlockSpec((None, ta, tb), lambda i, j, s: (j, i, 0))
        out_shape = jax.ShapeDtypeStruct((Nb // tb, Ka, tb), F32)
    else:
        out_spec = pl.BlockSpec((ta, tb), lambda i, j, s: (i, j))
        out_shape = jax.ShapeDtypeStruct((Ka, Nb), F32)
    return pl.pallas_call(
        body, name=name, grid=(Ka // ta, Nb // tb, S // ts),
        in_specs=[pl.BlockSpec((ts, ta), lambda i, j, s: (s, i)), pl.BlockSpec((ts, tb), lambda i, j, s: (s, j))],
        out_specs=out_spec, out_shape=out_shape,
        compiler_params=_params(("parallel", "parallel", "arbitrary"), VMEM_BIG),
    )(a, b)


def _head_sums(do, o, name):
    S = do.shape[0]
    tm = 512
    sel = (np.arange(D_MODEL)[:, None] // HEAD_DIM == np.arange(LANES)[None, :]).astype(np.float32)

    def body(d_ref, o_ref, e_ref, out_ref):
        out_ref[...] = _dot3(d_ref[...] * o_ref[...], e_ref[...])

    return pl.pallas_call(
        body, name=name, grid=(S // tm,),
        in_specs=[pl.BlockSpec((tm, D_MODEL), lambda i: (i, 0))] * 2 + [pl.BlockSpec((D_MODEL, LANES), lambda i: (0, 0))],
        out_specs=pl.BlockSpec((tm, LANES), lambda i: (i, 0)),
        out_shape=jax.ShapeDtypeStruct((S, LANES), F32),
        compiler_params=_params(("parallel",)),
    )(do, o, jnp.asarray(sel, BF16))


def _sb_tmat(later):
    r = np.arange(SB_CH)
    t = (r[None, :] > r[:, None]) if later else (r[None, :] <= r[:, None])
    return jnp.asarray(np.concatenate([t.astype(np.float32), np.ones((8, SB_CH), np.float32)], axis=0), BF16)


def _sb_gates(z2):
    neg_abs = lax.bitcast_convert_type(lax.bitcast_convert_type(z2, jnp.uint32) | jnp.uint32(0x80000000), F32)
    l1 = jnp.log2(1.0 + jnp.exp2(neg_abs))
    a = jnp.minimum(z2, 0.0) - l1
    return a, a - z2


def _head_masks(x2):
    lane = lax.broadcasted_iota(jnp.int32, x2.shape, 1)
    zero = jnp.zeros_like(x2)
    return jnp.where(lane < HEAD_DIM, x2, zero), jnp.where(lane >= HEAD_DIM, x2, zero)


def _sb_fwd(qkv, vT3, tmat, name):
    S = qkv.shape[0]
    nq = S // SB_TQ
    nch = S // SB_CH
    ns = SB_SAVE_SLOTS

    def body(q_ref, k_ref, vT_ref, t_ref, o_ref, ws_hbm, z_scr, a_scr, cum_scr, oT_scr, stage, sems):
        hp = pl.program_id(0)
        i = pl.program_id(1)
        base = (i * (i + 1)) // 2
        qm = _head_masks(q_ref[...])

        def save(src, sem, c):
            return pltpu.make_async_copy(src, ws_hbm.at[hp, base + c], sem)

        causal = (lax.broadcasted_iota(jnp.int32, (SB_CH, SB_TQ), 0)
                  < lax.broadcasted_iota(jnp.int32, (SB_CH, SB_TQ), 1))

        def head_rows(vTc, h):
            return vTc[h * HEAD_DIM:(h + 1) * HEAD_DIM, :]

        @pl.when(jnp.logical_and(hp == 0, i == 0))
        def _():
            z_scr[...] = jnp.zeros_like(z_scr)
            a_scr[...] = jnp.zeros_like(a_scr)
            cum_scr[...] = jnp.zeros_like(cum_scr)

        oT_scr[...] = jnp.zeros_like(oT_scr)

        def c_valid(t):
            return jnp.logical_and(t >= 2, t - 2 <= i)

        def c_chunk(t):
            return jnp.clip(i + 2 - t, 0, nch - 1)

        def step(t, p, slot, R, own_b, own_c):
            cA = jnp.maximum(i - t, 0)
            kA = k_ref[pl.ds(pl.multiple_of(cA * SB_CH, SB_CH), SB_CH), :]
            valid = c_valid(t)
            vC = vT_ref[c_chunk(t)]
            out = []
            for h in range(2):
                z_scr[p, h] = _dot_nt(kA, qm[h])
                a, lf = _sb_gates(z_scr[1 - p, h])
                if own_b:
                    lf = jnp.where(causal, lf, 0.0)
                a_scr[1 - p, h] = a
                cum_scr[1 - p, h] = _dot(t_ref[...], lf.astype(BF16))
                a_c = a_scr[p, h]
                w = jnp.exp2(a_c + cum_scr[p, h, :SB_CH, :] + R[h])
                if own_c:
                    w = jnp.where(causal, w, 0.0)
                wb = w.astype(BF16)
                stage[slot, 2 * h] = wb
                stage[slot, 2 * h + 1] = a_c.astype(BF16)
                oT_scr[h] += jnp.where(valid, _dot(head_rows(vC, h), wb), 0.0)
                out.append(R[h] + jnp.where(valid, cum_scr[p, h, SB_CH:SB_CH + 1, :], 0.0))
            return tuple(out)

        def trip(tt, R, first):
            half = 0 if first else lax.rem(tt, 2) * SB_STEPS
            if not first:
                for j in range(SB_STEPS):
                    @pl.when(c_valid(SB_STEPS * tt + j - ns))
                    def _():
                        save(stage.at[half + j], sems.at[half + j], 0).wait()

            for j in range(SB_STEPS):
                R = step(SB_STEPS * tt + j, j % 2, half + j, R, first and j == 1, first and j == 2)
            for j in range(SB_STEPS):
                t = SB_STEPS * tt + j

                @pl.when(c_valid(t))
                def _():
                    save(stage.at[half + j], sems.at[half + j], c_chunk(t)).start()

            return R

        z1 = jnp.zeros((1, SB_TQ), F32)
        trips = (i + 3 + SB_STEPS - 1) // SB_STEPS
        lax.fori_loop(1, trips, lambda tt, R: trip(tt, R, False), trip(0, (z1, z1), True))
        for back in (2, 1):
            tl = trips - back
            half = lax.rem(tl + 2, 2) * SB_STEPS
            for j in range(SB_STEPS):
                @pl.when(c_valid(SB_STEPS * tl + j))
                def _():
                    save(stage.at[half + j], sems.at[half + j], 0).wait()

        o_ref[...] = jnp.concatenate([oT_scr[0], oT_scr[1]], axis=0).T

    ntile = nq * (nq + 1) // 2
    return pl.pallas_call(
        body, name=name, grid=(HEAD_PAIRS, nq),
        in_specs=[pl.BlockSpec((SB_TQ, LANES), lambda hp, i: (i, hp)),
                  pl.BlockSpec((S, LANES), lambda hp, i: (0, HEAD_PAIRS + hp)),
                  pl.BlockSpec((None, nch, LANES, SB_CH), lambda hp, i: (hp, 0, 0, 0)),
                  pl.BlockSpec((SB_CH + 8, SB_CH), lambda hp, i: (0, 0))],
        out_specs=[pl.BlockSpec((SB_TQ, LANES), lambda hp, i: (i, hp)), pl.BlockSpec(memory_space=pl.ANY)],
        out_shape=[jax.ShapeDtypeStruct((S, D_MODEL), F32),
                   jax.ShapeDtypeStruct((HEAD_PAIRS, ntile, 4, SB_CH, SB_TQ), BF16)],
        scratch_shapes=[pltpu.VMEM((2, 2, SB_CH, SB_TQ), F32), pltpu.VMEM((2, 2, SB_CH, SB_TQ), F32),
                        pltpu.VMEM((2, 2, SB_CH + 8, SB_TQ), F32), pltpu.VMEM((2, HEAD_DIM, SB_TQ), F32),
                        pltpu.VMEM((ns, 4, SB_CH, SB_TQ), BF16), pltpu.SemaphoreType.DMA((ns,))],
        compiler_params=_params(("arbitrary", "arbitrary"), VMEM_BIG),
    )(qkv, qkv, vT3, tmat)


def _sb_bwd(qkv, kT3, do, ws, tmat_g, name):
    S = qkv.shape[0]
    nq = S // SB_TQ
    nch = S // SB_CH
    nl = SB_LOAD_SLOTS
    ahead = SB_LOAD_AHEAD

    def body(q_ref, do_ref, v_ref, kT_ref, tg_ref, ws_hbm, dq_ref, dk_hbm, dv_hbm, dk_acc, dv_acc, sems,
             dwv_scr, g_scr, sig_scr, cumg_scr, dqT_scr, ring, ring_sems):
        hp = pl.program_id(0)
        i = pl.program_id(1)
        base = (i * (i + 1)) // 2

        @pl.when(i == 0)
        def _():
            dk_acc[...] = jnp.zeros_like(dk_acc)
            dv_acc[...] = jnp.zeros_like(dv_acc)

        @pl.when(jnp.logical_and(hp == 0, i == 0))
        def _():
            for scr in (dwv_scr, g_scr, sig_scr, cumg_scr, ring):
                scr[...] = jnp.zeros_like(scr)

        def load(u):
            slot = lax.rem(u, nl)
            return pltpu.make_async_copy(ws_hbm.at[hp, base + u], ring.at[slot], ring_sems.at[slot])

        for u in range(ahead):
            @pl.when(u <= i)
            def _():
                load(u).start()

        dqT_scr[...] = jnp.zeros_like(dqT_scr)
        qm = _head_masks(q_ref[...])
        dom = _head_masks(do_ref[...].astype(BF16))
        causal = (lax.broadcasted_iota(jnp.int32, (SB_CH, SB_TQ), 0)
                  < lax.broadcasted_iota(jnp.int32, (SB_CH, SB_TQ), 1))

        def rows_of(c):
            return pl.ds(pl.multiple_of(c * SB_CH, SB_CH), SB_CH)

        def head_rows(kTc, h):
            return kTc[h * HEAD_DIM:(h + 1) * HEAD_DIM, :]

        def step(t, p, Gs):
            q = 1 - p
            valid_b = jnp.logical_and(t >= 1, t - 1 <= i)
            valid_c = jnp.logical_and(t >= 2, t - 2 <= i)
            c_b = jnp.clip(t - 1, 0, i)
            c_c = jnp.clip(t - 2, 0, i)
            slot = jnp.where(valid_b, lax.rem(jnp.maximum(t - 1, 0), nl), nl)
            vA = v_ref[rows_of(jnp.minimum(t, i)), :]
            kTc = kT_ref[c_c]
            keep = jnp.logical_or(causal, t - 2 != i)
            out = []
            for h in range(2):
                dwv_scr[p, h] = _dot_nt(vA, dom[h])

                wb = ring[slot, 2 * h]
                g = wb.astype(F32) * dwv_scr[q, h]
                g_scr[q, h] = g
                sig_scr[q, h] = jnp.exp2(ring[slot, 2 * h + 1].astype(F32))
                cumg_scr[q, h] = _dot(tg_ref[...], g.astype(BF16))
                dv_h = _dot(wb, dom[h])

                dz = g_scr[p, h] - sig_scr[p, h] * (Gs[h] + cumg_scr[p, h, :SB_CH, :])
                dzb = jnp.where(keep, dz, 0.0).astype(BF16)
                dk_h = _dot(dzb, qm[h])
                dqT_scr[h] += jnp.where(valid_c, _dot(head_rows(kTc, h), dzb), 0.0)
                out.append(Gs[h] + jnp.where(valid_c, cumg_scr[p, h, SB_CH:SB_CH + 1, :], 0.0))
                dk_c = dk_h if h == 0 else dk_c + dk_h
                dv_c = dv_h if h == 0 else dv_c + dv_h
            dv_acc[rows_of(c_b), :] += jnp.where(valid_b, dv_c, 0.0)
            dk_acc[rows_of(c_c), :] += jnp.where(valid_c, dk_c, 0.0)
            return tuple(out)

        def trip(tt, Gs):
            for j in range(SB_STEPS):
                t = SB_STEPS * tt + j

                @pl.when(jnp.logical_and(t >= 1, t - 1 <= i))
                def _():
                    load(t - 1).wait()

            for j in range(SB_STEPS):
                t = SB_STEPS * tt + j

                @pl.when(t + ahead <= i)
                def _():
                    load(t + ahead).start()

            for j in range(SB_STEPS):
                Gs = step(SB_STEPS * tt + j, j % 2, Gs)
            return Gs

        z1 = jnp.zeros((1, SB_TQ), F32)
        lax.fori_loop(0, (i + 3 + SB_STEPS - 1) // SB_STEPS, trip, (z1, z1))
        dq_ref[...] = jnp.concatenate([dqT_scr[0], dqT_scr[1]], axis=0).T * Q_SCALE

        @pl.when(i == nq - 1)
        def _():
            dk_acc[...] = dk_acc[...] * LN2
            cols = pl.ds(pl.multiple_of(hp * LANES, LANES), LANES)
            ck = pltpu.make_async_copy(dk_acc, dk_hbm.at[:, cols], sems.at[0])
            cv = pltpu.make_async_copy(dv_acc, dv_hbm.at[:, cols], sems.at[1])
            ck.start()
            cv.start()
            ck.wait()
            cv.wait()

    blk = lambda hp, i: (i, hp)
    return pl.pallas_call(
        body, name=name, grid=(HEAD_PAIRS, nq),
        in_specs=[pl.BlockSpec((SB_TQ, LANES), blk),
                  pl.BlockSpec((SB_TQ, LANES), blk),
                  pl.BlockSpec((S, LANES), lambda hp, i: (0, 2 * HEAD_PAIRS + hp)),
                  pl.BlockSpec((None, nch, LANES, SB_CH), lambda hp, i: (hp, 0, 0, 0)),
                  pl.BlockSpec((SB_CH + 8, SB_CH), lambda hp, i: (0, 0)),
                  pl.BlockSpec(memory_space=pl.ANY)],
        out_specs=[pl.BlockSpec((SB_TQ, LANES), blk), pl.BlockSpec(memory_space=pl.ANY),
                   pl.BlockSpec(memory_space=pl.ANY)],
        out_shape=[jax.ShapeDtypeStruct((S, D_MODEL), F32)] * 3,
        scratch_shapes=[pltpu.VMEM((S, LANES), F32), pltpu.VMEM((S, LANES), F32), pltpu.SemaphoreType.DMA((2,))]
        + [pltpu.VMEM((2, 2, SB_CH, SB_TQ), F32)] * 3
        + [pltpu.VMEM((2, 2, SB_CH + 8, SB_TQ), F32), pltpu.VMEM((2, HEAD_DIM, SB_TQ), F32)]
        + [pltpu.VMEM((nl + 1, 4, SB_CH, SB_TQ), BF16), pltpu.SemaphoreType.DMA((nl,))],
        compiler_params=_params(("arbitrary", "arbitrary"), VMEM_BIG),
    )(qkv, do, qkv, kT3, tmat_g, ws)


def _dil_valid(first):
    qi = lax.broadcasted_iota(jnp.int32, (DIL_BLK, 2 * DIL_BLK), 0)
    kj = lax.broadcasted_iota(jnp.int32, (DIL_BLK, 2 * DIL_BLK), 1)
    dist = DIL_BLK + qi - kj
    return (dist >= 0) & (dist <= DIL_BLK) & (jnp.logical_not(first) | (kj >= DIL_BLK))


def _lane_pick(tile, idx):
    lane = lax.broadcasted_iota(jnp.int32, tile.shape, 1)
    return jnp.sum(jnp.where(lane == idx, tile, 0.0), axis=-1, keepdims=True)


class _DilPlan:
    def __init__(self, S, d):
        self.d = d
        self.span = DIL_BLK * d
        self.groups = max(1, 1024 // self.span)
        self.rows = self.span * self.groups
        self.steps = S // self.rows

    def cur(self, col0):
        return pl.BlockSpec((self.rows, LANES), lambda n, hp: (n, col0 + hp))

    def prev(self, col0):
        g = self.groups
        return pl.BlockSpec((self.span, LANES), lambda n, hp: (jnp.maximum(n * g - 1, 0), col0 + hp))

    def shared(self):
        return pl.BlockSpec((self.rows, LANES), lambda n, hp: (n, 0))

    def units(self, fn):
        n = pl.program_id(0)
        batch = 8
        if self.d * self.groups <= batch:
            for g in range(self.groups):
                for r in range(self.d):
                    fn(g, r, jnp.logical_and(n == 0, g == 0))
        else:
            assert self.groups == 1 and self.d % batch == 0

            def body(rb, carry):
                for rr in range(batch):
                    fn(0, rb * batch + rr, n == 0)
                return carry

            lax.fori_loop(0, self.d // batch, body, 0)

    def rows_of(self, g, r):
        return pl.ds(g * self.span + r, DIL_BLK, stride=self.d)

    def keys(self, cur_ref, prev_ref, g, r):
        before = prev_ref[pl.ds(r, DIL_BLK, stride=self.d), :] if g == 0 else cur_ref[self.rows_of(g - 1, r), :]
        return jnp.concatenate([before, cur_ref[self.rows_of(g, r), :]], axis=0).astype(BF16)


def _dil_fwd(qkv, d, name):
    S = qkv.shape[0]
    plan = _DilPlan(S, d)

    def body(q_ref, kc_ref, kp_ref, vc_ref, vp_ref, o_ref, lse_ref):
        hp = pl.program_id(1)

        @pl.when(hp == 0)
        def _():
            lse_ref[...] = jnp.zeros_like(lse_ref)

        lane = lax.broadcasted_iota(jnp.int32, (DIL_BLK, LANES), 1)

        def unit(g, r, first):
            valid = _dil_valid(first)
            rows = plan.rows_of(g, r)
            qm = _head_masks(q_ref[rows, :].astype(BF16))
            kk = plan.keys(kc_ref, kp_ref, g, r)
            vm = _head_masks(plan.keys(vc_ref, vp_ref, g, r))
            lse_t = lse_ref[rows, :]
            o2 = None
            for h in range(2):
                s = jnp.where(valid, _dot_nt(qm[h], kk), -1e30)
                m = jnp.max(s, axis=-1, keepdims=True)
                p = jnp.exp(s - m)
                den = jnp.sum(p, axis=-1, keepdims=True)
                oh = _dot(p.astype(BF16), vm[h]) / den
                o2 = oh if o2 is None else o2 + oh
                lse_t = jnp.where(lane == 2 * hp + h, m + jnp.log(den), lse_t)
            o_ref[rows, :] = o2
            lse_ref[rows, :] = lse_t

        plan.units(unit)

    return pl.pallas_call(
        body, name=name, grid=(plan.steps, HEAD_PAIRS),
        in_specs=[plan.cur(0), plan.cur(HEAD_PAIRS), plan.prev(HEAD_PAIRS), plan.cur(2 * HEAD_PAIRS),
                  plan.prev(2 * HEAD_PAIRS)],
        out_specs=[plan.cur(0), plan.shared()],
        out_shape=[jax.ShapeDtypeStruct((S, D_MODEL), F32), jax.ShapeDtypeStruct((S, LANES), F32)],
        compiler_params=_params(("parallel", "arbitrary")),
    )(qkv, qkv, qkv, qkv, qkv)


def _head_expand():
    return jnp.asarray((np.arange(LANES)[:, None] == np.arange(D_MODEL)[None, :] // HEAD_DIM).astype(np.float32), BF16)


def _dil_merge(os_, lses, name):
    S = os_[0].shape[0]
    tm = 256
    nbr = len(os_)

    def body(*refs):
        o_refs, l_refs, e_ref = refs[:nbr], refs[nbr:2 * nbr], refs[2 * nbr]
        out_ref, outb_ref, lse_ref = refs[2 * nbr + 1:]
        ls = [r[...] for r in l_refs]
        m = ls[0]
        for l in ls[1:]:
            m = jnp.maximum(m, l)
        tot = jnp.exp(ls[0] - m)
        for l in ls[1:]:
            tot = tot + jnp.exp(l - m)
        lse = m + jnp.log(tot)
        acc = None
        for o_r, l in zip(o_refs, ls):
            wt = _dot3(jnp.exp(l - lse), e_ref[...])
            term = wt * o_r[...]
            acc = term if acc is None else acc + term
        out_ref[...] = acc
        outb_ref[...] = acc.astype(BF16)
        lse_ref[...] = lse

    row = lambda i: (i, 0)
    return pl.pallas_call(
        body, name=name, grid=(S // tm,),
        in_specs=[pl.BlockSpec((tm, D_MODEL), row)] * nbr + [pl.BlockSpec((tm, LANES), row)] * nbr
        + [pl.BlockSpec((LANES, D_MODEL), lambda i: (0, 0))],
        out_specs=[pl.BlockSpec((tm, D_MODEL), row), pl.BlockSpec((tm, D_MODEL), row), pl.BlockSpec((tm, LANES), row)],
        out_shape=[jax.ShapeDtypeStruct((S, D_MODEL), F32), jax.ShapeDtypeStruct((S, D_MODEL), BF16),
                   jax.ShapeDtypeStruct((S, LANES), F32)],
        compiler_params=_params(("parallel",)),
    )(*os_, *lses, _head_expand())


def _dil_bwd(qkv, do, lse, dlt, d, name):
    S = qkv.shape[0]
    plan = _DilPlan(S, d)

    def body(q_ref, kc_ref, kp_ref, vc_ref, vp_ref, do_ref, lse_ref, dl_ref,
             dq_ref, dka_ref, dkb_ref, dva_ref, dvb_ref):
        hp = pl.program_id(1)

        def unit(g, r, first):
            valid = _dil_valid(first)
            rows = plan.rows_of(g, r)
            qm = _head_masks(q_ref[rows, :].astype(BF16))
            dom = _head_masks(do_ref[rows, :].astype(BF16))
            kk = plan.keys(kc_ref, kp_ref, g, r)
            vv = plan.keys(vc_ref, vp_ref, g, r)
            km = _head_masks(kk)
            lse_t = lse_ref[rows, :]
            dl_t = dl_ref[rows, :]
            dq2 = dkk = dvv = None
            for h in range(2):
                s = _dot_nt(qm[h], kk)
                p = jnp.where(valid, jnp.exp(s - _lane_pick(lse_t, 2 * hp + h)), 0.0)
                ds = (p * (_dot_nt(dom[h], vv) - _lane_pick(dl_t, 2 * hp + h))).astype(BF16)
                t_q = _dot(ds, km[h])
                t_k = _dot_tn(ds, qm[h])
                t_v = _dot_tn(p.astype(BF16), dom[h])
                dq2 = t_q if dq2 is None else dq2 + t_q
                dkk = t_k if dkk is None else dkk + t_k
                dvv = t_v if dvv is None else dvv + t_v
            dq_ref[rows, :] = dq2
            dkb_ref[rows, :] = dkk[:DIL_BLK]
            dka_ref[rows, :] = dkk[DIL_BLK:]
            dvb_ref[rows, :] = dvv[:DIL_BLK]
            dva_ref[rows, :] = dvv[DIL_BLK:]

        plan.units(unit)

    return pl.pallas_call(
        body, name=name, grid=(plan.steps, HEAD_PAIRS),
        in_specs=[plan.cur(0), plan.cur(HEAD_PAIRS), plan.prev(HEAD_PAIRS), plan.cur(2 * HEAD_PAIRS),
                  plan.prev(2 * HEAD_PAIRS), plan.cur(0), plan.shared(), plan.shared()],
        out_specs=[plan.cur(0)] * 5,
        out_shape=[jax.ShapeDtypeStruct((S, D_MODEL), F32)] * 5,
        compiler_params=_params(("parallel", "arbitrary"), VMEM_BIG),
    )(qkv, qkv, qkv, qkv, qkv, do, lse, dlt)


def _dil_combine(parts, rope, name):
    S = parts[0][0].shape[0]
    tm = DIL_BLK
    nblk = S // tm
    dils = [d for _, d in DILATED_BRANCHES]

    def body(*refs):
        ins = refs[:5 * len(dils)]
        c_ref, s1_ref, s2_ref, o_ref = refs[5 * len(dils):]
        i = pl.program_id(0)
        tabs = (c_ref[...], s1_ref[...], s2_ref[...])
        dq = dk = dv = None
        for b, d in enumerate(dils):
            dq_r, dka_r, dkb_r, dva_r, dvb_r = ins[5 * b:5 * b + 5]
            live = (i + d < nblk).astype(F32)
            tq = dq_r[...]
            tk = dka_r[...] + live * dkb_r[...]
            tv = dva_r[...] + live * dvb_r[...]
            dq = tq if dq is None else dq + tq
            dk = tk if dk is None else dk + tk
            dv = tv if dv is None else dv + tv
        dq = dq * Q_SCALE
        for g in range(HEAD_PAIRS):
            cols = slice(g * LANES, (g + 1) * LANES)
            o_ref[:, g * LANES:(g + 1) * LANES] = _rope_apply(dq[:, cols], *tabs, -1.0).astype(BF16)
            o_ref[:, D_MODEL + g * LANES:D_MODEL + (g + 1) * LANES] = _rope_apply(dk[:, cols], *tabs, -1.0).astype(BF16)
        o_ref[:, 2 * D_MODEL:] = dv.astype(BF16)

    row = pl.BlockSpec((tm, D_MODEL), lambda i: (i, 0))
    in_specs = []
    args = []
    for (dq_b, dka, dkb, dva, dvb), d in zip(parts, dils):
        ahead = pl.BlockSpec((tm, D_MODEL), lambda i, d=d: (jnp.minimum(i + d, nblk - 1), 0))
        in_specs += [row, row, ahead, row, ahead]
        args += [dq_b, dka, dkb, dva, dvb]
    in_specs += [pl.BlockSpec((tm, LANES), lambda i: (i, 0))] * 3
    return pl.pallas_call(
        body, name=name, grid=(nblk,),
        in_specs=in_specs,
        out_specs=pl.BlockSpec((tm, 3 * D_MODEL), lambda i: (i, 0)),
        out_shape=jax.ShapeDtypeStruct((S, 3 * D_MODEL), BF16),
        compiler_params=_params(("parallel",), VMEM_BIG),
    )(*args, *rope)


def _mesh_pos():
    return lax.axis_index("x"), lax.axis_index("y"), lax.axis_index("c")


def _all_gather(shard, name):
    R, C = shard.shape

    def body(x_ref, out_ref, send_sems, recv_sems, local_sem):
        x, y, c = _mesh_pos()
        me, sibling = (x, y, c), (x, y, 1 - c)
        chips = [(1 - x, y), (x, 1 - y), (1 - x, 1 - y)]

        def blk(p):
            return out_ref.at[4 * p[0] + 2 * p[1] + p[2]]

        def copy(k, block, to, src=None):
            return pltpu.make_async_remote_copy(
                src_ref=blk(block) if src is None else src, dst_ref=blk(block),
                send_sem=send_sems.at[k], recv_sem=recv_sems.at[k],
                device_id=to, device_id_type=pl.DeviceIdType.MESH)

        mine = pltpu.make_async_copy(x_ref, blk(me), local_sem)
        mine.start()
        first = [copy(0, me, sibling, src=x_ref)]
        first += [copy(1 + j, me, (*chip, c), src=x_ref) for j, chip in enumerate(chips)]
        for cp in first:
            cp.start()
        passed = [copy(4 + j, (*chip, c), sibling) for j, chip in enumerate(chips)]
        for j, chip in enumerate(chips):
            copy(1 + j, (*chip, c), me).wait_recv()
            passed[j].start()
        copy(0, sibling, me).wait_recv()
        for j, chip in enumerate(chips):
            copy(4 + j, (*chip, 1 - c), me).wait_recv()
        for cp in first + passed:
            cp.wait_send()
        mine.wait()

    return pl.pallas_call(
        body, name=name,
        in_specs=[pl.BlockSpec(memory_space=pl.ANY)],
        out_specs=pl.BlockSpec(memory_space=pl.ANY),
        out_shape=jax.ShapeDtypeStruct((N_DEV, R, C), shard.dtype),
        scratch_shapes=[pltpu.SemaphoreType.DMA((7,)), pltpu.SemaphoreType.DMA((7,)), pltpu.SemaphoreType.DMA],
    )(shard)


def _rs_pair(g, name):
    _, R, C = g.shape

    def body(g_ref, out_ref, send_sems, recv_sems):
        x, y, c = _mesh_pos()
        sibling = (x, y, 1 - c)
        cps = []
        for chip in range(4):
            cps.append(pltpu.make_async_remote_copy(
                src_ref=g_ref.at[2 * chip + (1 - c)], dst_ref=out_ref.at[chip],
                send_sem=send_sems.at[chip], recv_sem=recv_sems.at[chip],
                device_id=sibling, device_id_type=pl.DeviceIdType.MESH))
        for cp in cps:
            cp.start()
        for cp in cps:
            cp.wait_recv()
        for cp in cps:
            cp.wait_send()

    return pl.pallas_call(
        body, name=name,
        in_specs=[pl.BlockSpec(memory_space=pl.ANY)],
        out_specs=pl.BlockSpec(memory_space=pl.ANY),
        out_shape=jax.ShapeDtypeStruct((4, R, C), g.dtype),
        scratch_shapes=[pltpu.SemaphoreType.DMA((4,)), pltpu.SemaphoreType.DMA((4,))],
    )(g)


def _pair_add(g, got, cidx, name):
    _, R, C = g.shape
    tr = 256

    def body(c_ref, g_ref, r_ref, o_ref, ob_ref):
        s = g_ref[...] + r_ref[...].astype(F32)
        o_ref[...] = s
        ob_ref[...] = s.astype(BF16)

    blk = pl.BlockSpec((None, tr, C), lambda k, i, c: (k, i, 0))
    return pl.pallas_call(
        body, name=name,
        grid_spec=pltpu.PrefetchScalarGridSpec(
            num_scalar_prefetch=1, grid=(4, R // tr),
            in_specs=[pl.BlockSpec((None, tr, C), lambda k, i, c: (2 * k + c[0], i, 0)), blk],
            out_specs=[blk, blk]),
        out_shape=[jax.ShapeDtypeStruct((4, R, C), F32), jax.ShapeDtypeStruct((4, R, C), BF16)],
        compiler_params=_params(("parallel", "parallel")),
    )(cidx, g, got)


def _rs_chips(p, name):
    _, R, C = p.shape

    def body(p_ref, out_ref, send_sems, recv_sems):
        x, y, c = _mesh_pos()
        chips = [(1 - x, y), (x, 1 - y), (1 - x, 1 - y)]
        cps = []
        for j, (cx, cy) in enumerate(chips):
            cps.append(pltpu.make_async_remote_copy(
                src_ref=p_ref.at[2 * cx + cy], dst_ref=out_ref.at[j],
                send_sem=send_sems.at[j], recv_sem=recv_sems.at[j],
                device_id=(cx, cy, c), device_id_type=pl.DeviceIdType.MESH))
        for cp in cps:
            cp.start()
        for cp in cps:
            cp.wait_recv()
        for cp in cps:
            cp.wait_send()

    return pl.pallas_call(
        body, name=name,
        in_specs=[pl.BlockSpec(memory_space=pl.ANY)],
        out_specs=pl.BlockSpec(memory_space=pl.ANY),
        out_shape=jax.ShapeDtypeStruct((3, R, C), p.dtype),
        scratch_shapes=[pltpu.SemaphoreType.DMA((3,)), pltpu.SemaphoreType.DMA((3,))],
    )(p)


def _adamw_math(w, g, m, v):
    m2 = ADAM_B1 * m + (1.0 - ADAM_B1) * g
    v2 = ADAM_B2 * v + (1.0 - ADAM_B2) * (g * g)
    m_hat = m2 / (1.0 - ADAM_B1 ** ADAM_STEP)
    v_hat = v2 / (1.0 - ADAM_B2 ** ADAM_STEP)
    delta = -ADAM_LR * (m_hat / (jnp.sqrt(v_hat) + ADAM_EPS) + ADAM_WD * w)
    return delta, m2, v2


def _adamw_shard(p, got, chip_idx, w, m, v, name):
    R, C = w.shape
    tr = 256

    def body(k_ref, p_ref, r_ref, w_ref, m_ref, v_ref, g_out, d_out, m_out, v_out):
        g = ((p_ref[...] + r_ref[0].astype(F32)) + r_ref[1].astype(F32)) + r_ref[2].astype(F32)
        delta, m2, v2 = _adamw_math(w_ref[...], g, m_ref[...], v_ref[...])
        g_out[...] = g
        d_out[...] = delta
        m_out[...] = m2
        v_out[...] = v2

    row = pl.BlockSpec((tr, C), lambda i, k: (i, 0))
    return pl.pallas_call(
        body, name=name,
        grid_spec=pltpu.PrefetchScalarGridSpec(
            num_scalar_prefetch=1, grid=(R // tr,),
            in_specs=[pl.BlockSpec((None, tr, C), lambda i, k: (k[0], i, 0)),
                      pl.BlockSpec((3, tr, C), lambda i, k: (0, i, 0)), row, row, row],
            out_specs=[row] * 4),
        out_shape=[jax.ShapeDtypeStruct((R, C), F32)] * 4,
        compiler_params=_params(("parallel",)),
    )(chip_idx, p, got, w, m, v)


def _adamw_small(gathered, w, m, v, name):
    _, R, C = gathered.shape

    def body(a_ref, w_ref, m_ref, v_ref, g_out, d_out, m_out, v_out):
        g = a_ref[0]
        for k in range(1, N_DEV):
            g = g + a_ref[k]
        delta, m2, v2 = _adamw_math(w_ref[...], g, m_ref[...], v_ref[...])
        g_out[...] = g
        d_out[...] = delta
        m_out[...] = m2
        v_out[...] = v2

    return pl.pallas_call(
        body, name=name, out_shape=[jax.ShapeDtypeStruct((R, C), F32)] * 4,
    )(gathered, w, m, v)


def _rope_tables(S):
    half = ROPE_DIM // 2
    inv_freq = ROPE_THETA ** (-jnp.arange(half, dtype=F32) / half)
    ang = jnp.arange(S, dtype=jnp.int32).astype(F32)[:, None] * inv_freq[None, :]
    cos, sin = jnp.cos(ang), jnp.sin(ang)
    ones = jnp.ones((S, HEAD_DIM - ROPE_DIM), F32)
    zeros = jnp.zeros((S, HEAD_DIM - ROPE_DIM), F32)
    zh = jnp.zeros((S, half), F32)
    c = jnp.concatenate([cos, cos, ones], axis=1)
    s1 = jnp.concatenate([zh, sin, zeros], axis=1)
    s2 = jnp.concatenate([-sin, zh, zeros], axis=1)
    two = lambda t: jnp.concatenate([t, t], axis=1)
    return two(c), two(s1), two(s2)


def _chunk_transposed(a, S):
    return a.reshape(S // SB_CH, SB_CH, HEAD_PAIRS, LANES).transpose(2, 0, 3, 1)


def _flat_shards(ws):
    return jnp.concatenate([w.reshape(-1, D_MODEL) for layer in ws for w in layer], axis=0)


def kernel(x, w_qkv_0, w_o_0, ln1_g_0, ln1_b_0, w_ff1_0, w_ff2_0, ln2_g_0, ln2_b_0, w_qkv_1, w_o_1, ln1_g_1, ln1_b_1, w_ff1_1, w_ff2_1, ln2_g_1, ln2_b_1, loss_target, m_w_qkv_0, m_w_o_0, m_ln1_g_0, m_ln1_b_0, m_w_ff1_0, m_w_ff2_0, m_ln2_g_0, m_ln2_b_0, m_w_qkv_1, m_w_o_1, m_ln1_g_1, m_ln1_b_1, m_w_ff1_1, m_w_ff2_1, m_ln2_g_1, m_ln2_b_1, v_w_qkv_0, v_w_o_0, v_ln1_g_0, v_ln1_b_0, v_w_ff1_0, v_w_ff2_0, v_ln2_g_0, v_ln2_b_0, v_w_qkv_1, v_w_o_1, v_ln1_g_1, v_ln1_b_1, v_w_ff1_1, v_w_ff2_1, v_ln2_g_1, v_ln2_b_1):
    S = x.shape[1]
    x0 = x.reshape(S, D_MODEL)
    target = loss_target.reshape(S, D_MODEL)
    mats = ((w_qkv_0, w_o_0, w_ff1_0, w_ff2_0), (w_qkv_1, w_o_1, w_ff1_1, w_ff2_1))
    mats_m = ((m_w_qkv_0, m_w_o_0, m_w_ff1_0, m_w_ff2_0), (m_w_qkv_1, m_w_o_1, m_w_ff1_1, m_w_ff2_1))
    mats_v = ((v_w_qkv_0, v_w_o_0, v_w_ff1_0, v_w_ff2_0), (v_w_qkv_1, v_w_o_1, v_w_ff1_1, v_w_ff2_1))
    vecs = (ln1_g_0, ln1_b_0, ln2_g_0, ln2_b_0, ln1_g_1, ln1_b_1, ln2_g_1, ln2_b_1)
    vecs_m = (m_ln1_g_0, m_ln1_b_0, m_ln2_g_0, m_ln2_b_0, m_ln1_g_1, m_ln1_b_1, m_ln2_g_1, m_ln2_b_1)
    vecs_v = (v_ln1_g_0, v_ln1_b_0, v_ln2_g_0, v_ln2_b_0, v_ln1_g_1, v_ln1_b_1, v_ln2_g_1, v_ln2_b_1)

    w_flat = _flat_shards(mats)
    w_all = _all_gather(w_flat.astype(BF16), "ag_weights")
    layers = []
    for l in range(N_LAYERS):
        base = l * LAYER_ROWS
        r0, r1, r2, r3 = np.cumsum((0,) + SHARD_ROWS)[:4] + base
        layers.append(dict(
            qkv=w_all[:, r0:r0 + 384].reshape(N_DEV, D_MODEL, 384),
            o=w_all[:, r1:r1 + 128].reshape(D_MODEL, D_MODEL),
            ff1=w_all[:, r2:r2 + 512].reshape(N_DEV, D_MODEL, 512),
            ff2=w_all[:, r3:r3 + 512].reshape(D_FF, D_MODEL),
            g1=vecs[4 * l].reshape(1, D_MODEL), b1=vecs[4 * l + 1].reshape(1, D_MODEL),
            g2=vecs[4 * l + 2].reshape(1, D_MODEL), b2=vecs[4 * l + 3].reshape(1, D_MODEL)))

    rope = _rope_tables(S)
    tmat_later = _sb_tmat(True)
    tmat_upto = _sb_tmat(False)

    saved = []
    xin, xinb = x0, x0.astype(BF16)
    for l, W in enumerate(layers):
        sv = dict(xin=xin, xinb=xinb)
        qkv = _qkv_proj(xinb, W["qkv"], rope if l == 1 else None, Q_SCALE * LOG2E if l == 0 else Q_SCALE,
                        BF16 if l == 0 else F32, f"qkv_proj_{l}")
        sv["qkv"] = qkv
        if l == 0:
            vT3 = _chunk_transposed(qkv[:, 2 * D_MODEL:], S)
            o, sb_tiles = _sb_fwd(qkv, vT3, tmat_later, "sb_fwd")
            ob = o.astype(BF16)
            sv["sb_tiles"] = sb_tiles
        else:
            outs = [_dil_fwd(qkv, d, f"dil_fwd_{d}") for _, d in DILATED_BRANCHES]
            o, ob, lse = _dil_merge([t[0] for t in outs], [t[1] for t in outs], "dil_merge")
            sv["lse"] = lse
        sv.update(o=o, ob=ob)
        y1, x1, x1b = _mm_res_ln(ob, xin, W["o"], W["g1"], W["b1"], f"attn_out_ln_{l}")
        hpre, h = _ff1(x1b, W["ff1"], f"ff1_{l}")
        y2, x2, x2b = _mm_res_ln(h, x1, W["ff2"], W["g2"], W["b2"], f"ff2_ln_{l}")
        sv.update(y1=y1, x1=x1, x1b=x1b, hpre=hpre, h=h, y2=y2)
        saved.append(sv)
        xin, xinb = x2, x2b

    dout, loss_parts = _loss_grad(xin, target, "loss_grad")
    loss = lax.psum(jnp.sum(loss_parts), MESH_AXES)

    gmats = [None] * N_LAYERS
    gvecs = [None] * (4 * N_LAYERS)
    for l in reversed(range(N_LAYERS)):
        W, sv = layers[l], saved[l]
        dy2, dy2b, gb2 = _ln_bwd(dout, sv["y2"], W["g2"], f"ln2_bwd_{l}")
        dhp = _dh(dy2b, W["ff2"], sv["hpre"], f"dh_{l}")
        g_ff2 = _mm_tn(sv["h"], dy2b, 512, D_MODEL, False, f"dw_ff2_{l}")
        dx1 = _dx_blk(dy2, dhp, W["ff1"], f"dx_ff1_{l}")
        g_ff1 = _mm_tn(sv["x1b"], dhp, D_MODEL, 512, True, f"dw_ff1_{l}")
        dy1, dy1b, gb1 = _ln_bwd(dx1, sv["y1"], W["g1"], f"ln1_bwd_{l}")
        do = _mm_nt_plain(dy1b, W["o"], f"do_{l}")
        g_o = _mm_tn(sv["ob"], dy1b, 512, D_MODEL, False, f"dw_o_{l}")
        if l == 0:
            kT3 = _chunk_transposed(sv["qkv"][:, D_MODEL:2 * D_MODEL], S)
            dq, dk, dv = _sb_bwd(sv["qkv"], kT3, do, sv["sb_tiles"], tmat_upto, "sb_bwd")
            dqkv = jnp.concatenate([dq, dk, dv], axis=1).astype(BF16)
        else:
            dlt = _head_sums(do, sv["o"], "head_sums")
            parts = [_dil_bwd(sv["qkv"], do, sv["lse"], dlt, d, f"dil_bwd_{d}") for _, d in DILATED_BRANCHES]
            dqkv = _dil_combine(parts, rope, "dil_combine")
        dout = _dx_blk(dy1, dqkv, W["qkv"], f"dx_qkv_{l}")
        g_qkv = _mm_tn(sv["xinb"], dqkv, D_MODEL, 384, True, f"dw_qkv_{l}")
        gmats[l] = (g_qkv.reshape(N_DEV, 384, D_MODEL), g_o.reshape(N_DEV, 128, D_MODEL),
                    g_ff1.reshape(N_DEV, 512, D_MODEL), g_ff2.reshape(N_DEV, 512, D_MODEL))
        gvecs[4 * l:4 * l + 4] = [gb1[0], gb1[1], gb2[0], gb2[1]]
    grad_x = dout.reshape(1, S, D_MODEL)

    cx, cy, cc = _mesh_pos()
    g_all = jnp.concatenate([g for layer in gmats for g in layer], axis=1)
    got_pair = _rs_pair(g_all.astype(BF16), "rs_pair")
    chip_part, chip_part_b = _pair_add(g_all, got_pair, cc.astype(jnp.int32).reshape(1), "rs_pair_add")
    got_chips = _rs_chips(chip_part_b, "rs_chips")
    chip_idx = (2 * cx + cy).astype(jnp.int32).reshape(1)
    g_sh, d_sh, m_sh, v_sh = _adamw_shard(chip_part, got_chips, chip_idx, w_flat, _flat_shards(mats_m),
                                          _flat_shards(mats_v), "adamw_mats")

    def unflat(a):
        out, pos = [], 0
        for layer in mats:
            for w in layer:
                n = w.size // D_MODEL
                out.append(a[pos:pos + n].reshape(w.shape))
                pos += n
        return out

    gv_all = _all_gather(jnp.stack(gvecs), "ag_vec_grads")
    g_v, d_v, m_v, v_v = _adamw_small(gv_all, jnp.stack(vecs), jnp.stack(vecs_m), jnp.stack(vecs_v), "adamw_vecs")

    def interleave(mat_list, vec_arr):
        out = []
        for l in range(N_LAYERS):
            qkv_, o_, ff1_, ff2_ = mat_list[4 * l:4 * l + 4]
            out += [qkv_, o_, vec_arr[4 * l], vec_arr[4 * l + 1], ff1_, ff2_, vec_arr[4 * l + 2], vec_arr[4 * l + 3]]
        return out

    return (loss, grad_x, *interleave(unflat(g_sh), g_v), *interleave(unflat(d_sh), d_v),
            *interleave(unflat(m_sh), m_v), *interleave(unflat(v_sh), v_v))
```

```python
import functools
import math

import jax
import jax.numpy as jnp
import numpy as np
from jax import lax
from jax.experimental import pallas as pl
from jax.experimental.pallas import tpu as pltpu

F32 = jnp.float32
BF16 = jnp.bfloat16

D_MODEL = 1024
N_HEADS = 16
HEAD_DIM = 64
D_FF = 4096
N_DEV = 8
N_LAYERS = 2
ROPE_THETA = 500000.0
ROPE_DIM = 16
DILATED_BRANCHES = ((128, 1), (512, 4), (2048, 16))
ALPHA = (2 * N_LAYERS) ** 0.25
LN_EPS = 1e-5
Q_SCALE = 1.0 / math.sqrt(HEAD_DIM)
LOG2E = math.log2(math.e)
LN2 = math.log(2.0)
ADAM_LR, ADAM_B1, ADAM_B2, ADAM_EPS, ADAM_WD, ADAM_STEP = 0.001, 0.9, 0.999, 1e-08, 0.01, 10

LANES = 128
HEAD_PAIRS = D_MODEL // LANES
SB_TQ = 256
SB_CH = 256
SB_STEPS = 4
SB_SAVE_SLOTS = 2 * SB_STEPS
SB_LOAD_SLOTS = 12
SB_LOAD_AHEAD = SB_LOAD_SLOTS - SB_STEPS - 1
SB_DEAD = -160.0
DIL_BLK = 128
VMEM_BIG = 56 * 2 ** 20
MESH_AXES = ("x", "y", "c")

SHARD_ROWS = (384, 128, 512, 512)
LAYER_ROWS = sum(SHARD_ROWS)
ALL_ROWS = N_LAYERS * LAYER_ROWS


def _params(sem=None, vmem=None):
    kw = {}
    if sem is not None:
        kw["dimension_semantics"] = sem
    if vmem is not None:
        kw["vmem_limit_bytes"] = vmem
    return pltpu.CompilerParams(**kw)


def _dot(a, b):
    return jnp.dot(a, b, preferred_element_type=F32)


def _dot_nt(a, b):
    return lax.dot_general(a, b, (((1,), (1,)), ((), ())), preferred_element_type=F32)


def _dot_tn(a, b):
    return lax.dot_general(a, b, (((0,), (0,)), ((), ())), preferred_element_type=F32)


def _split3(p):
    hi = p.astype(BF16)
    r1 = p - hi.astype(F32)
    mid = r1.astype(BF16)
    lo = (r1 - mid.astype(F32)).astype(BF16)
    return hi, mid, lo


def _dot3(p, e):
    hi, mid, lo = _split3(p)
    return _dot(hi, e) + _dot(mid, e) + _dot(lo, e)


def _rope_apply(a, c, s1, s2, sign):
    return a * c + sign * (pltpu.roll(a, 8, 1) * s1 + pltpu.roll(a, LANES - 8, 1) * s2)


def _qkv_proj(xb, w_blk, rope, q_mult, out_dtype, name):
    S = xb.shape[0]
    tm = 512
    n_rope = 0 if rope is None else 3

    def body(*refs):
        x_ref, w_ref = refs[:2]
        tabs = [r[...] for r in refs[2:2 + n_rope]]
        o_ref = refs[2 + n_rope]
        x = x_ref[...]
        for j in range(N_DEV):
            acc = _dot(x, w_ref[j])
            for g in range(3):
                col = j * 384 + g * LANES
                a = acc[:, g * LANES:(g + 1) * LANES]
                if n_rope and col < 2 * D_MODEL:
                    a = _rope_apply(a, *tabs, 1.0)
                if col < D_MODEL:
                    a = a * q_mult
                o_ref[:, col:col + LANES] = a.astype(out_dtype)

    tab_specs = [pl.BlockSpec((tm, LANES), lambda i: (i, 0))] * n_rope
    return pl.pallas_call(
        body, name=name, grid=(S // tm,),
        in_specs=[pl.BlockSpec((tm, D_MODEL), lambda i: (i, 0)),
                  pl.BlockSpec((N_DEV, D_MODEL, 384), lambda i: (0, 0, 0))] + tab_specs,
        out_specs=pl.BlockSpec((tm, 3 * D_MODEL), lambda i: (i, 0)),
        out_shape=jax.ShapeDtypeStruct((S, 3 * D_MODEL), out_dtype),
        compiler_params=_params(("parallel",), VMEM_BIG),
    )(xb, w_blk, *(rope or ()))


def _layer_norm_rows(y, g, b):
    mu = jnp.mean(y, axis=-1, keepdims=True)
    yc = y - mu
    var = jnp.mean(yc * yc, axis=-1, keepdims=True)
    return yc * lax.rsqrt(var + LN_EPS) * g + b


def _mm_res_ln(a, xres, w, g, b, name):
    S, K = a.shape
    tm = 512 if K <= 1024 else 256

    def body(a_ref, x_ref, w_ref, g_ref, b_ref, y_ref, xn_ref, xb_ref):
        y = ALPHA * x_ref[...] + _dot(a_ref[...], w_ref[...])
        xn = _layer_norm_rows(y, g_ref[...], b_ref[...])
        y_ref[...] = y
        xn_ref[...] = xn
        xb_ref[...] = xn.astype(BF16)

    row = lambda i: (i, 0)
    fix = lambda i: (0, 0)
    return pl.pallas_call(
        body, name=name, grid=(S // tm,),
        in_specs=[pl.BlockSpec((tm, K), row), pl.BlockSpec((tm, D_MODEL), row),
                  pl.BlockSpec((K, D_MODEL), fix), pl.BlockSpec((1, D_MODEL), fix),
                  pl.BlockSpec((1, D_MODEL), fix)],
        out_specs=[pl.BlockSpec((tm, D_MODEL), row)] * 3,
        out_shape=[jax.ShapeDtypeStruct((S, D_MODEL), F32), jax.ShapeDtypeStruct((S, D_MODEL), F32),
                   jax.ShapeDtypeStruct((S, D_MODEL), BF16)],
        compiler_params=_params(("parallel",), VMEM_BIG),
    )(a, xres, w, g, b)


def _ff1(xb, w_blk, name):
    S = xb.shape[0]
    tm = 256

    def body(x_ref, w_ref, hp_ref, h_ref):
        x = x_ref[...]
        for j in range(N_DEV):
            acc = _dot(x, w_ref[j])
            r = jnp.maximum(acc, 0.0)
            hp_ref[:, j * 512:(j + 1) * 512] = acc
            h_ref[:, j * 512:(j + 1) * 512] = (r * r).astype(BF16)

    return pl.pallas_call(
        body, name=name, grid=(S // tm,),
        in_specs=[pl.BlockSpec((tm, D_MODEL), lambda i: (i, 0)),
                  pl.BlockSpec((N_DEV, D_MODEL, 512), lambda i: (0, 0, 0))],
        out_specs=[pl.BlockSpec((tm, D_FF), lambda i: (i, 0))] * 2,
        out_shape=[jax.ShapeDtypeStruct((S, D_FF), F32), jax.ShapeDtypeStruct((S, D_FF), BF16)],
        compiler_params=_params(("parallel",), VMEM_BIG),
    )(xb, w_blk)


def _loss_grad(y, target, name):
    S = y.shape[0]
    tm = 512

    def body(y_ref, t_ref, dy_ref, l_ref):
        @pl.when(pl.program_id(0) == 0)
        def _():
            l_ref[...] = jnp.zeros_like(l_ref)

        err = y_ref[...] - t_ref[...]
        dy_ref[...] = err * (1.0 / D_MODEL)
        sq = err * err
        rows = sq[0:8]
        for r in range(1, tm // 8):
            rows = rows + sq[r * 8:(r + 1) * 8]
        acc = rows[:, 0:LANES]
        for g in range(1, D_MODEL // LANES):
            acc = acc + rows[:, g * LANES:(g + 1) * LANES]
        l_ref[...] += acc * (0.5 / D_MODEL)

    return pl.pallas_call(
        body, name=name, grid=(S // tm,),
        in_specs=[pl.BlockSpec((tm, D_MODEL), lambda i: (i, 0))] * 2,
        out_specs=[pl.BlockSpec((tm, D_MODEL), lambda i: (i, 0)), pl.BlockSpec((8, LANES), lambda i: (0, 0))],
        out_shape=[jax.ShapeDtypeStruct((S, D_MODEL), F32), jax.ShapeDtypeStruct((8, LANES), F32)],
        compiler_params=_params(("arbitrary",)),
    )(y, target)


def _ln_bwd(dout, y, g, name):
    S = y.shape[0]
    tm = 512
    steps = S // tm

    def body(d_ref, y_ref, g_ref, dy_ref, dyb_ref, gb_ref, acc_g, acc_b):
        i = pl.program_id(0)

        @pl.when(i == 0)
        def _():
            acc_g[...] = jnp.zeros_like(acc_g)
            acc_b[...] = jnp.zeros_like(acc_b)

        d = d_ref[...]
        yv = y_ref[...]
        mu = jnp.mean(yv, axis=-1, keepdims=True)
        yc = yv - mu
        var = jnp.mean(yc * yc, axis=-1, keepdims=True)
        rstd = lax.rsqrt(var + LN_EPS)
        xhat = yc * rstd
        dxh = d * g_ref[...]
        m1 = jnp.mean(dxh, axis=-1, keepdims=True)
        m2 = jnp.mean(dxh * xhat, axis=-1, keepdims=True)
        dy = rstd * (dxh - m1 - xhat * m2)
        dy_ref[...] = dy
        dyb_ref[...] = dy.astype(BF16)
        pg = d * xhat
        sg = pg[0:8]
        sb = d[0:8]
        for r in range(1, tm // 8):
            sg = sg + pg[r * 8:(r + 1) * 8]
            sb = sb + d[r * 8:(r + 1) * 8]
        acc_g[...] += sg
        acc_b[...] += sb

        @pl.when(i == steps - 1)
        def _():
            gb_ref[0:1, :] = jnp.sum(acc_g[...], axis=0, keepdims=True)
            gb_ref[1:2, :] = jnp.sum(acc_b[...], axis=0, keepdims=True)

    row = lambda i: (i, 0)
    fix = lambda i: (0, 0)
    return pl.pallas_call(
        body, name=name, grid=(steps,),
        in_specs=[pl.BlockSpec((tm, D_MODEL), row), pl.BlockSpec((tm, D_MODEL), row), pl.BlockSpec((1, D_MODEL), fix)],
        out_specs=[pl.BlockSpec((tm, D_MODEL), row), pl.BlockSpec((tm, D_MODEL), row), pl.BlockSpec((2, D_MODEL), fix)],
        out_shape=[jax.ShapeDtypeStruct((S, D_MODEL), F32), jax.ShapeDtypeStruct((S, D_MODEL), BF16),
                   jax.ShapeDtypeStruct((2, D_MODEL), F32)],
        scratch_shapes=[pltpu.VMEM((8, D_MODEL), F32), pltpu.VMEM((8, D_MODEL), F32)],
        compiler_params=_params(("arbitrary",)),
    )(dout, y, g)


def _dh(dyb, w2, hpre, name):
    S = dyb.shape[0]
    tm = 256
    tn = 512

    def body(dy_ref, w_ref, hp_ref, o_ref):
        dy = dy_ref[...]
        for n in range(0, D_FF, tn):
            dh = _dot_nt(dy, w_ref[n:n + tn, :])
            o_ref[:, n:n + tn] = (dh * (2.0 * jnp.maximum(hp_ref[:, n:n + tn], 0.0))).astype(BF16)

    return pl.pallas_call(
        body, name=name, grid=(S // tm,),
        in_specs=[pl.BlockSpec((tm, D_MODEL), lambda i: (i, 0)), pl.BlockSpec((D_FF, D_MODEL), lambda i: (0, 0)),
                  pl.BlockSpec((tm, D_FF), lambda i: (i, 0))],
        out_specs=pl.BlockSpec((tm, D_FF), lambda i: (i, 0)),
        out_shape=jax.ShapeDtypeStruct((S, D_FF), BF16),
        compiler_params=_params(("parallel",), VMEM_BIG),
    )(dyb, w2, hpre)


def _dx_blk(dres, dz, w_blk, name):
    S, N = dz.shape
    bw = w_blk.shape[2]
    tm = 256

    def body(r_ref, z_ref, w_ref, o_ref):
        acc = ALPHA * r_ref[...]
        for j in range(N_DEV):
            acc = acc + _dot_nt(z_ref[:, j * bw:(j + 1) * bw], w_ref[j])
        o_ref[...] = acc

    return pl.pallas_call(
        body, name=name, grid=(S // tm,),
        in_specs=[pl.BlockSpec((tm, D_MODEL), lambda i: (i, 0)), pl.BlockSpec((tm, N), lambda i: (i, 0)),
                  pl.BlockSpec((N_DEV, D_MODEL, bw), lambda i: (0, 0, 0))],
        out_specs=pl.BlockSpec((tm, D_MODEL), lambda i: (i, 0)),
        out_shape=jax.ShapeDtypeStruct((S, D_MODEL), F32),
        compiler_params=_params(("parallel",), VMEM_BIG),
    )(dres, dz, w_blk)


def _mm_nt_plain(a, w, name):
    S = a.shape[0]
    tm = 512

    def body(a_ref, w_ref, o_ref):
        o_ref[...] = _dot_nt(a_ref[...], w_ref[...])

    return pl.pallas_call(
        body, name=name, grid=(S // tm,),
        in_specs=[pl.BlockSpec((tm, D_MODEL), lambda i: (i, 0)), pl.BlockSpec((D_MODEL, D_MODEL), lambda i: (0, 0))],
        out_specs=pl.BlockSpec((tm, D_MODEL), lambda i: (i, 0)),
        out_shape=jax.ShapeDtypeStruct((S, D_MODEL), F32),
        compiler_params=_params(("parallel",)),
    )(a, w)


def _mm_tn(a, b, ta, tb, blocked, name):
    S, Ka = a.shape
    Nb = b.shape[1]
    ts = 2048

    def body(a_ref, b_ref, o_ref):
        @pl.when(pl.program_id(2) == 0)
        def _():
            o_ref[...] = jnp.zeros_like(o_ref)

        o_ref[...] += _dot_tn(a_ref[...], b_ref[...])

    if blocked:
        out_spec = pl.BlockSpec((None, ta, tb), lambda i, j, s: (j, i, 0))
        out_shape = jax.ShapeDtypeStruct((Nb // tb, Ka, tb), F32)
    else:
        out_spec = pl.BlockSpec((ta, tb), lambda i, j, s: (i, j))
        out_shape = jax.ShapeDtypeStruct((Ka, Nb), F32)
    return pl.pallas_call(
        body, name=name, grid=(Ka // ta, Nb // tb, S // ts),
        in_specs=[pl.BlockSpec((ts, ta), lambda i, j, s: (s, i)), pl.BlockSpec((ts, tb), lambda i, j, s: (s, j))],
        out_specs=out_spec, out_shape=out_shape,
        compiler_params=_params(("parallel", "parallel", "arbitrary"), VMEM_BIG),
    )(a, b)


def _head_sums(do, o, name):
    S = do.shape[0]
    tm = 512
    sel = (np.arange(D_MODEL)[:, None] // HEAD_DIM == np.arange(LANES)[None, :]).astype(np.float32)

    def body(d_ref, o_ref, e_ref, out_ref):
        out_ref[...] = _dot3(d_ref[...] * o_ref[...], e_ref[...])

    return pl.pallas_call(
        body, name=name, grid=(S // tm,),
        in_specs=[pl.BlockSpec((tm, D_MODEL), lambda i: (i, 0))] * 2 + [pl.BlockSpec((D_MODEL, LANES), lambda i: (0, 0))],
        out_specs=pl.BlockSpec((tm, LANES), lambda i: (i, 0)),
        out_shape=jax.ShapeDtypeStruct((S, LANES), F32),
        compiler_params=_params(("parallel",)),
    )(do, o, jnp.asarray(sel, BF16))


def _sb_tmat(later):
    r = np.arange(SB_CH)
    t = (r[None, :] > r[:, None]) if later else (r[None, :] <= r[:, None])
    return jnp.asarray(np.concatenate([t.astype(np.float32), np.ones((8, SB_CH), np.float32)], axis=0), BF16)


def _sb_gates(z2):
    neg_abs = lax.bitcast_convert_type(lax.bitcast_convert_type(z2, jnp.uint32) | jnp.uint32(0x80000000), F32)
    l1 = jnp.log2(1.0 + jnp.exp2(neg_abs))
    a = jnp.minimum(z2, 0.0) - l1
    return a, a - z2


def _head_masks(x2):
    lane = lax.broadcasted_iota(jnp.int32, x2.shape, 1)
    zero = jnp.zeros_like(x2)
    return jnp.where(lane < HEAD_DIM, x2, zero), jnp.where(lane >= HEAD_DIM, x2, zero)


def _sb_fwd(qkv, vT3, tmat, name):
    S = qkv.shape[0]
    nq = S // SB_TQ
    nch = S // SB_CH
    ns = SB_SAVE_SLOTS

    def body(q_ref, k_ref, vT_ref, t_ref, o_ref, ws_hbm, first_ref, z_scr, a_scr, cum_scr, oT_scr, stage, sems):
        hp = pl.program_id(0)
        i = pl.program_id(1)
        base = (i * (i + 1)) // 2
        qm = _head_masks(q_ref[...])

        def save(src, sem, c):
            return pltpu.make_async_copy(src, ws_hbm.at[hp, base + c], sem)

        causal = (lax.broadcasted_iota(jnp.int32, (SB_CH, SB_TQ), 0)
                  < lax.broadcasted_iota(jnp.int32, (SB_CH, SB_TQ), 1))

        def head_rows(vTc, h):
            return vTc[h * HEAD_DIM:(h + 1) * HEAD_DIM, :]

        @pl.when(jnp.logical_and(hp == 0, i == 0))
        def _():
            z_scr[...] = jnp.zeros_like(z_scr)
            a_scr[...] = jnp.zeros_like(a_scr)
            cum_scr[...] = jnp.zeros_like(cum_scr)

        oT_scr[...] = jnp.zeros_like(oT_scr)

        def c_valid(t):
            return jnp.logical_and(t >= 2, t - 2 <= i)

        def c_chunk(t):
            return jnp.clip(i + 2 - t, 0, nch - 1)

        def step(t, p, slot, R, own_b, own_c):
            cA = jnp.maximum(i - t, 0)
            kA = k_ref[pl.ds(pl.multiple_of(cA * SB_CH, SB_CH), SB_CH), :]
            valid = c_valid(t)
            vC = vT_ref[c_chunk(t)]
            out = []
            for h in range(2):
                z_scr[p, h] = _dot_nt(kA, qm[h])
                a, lf = _sb_gates(z_scr[1 - p, h])
                if own_b:
                    lf = jnp.where(causal, lf, 0.0)
                a_scr[1 - p, h] = a
                cum_scr[1 - p, h] = _dot(t_ref[...], lf.astype(BF16))
                a_c = a_scr[p, h]
                w = jnp.exp2(a_c + cum_scr[p, h, :SB_CH, :] + R[h])
                if own_c:
                    w = jnp.where(causal, w, 0.0)
                wb = w.astype(BF16)
                stage[slot, 2 * h] = wb
                stage[slot, 2 * h + 1] = a_c.astype(BF16)
                oT_scr[h] += jnp.where(valid, _dot(head_rows(vC, h), wb), 0.0)
                out.append(R[h] + jnp.where(valid, cum_scr[p, h, SB_CH:SB_CH + 1, :], 0.0))
            return tuple(out)

        def trip(tt, R, first):
            half = 0 if first else lax.rem(tt, 2) * SB_STEPS
            if not first:
                for j in range(SB_STEPS):
                    @pl.when(c_valid(SB_STEPS * tt + j - ns))
                    def _():
                        save(stage.at[half + j], sems.at[half + j], 0).wait()

            for j in range(SB_STEPS):
                R = step(SB_STEPS * tt + j, j % 2, half + j, R, first and j == 1, first and j == 2)
            for j in range(SB_STEPS):
                t = SB_STEPS * tt + j

                @pl.when(c_valid(t))
                def _():
                    save(stage.at[half + j], sems.at[half + j], c_chunk(t)).start()

            return R

        z1 = jnp.zeros((1, SB_TQ), F32)
        trips = (i + 3 + SB_STEPS - 1) // SB_STEPS

        def alive(carry):
            tt, R = carry
            return jnp.logical_and(tt < trips, jnp.max(jnp.maximum(R[0], R[1])) > SB_DEAD)

        trips, _ = lax.while_loop(alive, lambda carry: (carry[0] + 1, trip(carry[0], carry[1], False)),
                                  (jnp.int32(1), trip(0, (z1, z1), True)))
        first_ref[hp, i] = jnp.maximum(i - (SB_STEPS * trips - 3), 0)
        for back in (2, 1):
            tl = trips - back
            half = lax.rem(tl + 2, 2) * SB_STEPS
            for j in range(SB_STEPS):
                @pl.when(c_valid(SB_STEPS * tl + j))
                def _():
                    save(stage.at[half + j], sems.at[half + j], 0).wait()

        o_ref[...] = jnp.concatenate([oT_scr[0], oT_scr[1]], axis=0).T

    ntile = nq * (nq + 1) // 2
    return pl.pallas_call(
        body, name=name, grid=(HEAD_PAIRS, nq),
        in_specs=[pl.BlockSpec((SB_TQ, LANES), lambda hp, i: (i, hp)),
                  pl.BlockSpec((S, LANES), lambda hp, i: (0, HEAD_PAIRS + hp)),
                  pl.BlockSpec((None, nch, LANES, SB_CH), lambda hp, i: (hp, 0, 0, 0)),
                  pl.BlockSpec((SB_CH + 8, SB_CH), lambda hp, i: (0, 0))],
        out_specs=[pl.BlockSpec((SB_TQ, LANES), lambda hp, i: (i, hp)), pl.BlockSpec(memory_space=pl.ANY),
                   pl.BlockSpec(memory_space=pltpu.SMEM)],
        out_shape=[jax.ShapeDtypeStruct((S, D_MODEL), F32),
                   jax.ShapeDtypeStruct((HEAD_PAIRS, ntile, 4, SB_CH, SB_TQ), BF16),
                   jax.ShapeDtypeStruct((HEAD_PAIRS, nq), jnp.int32)],
        scratch_shapes=[pltpu.VMEM((2, 2, SB_CH, SB_TQ), F32), pltpu.VMEM((2, 2, SB_CH, SB_TQ), F32),
                        pltpu.VMEM((2, 2, SB_CH + 8, SB_TQ), F32), pltpu.VMEM((2, HEAD_DIM, SB_TQ), F32),
                        pltpu.VMEM((ns, 4, SB_CH, SB_TQ), BF16), pltpu.SemaphoreType.DMA((ns,))],
        compiler_params=_params(("arbitrary", "arbitrary"), VMEM_BIG),
    )(qkv, qkv, vT3, tmat)


def _sb_bwd(qkv, kT3, do, ws, first, tmat_g, name):
    S = qkv.shape[0]
    nq = S // SB_TQ
    nch = S // SB_CH
    nl = SB_LOAD_SLOTS
    ahead = SB_LOAD_AHEAD

    def body(first_ref, q_ref, do_ref, v_ref, kT_ref, tg_ref, ws_hbm, dq_ref, dk_hbm, dv_hbm, dk_acc, dv_acc, sems,
             dwv_scr, g_scr, sig_scr, cumg_scr, dqT_scr, ring, ring_sems):
        hp = pl.program_id(0)
        i = pl.program_id(1)
        c0 = first_ref[hp, i]
        n = i - c0
        base = (i * (i + 1)) // 2 + c0

        @pl.when(i == 0)
        def _():
            dk_acc[...] = jnp.zeros_like(dk_acc)
            dv_acc[...] = jnp.zeros_like(dv_acc)

        @pl.when(jnp.logical_and(hp == 0, i == 0))
        def _():
            for scr in (dwv_scr, g_scr, sig_scr, cumg_scr, ring):
                scr[...] = jnp.zeros_like(scr)

        def load(u):
            slot = lax.rem(u, nl)
            return pltpu.make_async_copy(ws_hbm.at[hp, base + u], ring.at[slot], ring_sems.at[slot])

        for u in range(ahead):
            @pl.when(u <= n)
            def _():
                load(u).start()

        dqT_scr[...] = jnp.zeros_like(dqT_scr)
        qm = _head_masks(q_ref[...])
        dom = _head_masks(do_ref[...].astype(BF16))
        causal = (lax.broadcasted_iota(jnp.int32, (SB_CH, SB_TQ), 0)
                  < lax.broadcasted_iota(jnp.int32, (SB_CH, SB_TQ), 1))

        def rows_of(c):
            return pl.ds(pl.multiple_of(c * SB_CH, SB_CH), SB_CH)

        def head_rows(kTc, h):
            return kTc[h * HEAD_DIM:(h + 1) * HEAD_DIM, :]

        def step(t, p, Gs):
            q = 1 - p
            valid_b = jnp.logical_and(t >= 1, t - 1 <= n)
            valid_c = jnp.logical_and(t >= 2, t - 2 <= n)
            c_b = c0 + jnp.clip(t - 1, 0, n)
            c_c = c0 + jnp.clip(t - 2, 0, n)
            slot = jnp.where(valid_b, lax.rem(jnp.maximum(t - 1, 0), nl), nl)
            vA = v_ref[rows_of(c0 + jnp.minimum(t, n)), :]
            kTc = kT_ref[c_c]
            keep = jnp.logical_or(causal, t - 2 != n)
            out = []
            for h in range(2):
                dwv_scr[p, h] = _dot_nt(vA, dom[h])

                wb = ring[slot, 2 * h]
                g = wb.astype(F32) * dwv_scr[q, h]
                g_scr[q, h] = g
                sig_scr[q, h] = jnp.exp2(ring[slot, 2 * h + 1].astype(F32))
                cumg_scr[q, h] = _dot(tg_ref[...], g.astype(BF16))
                dv_h = _dot(wb, dom[h])

                dz = g_scr[p, h] - sig_scr[p, h] * (Gs[h] + cumg_scr[p, h, :SB_CH, :])
                dzb = jnp.where(keep, dz, 0.0).astype(BF16)
                dk_h = _dot(dzb, qm[h])
                dqT_scr[h] += jnp.where(valid_c, _dot(head_rows(kTc, h), dzb), 0.0)
                out.append(Gs[h] + jnp.where(valid_c, cumg_scr[p, h, SB_CH:SB_CH + 1, :], 0.0))
                dk_c = dk_h if h == 0 else dk_c + dk_h
                dv_c = dv_h if h == 0 else dv_c + dv_h
            dv_acc[rows_of(c_b), :] += jnp.where(valid_b, dv_c, 0.0)
            dk_acc[rows_of(c_c), :] += jnp.where(valid_c, dk_c, 0.0)
            return tuple(out)

        def trip(tt, Gs):
            for j in range(SB_STEPS):
                t = SB_STEPS * tt + j

                @pl.when(jnp.logical_and(t >= 1, t - 1 <= n))
                def _():
                    load(t - 1).wait()

            for j in range(SB_STEPS):
                t = SB_STEPS * tt + j

                @pl.when(t + ahead <= n)
                def _():
                    load(t + ahead).start()

            for j in range(SB_STEPS):
                Gs = step(SB_STEPS * tt + j, j % 2, Gs)
            return Gs

        z1 = jnp.zeros((1, SB_TQ), F32)
        lax.fori_loop(0, (n + 3 + SB_STEPS - 1) // SB_STEPS, trip, (z1, z1))
        dq_ref[...] = jnp.concatenate([dqT_scr[0], dqT_scr[1]], axis=0).T * Q_SCALE

        @pl.when(i == nq - 1)
        def _():
            dk_acc[...] = dk_acc[...] * LN2
            cols = pl.ds(pl.multiple_of(hp * LANES, LANES), LANES)
            ck = pltpu.make_async_copy(dk_acc, dk_hbm.at[:, cols], sems.at[0])
            cv = pltpu.make_async_copy(dv_acc, dv_hbm.at[:, cols], sems.at[1])
            ck.start()
            cv.start()
            ck.wait()
            cv.wait()

    blk = lambda hp, i, first: (i, hp)
    return pl.pallas_call(
        body, name=name,
        grid_spec=pltpu.PrefetchScalarGridSpec(
            num_scalar_prefetch=1, grid=(HEAD_PAIRS, nq),
            in_specs=[pl.BlockSpec((SB_TQ, LANES), blk),
                      pl.BlockSpec((SB_TQ, LANES), blk),
                      pl.BlockSpec((S, LANES), lambda hp, i, first: (0, 2 * HEAD_PAIRS + hp)),
                      pl.BlockSpec((None, nch, LANES, SB_CH), lambda hp, i, first: (hp, 0, 0, 0)),
                      pl.BlockSpec((SB_CH + 8, SB_CH), lambda hp, i, first: (0, 0)),
                      pl.BlockSpec(memory_space=pl.ANY)],
            out_specs=[pl.BlockSpec((SB_TQ, LANES), blk), pl.BlockSpec(memory_space=pl.ANY),
                       pl.BlockSpec(memory_space=pl.ANY)],
            scratch_shapes=[pltpu.VMEM((S, LANES), F32), pltpu.VMEM((S, LANES), F32),
                            pltpu.SemaphoreType.DMA((2,))]
            + [pltpu.VMEM((2, 2, SB_CH, SB_TQ), F32)] * 3
            + [pltpu.VMEM((2, 2, SB_CH + 8, SB_TQ), F32), pltpu.VMEM((2, HEAD_DIM, SB_TQ), F32)]
            + [pltpu.VMEM((nl + 1, 4, SB_CH, SB_TQ), BF16), pltpu.SemaphoreType.DMA((nl,))]),
        out_shape=[jax.ShapeDtypeStruct((S, D_MODEL), F32)] * 3,
        compiler_params=_params(("arbitrary", "arbitrary"), VMEM_BIG),
    )(first, qkv, do, qkv, kT3, tmat_g, ws)


def _dil_valid(first):
    qi = lax.broadcasted_iota(jnp.int32, (DIL_BLK, 2 * DIL_BLK), 0)
    kj = lax.broadcasted_iota(jnp.int32, (DIL_BLK, 2 * DIL_BLK), 1)
    dist = DIL_BLK + qi - kj
    return (dist >= 0) & (dist <= DIL_BLK) & (jnp.logical_not(first) | (kj >= DIL_BLK))


def _lane_pick(tile, idx):
    lane = lax.broadcasted_iota(jnp.int32, tile.shape, 1)
    return jnp.sum(jnp.where(lane == idx, tile, 0.0), axis=-1, keepdims=True)


class _DilPlan:
    def __init__(self, S, d):
        self.d = d
        self.span = DIL_BLK * d
        self.groups = max(1, 1024 // self.span)
        self.rows = self.span * self.groups
        self.steps = S // self.rows

    def cur(self, col0):
        return pl.BlockSpec((self.rows, LANES), lambda n, hp: (n, col0 + hp))

    def prev(self, col0):
        g = self.groups
        return pl.BlockSpec((self.span, LANES), lambda n, hp: (jnp.maximum(n * g - 1, 0), col0 + hp))

    def shared(self):
        return pl.BlockSpec((self.rows, LANES), lambda n, hp: (n, 0))

    def units(self, fn):
        n = pl.program_id(0)
        batch = 8
        if self.d * self.groups <= batch:
            for g in range(self.groups):
                for r in range(self.d):
                    fn(g, r, jnp.logical_and(n == 0, g == 0))
        else:
            assert self.groups == 1 and self.d % batch == 0

            def body(rb, carry):
                for rr in range(batch):
                    fn(0, rb * batch + rr, n == 0)
                return carry

            lax.fori_loop(0, self.d // batch, body, 0)

    def rows_of(self, g, r):
        return pl.ds(g * self.span + r, DIL_BLK, stride=self.d)

    def keys(self, cur_ref, prev_ref, g, r):
        before = prev_ref[pl.ds(r, DIL_BLK, stride=self.d), :] if g == 0 else cur_ref[self.rows_of(g - 1, r), :]
        return jnp.concatenate([before, cur_ref[self.rows_of(g, r), :]], axis=0).astype(BF16)


def _dil_fwd(qkv, d, name):
    S = qkv.shape[0]
    plan = _DilPlan(S, d)

    def body(q_ref, kc_ref, kp_ref, vc_ref, vp_ref, o_ref, lse_ref):
        hp = pl.program_id(1)

        @pl.when(hp == 0)
        def _():
            lse_ref[...] = jnp.zeros_like(lse_ref)

        lane = lax.broadcasted_iota(jnp.int32, (DIL_BLK, LANES), 1)

        def unit(g, r, first):
            valid = _dil_valid(first)
            rows = plan.rows_of(g, r)
            qm = _head_masks(q_ref[rows, :].astype(BF16))
            kk = plan.keys(kc_ref, kp_ref, g, r)
            vm = _head_masks(plan.keys(vc_ref, vp_ref, g, r))
            lse_t = lse_ref[rows, :]
            o2 = None
            for h in range(2):
                s = jnp.where(valid, _dot_nt(qm[h], kk), -1e30)
                m = jnp.max(s, axis=-1, keepdims=True)
                p = jnp.exp(s - m)
                den = jnp.sum(p, axis=-1, keepdims=True)
                oh = _dot(p.astype(BF16), vm[h]) / den
                o2 = oh if o2 is None else o2 + oh
                lse_t = jnp.where(lane == 2 * hp + h, m + jnp.log(den), lse_t)
            o_ref[rows, :] = o2
            lse_ref[rows, :] = lse_t

        plan.units(unit)

    return pl.pallas_call(
        body, name=name, grid=(plan.steps, HEAD_PAIRS),
        in_specs=[plan.cur(0), plan.cur(HEAD_PAIRS), plan.prev(HEAD_PAIRS), plan.cur(2 * HEAD_PAIRS),
                  plan.prev(2 * HEAD_PAIRS)],
        out_specs=[plan.cur(0), plan.shared()],
        out_shape=[jax.ShapeDtypeStruct((S, D_MODEL), F32), jax.ShapeDtypeStruct((S, LANES), F32)],
        compiler_params=_params(("parallel", "arbitrary")),
    )(qkv, qkv, qkv, qkv, qkv)


def _head_expand():
    return jnp.asarray((np.arange(LANES)[:, None] == np.arange(D_MODEL)[None, :] // HEAD_DIM).astype(np.float32), BF16)


def _dil_merge(os_, lses, name):
    S = os_[0].shape[0]
    tm = 256
    nbr = len(os_)

    def body(*refs):
        o_refs, l_refs, e_ref = refs[:nbr], refs[nbr:2 * nbr], refs[2 * nbr]
        out_ref, outb_ref, lse_ref = refs[2 * nbr + 1:]
        ls = [r[...] for r in l_refs]
        m = ls[0]
        for l in ls[1:]:
            m = jnp.maximum(m, l)
        tot = jnp.exp(ls[0] - m)
        for l in ls[1:]:
            tot = tot + jnp.exp(l - m)
        lse = m + jnp.log(tot)
        acc = None
        for o_r, l in zip(o_refs, ls):
            wt = _dot3(jnp.exp(l - lse), e_ref[...])
            term = wt * o_r[...]
            acc = term if acc is None else acc + term
        out_ref[...] = acc
        outb_ref[...] = acc.astype(BF16)
        lse_ref[...] = lse

    row = lambda i: (i, 0)
    return pl.pallas_call(
        body, name=name, grid=(S // tm,),
        in_specs=[pl.BlockSpec((tm, D_MODEL), row)] * nbr + [pl.BlockSpec((tm, LANES), row)] * nbr
        + [pl.BlockSpec((LANES, D_MODEL), lambda i: (0, 0))],
        out_specs=[pl.BlockSpec((tm, D_MODEL), row), pl.BlockSpec((tm, D_MODEL), row), pl.BlockSpec((tm, LANES), row)],
        out_shape=[jax.ShapeDtypeStruct((S, D_MODEL), F32), jax.ShapeDtypeStruct((S, D_MODEL), BF16),
                   jax.ShapeDtypeStruct((S, LANES), F32)],
        compiler_params=_params(("parallel",)),
    )(*os_, *lses, _head_expand())


def _dil_bwd(qkv, do, lse, dlt, d, name):
    S = qkv.shape[0]
    plan = _DilPlan(S, d)

    def body(q_ref, kc_ref, kp_ref, vc_ref, vp_ref, do_ref, lse_ref, dl_ref,
             dq_ref, dka_ref, dkb_ref, dva_ref, dvb_ref):
        hp = pl.program_id(1)

        def unit(g, r, first):
            valid = _dil_valid(first)
            rows = plan.rows_of(g, r)
            qm = _head_masks(q_ref[rows, :].astype(BF16))
            dom = _head_masks(do_ref[rows, :].astype(BF16))
            kk = plan.keys(kc_ref, kp_ref, g, r)
            vv = plan.keys(vc_ref, vp_ref, g, r)
            km = _head_masks(kk)
            lse_t = lse_ref[rows, :]
            dl_t = dl_ref[rows, :]
            dq2 = dkk = dvv = None
            for h in range(2):
                s = _dot_nt(qm[h], kk)
                p = jnp.where(valid, jnp.exp(s - _lane_pick(lse_t, 2 * hp + h)), 0.0)
                ds = (p * (_dot_nt(dom[h], vv) - _lane_pick(dl_t, 2 * hp + h))).astype(BF16)
                t_q = _dot(ds, km[h])
                t_k = _dot_tn(ds, qm[h])
                t_v = _dot_tn(p.astype(BF16), dom[h])
                dq2 = t_q if dq2 is None else dq2 + t_q
                dkk = t_k if dkk is None else dkk + t_k
                dvv = t_v if dvv is None else dvv + t_v
            dq_ref[rows, :] = dq2
            dkb_ref[rows, :] = dkk[:DIL_BLK]
            dka_ref[rows, :] = dkk[DIL_BLK:]
            dvb_ref[rows, :] = dvv[:DIL_BLK]
            dva_ref[rows, :] = dvv[DIL_BLK:]

        plan.units(unit)

    return pl.pallas_call(
        body, name=name, grid=(plan.steps, HEAD_PAIRS),
        in_specs=[plan.cur(0), plan.cur(HEAD_PAIRS), plan.prev(HEAD_PAIRS), plan.cur(2 * HEAD_PAIRS),
                  plan.prev(2 * HEAD_PAIRS), plan.cur(0), plan.shared(), plan.shared()],
        out_specs=[plan.cur(0)] * 5,
        out_shape=[jax.ShapeDtypeStruct((S, D_MODEL), F32)] * 5,
        compiler_params=_params(("parallel", "arbitrary"), VMEM_BIG),
    )(qkv, qkv, qkv, qkv, qkv, do, lse, dlt)


def _dil_combine(parts, rope, name):
    S = parts[0][0].shape[0]
    tm = DIL_BLK
    nblk = S // tm
    dils = [d for _, d in DILATED_BRANCHES]

    def body(*refs):
        ins = refs[:5 * len(dils)]
        c_ref, s1_ref, s2_ref, o_ref = refs[5 * len(dils):]
        i = pl.program_id(0)
        tabs = (c_ref[...], s1_ref[...], s2_ref[...])
        dq = dk = dv = None
        for b, d in enumerate(dils):
            dq_r, dka_r, dkb_r, dva_r, dvb_r = ins[5 * b:5 * b + 5]
            live = (i + d < nblk).astype(F32)
            tq = dq_r[...]
            tk = dka_r[...] + live * dkb_r[...]
            tv = dva_r[...] + live * dvb_r[...]
            dq = tq if dq is None else dq + tq
            dk = tk if dk is None else dk + tk
            dv = tv if dv is None else dv + tv
        dq = dq * Q_SCALE
        for g in range(HEAD_PAIRS):
            cols = slice(g * LANES, (g + 1) * LANES)
            o_ref[:, g * LANES:(g + 1) * LANES] = _rope_apply(dq[:, cols], *tabs, -1.0).astype(BF16)
            o_ref[:, D_MODEL + g * LANES:D_MODEL + (g + 1) * LANES] = _rope_apply(dk[:, cols], *tabs, -1.0).astype(BF16)
        o_ref[:, 2 * D_MODEL:] = dv.astype(BF16)

    row = pl.BlockSpec((tm, D_MODEL), lambda i: (i, 0))
    in_specs = []
    args = []
    for (dq_b, dka, dkb, dva, dvb), d in zip(parts, dils):
        ahead = pl.BlockSpec((tm, D_MODEL), lambda i, d=d: (jnp.minimum(i + d, nblk - 1), 0))
        in_specs += [row, row, ahead, row, ahead]
        args += [dq_b, dka, dkb, dva, dvb]
    in_specs += [pl.BlockSpec((tm, LANES), lambda i: (i, 0))] * 3
    return pl.pallas_call(
        body, name=name, grid=(nblk,),
        in_specs=in_specs,
        out_specs=pl.BlockSpec((tm, 3 * D_MODEL), lambda i: (i, 0)),
        out_shape=jax.ShapeDtypeStruct((S, 3 * D_MODEL), BF16),
        compiler_params=_params(("parallel",), VMEM_BIG),
    )(*args, *rope)


def _mesh_pos():
    return lax.axis_index("x"), lax.axis_index("y"), lax.axis_index("c")


def _all_gather(shard, name):
    R, C = shard.shape

    def body(x_ref, out_ref, send_sems, recv_sems, local_sem):
        x, y, c = _mesh_pos()
        me, sibling = (x, y, c), (x, y, 1 - c)
        chips = [(1 - x, y), (x, 1 - y), (1 - x, 1 - y)]

        def blk(p):
            return out_ref.at[4 * p[0] + 2 * p[1] + p[2]]

        def copy(k, block, to, src=None):
            return pltpu.make_async_remote_copy(
                src_ref=blk(block) if src is None else src, dst_ref=blk(block),
                send_sem=send_sems.at[k], recv_sem=recv_sems.at[k],
                device_id=to, device_id_type=pl.DeviceIdType.MESH)

        mine = pltpu.make_async_copy(x_ref, blk(me), local_sem)
        mine.start()
        first = [copy(0, me, sibling, src=x_ref)]
        first += [copy(1 + j, me, (*chip, c), src=x_ref) for j, chip in enumerate(chips)]
        for cp in first:
            cp.start()
        passed = [copy(4 + j, (*chip, c), sibling) for j, chip in enumerate(chips)]
        for j, chip in enumerate(chips):
            copy(1 + j, (*chip, c), me).wait_recv()
            passed[j].start()
        copy(0, sibling, me).wait_recv()
        for j, chip in enumerate(chips):
            copy(4 + j, (*chip, 1 - c), me).wait_recv()
        for cp in first + passed:
            cp.wait_send()
        mine.wait()

    return pl.pallas_call(
        body, name=name,
        in_specs=[pl.BlockSpec(memory_space=pl.ANY)],
        out_specs=pl.BlockSpec(memory_space=pl.ANY),
        out_shape=jax.ShapeDtypeStruct((N_DEV, R, C), shard.dtype),
        scratch_shapes=[pltpu.SemaphoreType.DMA((7,)), pltpu.SemaphoreType.DMA((7,)), pltpu.SemaphoreType.DMA],
    )(shard)


def _rs_pair(g, name):
    _, R, C = g.shape

    def body(g_ref, out_ref, send_sems, recv_sems):
        x, y, c = _mesh_pos()
        sibling = (x, y, 1 - c)
        cps = []
        for chip in range(4):
            cps.append(pltpu.make_async_remote_copy(
                src_ref=g_ref.at[2 * chip + (1 - c)], dst_ref=out_ref.at[chip],
                send_sem=send_sems.at[chip], recv_sem=recv_sems.at[chip],
                device_id=sibling, device_id_type=pl.DeviceIdType.MESH))
        for cp in cps:
            cp.start()
        for cp in cps:
            cp.wait_recv()
        for cp in cps:
            cp.wait_send()

    return pl.pallas_call(
        body, name=name,
        in_specs=[pl.BlockSpec(memory_space=pl.ANY)],
        out_specs=pl.BlockSpec(memory_space=pl.ANY),
        out_shape=jax.ShapeDtypeStruct((4, R, C), g.dtype),
        scratch_shapes=[pltpu.SemaphoreType.DMA((4,)), pltpu.SemaphoreType.DMA((4,))],
    )(g)


def _pair_add(g, got, cidx, name):
    _, R, C = g.shape
    tr = 256

    def body(c_ref, g_ref, r_ref, o_ref, ob_ref):
        s = g_ref[...] + r_ref[...].astype(F32)
        o_ref[...] = s
        ob_ref[...] = s.astype(BF16)

    blk = pl.BlockSpec((None, tr, C), lambda k, i, c: (k, i, 0))
    return pl.pallas_call(
        body, name=name,
        grid_spec=pltpu.PrefetchScalarGridSpec(
            num_scalar_prefetch=1, grid=(4, R // tr),
            in_specs=[pl.BlockSpec((None, tr, C), lambda k, i, c: (2 * k + c[0], i, 0)), blk],
            out_specs=[blk, blk]),
        out_shape=[jax.ShapeDtypeStruct((4, R, C), F32), jax.ShapeDtypeStruct((4, R, C), BF16)],
        compiler_params=_params(("parallel", "parallel")),
    )(cidx, g, got)


def _rs_chips(p, name):
    _, R, C = p.shape

    def body(p_ref, out_ref, send_sems, recv_sems):
        x, y, c = _mesh_pos()
        chips = [(1 - x, y), (x, 1 - y), (1 - x, 1 - y)]
        cps = []
        for j, (cx, cy) in enumerate(chips):
            cps.append(pltpu.make_async_remote_copy(
                src_ref=p_ref.at[2 * cx + cy], dst_ref=out_ref.at[j],
                send_sem=send_sems.at[j], recv_sem=recv_sems.at[j],
                device_id=(cx, cy, c), device_id_type=pl.DeviceIdType.MESH))
        for cp in cps:
            cp.start()
        for cp in cps:
            cp.wait_recv()
        for cp in cps:
            cp.wait_send()

    return pl.pallas_call(
        body, name=name,
        in_specs=[pl.BlockSpec(memory_space=pl.ANY)],
        out_specs=pl.BlockSpec(memory_space=pl.ANY),
        out_shape=jax.ShapeDtypeStruct((3, R, C), p.dtype),
        scratch_shapes=[pltpu.SemaphoreType.DMA((3,)), pltpu.SemaphoreType.DMA((3,))],
    )(p)


def _adamw_math(w, g, m, v):
    m2 = ADAM_B1 * m + (1.0 - ADAM_B1) * g
    v2 = ADAM_B2 * v + (1.0 - ADAM_B2) * (g * g)
    m_hat = m2 / (1.0 - ADAM_B1 ** ADAM_STEP)
    v_hat = v2 / (1.0 - ADAM_B2 ** ADAM_STEP)
    delta = -ADAM_LR * (m_hat / (jnp.sqrt(v_hat) + ADAM_EPS) + ADAM_WD * w)
    return delta, m2, v2


def _adamw_shard(p, got, chip_idx, w, m, v, name):
    R, C = w.shape
    tr = 256

    def body(k_ref, p_ref, r_ref, w_ref, m_ref, v_ref, g_out, d_out, m_out, v_out):
        g = ((p_ref[...] + r_ref[0].astype(F32)) + r_ref[1].astype(F32)) + r_ref[2].astype(F32)
        delta, m2, v2 = _adamw_math(w_ref[...], g, m_ref[...], v_ref[...])
        g_out[...] = g
        d_out[...] = delta
        m_out[...] = m2
        v_out[...] = v2

    row = pl.BlockSpec((tr, C), lambda i, k: (i, 0))
    return pl.pallas_call(
        body, name=name,
        grid_spec=pltpu.PrefetchScalarGridSpec(
            num_scalar_prefetch=1, grid=(R // tr,),
            in_specs=[pl.BlockSpec((None, tr, C), lambda i, k: (k[0], i, 0)),
                      pl.BlockSpec((3, tr, C), lambda i, k: (0, i, 0)), row, row, row],
            out_specs=[row] * 4),
        out_shape=[jax.ShapeDtypeStruct((R, C), F32)] * 4,
        compiler_params=_params(("parallel",)),
    )(chip_idx, p, got, w, m, v)


def _adamw_small(gathered, w, m, v, name):
    _, R, C = gathered.shape

    def body(a_ref, w_ref, m_ref, v_ref, g_out, d_out, m_out, v_out):
        g = a_ref[0]
        for k in range(1, N_DEV):
            g = g + a_ref[k]
        delta, m2, v2 = _adamw_math(w_ref[...], g, m_ref[...], v_ref[...])
        g_out[...] = g
        d_out[...] = delta
        m_out[...] = m2
        v_out[...] = v2

    return pl.pallas_call(
        body, name=name, out_shape=[jax.ShapeDtypeStruct((R, C), F32)] * 4,
    )(gathered, w, m, v)


def _rope_tables(S):
    half = ROPE_DIM // 2
    inv_freq = ROPE_THETA ** (-jnp.arange(half, dtype=F32) / half)
    ang = jnp.arange(S, dtype=jnp.int32).astype(F32)[:, None] * inv_freq[None, :]
    cos, sin = jnp.cos(ang), jnp.sin(ang)
    ones = jnp.ones((S, HEAD_DIM - ROPE_DIM), F32)
    zeros = jnp.zeros((S, HEAD_DIM - ROPE_DIM), F32)
    zh = jnp.zeros((S, half), F32)
    c = jnp.concatenate([cos, cos, ones], axis=1)
    s1 = jnp.concatenate([zh, sin, zeros], axis=1)
    s2 = jnp.concatenate([-sin, zh, zeros], axis=1)
    two = lambda t: jnp.concatenate([t, t], axis=1)
    return two(c), two(s1), two(s2)


def _chunk_transposed(a, S):
    return a.reshape(S // SB_CH, SB_CH, HEAD_PAIRS, LANES).transpose(2, 0, 3, 1)


def _flat_shards(ws):
    return jnp.concatenate([w.reshape(-1, D_MODEL) for layer in ws for w in layer], axis=0)


def kernel(x, w_qkv_0, w_o_0, ln1_g_0, ln1_b_0, w_ff1_0, w_ff2_0, ln2_g_0, ln2_b_0, w_qkv_1, w_o_1, ln1_g_1, ln1_b_1, w_ff1_1, w_ff2_1, ln2_g_1, ln2_b_1, loss_target, m_w_qkv_0, m_w_o_0, m_ln1_g_0, m_ln1_b_0, m_w_ff1_0, m_w_ff2_0, m_ln2_g_0, m_ln2_b_0, m_w_qkv_1, m_w_o_1, m_ln1_g_1, m_ln1_b_1, m_w_ff1_1, m_w_ff2_1, m_ln2_g_1, m_ln2_b_1, v_w_qkv_0, v_w_o_0, v_ln1_g_0, v_ln1_b_0, v_w_ff1_0, v_w_ff2_0, v_ln2_g_0, v_ln2_b_0, v_w_qkv_1, v_w_o_1, v_ln1_g_1, v_ln1_b_1, v_w_ff1_1, v_w_ff2_1, v_ln2_g_1, v_ln2_b_1):
    S = x.shape[1]
    x0 = x.reshape(S, D_MODEL)
    target = loss_target.reshape(S, D_MODEL)
    mats = ((w_qkv_0, w_o_0, w_ff1_0, w_ff2_0), (w_qkv_1, w_o_1, w_ff1_1, w_ff2_1))
    mats_m = ((m_w_qkv_0, m_w_o_0, m_w_ff1_0, m_w_ff2_0), (m_w_qkv_1, m_w_o_1, m_w_ff1_1, m_w_ff2_1))
    mats_v = ((v_w_qkv_0, v_w_o_0, v_w_ff1_0, v_w_ff2_0), (v_w_qkv_1, v_w_o_1, v_w_ff1_1, v_w_ff2_1))
    vecs = (ln1_g_0, ln1_b_0, ln2_g_0, ln2_b_0, ln1_g_1, ln1_b_1, ln2_g_1, ln2_b_1)
    vecs_m = (m_ln1_g_0, m_ln1_b_0, m_ln2_g_0, m_ln2_b_0, m_ln1_g_1, m_ln1_b_1, m_ln2_g_1, m_ln2_b_1)
    vecs_v = (v_ln1_g_0, v_ln1_b_0, v_ln2_g_0, v_ln2_b_0, v_ln1_g_1, v_ln1_b_1, v_ln2_g_1, v_ln2_b_1)

    w_flat = _flat_shards(mats)
    w_all = _all_gather(w_flat.astype(BF16), "ag_weights")
    layers = []
    for l in range(N_LAYERS):
        base = l * LAYER_ROWS
        r0, r1, r2, r3 = np.cumsum((0,) + SHARD_ROWS)[:4] + base
        layers.append(dict(
            qkv=w_all[:, r0:r0 + 384].reshape(N_DEV, D_MODEL, 384),
            o=w_all[:, r1:r1 + 128].reshape(D_MODEL, D_MODEL),
            ff1=w_all[:, r2:r2 + 512].reshape(N_DEV, D_MODEL, 512),
            ff2=w_all[:, r3:r3 + 512].reshape(D_FF, D_MODEL),
            g1=vecs[4 * l].reshape(1, D_MODEL), b1=vecs[4 * l + 1].reshape(1, D_MODEL),
            g2=vecs[4 * l + 2].reshape(1, D_MODEL), b2=vecs[4 * l + 3].reshape(1, D_MODEL)))

    rope = _rope_tables(S)
    tmat_later = _sb_tmat(True)
    tmat_upto = _sb_tmat(False)

    saved = []
    xin, xinb = x0, x0.astype(BF16)
    for l, W in enumerate(layers):
        sv = dict(xin=xin, xinb=xinb)
        qkv = _qkv_proj(xinb, W["qkv"], rope if l == 1 else None, Q_SCALE * LOG2E if l == 0 else Q_SCALE,
                        BF16 if l == 0 else F32, f"qkv_proj_{l}")
        sv["qkv"] = qkv
        if l == 0:
            vT3 = _chunk_transposed(qkv[:, 2 * D_MODEL:], S)
            o, sb_tiles, sb_first = _sb_fwd(qkv, vT3, tmat_later, "sb_fwd")
            ob = o.astype(BF16)
            sv.update(sb_tiles=sb_tiles, sb_first=sb_first)
        else:
            outs = [_dil_fwd(qkv, d, f"dil_fwd_{d}") for _, d in DILATED_BRANCHES]
            o, ob, lse = _dil_merge([t[0] for t in outs], [t[1] for t in outs], "dil_merge")
            sv["lse"] = lse
        sv.update(o=o, ob=ob)
        y1, x1, x1b = _mm_res_ln(ob, xin, W["o"], W["g1"], W["b1"], f"attn_out_ln_{l}")
        hpre, h = _ff1(x1b, W["ff1"], f"ff1_{l}")
        y2, x2, x2b = _mm_res_ln(h, x1, W["ff2"], W["g2"], W["b2"], f"ff2_ln_{l}")
        sv.update(y1=y1, x1=x1, x1b=x1b, hpre=hpre, h=h, y2=y2)
        saved.append(sv)
        xin, xinb = x2, x2b

    dout, loss_parts = _loss_grad(xin, target, "loss_grad")
    loss = lax.psum(jnp.sum(loss_parts), MESH_AXES)

    gmats = [None] * N_LAYERS
    gvecs = [None] * (4 * N_LAYERS)
    for l in reversed(range(N_LAYERS)):
        W, sv = layers[l], saved[l]
        dy2, dy2b, gb2 = _ln_bwd(dout, sv["y2"], W["g2"], f"ln2_bwd_{l}")
        dhp = _dh(dy2b, W["ff2"], sv["hpre"], f"dh_{l}")
        g_ff2 = _mm_tn(sv["h"], dy2b, 512, D_MODEL, False, f"dw_ff2_{l}")
        dx1 = _dx_blk(dy2, dhp, W["ff1"], f"dx_ff1_{l}")
        g_ff1 = _mm_tn(sv["x1b"], dhp, D_MODEL, 512, True, f"dw_ff1_{l}")
        dy1, dy1b, gb1 = _ln_bwd(dx1, sv["y1"], W["g1"], f"ln1_bwd_{l}")
        do = _mm_nt_plain(dy1b, W["o"], f"do_{l}")
        g_o = _mm_tn(sv["ob"], dy1b, 512, D_MODEL, False, f"dw_o_{l}")
        if l == 0:
            kT3 = _chunk_transposed(sv["qkv"][:, D_MODEL:2 * D_MODEL], S)
            dq, dk, dv = _sb_bwd(sv["qkv"], kT3, do, sv["sb_tiles"], sv["sb_first"], tmat_upto, "sb_bwd")
            dqkv = jnp.concatenate([dq, dk, dv], axis=1).astype(BF16)
        else:
            dlt = _head_sums(do, sv["o"], "head_sums")
            parts = [_dil_bwd(sv["qkv"], do, sv["lse"], dlt, d, f"dil_bwd_{d}") for _, d in DILATED_BRANCHES]
            dqkv = _dil_combine(parts, rope, "dil_combine")
        dout = _dx_blk(dy1, dqkv, W["qkv"], f"dx_qkv_{l}")
        g_qkv = _mm_tn(sv["xinb"], dqkv, D_MODEL, 384, True, f"dw_qkv_{l}")
        gmats[l] = (g_qkv.reshape(N_DEV, 384, D_MODEL), g_o.reshape(N_DEV, 128, D_MODEL),
                    g_ff1.reshape(N_DEV, 512, D_MODEL), g_ff2.reshape(N_DEV, 512, D_MODEL))
        gvecs[4 * l:4 * l + 4] = [gb1[0], gb1[1], gb2[0], gb2[1]]
    grad_x = dout.reshape(1, S, D_MODEL)

    cx, cy, cc = _mesh_pos()
    g_all = jnp.concatenate([g for layer in gmats for g in layer], axis=1)
    got_pair = _rs_pair(g_all.astype(BF16), "rs_pair")
    chip_part, chip_part_b = _pair_add(g_all, got_pair, cc.astype(jnp.int32).reshape(1), "rs_pair_add")
    got_chips = _rs_chips(chip_part_b, "rs_chips")
    chip_idx = (2 * cx + cy).astype(jnp.int32).reshape(1)
    g_sh, d_sh, m_sh, v_sh = _adamw_shard(chip_part, got_chips, chip_idx, w_flat, _flat_shards(mats_m),
                                          _flat_shards(mats_v), "adamw_mats")

    def unflat(a):
        out, pos = [], 0
        for layer in mats:
            for w in layer:
                n = w.size // D_MODEL
                out.append(a[pos:pos + n].reshape(w.shape))
                pos += n
        return out

    gv_all = _all_gather(jnp.stack(gvecs), "ag_vec_grads")
    g_v, d_v, m_v, v_v = _adamw_small(gv_all, jnp.stack(vecs), jnp.stack(vecs_m), jnp.stack(vecs_v), "adamw_vecs")

    def interleave(mat_list, vec_arr):
        out = []
        for l in range(N_LAYERS):
            qkv_, o_, ff1_, ff2_ = mat_list[4 * l:4 * l + 4]
            out += [qkv_, o_, vec_arr[4 * l], vec_arr[4 * l + 1], ff1_, ff2_, vec_arr[4 * l + 2], vec_arr[4 * l + 3]]
        return out

    return (loss, grad_x, *interleave(unflat(g_sh), g_v), *interleave(unflat(d_sh), d_v),
            *interleave(unflat(m_sh), m_v), *interleave(unflat(v_sh), v_v))
```

```python
import functools
import math

import jax
import jax.numpy as jnp
import numpy as np
from jax import lax
from jax.experimental import pallas as pl
from jax.experimental.pallas import tpu as pltpu

F32 = jnp.float32
BF16 = jnp.bfloat16

D_MODEL = 1024
N_HEADS = 16
HEAD_DIM = 64
D_FF = 4096
N_DEV = 8
N_LAYERS = 2
ROPE_THETA = 500000.0
ROPE_DIM = 16
DILATED_BRANCHES = ((128, 1), (512, 4), (2048, 16))
ALPHA = (2 * N_LAYERS) ** 0.25
LN_EPS = 1e-5
Q_SCALE = 1.0 / math.sqrt(HEAD_DIM)
LOG2E = math.log2(math.e)
LN2 = math.log(2.0)
ADAM_LR, ADAM_B1, ADAM_B2, ADAM_EPS, ADAM_WD, ADAM_STEP = 0.001, 0.9, 0.999, 1e-08, 0.01, 10

LANES = 128
HEAD_PAIRS = D_MODEL // LANES
SB_TQ = 256
SB_CH = 256
SB_STEPS = 4
SB_SAVE_SLOTS = 2 * SB_STEPS
SB_LOAD_SLOTS = 12
SB_LOAD_AHEAD = SB_LOAD_SLOTS - SB_STEPS - 1
SB_DEAD = -160.0
DIL_BLK = 128
VMEM_BIG = 56 * 2 ** 20
MESH_AXES = ("x", "y", "c")

SHARD_ROWS = (384, 128, 512, 512)
LAYER_ROWS = sum(SHARD_ROWS)
ALL_ROWS = N_LAYERS * LAYER_ROWS


def _params(sem=None, vmem=None):
    kw = {}
    if sem is not None:
        kw["dimension_semantics"] = sem
    if vmem is not None:
        kw["vmem_limit_bytes"] = vmem
    return pltpu.CompilerParams(**kw)


def _dot(a, b):
    return jnp.dot(a, b, preferred_element_type=F32)


def _dot_nt(a, b):
    return lax.dot_general(a, b, (((1,), (1,)), ((), ())), preferred_element_type=F32)


def _dot_tn(a, b):
    return lax.dot_general(a, b, (((0,), (0,)), ((), ())), preferred_element_type=F32)


def _split3(p):
    hi = p.astype(BF16)
    r1 = p - hi.astype(F32)
    mid = r1.astype(BF16)
    lo = (r1 - mid.astype(F32)).astype(BF16)
    return hi, mid, lo


def _dot3(p, e):
    hi, mid, lo = _split3(p)
    return _dot(hi, e) + _dot(mid, e) + _dot(lo, e)


def _rope_apply(a, c, s1, s2, sign):
    return a * c + sign * (pltpu.roll(a, 8, 1) * s1 + pltpu.roll(a, LANES - 8, 1) * s2)


def _qkv_proj(xb, w_blk, rope, q_mult, out_dtype, name):
    S = xb.shape[0]
    tm = 512
    n_rope = 0 if rope is None else 3

    def body(*refs):
        x_ref, w_ref = refs[:2]
        tabs = [r[...] for r in refs[2:2 + n_rope]]
        o_ref = refs[2 + n_rope]
        x = x_ref[...]
        for j in range(N_DEV):
            acc = _dot(x, w_ref[j])
            for g in range(3):
                col = j * 384 + g * LANES
                a = acc[:, g * LANES:(g + 1) * LANES]
                if n_rope and col < 2 * D_MODEL:
                    a = _rope_apply(a, *tabs, 1.0)
                if col < D_MODEL:
                    a = a * q_mult
                o_ref[:, col:col + LANES] = a.astype(out_dtype)

    tab_specs = [pl.BlockSpec((tm, LANES), lambda i: (i, 0))] * n_rope
    return pl.pallas_call(
        body, name=name, grid=(S // tm,),
        in_specs=[pl.BlockSpec((tm, D_MODEL), lambda i: (i, 0)),
                  pl.BlockSpec((N_DEV, D_MODEL, 384), lambda i: (0, 0, 0))] + tab_specs,
        out_specs=pl.BlockSpec((tm, 3 * D_MODEL), lambda i: (i, 0)),
        out_shape=jax.ShapeDtypeStruct((S, 3 * D_MODEL), out_dtype),
        compiler_params=_params(("parallel",), VMEM_BIG),
    )(xb, w_blk, *(rope or ()))


def _layer_norm_rows(y, g, b):
    mu = jnp.mean(y, axis=-1, keepdims=True)
    yc = y - mu
    var = jnp.mean(yc * yc, axis=-1, keepdims=True)
    return yc * lax.rsqrt(var + LN_EPS) * g + b


def _mm_res_ln(a, xres, w, g, b, name):
    S, K = a.shape
    tm = 512 if K <= 1024 else 256

    def body(a_ref, x_ref, w_ref, g_ref, b_ref, y_ref, xn_ref, xb_ref):
        y = ALPHA * x_ref[...] + _dot(a_ref[...], w_ref[...])
        xn = _layer_norm_rows(y, g_ref[...], b_ref[...])
        y_ref[...] = y
        xn_ref[...] = xn
        xb_ref[...] = xn.astype(BF16)

    row = lambda i: (i, 0)
    fix = lambda i: (0, 0)
    return pl.pallas_call(
        body, name=name, grid=(S // tm,),
        in_specs=[pl.BlockSpec((tm, K), row), pl.BlockSpec((tm, D_MODEL), row),
                  pl.BlockSpec((K, D_MODEL), fix), pl.BlockSpec((1, D_MODEL), fix),
                  pl.BlockSpec((1, D_MODEL), fix)],
        out_specs=[pl.BlockSpec((tm, D_MODEL), row)] * 3,
        out_shape=[jax.ShapeDtypeStruct((S, D_MODEL), F32), jax.ShapeDtypeStruct((S, D_MODEL), F32),
                   jax.ShapeDtypeStruct((S, D_MODEL), BF16)],
        compiler_params=_params(("parallel",), VMEM_BIG),
    )(a, xres, w, g, b)


def _ff1(xb, w_blk, name):
    S = xb.shape[0]
    tm = 256

    def body(x_ref, w_ref, hp_ref, h_ref):
        x = x_ref[...]
        for j in range(N_DEV):
            acc = _dot(x, w_ref[j])
            r = jnp.maximum(acc, 0.0)
            hp_ref[:, j * 512:(j + 1) * 512] = acc
            h_ref[:, j * 512:(j + 1) * 512] = (r * r).astype(BF16)

    return pl.pallas_call(
        body, name=name, grid=(S // tm,),
        in_specs=[pl.BlockSpec((tm, D_MODEL), lambda i: (i, 0)),
                  pl.BlockSpec((N_DEV, D_MODEL, 512), lambda i: (0, 0, 0))],
        out_specs=[pl.BlockSpec((tm, D_FF), lambda i: (i, 0))] * 2,
        out_shape=[jax.ShapeDtypeStruct((S, D_FF), F32), jax.ShapeDtypeStruct((S, D_FF), BF16)],
        compiler_params=_params(("parallel",), VMEM_BIG),
    )(xb, w_blk)


def _loss_grad(y, target, name):
    S = y.shape[0]
    tm = 512

    def body(y_ref, t_ref, dy_ref, l_ref):
        @pl.when(pl.program_id(0) == 0)
        def _():
            l_ref[...] = jnp.zeros_like(l_ref)

        err = y_ref[...] - t_ref[...]
        dy_ref[...] = err * (1.0 / D_MODEL)
        sq = err * err
        rows = sq[0:8]
        for r in range(1, tm // 8):
            rows = rows + sq[r * 8:(r + 1) * 8]
        acc = rows[:, 0:LANES]
        for g in range(1, D_MODEL // LANES):
            acc = acc + rows[:, g * LANES:(g + 1) * LANES]
        l_ref[...] += acc * (0.5 / D_MODEL)

    return pl.pallas_call(
        body, name=name, grid=(S // tm,),
        in_specs=[pl.BlockSpec((tm, D_MODEL), lambda i: (i, 0))] * 2,
        out_specs=[pl.BlockSpec((tm, D_MODEL), lambda i: (i, 0)), pl.BlockSpec((8, LANES), lambda i: (0, 0))],
        out_shape=[jax.ShapeDtypeStruct((S, D_MODEL), F32), jax.ShapeDtypeStruct((8, LANES), F32)],
        compiler_params=_params(("arbitrary",)),
    )(y, target)


def _ln_bwd(dout, y, g, name):
    S = y.shape[0]
    tm = 512
    steps = S // tm

    def body(d_ref, y_ref, g_ref, dy_ref, dyb_ref, gb_ref, acc_g, acc_b):
        i = pl.program_id(0)

        @pl.when(i == 0)
        def _():
            acc_g[...] = jnp.zeros_like(acc_g)
            acc_b[...] = jnp.zeros_like(acc_b)

        d = d_ref[...]
        yv = y_ref[...]
        mu = jnp.mean(yv, axis=-1, keepdims=True)
        yc = yv - mu
        var = jnp.mean(yc * yc, axis=-1, keepdims=True)
        rstd = lax.rsqrt(var + LN_EPS)
        xhat = yc * rstd
        dxh = d * g_ref[...]
        m1 = jnp.mean(dxh, axis=-1, keepdims=True)
        m2 = jnp.mean(dxh * xhat, axis=-1, keepdims=True)
        dy = rstd * (dxh - m1 - xhat * m2)
        dy_ref[...] = dy
        dyb_ref[...] = dy.astype(BF16)
        pg = d * xhat
        sg = pg[0:8]
        sb = d[0:8]
        for r in range(1, tm // 8):
            sg = sg + pg[r * 8:(r + 1) * 8]
            sb = sb + d[r * 8:(r + 1) * 8]
        acc_g[...] += sg
        acc_b[...] += sb

        @pl.when(i == steps - 1)
        def _():
            gb_ref[0:1, :] = jnp.sum(acc_g[...], axis=0, keepdims=True)
            gb_ref[1:2, :] = jnp.sum(acc_b[...], axis=0, keepdims=True)

    row = lambda i: (i, 0)
    fix = lambda i: (0, 0)
    return pl.pallas_call(
        body, name=name, grid=(steps,),
        in_specs=[pl.BlockSpec((tm, D_MODEL), row), pl.BlockSpec((tm, D_MODEL), row), pl.BlockSpec((1, D_MODEL), fix)],
        out_specs=[pl.BlockSpec((tm, D_MODEL), row), pl.BlockSpec((tm, D_MODEL), row), pl.BlockSpec((2, D_MODEL), fix)],
        out_shape=[jax.ShapeDtypeStruct((S, D_MODEL), F32), jax.ShapeDtypeStruct((S, D_MODEL), BF16),
                   jax.ShapeDtypeStruct((2, D_MODEL), F32)],
        scratch_shapes=[pltpu.VMEM((8, D_MODEL), F32), pltpu.VMEM((8, D_MODEL), F32)],
        compiler_params=_params(("arbitrary",)),
    )(dout, y, g)


def _dh(dyb, w2, hpre, name):
    S = dyb.shape[0]
    tm = 256
    tn = 512

    def body(dy_ref, w_ref, hp_ref, o_ref):
        dy = dy_ref[...]
        for n in range(0, D_FF, tn):
            dh = _dot_nt(dy, w_ref[n:n + tn, :])
            o_ref[:, n:n + tn] = (dh * (2.0 * jnp.maximum(hp_ref[:, n:n + tn], 0.0))).astype(BF16)

    return pl.pallas_call(
        body, name=name, grid=(S // tm,),
        in_specs=[pl.BlockSpec((tm, D_MODEL), lambda i: (i, 0)), pl.BlockSpec((D_FF, D_MODEL), lambda i: (0, 0)),
                  pl.BlockSpec((tm, D_FF), lambda i: (i, 0))],
        out_specs=pl.BlockSpec((tm, D_FF), lambda i: (i, 0)),
        out_shape=jax.ShapeDtypeStruct((S, D_FF), BF16),
        compiler_params=_params(("parallel",), VMEM_BIG),
    )(dyb, w2, hpre)


def _dx_blk(dres, dz, w_blk, name):
    S, N = dz.shape
    bw = w_blk.shape[2]
    tm = 256

    def body(r_ref, z_ref, w_ref, o_ref):
        acc = ALPHA * r_ref[...]
        for j in range(N_DEV):
            acc = acc + _dot_nt(z_ref[:, j * bw:(j + 1) * bw], w_ref[j])
        o_ref[...] = acc

    return pl.pallas_call(
        body, name=name, grid=(S // tm,),
        in_specs=[pl.BlockSpec((tm, D_MODEL), lambda i: (i, 0)), pl.BlockSpec((tm, N), lambda i: (i, 0)),
                  pl.BlockSpec((N_DEV, D_MODEL, bw), lambda i: (0, 0, 0))],
        out_specs=pl.BlockSpec((tm, D_MODEL), lambda i: (i, 0)),
        out_shape=jax.ShapeDtypeStruct((S, D_MODEL), F32),
        compiler_params=_params(("parallel",), VMEM_BIG),
    )(dres, dz, w_blk)


def _mm_nt_plain(a, w, name):
    S = a.shape[0]
    tm = 512

    def body(a_ref, w_ref, o_ref):
        o_ref[...] = _dot_nt(a_ref[...], w_ref[...])

    return pl.pallas_call(
        body, name=name, grid=(S // tm,),
        in_specs=[pl.BlockSpec((tm, D_MODEL), lambda i: (i, 0)), pl.BlockSpec((D_MODEL, D_MODEL), lambda i: (0, 0))],
        out_specs=pl.BlockSpec((tm, D_MODEL), lambda i: (i, 0)),
        out_shape=jax.ShapeDtypeStruct((S, D_MODEL), F32),
        compiler_params=_params(("parallel",)),
    )(a, w)


def _mm_tn(a, b, ta, tb, blocked, name):
    S, Ka = a.shape
    Nb = b.shape[1]
    ts = 2048

    def body(a_ref, b_ref, o_ref):
        @pl.when(pl.program_id(2) == 0)
        def _():
            o_ref[...] = jnp.zeros_like(o_ref)

        o_ref[...] += _dot_tn(a_ref[...], b_ref[...])

    if blocked:
        out_spec = pl.BlockSpec((None, ta, tb), lambda i, j, s: (j, i, 0))
        out_shape = jax.ShapeDtypeStruct((Nb // tb, Ka, tb), F32)
    else:
        out_spec = pl.BlockSpec((ta, tb), lambda i, j, s: (i, j))
        out_shape = jax.ShapeDtypeStruct((Ka, Nb), F32)
    return pl.pallas_call(
        body, name=name, grid=(Ka // ta, Nb // tb, S // ts),
        in_specs=[pl.BlockSpec((ts, ta), lambda i, j, s: (s, i)), pl.BlockSpec((ts, tb), lambda i, j, s: (s, j))],
        out_specs=out_spec, out_shape=out_shape,
        compiler_params=_params(("parallel", "parallel", "arbitrary"), VMEM_BIG),
    )(a, b)


def _head_sums(do, o, name):
    S = do.shape[0]
    tm = 512
    sel = (np.arange(D_MODEL)[:, None] // HEAD_DIM == np.arange(LANES)[None, :]).astype(np.float32)

    def body(d_ref, o_ref, e_ref, out_ref):
        out_ref[...] = _dot3(d_ref[...] * o_ref[...], e_ref[...])

    return pl.pallas_call(
        body, name=name, grid=(S // tm,),
        in_specs=[pl.BlockSpec((tm, D_MODEL), lambda i: (i, 0))] * 2 + [pl.BlockSpec((D_MODEL, LANES), lambda i: (0, 0))],
        out_specs=pl.BlockSpec((tm, LANES), lambda i: (i, 0)),
        out_shape=jax.ShapeDtypeStruct((S, LANES), F32),
        compiler_params=_params(("parallel",)),
    )(do, o, jnp.asarray(sel, BF16))


def _sb_tmat(later):
    r = np.arange(SB_CH)
    t = (r[None, :] > r[:, None]) if later else (r[None, :] <= r[:, None])
    return jnp.asarray(np.concatenate([t.astype(np.float32), np.ones((8, SB_CH), np.float32)], axis=0), BF16)


def _sb_gates(z2):
    neg_abs = lax.bitcast_convert_type(lax.bitcast_convert_type(z2, jnp.uint32) | jnp.uint32(0x80000000), F32)
    l1 = jnp.log2(1.0 + jnp.exp2(neg_abs))
    a = jnp.minimum(z2, 0.0) - l1
    return a, a - z2


def _head_masks(x2):
    lane = lax.broadcasted_iota(jnp.int32, x2.shape, 1)
    zero = jnp.zeros_like(x2)
    return jnp.where(lane < HEAD_DIM, x2, zero), jnp.where(lane >= HEAD_DIM, x2, zero)


def _sb_fwd(qkv, vT3, tmat, name):
    S = qkv.shape[0]
    nq = S // SB_TQ
    nch = S // SB_CH
    ns = SB_SAVE_SLOTS

    def body(q_ref, k_ref, vT_ref, t_ref, o_ref, ws_hbm, first_ref, z_scr, a_scr, cum_scr, oT_scr, stage, sems,
             pending):
        hp = pl.program_id(0)
        i = pl.program_id(1)
        base = (i * (i + 1)) // 2
        qm = _head_masks(q_ref[...])

        def save(src, sem, c):
            return pltpu.make_async_copy(src, ws_hbm.at[hp, base + c], sem)

        causal = (lax.broadcasted_iota(jnp.int32, (SB_CH, SB_TQ), 0)
                  < lax.broadcasted_iota(jnp.int32, (SB_CH, SB_TQ), 1))

        def head_rows(vTc, h):
            return vTc[h * HEAD_DIM:(h + 1) * HEAD_DIM, :]

        @pl.when(jnp.logical_and(hp == 0, i == 0))
        def _():
            z_scr[...] = jnp.zeros_like(z_scr)
            a_scr[...] = jnp.zeros_like(a_scr)
            cum_scr[...] = jnp.zeros_like(cum_scr)

        oT_scr[...] = jnp.zeros_like(oT_scr)

        def c_valid(t):
            return jnp.logical_and(t >= 2, t - 2 <= i)

        def c_chunk(t):
            return jnp.clip(i + 2 - t, 0, nch - 1)

        def step(t, p, slot, R, own_b, own_c):
            cA = jnp.maximum(i - t, 0)
            kA = k_ref[pl.ds(pl.multiple_of(cA * SB_CH, SB_CH), SB_CH), :]
            valid = c_valid(t)
            vC = vT_ref[c_chunk(t)]
            out = []
            for h in range(2):
                z_scr[p, h] = _dot_nt(kA, qm[h])
                a, lf = _sb_gates(z_scr[1 - p, h])
                if own_b:
                    lf = jnp.where(causal, lf, 0.0)
                a_scr[1 - p, h] = a
                cum_scr[1 - p, h] = _dot(t_ref[...], lf.astype(BF16))
                a_c = a_scr[p, h]
                w = jnp.exp2(a_c + cum_scr[p, h, :SB_CH, :] + R[h])
                if own_c:
                    w = jnp.where(causal, w, 0.0)
                wb = w.astype(BF16)
                stage[slot, 2 * h] = wb
                stage[slot, 2 * h + 1] = a_c.astype(BF16)
                oT_scr[h] += jnp.where(valid, _dot(head_rows(vC, h), wb), 0.0)
                out.append(R[h] + jnp.where(valid, cum_scr[p, h, SB_CH:SB_CH + 1, :], 0.0))
            return tuple(out)

        step_no = hp * nq + i

        @pl.when(step_no == 0)
        def _():
            for s in range(ns):
                pending[s] = 0

        def settle(slot):
            @pl.when(pending[slot] == 1)
            def _():
                save(stage.at[slot], sems.at[slot], 0).wait()
                pending[slot] = 0

        def trip(tt, R, first):
            half = lax.rem(step_no + tt, 2) * SB_STEPS
            for j in range(SB_STEPS):
                settle(half + j)
            for j in range(SB_STEPS):
                R = step(SB_STEPS * tt + j, j % 2, half + j, R, first and j == 1, first and j == 2)
            for j in range(SB_STEPS):
                t = SB_STEPS * tt + j

                @pl.when(c_valid(t))
                def _():
                    save(stage.at[half + j], sems.at[half + j], c_chunk(t)).start()
                    pending[half + j] = 1

            return R

        z1 = jnp.zeros((1, SB_TQ), F32)
        trips = (i + 3 + SB_STEPS - 1) // SB_STEPS

        def alive(carry):
            tt, R = carry
            return jnp.logical_and(tt < trips, jnp.max(jnp.maximum(R[0], R[1])) > SB_DEAD)

        trips, _ = lax.while_loop(alive, lambda carry: (carry[0] + 1, trip(carry[0], carry[1], False)),
                                  (jnp.int32(1), trip(0, (z1, z1), True)))
        first_ref[hp, i] = jnp.maximum(i - (SB_STEPS * trips - 3), 0)

        @pl.when(step_no == HEAD_PAIRS * nq - 1)
        def _():
            for s in range(ns):
                settle(s)

        o_ref[...] = jnp.concatenate([oT_scr[0], oT_scr[1]], axis=0).T

    ntile = nq * (nq + 1) // 2
    return pl.pallas_call(
        body, name=name, grid=(HEAD_PAIRS, nq),
        in_specs=[pl.BlockSpec((SB_TQ, LANES), lambda hp, i: (i, hp)),
                  pl.BlockSpec((S, LANES), lambda hp, i: (0, HEAD_PAIRS + hp)),
                  pl.BlockSpec((None, nch, LANES, SB_CH), lambda hp, i: (hp, 0, 0, 0)),
                  pl.BlockSpec((SB_CH + 8, SB_CH), lambda hp, i: (0, 0))],
        out_specs=[pl.BlockSpec((SB_TQ, LANES), lambda hp, i: (i, hp)), pl.BlockSpec(memory_space=pl.ANY),
                   pl.BlockSpec(memory_space=pltpu.SMEM)],
        out_shape=[jax.ShapeDtypeStruct((S, D_MODEL), F32),
                   jax.ShapeDtypeStruct((HEAD_PAIRS, ntile, 4, SB_CH, SB_TQ), BF16),
                   jax.ShapeDtypeStruct((HEAD_PAIRS, nq), jnp.int32)],
        scratch_shapes=[pltpu.VMEM((2, 2, SB_CH, SB_TQ), F32), pltpu.VMEM((2, 2, SB_CH, SB_TQ), F32),
                        pltpu.VMEM((2, 2, SB_CH + 8, SB_TQ), F32), pltpu.VMEM((2, HEAD_DIM, SB_TQ), F32),
                        pltpu.VMEM((ns, 4, SB_CH, SB_TQ), BF16), pltpu.SemaphoreType.DMA((ns,)),
                        pltpu.SMEM((ns,), jnp.int32)],
        compiler_params=_params(("arbitrary", "arbitrary"), VMEM_BIG),
    )(qkv, qkv, vT3, tmat)


def _sb_bwd(qkv, kT3, do, ws, first, tmat_g, name):
    S = qkv.shape[0]
    nq = S // SB_TQ
    nch = S // SB_CH
    nl = SB_LOAD_SLOTS
    ahead = SB_LOAD_AHEAD

    def body(first_ref, q_ref, do_ref, v_ref, kT_ref, tg_ref, ws_hbm, dq_ref, dk_hbm, dv_hbm, dk_acc, dv_acc, sems,
             dwv_scr, g_scr, sig_scr, cumg_scr, dqT_scr, ring, ring_sems):
        hp = pl.program_id(0)
        i = pl.program_id(1)
        c0 = first_ref[hp, i]
        n = i - c0
        base = (i * (i + 1)) // 2 + c0

        @pl.when(i == 0)
        def _():
            dk_acc[...] = jnp.zeros_like(dk_acc)
            dv_acc[...] = jnp.zeros_like(dv_acc)

        @pl.when(jnp.logical_and(hp == 0, i == 0))
        def _():
            for scr in (dwv_scr, g_scr, sig_scr, cumg_scr, ring):
                scr[...] = jnp.zeros_like(scr)

        step_no = hp * nq + i
        parity = lax.rem(step_no, 2)

        def slot_of(u, par):
            return jnp.where(u < 2, nl + 1 + 2 * par + u, lax.rem(u, nl))

        def copy_in(hp_, tile, u, par):
            sem = jnp.where(u < 2, nl + 2 * par + u, lax.rem(u, nl))
            return pltpu.make_async_copy(ws_hbm.at[hp_, tile + u], ring.at[slot_of(u, par)], ring_sems.at[sem])

        def load(u):
            return copy_in(hp, base, u, parity)

        for u in range(ahead):
            @pl.when(jnp.logical_and(u <= n, jnp.logical_or(u >= 2, step_no == 0)))
            def _():
                load(u).start()

        nxt = jnp.minimum(step_no + 1, HEAD_PAIRS * nq - 1)
        hp_n, i_n = nxt // nq, lax.rem(nxt, nq)
        c0_n = first_ref[hp_n, i_n]
        for u in range(2):
            @pl.when(jnp.logical_and(step_no + 1 < HEAD_PAIRS * nq, u <= i_n - c0_n))
            def _():
                copy_in(hp_n, (i_n * (i_n + 1)) // 2 + c0_n, u, 1 - parity).start()

        dqT_scr[...] = jnp.zeros_like(dqT_scr)
        qm = _head_masks(q_ref[...])
        dom = _head_masks(do_ref[...].astype(BF16))
        causal = (lax.broadcasted_iota(jnp.int32, (SB_CH, SB_TQ), 0)
                  < lax.broadcasted_iota(jnp.int32, (SB_CH, SB_TQ), 1))

        def rows_of(c):
            return pl.ds(pl.multiple_of(c * SB_CH, SB_CH), SB_CH)

        def head_rows(kTc, h):
            return kTc[h * HEAD_DIM:(h + 1) * HEAD_DIM, :]

        def step(t, p, Gs):
            q = 1 - p
            valid_b = jnp.logical_and(t >= 1, t - 1 <= n)
            valid_c = jnp.logical_and(t >= 2, t - 2 <= n)
            c_b = c0 + jnp.clip(t - 1, 0, n)
            c_c = c0 + jnp.clip(t - 2, 0, n)
            slot = jnp.where(valid_b, slot_of(jnp.maximum(t - 1, 0), parity), nl)
            vA = v_ref[rows_of(c0 + jnp.minimum(t, n)), :]
            kTc = kT_ref[c_c]
            keep = jnp.logical_or(causal, t - 2 != n)
            out = []
            for h in range(2):
                dwv_scr[p, h] = _dot_nt(vA, dom[h])

                wb = ring[slot, 2 * h]
                g = wb.astype(F32) * dwv_scr[q, h]
                g_scr[q, h] = g
                sig_scr[q, h] = jnp.exp2(ring[slot, 2 * h + 1].astype(F32))
                cumg_scr[q, h] = _dot(tg_ref[...], g.astype(BF16))
                dv_h = _dot(wb, dom[h])

                dz = g_scr[p, h] - sig_scr[p, h] * (Gs[h] + cumg_scr[p, h, :SB_CH, :])
                dzb = jnp.where(keep, dz, 0.0).astype(BF16)
                dk_h = _dot(dzb, qm[h])
                dqT_scr[h] += jnp.where(valid_c, _dot(head_rows(kTc, h), dzb), 0.0)
                out.append(Gs[h] + jnp.where(valid_c, cumg_scr[p, h, SB_CH:SB_CH + 1, :], 0.0))
                dk_c = dk_h if h == 0 else dk_c + dk_h
                dv_c = dv_h if h == 0 else dv_c + dv_h
            dv_acc[rows_of(c_b), :] += jnp.where(valid_b, dv_c, 0.0)
            dk_acc[rows_of(c_c), :] += jnp.where(valid_c, dk_c, 0.0)
            return tuple(out)

        def trip(tt, Gs):
            for j in range(SB_STEPS):
                t = SB_STEPS * tt + j

                @pl.when(jnp.logical_and(t >= 1, t - 1 <= n))
                def _():
                    load(t - 1).wait()

            for j in range(SB_STEPS):
                t = SB_STEPS * tt + j

                @pl.when(t + ahead <= n)
                def _():
                    load(t + ahead).start()

            for j in range(SB_STEPS):
                Gs = step(SB_STEPS * tt + j, j % 2, Gs)
            return Gs

        z1 = jnp.zeros((1, SB_TQ), F32)
        lax.fori_loop(0, (n + 3 + SB_STEPS - 1) // SB_STEPS, trip, (z1, z1))
        dq_ref[...] = jnp.concatenate([dqT_scr[0], dqT_scr[1]], axis=0).T * Q_SCALE

        @pl.when(i == nq - 1)
        def _():
            dk_acc[...] = dk_acc[...] * LN2
            cols = pl.ds(pl.multiple_of(hp * LANES, LANES), LANES)
            ck = pltpu.make_async_copy(dk_acc, dk_hbm.at[:, cols], sems.at[0])
            cv = pltpu.make_async_copy(dv_acc, dv_hbm.at[:, cols], sems.at[1])
            ck.start()
            cv.start()
            ck.wait()
            cv.wait()

    blk = lambda hp, i, first: (i, hp)
    return pl.pallas_call(
        body, name=name,
        grid_spec=pltpu.PrefetchScalarGridSpec(
            num_scalar_prefetch=1, grid=(HEAD_PAIRS, nq),
            in_specs=[pl.BlockSpec((SB_TQ, LANES), blk),
                      pl.BlockSpec((SB_TQ, LANES), blk),
                      pl.BlockSpec((S, LANES), lambda hp, i, first: (0, 2 * HEAD_PAIRS + hp)),
                      pl.BlockSpec((None, nch, LANES, SB_CH), lambda hp, i, first: (hp, 0, 0, 0)),
                      pl.BlockSpec((SB_CH + 8, SB_CH), lambda hp, i, first: (0, 0)),
                      pl.BlockSpec(memory_space=pl.ANY)],
            out_specs=[pl.BlockSpec((SB_TQ, LANES), blk), pl.BlockSpec(memory_space=pl.ANY),
                       pl.BlockSpec(memory_space=pl.ANY)],
            scratch_shapes=[pltpu.VMEM((S, LANES), F32), pltpu.VMEM((S, LANES), F32),
                            pltpu.SemaphoreType.DMA((2,))]
            + [pltpu.VMEM((2, 2, SB_CH, SB_TQ), F32)] * 3
            + [pltpu.VMEM((2, 2, SB_CH + 8, SB_TQ), F32), pltpu.VMEM((2, HEAD_DIM, SB_TQ), F32)]
            + [pltpu.VMEM((nl + 5, 4, SB_CH, SB_TQ), BF16), pltpu.SemaphoreType.DMA((nl + 4,))]),
        out_shape=[jax.ShapeDtypeStruct((S, D_MODEL), F32)] * 3,
        compiler_params=_params(("arbitrary", "arbitrary"), VMEM_BIG),
    )(first, qkv, do, qkv, kT3, tmat_g, ws)


def _dil_valid(first):
    qi = lax.broadcasted_iota(jnp.int32, (DIL_BLK, 2 * DIL_BLK), 0)
    kj = lax.broadcasted_iota(jnp.int32, (DIL_BLK, 2 * DIL_BLK), 1)
    dist = DIL_BLK + qi - kj
    return (dist >= 0) & (dist <= DIL_BLK) & (jnp.logical_not(first) | (kj >= DIL_BLK))


def _lane_pick(tile, idx):
    lane = lax.broadcasted_iota(jnp.int32, tile.shape, 1)
    return jnp.sum(jnp.where(lane == idx, tile, 0.0), axis=-1, keepdims=True)


class _DilPlan:
    def __init__(self, S, d):
        self.d = d
        self.span = DIL_BLK * d
        self.groups = max(1, 1024 // self.span)
        self.rows = self.span * self.groups
        self.steps = S // self.rows

    def cur(self, col0):
        return pl.BlockSpec((self.rows, LANES), lambda n, hp: (n, col0 + hp))

    def prev(self, col0):
        g = self.groups
        return pl.BlockSpec((self.span, LANES), lambda n, hp: (jnp.maximum(n * g - 1, 0), col0 + hp))

    def shared(self):
        return pl.BlockSpec((self.rows, LANES), lambda n, hp: (n, 0))

    def units(self, fn):
        n = pl.program_id(0)
        batch = 8
        if self.d * self.groups <= batch:
            for g in range(self.groups):
                for r in range(self.d):
                    fn(g, r, jnp.logical_and(n == 0, g == 0))
        else:
            assert self.groups == 1 and self.d % batch == 0

            def body(rb, carry):
                for rr in range(batch):
                    fn(0, rb * batch + rr, n == 0)
                return carry

            lax.fori_loop(0, self.d // batch, body, 0)

    def rows_of(self, g, r):
        return pl.ds(g * self.span + r, DIL_BLK, stride=self.d)

    def keys(self, cur_ref, prev_ref, g, r):
        before = prev_ref[pl.ds(r, DIL_BLK, stride=self.d), :] if g == 0 else cur_ref[self.rows_of(g - 1, r), :]
        return jnp.concatenate([before, cur_ref[self.rows_of(g, r), :]], axis=0).astype(BF16)


def _dil_fwd(qkv, d, name):
    S = qkv.shape[0]
    plan = _DilPlan(S, d)

    def body(q_ref, kc_ref, kp_ref, vc_ref, vp_ref, o_ref, lse_ref):
        hp = pl.program_id(1)

        @pl.when(hp == 0)
        def _():
            lse_ref[...] = jnp.zeros_like(lse_ref)

        lane = lax.broadcasted_iota(jnp.int32, (DIL_BLK, LANES), 1)

        def unit(g, r, first):
            valid = _dil_valid(first)
            rows = plan.rows_of(g, r)
            qm = _head_masks(q_ref[rows, :].astype(BF16))
            kk = plan.keys(kc_ref, kp_ref, g, r)
            vm = _head_masks(plan.keys(vc_ref, vp_ref, g, r))
            lse_t = lse_ref[rows, :]
            o2 = None
            for h in range(2):
                s = jnp.where(valid, _dot_nt(qm[h], kk), -1e30)
                m = jnp.max(s, axis=-1, keepdims=True)
                p = jnp.exp(s - m)
                den = jnp.sum(p, axis=-1, keepdims=True)
                oh = _dot(p.astype(BF16), vm[h]) / den
                o2 = oh if o2 is None else o2 + oh
                lse_t = jnp.where(lane == 2 * hp + h, m + jnp.log(den), lse_t)
            o_ref[rows, :] = o2
            lse_ref[rows, :] = lse_t

        plan.units(unit)

    return pl.pallas_call(
        body, name=name, grid=(plan.steps, HEAD_PAIRS),
        in_specs=[plan.cur(0), plan.cur(HEAD_PAIRS), plan.prev(HEAD_PAIRS), plan.cur(2 * HEAD_PAIRS),
                  plan.prev(2 * HEAD_PAIRS)],
        out_specs=[plan.cur(0), plan.shared()],
        out_shape=[jax.ShapeDtypeStruct((S, D_MODEL), F32), jax.ShapeDtypeStruct((S, LANES), F32)],
        compiler_params=_params(("parallel", "arbitrary")),
    )(qkv, qkv, qkv, qkv, qkv)


def _head_expand():
    return jnp.asarray((np.arange(LANES)[:, None] == np.arange(D_MODEL)[None, :] // HEAD_DIM).astype(np.float32), BF16)


def _dil_merge(os_, lses, name):
    S = os_[0].shape[0]
    tm = 256
    nbr = len(os_)

    def body(*refs):
        o_refs, l_refs, e_ref = refs[:nbr], refs[nbr:2 * nbr], refs[2 * nbr]
        out_ref, outb_ref, lse_ref = refs[2 * nbr + 1:]
        ls = [r[...] for r in l_refs]
        m = ls[0]
        for l in ls[1:]:
            m = jnp.maximum(m, l)
        tot = jnp.exp(ls[0] - m)
        for l in ls[1:]:
            tot = tot + jnp.exp(l - m)
        lse = m + jnp.log(tot)
        acc = None
        for o_r, l in zip(o_refs, ls):
            wt = _dot3(jnp.exp(l - lse), e_ref[...])
            term = wt * o_r[...]
            acc = term if acc is None else acc + term
        out_ref[...] = acc
        outb_ref[...] = acc.astype(BF16)
        lse_ref[...] = lse

    row = lambda i: (i, 0)
    return pl.pallas_call(
        body, name=name, grid=(S // tm,),
        in_specs=[pl.BlockSpec((tm, D_MODEL), row)] * nbr + [pl.BlockSpec((tm, LANES), row)] * nbr
        + [pl.BlockSpec((LANES, D_MODEL), lambda i: (0, 0))],
        out_specs=[pl.BlockSpec((tm, D_MODEL), row), pl.BlockSpec((tm, D_MODEL), row), pl.BlockSpec((tm, LANES), row)],
        out_shape=[jax.ShapeDtypeStruct((S, D_MODEL), F32), jax.ShapeDtypeStruct((S, D_MODEL), BF16),
                   jax.ShapeDtypeStruct((S, LANES), F32)],
        compiler_params=_params(("parallel",)),
    )(*os_, *lses, _head_expand())


def _dil_bwd(qkv, do, lse, dlt, d, name):
    S = qkv.shape[0]
    plan = _DilPlan(S, d)

    def body(q_ref, kc_ref, kp_ref, vc_ref, vp_ref, do_ref, lse_ref, dl_ref,
             dq_ref, dka_ref, dkb_ref, dva_ref, dvb_ref):
        hp = pl.program_id(1)

        def unit(g, r, first):
            valid = _dil_valid(first)
            rows = plan.rows_of(g, r)
            qm = _head_masks(q_ref[rows, :].astype(BF16))
            dom = _head_masks(do_ref[rows, :].astype(BF16))
            kk = plan.keys(kc_ref, kp_ref, g, r)
            vv = plan.keys(vc_ref, vp_ref, g, r)
            km = _head_masks(kk)
            lse_t = lse_ref[rows, :]
            dl_t = dl_ref[rows, :]
            dq2 = dkk = dvv = None
            for h in range(2):
                s = _dot_nt(qm[h], kk)
                p = jnp.where(valid, jnp.exp(s - _lane_pick(lse_t, 2 * hp + h)), 0.0)
                ds = (p * (_dot_nt(dom[h], vv) - _lane_pick(dl_t, 2 * hp + h))).astype(BF16)
                t_q = _dot(ds, km[h])
                t_k = _dot_tn(ds, qm[h])
                t_v = _dot_tn(p.astype(BF16), dom[h])
                dq2 = t_q if dq2 is None else dq2 + t_q
                dkk = t_k if dkk is None else dkk + t_k
                dvv = t_v if dvv is None else dvv + t_v
            dq_ref[rows, :] = dq2
            dkb_ref[rows, :] = dkk[:DIL_BLK]
            dka_ref[rows, :] = dkk[DIL_BLK:]
            dvb_ref[rows, :] = dvv[:DIL_BLK]
            dva_ref[rows, :] = dvv[DIL_BLK:]

        plan.units(unit)

    return pl.pallas_call(
        body, name=name, grid=(plan.steps, HEAD_PAIRS),
        in_specs=[plan.cur(0), plan.cur(HEAD_PAIRS), plan.prev(HEAD_PAIRS), plan.cur(2 * HEAD_PAIRS),
                  plan.prev(2 * HEAD_PAIRS), plan.cur(0), plan.shared(), plan.shared()],
        out_specs=[plan.cur(0)] * 5,
        out_shape=[jax.ShapeDtypeStruct((S, D_MODEL), F32)] * 5,
        compiler_params=_params(("parallel", "arbitrary"), VMEM_BIG),
    )(qkv, qkv, qkv, qkv, qkv, do, lse, dlt)


def _dil_combine(parts, rope, name):
    S = parts[0][0].shape[0]
    tm = DIL_BLK
    nblk = S // tm
    dils = [d for _, d in DILATED_BRANCHES]

    def body(*refs):
        ins = refs[:5 * len(dils)]
        c_ref, s1_ref, s2_ref, o_ref = refs[5 * len(dils):]
        i = pl.program_id(0)
        tabs = (c_ref[...], s1_ref[...], s2_ref[...])
        dq = dk = dv = None
        for b, d in enumerate(dils):
            dq_r, dka_r, dkb_r, dva_r, dvb_r = ins[5 * b:5 * b + 5]
            live = (i + d < nblk).astype(F32)
            tq = dq_r[...]
            tk = dka_r[...] + live * dkb_r[...]
            tv = dva_r[...] + live * dvb_r[...]
            dq = tq if dq is None else dq + tq
            dk = tk if dk is None else dk + tk
            dv = tv if dv is None else dv + tv
        dq = dq * Q_SCALE
        for g in range(HEAD_PAIRS):
            cols = slice(g * LANES, (g + 1) * LANES)
            o_ref[:, g * LANES:(g + 1) * LANES] = _rope_apply(dq[:, cols], *tabs, -1.0).astype(BF16)
            o_ref[:, D_MODEL + g * LANES:D_MODEL + (g + 1) * LANES] = _rope_apply(dk[:, cols], *tabs, -1.0).astype(BF16)
        o_ref[:, 2 * D_MODEL:] = dv.astype(BF16)

    row = pl.BlockSpec((tm, D_MODEL), lambda i: (i, 0))
    in_specs = []
    args = []
    for (dq_b, dka, dkb, dva, dvb), d in zip(parts, dils):
        ahead = pl.BlockSpec((tm, D_MODEL), lambda i, d=d: (jnp.minimum(i + d, nblk - 1), 0))
        in_specs += [row, row, ahead, row, ahead]
        args += [dq_b, dka, dkb, dva, dvb]
    in_specs += [pl.BlockSpec((tm, LANES), lambda i: (i, 0))] * 3
    return pl.pallas_call(
        body, name=name, grid=(nblk,),
        in_specs=in_specs,
        out_specs=pl.BlockSpec((tm, 3 * D_MODEL), lambda i: (i, 0)),
        out_shape=jax.ShapeDtypeStruct((S, 3 * D_MODEL), BF16),
        compiler_params=_params(("parallel",), VMEM_BIG),
    )(*args, *rope)


def _mesh_pos():
    return lax.axis_index("x"), lax.axis_index("y"), lax.axis_index("c")


def _all_gather(shard, name):
    R, C = shard.shape

    def body(x_ref, out_ref, send_sems, recv_sems, local_sem):
        x, y, c = _mesh_pos()
        me, sibling = (x, y, c), (x, y, 1 - c)
        chips = [(1 - x, y), (x, 1 - y), (1 - x, 1 - y)]

        def blk(p):
            return out_ref.at[4 * p[0] + 2 * p[1] + p[2]]

        def copy(k, block, to, src=None):
            return pltpu.make_async_remote_copy(
                src_ref=blk(block) if src is None else src, dst_ref=blk(block),
                send_sem=send_sems.at[k], recv_sem=recv_sems.at[k],
                device_id=to, device_id_type=pl.DeviceIdType.MESH)

        mine = pltpu.make_async_copy(x_ref, blk(me), local_sem)
        mine.start()
        first = [copy(0, me, sibling, src=x_ref)]
        first += [copy(1 + j, me, (*chip, c), src=x_ref) for j, chip in enumerate(chips)]
        for cp in first:
            cp.start()
        passed = [copy(4 + j, (*chip, c), sibling) for j, chip in enumerate(chips)]
        for j, chip in enumerate(chips):
            copy(1 + j, (*chip, c), me).wait_recv()
            passed[j].start()
        copy(0, sibling, me).wait_recv()
        for j, chip in enumerate(chips):
            copy(4 + j, (*chip, 1 - c), me).wait_recv()
        for cp in first + passed:
            cp.wait_send()
        mine.wait()

    return pl.pallas_call(
        body, name=name,
        in_specs=[pl.BlockSpec(memory_space=pl.ANY)],
        out_specs=pl.BlockSpec(memory_space=pl.ANY),
        out_shape=jax.ShapeDtypeStruct((N_DEV, R, C), shard.dtype),
        scratch_shapes=[pltpu.SemaphoreType.DMA((7,)), pltpu.SemaphoreType.DMA((7,)), pltpu.SemaphoreType.DMA],
    )(shard)


def _rs_pair(g, name):
    _, R, C = g.shape

    def body(g_ref, out_ref, send_sems, recv_sems):
        x, y, c = _mesh_pos()
        sibling = (x, y, 1 - c)
        cps = []
        for chip in range(4):
            cps.append(pltpu.make_async_remote_copy(
                src_ref=g_ref.at[2 * chip + (1 - c)], dst_ref=out_ref.at[chip],
                send_sem=send_sems.at[chip], recv_sem=recv_sems.at[chip],
                device_id=sibling, device_id_type=pl.DeviceIdType.MESH))
        for cp in cps:
            cp.start()
        for cp in cps:
            cp.wait_recv()
        for cp in cps:
            cp.wait_send()

    return pl.pallas_call(
        body, name=name,
        in_specs=[pl.BlockSpec(memory_space=pl.ANY)],
        out_specs=pl.BlockSpec(memory_space=pl.ANY),
        out_shape=jax.ShapeDtypeStruct((4, R, C), g.dtype),
        scratch_shapes=[pltpu.SemaphoreType.DMA((4,)), pltpu.SemaphoreType.DMA((4,))],
    )(g)


def _pair_add(g, got, cidx, name):
    _, R, C = g.shape
    tr = 256

    def body(c_ref, g_ref, r_ref, o_ref, ob_ref):
        s = g_ref[...] + r_ref[...].astype(F32)
        o_ref[...] = s
        ob_ref[...] = s.astype(BF16)

    blk = pl.BlockSpec((None, tr, C), lambda k, i, c: (k, i, 0))
    return pl.pallas_call(
        body, name=name,
        grid_spec=pltpu.PrefetchScalarGridSpec(
            num_scalar_prefetch=1, grid=(4, R // tr),
            in_specs=[pl.BlockSpec((None, tr, C), lambda k, i, c: (2 * k + c[0], i, 0)), blk],
            out_specs=[blk, blk]),
        out_shape=[jax.ShapeDtypeStruct((4, R, C), F32), jax.ShapeDtypeStruct((4, R, C), BF16)],
        compiler_params=_params(("parallel", "parallel")),
    )(cidx, g, got)


def _rs_chips(p, name):
    _, R, C = p.shape

    def body(p_ref, out_ref, send_sems, recv_sems):
        x, y, c = _mesh_pos()
        chips = [(1 - x, y), (x, 1 - y), (1 - x, 1 - y)]
        cps = []
        for j, (cx, cy) in enumerate(chips):
            cps.append(pltpu.make_async_remote_copy(
                src_ref=p_ref.at[2 * cx + cy], dst_ref=out_ref.at[j],
                send_sem=send_sems.at[j], recv_sem=recv_sems.at[j],
                device_id=(cx, cy, c), device_id_type=pl.DeviceIdType.MESH))
        for cp in cps:
            cp.start()
        for cp in cps:
            cp.wait_recv()
        for cp in cps:
            cp.wait_send()

    return pl.pallas_call(
        body, name=name,
        in_specs=[pl.BlockSpec(memory_space=pl.ANY)],
        out_specs=pl.BlockSpec(memory_space=pl.ANY),
        out_shape=jax.ShapeDtypeStruct((3, R, C), p.dtype),
        scratch_shapes=[pltpu.SemaphoreType.DMA((3,)), pltpu.SemaphoreType.DMA((3,))],
    )(p)


def _adamw_math(w, g, m, v):
    m2 = ADAM_B1 * m + (1.0 - ADAM_B1) * g
    v2 = ADAM_B2 * v + (1.0 - ADAM_B2) * (g * g)
    m_hat = m2 / (1.0 - ADAM_B1 ** ADAM_STEP)
    v_hat = v2 / (1.0 - ADAM_B2 ** ADAM_STEP)
    delta = -ADAM_LR * (m_hat / (jnp.sqrt(v_hat) + ADAM_EPS) + ADAM_WD * w)
    return delta, m2, v2


def _adamw_shard(p, got, chip_idx, w, m, v, name):
    R, C = w.shape
    tr = 256

    def body(k_ref, p_ref, r_ref, w_ref, m_ref, v_ref, g_out, d_out, m_out, v_out):
        g = ((p_ref[...] + r_ref[0].astype(F32)) + r_ref[1].astype(F32)) + r_ref[2].astype(F32)
        delta, m2, v2 = _adamw_math(w_ref[...], g, m_ref[...], v_ref[...])
        g_out[...] = g
        d_out[...] = delta
        m_out[...] = m2
        v_out[...] = v2

    row = pl.BlockSpec((tr, C), lambda i, k: (i, 0))
    return pl.pallas_call(
        body, name=name,
        grid_spec=pltpu.PrefetchScalarGridSpec(
            num_scalar_prefetch=1, grid=(R // tr,),
            in_specs=[pl.BlockSpec((None, tr, C), lambda i, k: (k[0], i, 0)),
                      pl.BlockSpec((3, tr, C), lambda i, k: (0, i, 0)), row, row, row],
            out_specs=[row] * 4),
        out_shape=[jax.ShapeDtypeStruct((R, C), F32)] * 4,
        compiler_params=_params(("parallel",)),
    )(chip_idx, p, got, w, m, v)


def _adamw_small(gathered, w, m, v, name):
    _, R, C = gathered.shape

    def body(a_ref, w_ref, m_ref, v_ref, g_out, d_out, m_out, v_out):
        g = a_ref[0]
        for k in range(1, N_DEV):
            g = g + a_ref[k]
        delta, m2, v2 = _adamw_math(w_ref[...], g, m_ref[...], v_ref[...])
        g_out[...] = g
        d_out[...] = delta
        m_out[...] = m2
        v_out[...] = v2

    return pl.pallas_call(
        body, name=name, out_shape=[jax.ShapeDtypeStruct((R, C), F32)] * 4,
    )(gathered, w, m, v)


def _rope_tables(S):
    half = ROPE_DIM // 2
    inv_freq = ROPE_THETA ** (-jnp.arange(half, dtype=F32) / half)
    ang = jnp.arange(S, dtype=jnp.int32).astype(F32)[:, None] * inv_freq[None, :]
    cos, sin = jnp.cos(ang), jnp.sin(ang)
    ones = jnp.ones((S, HEAD_DIM - ROPE_DIM), F32)
    zeros = jnp.zeros((S, HEAD_DIM - ROPE_DIM), F32)
    zh = jnp.zeros((S, half), F32)
    c = jnp.concatenate([cos, cos, ones], axis=1)
    s1 = jnp.concatenate([zh, sin, zeros], axis=1)
    s2 = jnp.concatenate([-sin, zh, zeros], axis=1)
    two = lambda t: jnp.concatenate([t, t], axis=1)
    return two(c), two(s1), two(s2)


def _chunk_transposed(a, S):
    return a.reshape(S // SB_CH, SB_CH, HEAD_PAIRS, LANES).transpose(2, 0, 3, 1)


def _flat_shards(ws):
    return jnp.concatenate([w.reshape(-1, D_MODEL) for layer in ws for w in layer], axis=0)


def kernel(x, w_qkv_0, w_o_0, ln1_g_0, ln1_b_0, w_ff1_0, w_ff2_0, ln2_g_0, ln2_b_0, w_qkv_1, w_o_1, ln1_g_1, ln1_b_1, w_ff1_1, w_ff2_1, ln2_g_1, ln2_b_1, loss_target, m_w_qkv_0, m_w_o_0, m_ln1_g_0, m_ln1_b_0, m_w_ff1_0, m_w_ff2_0, m_ln2_g_0, m_ln2_b_0, m_w_qkv_1, m_w_o_1, m_ln1_g_1, m_ln1_b_1, m_w_ff1_1, m_w_ff2_1, m_ln2_g_1, m_ln2_b_1, v_w_qkv_0, v_w_o_0, v_ln1_g_0, v_ln1_b_0, v_w_ff1_0, v_w_ff2_0, v_ln2_g_0, v_ln2_b_0, v_w_qkv_1, v_w_o_1, v_ln1_g_1, v_ln1_b_1, v_w_ff1_1, v_w_ff2_1, v_ln2_g_1, v_ln2_b_1):
    S = x.shape[1]
    x0 = x.reshape(S, D_MODEL)
    target = loss_target.reshape(S, D_MODEL)
    mats = ((w_qkv_0, w_o_0, w_ff1_0, w_ff2_0), (w_qkv_1, w_o_1, w_ff1_1, w_ff2_1))
    mats_m = ((m_w_qkv_0, m_w_o_0, m_w_ff1_0, m_w_ff2_0), (m_w_qkv_1, m_w_o_1, m_w_ff1_1, m_w_ff2_1))
    mats_v = ((v_w_qkv_0, v_w_o_0, v_w_ff1_0, v_w_ff2_0), (v_w_qkv_1, v_w_o_1, v_w_ff1_1, v_w_ff2_1))
    vecs = (ln1_g_0, ln1_b_0, ln2_g_0, ln2_b_0, ln1_g_1, ln1_b_1, ln2_g_1, ln2_b_1)
    vecs_m = (m_ln1_g_0, m_ln1_b_0, m_ln2_g_0, m_ln2_b_0, m_ln1_g_1, m_ln1_b_1, m_ln2_g_1, m_ln2_b_1)
    vecs_v = (v_ln1_g_0, v_ln1_b_0, v_ln2_g_0, v_ln2_b_0, v_ln1_g_1, v_ln1_b_1, v_ln2_g_1, v_ln2_b_1)

    w_flat = _flat_shards(mats)
    w_all = _all_gather(w_flat.astype(BF16), "ag_weights")
    layers = []
    for l in range(N_LAYERS):
        base = l * LAYER_ROWS
        r0, r1, r2, r3 = np.cumsum((0,) + SHARD_ROWS)[:4] + base
        layers.append(dict(
            qkv=w_all[:, r0:r0 + 384].reshape(N_DEV, D_MODEL, 384),
            o=w_all[:, r1:r1 + 128].reshape(D_MODEL, D_MODEL),
            ff1=w_all[:, r2:r2 + 512].reshape(N_DEV, D_MODEL, 512),
            ff2=w_all[:, r3:r3 + 512].reshape(D_FF, D_MODEL),
            g1=vecs[4 * l].reshape(1, D_MODEL), b1=vecs[4 * l + 1].reshape(1, D_MODEL),
            g2=vecs[4 * l + 2].reshape(1, D_MODEL), b2=vecs[4 * l + 3].reshape(1, D_MODEL)))

    rope = _rope_tables(S)
    tmat_later = _sb_tmat(True)
    tmat_upto = _sb_tmat(False)

    saved = []
    xin, xinb = x0, x0.astype(BF16)
    for l, W in enumerate(layers):
        sv = dict(xin=xin, xinb=xinb)
        qkv = _qkv_proj(xinb, W["qkv"], rope if l == 1 else None, Q_SCALE * LOG2E if l == 0 else Q_SCALE,
                        BF16 if l == 0 else F32, f"qkv_proj_{l}")
        sv["qkv"] = qkv
        if l == 0:
            vT3 = _chunk_transposed(qkv[:, 2 * D_MODEL:], S)
            o, sb_tiles, sb_first = _sb_fwd(qkv, vT3, tmat_later, "sb_fwd")
            ob = o.astype(BF16)
            sv.update(sb_tiles=sb_tiles, sb_first=sb_first)
        else:
            outs = [_dil_fwd(qkv, d, f"dil_fwd_{d}") for _, d in DILATED_BRANCHES]
            o, ob, lse = _dil_merge([t[0] for t in outs], [t[1] for t in outs], "dil_merge")
            sv["lse"] = lse
        sv.update(o=o, ob=ob)
        y1, x1, x1b = _mm_res_ln(ob, xin, W["o"], W["g1"], W["b1"], f"attn_out_ln_{l}")
        hpre, h = _ff1(x1b, W["ff1"], f"ff1_{l}")
        y2, x2, x2b = _mm_res_ln(h, x1, W["ff2"], W["g2"], W["b2"], f"ff2_ln_{l}")
        sv.update(y1=y1, x1=x1, x1b=x1b, hpre=hpre, h=h, y2=y2)
        saved.append(sv)
        xin, xinb = x2, x2b

    dout, loss_parts = _loss_grad(xin, target, "loss_grad")
    loss = lax.psum(jnp.sum(loss_parts), MESH_AXES)

    gmats = [None] * N_LAYERS
    gvecs = [None] * (4 * N_LAYERS)
    for l in reversed(range(N_LAYERS)):
        W, sv = layers[l], saved[l]
        dy2, dy2b, gb2 = _ln_bwd(dout, sv["y2"], W["g2"], f"ln2_bwd_{l}")
        dhp = _dh(dy2b, W["ff2"], sv["hpre"], f"dh_{l}")
        g_ff2 = _mm_tn(sv["h"], dy2b, 512, D_MODEL, False, f"dw_ff2_{l}")
        dx1 = _dx_blk(dy2, dhp, W["ff1"], f"dx_ff1_{l}")
        g_ff1 = _mm_tn(sv["x1b"], dhp, D_MODEL, 512, True, f"dw_ff1_{l}")
        dy1, dy1b, gb1 = _ln_bwd(dx1, sv["y1"], W["g1"], f"ln1_bwd_{l}")
        do = _mm_nt_plain(dy1b, W["o"], f"do_{l}")
        g_o = _mm_tn(sv["ob"], dy1b, 512, D_MODEL, False, f"dw_o_{l}")
        if l == 0:
            kT3 = _chunk_transposed(sv["qkv"][:, D_MODEL:2 * D_MODEL], S)
            dq, dk, dv = _sb_bwd(sv["qkv"], kT3, do, sv["sb_tiles"], sv["sb_first"], tmat_upto, "sb_bwd")
            dqkv = jnp.concatenate([dq, dk, dv], axis=1).astype(BF16)
        else:
            dlt = _head_sums(do, sv["o"], "head_sums")
            parts = [_dil_bwd(sv["qkv"], do, sv["lse"], dlt, d, f"dil_bwd_{d}") for _, d in DILATED_BRANCHES]
            dqkv = _dil_combine(parts, rope, "dil_combine")
        dout = _dx_blk(dy1, dqkv, W["qkv"], f"dx_qkv_{l}")
        g_qkv = _mm_tn(sv["xinb"], dqkv, D_MODEL, 384, True, f"dw_qkv_{l}")
        gmats[l] = (g_qkv.reshape(N_DEV, 384, D_MODEL), g_o.reshape(N_DEV, 128, D_MODEL),
                    g_ff1.reshape(N_DEV, 512, D_MODEL), g_ff2.reshape(N_DEV, 512, D_MODEL))
        gvecs[4 * l:4 * l + 4] = [gb1[0], gb1[1], gb2[0], gb2[1]]
    grad_x = dout.reshape(1, S, D_MODEL)

    cx, cy, cc = _mesh_pos()
    g_all = jnp.concatenate([g for layer in gmats for g in layer], axis=1)
    got_pair = _rs_pair(g_all.astype(BF16), "rs_pair")
    chip_part, chip_part_b = _pair_add(g_all, got_pair, cc.astype(jnp.int32).reshape(1), "rs_pair_add")
    got_chips = _rs_chips(chip_part_b, "rs_chips")
    chip_idx = (2 * cx + cy).astype(jnp.int32).reshape(1)
    g_sh, d_sh, m_sh, v_sh = _adamw_shard(chip_part, got_chips, chip_idx, w_flat, _flat_shards(mats_m),
                                          _flat_shards(mats_v), "adamw_mats")

    def unflat(a):
        out, pos = [], 0
        for layer in mats:
            for w in layer:
                n = w.size // D_MODEL
                out.append(a[pos:pos + n].reshape(w.shape))
                pos += n
        return out

    gv_all = _all_gather(jnp.stack(gvecs), "ag_vec_grads")
    g_v, d_v, m_v, v_v = _adamw_small(gv_all, jnp.stack(vecs), jnp.stack(vecs_m), jnp.stack(vecs_v), "adamw_vecs")

    def interleave(mat_list, vec_arr):
        out = []
        for l in range(N_LAYERS):
            qkv_, o_, ff1_, ff2_ = mat_list[4 * l:4 * l + 4]
            out += [qkv_, o_, vec_arr[4 * l], vec_arr[4 * l + 1], ff1_, ff2_, vec_arr[4 * l + 2], vec_arr[4 * l + 3]]
        return out

    return (loss, grad_x, *interleave(unflat(g_sh), g_v), *interleave(unflat(d_sh), d_v),
            *interleave(unflat(m_sh), m_v), *interleave(unflat(v_sh), v_v))
```

```python
import functools
import math

import jax
import jax.numpy as jnp
import numpy as np
from jax import lax
from jax.experimental import pallas as pl
from jax.experimental.pallas import tpu as pltpu

F32 = jnp.float32
BF16 = jnp.bfloat16

D_MODEL = 1024
N_HEADS = 16
HEAD_DIM = 64
D_FF = 4096
N_DEV = 8
N_LAYERS = 2
ROPE_THETA = 500000.0
ROPE_DIM = 16
DILATED_BRANCHES = ((128, 1), (512, 4), (2048, 16))
ALPHA = (2 * N_LAYERS) ** 0.25
LN_EPS = 1e-5
Q_SCALE = 1.0 / math.sqrt(HEAD_DIM)
LOG2E = math.log2(math.e)
LN2 = math.log(2.0)
ADAM_LR, ADAM_B1, ADAM_B2, ADAM_EPS, ADAM_WD, ADAM_STEP = 0.001, 0.9, 0.999, 1e-08, 0.01, 10

LANES = 128
HEAD_PAIRS = D_MODEL // LANES
SB_TQ = 256
SB_CH = 256
SB_STEPS = 4
SB_SAVE_SLOTS = 2 * SB_STEPS
SB_LOAD_SLOTS = 12
SB_LOAD_AHEAD = SB_LOAD_SLOTS - SB_STEPS - 1
SB_DEAD = -160.0
DIL_BLK = 128
VMEM_BIG = 56 * 2 ** 20
MESH_AXES = ("x", "y", "c")

SHARD_ROWS = (384, 128, 512, 512)
LAYER_ROWS = sum(SHARD_ROWS)
ALL_ROWS = N_LAYERS * LAYER_ROWS


def _params(sem=None, vmem=None):
    kw = {}
    if sem is not None:
        kw["dimension_semantics"] = sem
    if vmem is not None:
        kw["vmem_limit_bytes"] = vmem
    return pltpu.CompilerParams(**kw)


def _dot(a, b):
    return jnp.dot(a, b, preferred_element_type=F32)


def _dot_nt(a, b):
    return lax.dot_general(a, b, (((1,), (1,)), ((), ())), preferred_element_type=F32)


def _dot_tn(a, b):
    return lax.dot_general(a, b, (((0,), (0,)), ((), ())), preferred_element_type=F32)


def _split3(p):
    hi = p.astype(BF16)
    r1 = p - hi.astype(F32)
    mid = r1.astype(BF16)
    lo = (r1 - mid.astype(F32)).astype(BF16)
    return hi, mid, lo


def _dot3(p, e):
    hi, mid, lo = _split3(p)
    return _dot(hi, e) + _dot(mid, e) + _dot(lo, e)


def _rope_apply(a, c, s1, s2, sign):
    return a * c + sign * (pltpu.roll(a, 8, 1) * s1 + pltpu.roll(a, LANES - 8, 1) * s2)


def _qkv_proj(xb, w_blk, rope, q_mult, out_dtype, name):
    S = xb.shape[0]
    tm = 512
    n_rope = 0 if rope is None else 3

    def body(*refs):
        x_ref, w_ref = refs[:2]
        tabs = [r[...] for r in refs[2:2 + n_rope]]
        o_ref = refs[2 + n_rope]
        x = x_ref[...]
        for j in range(N_DEV):
            acc = _dot(x, w_ref[j])
            for g in range(3):
                col = j * 384 + g * LANES
                a = acc[:, g * LANES:(g + 1) * LANES]
                if n_rope and col < 2 * D_MODEL:
                    a = _rope_apply(a, *tabs, 1.0)
                if col < D_MODEL:
                    a = a * q_mult
                o_ref[:, col:col + LANES] = a.astype(out_dtype)

    tab_specs = [pl.BlockSpec((tm, LANES), lambda i: (i, 0))] * n_rope
    return pl.pallas_call(
        body, name=name, grid=(S // tm,),
        in_specs=[pl.BlockSpec((tm, D_MODEL), lambda i: (i, 0)),
                  pl.BlockSpec((N_DEV, D_MODEL, 384), lambda i: (0, 0, 0))] + tab_specs,
        out_specs=pl.BlockSpec((tm, 3 * D_MODEL), lambda i: (i, 0)),
        out_shape=jax.ShapeDtypeStruct((S, 3 * D_MODEL), out_dtype),
        compiler_params=_params(("parallel",), VMEM_BIG),
    )(xb, w_blk, *(rope or ()))


def _layer_norm_rows(y, g, b):
    mu = jnp.mean(y, axis=-1, keepdims=True)
    yc = y - mu
    var = jnp.mean(yc * yc, axis=-1, keepdims=True)
    return yc * lax.rsqrt(var + LN_EPS) * g + b


def _mm_res_ln(a, xres, w, g, b, name):
    S, K = a.shape
    tm = 512 if K <= 1024 else 256

    def body(a_ref, x_ref, w_ref, g_ref, b_ref, y_ref, xn_ref, xb_ref):
        y = ALPHA * x_ref[...] + _dot(a_ref[...], w_ref[...])
        xn = _layer_norm_rows(y, g_ref[...], b_ref[...])
        y_ref[...] = y
        xn_ref[...] = xn
        xb_ref[...] = xn.astype(BF16)

    row = lambda i: (i, 0)
    fix = lambda i: (0, 0)
    return pl.pallas_call(
        body, name=name, grid=(S // tm,),
        in_specs=[pl.BlockSpec((tm, K), row), pl.BlockSpec((tm, D_MODEL), row),
                  pl.BlockSpec((K, D_MODEL), fix), pl.BlockSpec((1, D_MODEL), fix),
                  pl.BlockSpec((1, D_MODEL), fix)],
        out_specs=[pl.BlockSpec((tm, D_MODEL), row)] * 3,
        out_shape=[jax.ShapeDtypeStruct((S, D_MODEL), F32), jax.ShapeDtypeStruct((S, D_MODEL), F32),
                   jax.ShapeDtypeStruct((S, D_MODEL), BF16)],
        compiler_params=_params(("parallel",), VMEM_BIG),
    )(a, xres, w, g, b)


def _ff1(xb, w_blk, name):
    S = xb.shape[0]
    tm = 256

    def body(x_ref, w_ref, hp_ref, h_ref):
        x = x_ref[...]
        for j in range(N_DEV):
            acc = _dot(x, w_ref[j])
            r = jnp.maximum(acc, 0.0)
            hp_ref[:, j * 512:(j + 1) * 512] = acc
            h_ref[:, j * 512:(j + 1) * 512] = (r * r).astype(BF16)

    return pl.pallas_call(
        body, name=name, grid=(S // tm,),
        in_specs=[pl.BlockSpec((tm, D_MODEL), lambda i: (i, 0)),
                  pl.BlockSpec((N_DEV, D_MODEL, 512), lambda i: (0, 0, 0))],
        out_specs=[pl.BlockSpec((tm, D_FF), lambda i: (i, 0))] * 2,
        out_shape=[jax.ShapeDtypeStruct((S, D_FF), F32), jax.ShapeDtypeStruct((S, D_FF), BF16)],
        compiler_params=_params(("parallel",), VMEM_BIG),
    )(xb, w_blk)


def _loss_grad(y, target, name):
    S = y.shape[0]
    tm = 512

    def body(y_ref, t_ref, dy_ref, l_ref):
        @pl.when(pl.program_id(0) == 0)
        def _():
            l_ref[...] = jnp.zeros_like(l_ref)

        err = y_ref[...] - t_ref[...]
        dy_ref[...] = err * (1.0 / D_MODEL)
        sq = err * err
        rows = sq[0:8]
        for r in range(1, tm // 8):
            rows = rows + sq[r * 8:(r + 1) * 8]
        acc = rows[:, 0:LANES]
        for g in range(1, D_MODEL // LANES):
            acc = acc + rows[:, g * LANES:(g + 1) * LANES]
        l_ref[...] += acc * (0.5 / D_MODEL)

    return pl.pallas_call(
        body, name=name, grid=(S // tm,),
        in_specs=[pl.BlockSpec((tm, D_MODEL), lambda i: (i, 0))] * 2,
        out_specs=[pl.BlockSpec((tm, D_MODEL), lambda i: (i, 0)), pl.BlockSpec((8, LANES), lambda i: (0, 0))],
        out_shape=[jax.ShapeDtypeStruct((S, D_MODEL), F32), jax.ShapeDtypeStruct((8, LANES), F32)],
        compiler_params=_params(("arbitrary",)),
    )(y, target)


def _ln_bwd(dout, y, g, name):
    S = y.shape[0]
    tm = 512
    steps = S // tm

    def body(d_ref, y_ref, g_ref, dy_ref, dyb_ref, gb_ref, acc_g, acc_b):
        i = pl.program_id(0)

        @pl.when(i == 0)
        def _():
            acc_g[...] = jnp.zeros_like(acc_g)
            acc_b[...] = jnp.zeros_like(acc_b)

        d = d_ref[...]
        yv = y_ref[...]
        mu = jnp.mean(yv, axis=-1, keepdims=True)
        yc = yv - mu
        var = jnp.mean(yc * yc, axis=-1, keepdims=True)
        rstd = lax.rsqrt(var + LN_EPS)
        xhat = yc * rstd
        dxh = d * g_ref[...]
        m1 = jnp.mean(dxh, axis=-1, keepdims=True)
        m2 = jnp.mean(dxh * xhat, axis=-1, keepdims=True)
        dy = rstd * (dxh - m1 - xhat * m2)
        dy_ref[...] = dy
        dyb_ref[...] = dy.astype(BF16)
        pg = d * xhat
        sg = pg[0:8]
        sb = d[0:8]
        for r in range(1, tm // 8):
            sg = sg + pg[r * 8:(r + 1) * 8]
            sb = sb + d[r * 8:(r + 1) * 8]
        acc_g[...] += sg
        acc_b[...] += sb

        @pl.when(i == steps - 1)
        def _():
            gb_ref[0:1, :] = jnp.sum(acc_g[...], axis=0, keepdims=True)
            gb_ref[1:2, :] = jnp.sum(acc_b[...], axis=0, keepdims=True)

    row = lambda i: (i, 0)
    fix = lambda i: (0, 0)
    return pl.pallas_call(
        body, name=name, grid=(steps,),
        in_specs=[pl.BlockSpec((tm, D_MODEL), row), pl.BlockSpec((tm, D_MODEL), row), pl.BlockSpec((1, D_MODEL), fix)],
        out_specs=[pl.BlockSpec((tm, D_MODEL), row), pl.BlockSpec((tm, D_MODEL), row), pl.BlockSpec((2, D_MODEL), fix)],
        out_shape=[jax.ShapeDtypeStruct((S, D_MODEL), F32), jax.ShapeDtypeStruct((S, D_MODEL), BF16),
                   jax.ShapeDtypeStruct((2, D_MODEL), F32)],
        scratch_shapes=[pltpu.VMEM((8, D_MODEL), F32), pltpu.VMEM((8, D_MODEL), F32)],
        compiler_params=_params(("arbitrary",)),
    )(dout, y, g)


def _dh(dyb, w2, hpre, name):
    S = dyb.shape[0]
    tm = 256
    tn = 512

    def body(dy_ref, w_ref, hp_ref, o_ref):
        dy = dy_ref[...]
        for n in range(0, D_FF, tn):
            dh = _dot_nt(dy, w_ref[n:n + tn, :])
            o_ref[:, n:n + tn] = (dh * (2.0 * jnp.maximum(hp_ref[:, n:n + tn], 0.0))).astype(BF16)

    return pl.pallas_call(
        body, name=name, grid=(S // tm,),
        in_specs=[pl.BlockSpec((tm, D_MODEL), lambda i: (i, 0)), pl.BlockSpec((D_FF, D_MODEL), lambda i: (0, 0)),
                  pl.BlockSpec((tm, D_FF), lambda i: (i, 0))],
        out_specs=pl.BlockSpec((tm, D_FF), lambda i: (i, 0)),
        out_shape=jax.ShapeDtypeStruct((S, D_FF), BF16),
        compiler_params=_params(("parallel",), VMEM_BIG),
    )(dyb, w2, hpre)


def _dx_blk(dres, dz, w_blk, name):
    S, N = dz.shape
    bw = w_blk.shape[2]
    tm = 256

    def body(r_ref, z_ref, w_ref, o_ref):
        acc = ALPHA * r_ref[...]
        for j in range(N_DEV):
            acc = acc + _dot_nt(z_ref[:, j * bw:(j + 1) * bw], w_ref[j])
        o_ref[...] = acc

    return pl.pallas_call(
        body, name=name, grid=(S // tm,),
        in_specs=[pl.BlockSpec((tm, D_MODEL), lambda i: (i, 0)), pl.BlockSpec((tm, N), lambda i: (i, 0)),
                  pl.BlockSpec((N_DEV, D_MODEL, bw), lambda i: (0, 0, 0))],
        out_specs=pl.BlockSpec((tm, D_MODEL), lambda i: (i, 0)),
        out_shape=jax.ShapeDtypeStruct((S, D_MODEL), F32),
        compiler_params=_params(("parallel",), VMEM_BIG),
    )(dres, dz, w_blk)


def _mm_nt_plain(a, w, name):
    S = a.shape[0]
    tm = 512

    def body(a_ref, w_ref, o_ref):
        o_ref[...] = _dot_nt(a_ref[...], w_ref[...])

    return pl.pallas_call(
        body, name=name, grid=(S // tm,),
        in_specs=[pl.BlockSpec((tm, D_MODEL), lambda i: (i, 0)), pl.BlockSpec((D_MODEL, D_MODEL), lambda i: (0, 0))],
        out_specs=pl.BlockSpec((tm, D_MODEL), lambda i: (i, 0)),
        out_shape=jax.ShapeDtypeStruct((S, D_MODEL), F32),
        compiler_params=_params(("parallel",)),
    )(a, w)


def _mm_tn(a, b, ta, tb, blocked, name):
    S, Ka = a.shape
    Nb = b.shape[1]
    ts = 2048

    def body(a_ref, b_ref, o_ref):
        @pl.when(pl.program_id(2) == 0)
        def _():
            o_ref[...] = jnp.zeros_like(o_ref)

        o_ref[...] += _dot_tn(a_ref[...], b_ref[...])

    if blocked:
        out_spec = pl.BlockSpec((None, ta, tb), lambda i, j, s: (j, i, 0))
        out_shape = jax.ShapeDtypeStruct((Nb // tb, Ka, tb), F32)
    else:
        out_spec = pl.BlockSpec((ta, tb), lambda i, j, s: (i, j))
        out_shape = jax.ShapeDtypeStruct((Ka, Nb), F32)
    return pl.pallas_call(
        body, name=name, grid=(Ka // ta, Nb // tb, S // ts),
        in_specs=[pl.BlockSpec((ts, ta), lambda i, j, s: (s, i)), pl.BlockSpec((ts, tb), lambda i, j, s: (s, j))],
        out_specs=out_spec, out_shape=out_shape,
        compiler_params=_params(("parallel", "parallel", "arbitrary"), VMEM_BIG),
    )(a, b)


def _head_sums(do, o, name):
    S = do.shape[0]
    tm = 512
    sel = (np.arange(D_MODEL)[:, None] // HEAD_DIM == np.arange(LANES)[None, :]).astype(np.float32)

    def body(d_ref, o_ref, e_ref, out_ref):
        out_ref[...] = _dot3(d_ref[...] * o_ref[...], e_ref[...])

    return pl.pallas_call(
        body, name=name, grid=(S // tm,),
        in_specs=[pl.BlockSpec((tm, D_MODEL), lambda i: (i, 0))] * 2 + [pl.BlockSpec((D_MODEL, LANES), lambda i: (0, 0))],
        out_specs=pl.BlockSpec((tm, LANES), lambda i: (i, 0)),
        out_shape=jax.ShapeDtypeStruct((S, LANES), F32),
        compiler_params=_params(("parallel",)),
    )(do, o, jnp.asarray(sel, BF16))


def _sb_tmat(later):
    r = np.arange(SB_CH)
    t = (r[None, :] > r[:, None]) if later else (r[None, :] <= r[:, None])
    return jnp.asarray(np.concatenate([t.astype(np.float32), np.ones((8, SB_CH), np.float32)], axis=0), BF16)


def _sb_gates(z2):
    neg_abs = lax.bitcast_convert_type(lax.bitcast_convert_type(z2, jnp.uint32) | jnp.uint32(0x80000000), F32)
    l1 = jnp.log2(1.0 + jnp.exp2(neg_abs))
    a = jnp.minimum(z2, 0.0) - l1
    return a, a - z2


def _head_masks(x2):
    lane = lax.broadcasted_iota(jnp.int32, x2.shape, 1)
    zero = jnp.zeros_like(x2)
    return jnp.where(lane < HEAD_DIM, x2, zero), jnp.where(lane >= HEAD_DIM, x2, zero)


def _sb_fwd(qkv, vT3, tmat, name):
    S = qkv.shape[0]
    nq = S // SB_TQ
    nch = S // SB_CH
    ns = SB_SAVE_SLOTS

    def body(q_ref, k_ref, vT_ref, t_ref, o_ref, ws_hbm, first_ref, z_scr, a_scr, cum_scr, oT_scr, stage, sems,
             pending):
        hp = pl.program_id(0)
        i = pl.program_id(1)
        base = (i * (i + 1)) // 2
        qm = _head_masks(q_ref[...])

        def save(src, sem, c):
            return pltpu.make_async_copy(src, ws_hbm.at[hp, base + c], sem)

        causal = (lax.broadcasted_iota(jnp.int32, (SB_CH, SB_TQ), 0)
                  < lax.broadcasted_iota(jnp.int32, (SB_CH, SB_TQ), 1))

        def head_rows(vTc, h):
            return vTc[h * HEAD_DIM:(h + 1) * HEAD_DIM, :]

        @pl.when(jnp.logical_and(hp == 0, i == 0))
        def _():
            z_scr[...] = jnp.zeros_like(z_scr)
            a_scr[...] = jnp.zeros_like(a_scr)
            cum_scr[...] = jnp.zeros_like(cum_scr)

        oT_scr[...] = jnp.zeros_like(oT_scr)

        def c_valid(t):
            return jnp.logical_and(t >= 2, t - 2 <= i)

        def c_chunk(t):
            return jnp.clip(i + 2 - t, 0, nch - 1)

        def step(t, p, slot, R, own_b, own_c, has_b=True, has_c=True):
            cA = jnp.maximum(i - t, 0)
            kA = k_ref[pl.ds(pl.multiple_of(cA * SB_CH, SB_CH), SB_CH), :]
            valid = c_valid(t)
            vC = vT_ref[c_chunk(t)]
            out = []
            for h in range(2):
                z_scr[p, h] = _dot_nt(kA, qm[h])
                if has_b:
                    a, lf = _sb_gates(z_scr[1 - p, h])
                    if own_b:
                        lf = jnp.where(causal, lf, 0.0)
                    a_scr[1 - p, h] = a
                    cum_scr[1 - p, h] = _dot(t_ref[...], lf.astype(BF16))
                if not has_c:
                    out.append(R[h])
                    continue
                a_c = a_scr[p, h]
                w = jnp.exp2(a_c + cum_scr[p, h, :SB_CH, :] + R[h])
                if own_c:
                    w = jnp.where(causal, w, 0.0)
                wb = w.astype(BF16)
                stage[slot, 2 * h] = wb
                stage[slot, 2 * h + 1] = a_c.astype(BF16)
                oT_scr[h] += jnp.where(valid, _dot(head_rows(vC, h), wb), 0.0)
                out.append(R[h] + jnp.where(valid, cum_scr[p, h, SB_CH:SB_CH + 1, :], 0.0))
            return tuple(out)

        step_no = hp * nq + i

        @pl.when(step_no == 0)
        def _():
            for s in range(ns):
                pending[s] = 0

        def settle(slot):
            @pl.when(pending[slot] == 1)
            def _():
                save(stage.at[slot], sems.at[slot], 0).wait()
                pending[slot] = 0

        def trip(tt, R, first):
            half = lax.rem(step_no + tt, 2) * SB_STEPS
            for j in range(SB_STEPS):
                settle(half + j)
            for j in range(SB_STEPS):
                R = step(SB_STEPS * tt + j, j % 2, half + j, R, first and j == 1, first and j == 2,
                         not (first and j == 0), not (first and j < 2))
            for j in range(SB_STEPS):
                t = SB_STEPS * tt + j

                @pl.when(c_valid(t))
                def _():
                    save(stage.at[half + j], sems.at[half + j], c_chunk(t)).start()
                    pending[half + j] = 1

            return R

        z1 = jnp.zeros((1, SB_TQ), F32)
        trips = (i + 3 + SB_STEPS - 1) // SB_STEPS

        def alive(carry):
            tt, R = carry
            return jnp.logical_and(tt < trips, jnp.max(jnp.maximum(R[0], R[1])) > SB_DEAD)

        trips, _ = lax.while_loop(alive, lambda carry: (carry[0] + 1, trip(carry[0], carry[1], False)),
                                  (jnp.int32(1), trip(0, (z1, z1), True)))
        first_ref[hp, i] = jnp.maximum(i - (SB_STEPS * trips - 3), 0)

        @pl.when(step_no == HEAD_PAIRS * nq - 1)
        def _():
            for s in range(ns):
                settle(s)

        o_ref[...] = jnp.concatenate([oT_scr[0], oT_scr[1]], axis=0).T.astype(BF16)

    ntile = nq * (nq + 1) // 2
    return pl.pallas_call(
        body, name=name, grid=(HEAD_PAIRS, nq),
        in_specs=[pl.BlockSpec((SB_TQ, LANES), lambda hp, i: (i, hp)),
                  pl.BlockSpec((S, LANES), lambda hp, i: (0, HEAD_PAIRS + hp)),
                  pl.BlockSpec((None, nch, LANES, SB_CH), lambda hp, i: (hp, 0, 0, 0)),
                  pl.BlockSpec((SB_CH + 8, SB_CH), lambda hp, i: (0, 0))],
        out_specs=[pl.BlockSpec((SB_TQ, LANES), lambda hp, i: (i, hp)), pl.BlockSpec(memory_space=pl.ANY),
                   pl.BlockSpec(memory_space=pltpu.SMEM)],
        out_shape=[jax.ShapeDtypeStruct((S, D_MODEL), BF16),
                   jax.ShapeDtypeStruct((HEAD_PAIRS, ntile, 4, SB_CH, SB_TQ), BF16),
                   jax.ShapeDtypeStruct((HEAD_PAIRS, nq), jnp.int32)],
        scratch_shapes=[pltpu.VMEM((2, 2, SB_CH, SB_TQ), F32), pltpu.VMEM((2, 2, SB_CH, SB_TQ), F32),
                        pltpu.VMEM((2, 2, SB_CH + 8, SB_TQ), F32), pltpu.VMEM((2, HEAD_DIM, SB_TQ), F32),
                        pltpu.VMEM((ns, 4, SB_CH, SB_TQ), BF16), pltpu.SemaphoreType.DMA((ns,)),
                        pltpu.SMEM((ns,), jnp.int32)],
        compiler_params=_params(("arbitrary", "arbitrary"), VMEM_BIG),
    )(qkv, qkv, vT3, tmat)


def _sb_bwd(qkv, kT3, do, ws, first, tmat_g, name):
    S = qkv.shape[0]
    nq = S // SB_TQ
    nch = S // SB_CH
    nl = SB_LOAD_SLOTS
    ahead = SB_LOAD_AHEAD

    def body(first_ref, q_ref, do_ref, v_ref, kT_ref, tg_ref, ws_hbm, dq_ref, dk_hbm, dv_hbm, dk_acc, dv_acc, sems,
             dwv_scr, g_scr, sig_scr, cumg_scr, dqT_scr, ring, ring_sems):
        hp = pl.program_id(0)
        i = pl.program_id(1)
        c0 = first_ref[hp, i]
        n = i - c0
        base = (i * (i + 1)) // 2 + c0

        @pl.when(i == 0)
        def _():
            dk_acc[...] = jnp.zeros_like(dk_acc)
            dv_acc[...] = jnp.zeros_like(dv_acc)

        @pl.when(jnp.logical_and(hp == 0, i == 0))
        def _():
            for scr in (dwv_scr, g_scr, sig_scr, cumg_scr, ring):
                scr[...] = jnp.zeros_like(scr)

        step_no = hp * nq + i
        parity = lax.rem(step_no, 2)

        def slot_of(u, par):
            return jnp.where(u < 2, nl + 1 + 2 * par + u, lax.rem(u, nl))

        def copy_in(hp_, tile, u, par):
            sem = jnp.where(u < 2, nl + 2 * par + u, lax.rem(u, nl))
            return pltpu.make_async_copy(ws_hbm.at[hp_, tile + u], ring.at[slot_of(u, par)], ring_sems.at[sem])

        def load(u):
            return copy_in(hp, base, u, parity)

        for u in range(ahead):
            @pl.when(jnp.logical_and(u <= n, jnp.logical_or(u >= 2, step_no == 0)))
            def _():
                load(u).start()

        nxt = jnp.minimum(step_no + 1, HEAD_PAIRS * nq - 1)
        hp_n, i_n = nxt // nq, lax.rem(nxt, nq)
        c0_n = first_ref[hp_n, i_n]
        for u in range(2):
            @pl.when(jnp.logical_and(step_no + 1 < HEAD_PAIRS * nq, u <= i_n - c0_n))
            def _():
                copy_in(hp_n, (i_n * (i_n + 1)) // 2 + c0_n, u, 1 - parity).start()

        dqT_scr[...] = jnp.zeros_like(dqT_scr)
        qm = _head_masks(q_ref[...])
        dom = _head_masks(do_ref[...].astype(BF16))
        causal = (lax.broadcasted_iota(jnp.int32, (SB_CH, SB_TQ), 0)
                  < lax.broadcasted_iota(jnp.int32, (SB_CH, SB_TQ), 1))

        def rows_of(c):
            return pl.ds(pl.multiple_of(c * SB_CH, SB_CH), SB_CH)

        def head_rows(kTc, h):
            return kTc[h * HEAD_DIM:(h + 1) * HEAD_DIM, :]

        def step(t, p, Gs, has_b=True, has_c=True):
            q = 1 - p
            valid_b = jnp.logical_and(t >= 1, t - 1 <= n)
            valid_c = jnp.logical_and(t >= 2, t - 2 <= n)
            c_b = c0 + jnp.clip(t - 1, 0, n)
            c_c = c0 + jnp.clip(t - 2, 0, n)
            slot = jnp.where(valid_b, slot_of(jnp.maximum(t - 1, 0), parity), nl)
            vA = v_ref[rows_of(c0 + jnp.minimum(t, n)), :]
            kTc = kT_ref[c_c]
            keep = jnp.logical_or(causal, t - 2 != n)
            out = []
            for h in range(2):
                dwv_scr[p, h] = _dot_nt(vA, dom[h])

                if has_b:
                    wb = ring[slot, 2 * h]
                    g = wb.astype(F32) * dwv_scr[q, h]
                    g_scr[q, h] = g
                    sig_scr[q, h] = jnp.exp2(ring[slot, 2 * h + 1].astype(F32))
                    cumg_scr[q, h] = _dot(tg_ref[...], g.astype(BF16))
                    dv_h = _dot(wb, dom[h])
                    dv_c = dv_h if h == 0 else dv_c + dv_h

                if not has_c:
                    out.append(Gs[h])
                    continue
                dz = g_scr[p, h] - sig_scr[p, h] * (Gs[h] + cumg_scr[p, h, :SB_CH, :])
                dzb = jnp.where(keep, dz, 0.0).astype(BF16)
                dk_h = _dot(dzb, qm[h])
                dqT_scr[h] += jnp.where(valid_c, _dot(head_rows(kTc, h), dzb), 0.0)
                out.append(Gs[h] + jnp.where(valid_c, cumg_scr[p, h, SB_CH:SB_CH + 1, :], 0.0))
                dk_c = dk_h if h == 0 else dk_c + dk_h
            if has_b:
                dv_acc[rows_of(c_b), :] += jnp.where(valid_b, dv_c, 0.0)
            if has_c:
                dk_acc[rows_of(c_c), :] += jnp.where(valid_c, dk_c, 0.0)
            return tuple(out)

        def trip(tt, Gs, first):
            for j in range(SB_STEPS):
                t = SB_STEPS * tt + j

                @pl.when(jnp.logical_and(t >= 1, t - 1 <= n))
                def _():
                    load(t - 1).wait()

            for j in range(SB_STEPS):
                t = SB_STEPS * tt + j

                @pl.when(t + ahead <= n)
                def _():
                    load(t + ahead).start()

            for j in range(SB_STEPS):
                Gs = step(SB_STEPS * tt + j, j % 2, Gs, not (first and j == 0), not (first and j < 2))
            return Gs

        z1 = jnp.zeros((1, SB_TQ), F32)
        lax.fori_loop(1, (n + 3 + SB_STEPS - 1) // SB_STEPS, lambda tt, Gs: trip(tt, Gs, False),
                      trip(0, (z1, z1), True))
        dq_ref[...] = jnp.concatenate([dqT_scr[0], dqT_scr[1]], axis=0).T * Q_SCALE

        @pl.when(i == nq - 1)
        def _():
            dk_acc[...] = dk_acc[...] * LN2
            cols = pl.ds(pl.multiple_of(hp * LANES, LANES), LANES)
            ck = pltpu.make_async_copy(dk_acc, dk_hbm.at[:, cols], sems.at[0])
            cv = pltpu.make_async_copy(dv_acc, dv_hbm.at[:, cols], sems.at[1])
            ck.start()
            cv.start()
            ck.wait()
            cv.wait()

    blk = lambda hp, i, first: (i, hp)
    return pl.pallas_call(
        body, name=name,
        grid_spec=pltpu.PrefetchScalarGridSpec(
            num_scalar_prefetch=1, grid=(HEAD_PAIRS, nq),
            in_specs=[pl.BlockSpec((SB_TQ, LANES), blk),
                      pl.BlockSpec((SB_TQ, LANES), blk),
                      pl.BlockSpec((S, LANES), lambda hp, i, first: (0, 2 * HEAD_PAIRS + hp)),
                      pl.BlockSpec((None, nch, LANES, SB_CH), lambda hp, i, first: (hp, 0, 0, 0)),
                      pl.BlockSpec((SB_CH + 8, SB_CH), lambda hp, i, first: (0, 0)),
                      pl.BlockSpec(memory_space=pl.ANY)],
            out_specs=[pl.BlockSpec((SB_TQ, LANES), blk), pl.BlockSpec(memory_space=pl.ANY),
                       pl.BlockSpec(memory_space=pl.ANY)],
            scratch_shapes=[pltpu.VMEM((S, LANES), F32), pltpu.VMEM((S, LANES), F32),
                            pltpu.SemaphoreType.DMA((2,))]
            + [pltpu.VMEM((2, 2, SB_CH, SB_TQ), F32)] * 3
            + [pltpu.VMEM((2, 2, SB_CH + 8, SB_TQ), F32), pltpu.VMEM((2, HEAD_DIM, SB_TQ), F32)]
            + [pltpu.VMEM((nl + 5, 4, SB_CH, SB_TQ), BF16), pltpu.SemaphoreType.DMA((nl + 4,))]),
        out_shape=[jax.ShapeDtypeStruct((S, D_MODEL), F32)] * 3,
        compiler_params=_params(("arbitrary", "arbitrary"), VMEM_BIG),
    )(first, qkv, do, qkv, kT3, tmat_g, ws)


def _dil_valid(first):
    qi = lax.broadcasted_iota(jnp.int32, (DIL_BLK, 2 * DIL_BLK), 0)
    kj = lax.broadcasted_iota(jnp.int32, (DIL_BLK, 2 * DIL_BLK), 1)
    dist = DIL_BLK + qi - kj
    return (dist >= 0) & (dist <= DIL_BLK) & (jnp.logical_not(first) | (kj >= DIL_BLK))


def _lane_pick(tile, idx):
    lane = lax.broadcasted_iota(jnp.int32, tile.shape, 1)
    return jnp.sum(jnp.where(lane == idx, tile, 0.0), axis=-1, keepdims=True)


class _DilPlan:
    def __init__(self, S, d):
        self.d = d
        self.span = DIL_BLK * d
        self.groups = max(1, 1024 // self.span)
        self.rows = self.span * self.groups
        self.steps = S // self.rows

    def cur(self, col0):
        return pl.BlockSpec((self.rows, LANES), lambda n, hp: (n, col0 + hp))

    def prev(self, col0):
        g = self.groups
        return pl.BlockSpec((self.span, LANES), lambda n, hp: (jnp.maximum(n * g - 1, 0), col0 + hp))

    def shared(self):
        return pl.BlockSpec((self.rows, LANES), lambda n, hp: (n, 0))

    def units(self, fn):
        n = pl.program_id(0)
        batch = 8
        if self.d * self.groups <= batch:
            for g in range(self.groups):
                for r in range(self.d):
                    fn(g, r, jnp.logical_and(n == 0, g == 0))
        else:
            assert self.groups == 1 and self.d % batch == 0

            def body(rb, carry):
                for rr in range(batch):
                    fn(0, rb * batch + rr, n == 0)
                return carry

            lax.fori_loop(0, self.d // batch, body, 0)

    def rows_of(self, g, r):
        return pl.ds(g * self.span + r, DIL_BLK, stride=self.d)

    def keys(self, cur_ref, prev_ref, g, r):
        before = prev_ref[pl.ds(r, DIL_BLK, stride=self.d), :] if g == 0 else cur_ref[self.rows_of(g - 1, r), :]
        return jnp.concatenate([before, cur_ref[self.rows_of(g, r), :]], axis=0).astype(BF16)


def _dil_fwd(qkv, d, name):
    S = qkv.shape[0]
    plan = _DilPlan(S, d)

    def body(q_ref, kc_ref, kp_ref, vc_ref, vp_ref, o_ref, lse_ref):
        hp = pl.program_id(1)

        @pl.when(hp == 0)
        def _():
            lse_ref[...] = jnp.zeros_like(lse_ref)

        lane = lax.broadcasted_iota(jnp.int32, (DIL_BLK, LANES), 1)

        def unit(g, r, first):
            valid = _dil_valid(first)
            rows = plan.rows_of(g, r)
            qm = _head_masks(q_ref[rows, :].astype(BF16))
            kk = plan.keys(kc_ref, kp_ref, g, r)
            vm = _head_masks(plan.keys(vc_ref, vp_ref, g, r))
            lse_t = lse_ref[rows, :]
            o2 = None
            for h in range(2):
                s = jnp.where(valid, _dot_nt(qm[h], kk), -1e30)
                m = jnp.max(s, axis=-1, keepdims=True)
                p = jnp.exp(s - m)
                den = jnp.sum(p, axis=-1, keepdims=True)
                oh = _dot(p.astype(BF16), vm[h]) / den
                o2 = oh if o2 is None else o2 + oh
                lse_t = jnp.where(lane == 2 * hp + h, m + jnp.log(den), lse_t)
            o_ref[rows, :] = o2
            lse_ref[rows, :] = lse_t

        plan.units(unit)

    return pl.pallas_call(
        body, name=name, grid=(plan.steps, HEAD_PAIRS),
        in_specs=[plan.cur(0), plan.cur(HEAD_PAIRS), plan.prev(HEAD_PAIRS), plan.cur(2 * HEAD_PAIRS),
                  plan.prev(2 * HEAD_PAIRS)],
        out_specs=[plan.cur(0), plan.shared()],
        out_shape=[jax.ShapeDtypeStruct((S, D_MODEL), F32), jax.ShapeDtypeStruct((S, LANES), F32)],
        compiler_params=_params(("parallel", "arbitrary")),
    )(qkv, qkv, qkv, qkv, qkv)


def _head_expand():
    return jnp.asarray((np.arange(LANES)[:, None] == np.arange(D_MODEL)[None, :] // HEAD_DIM).astype(np.float32), BF16)


def _dil_merge(os_, lses, name):
    S = os_[0].shape[0]
    tm = 256
    nbr = len(os_)

    def body(*refs):
        o_refs, l_refs, e_ref = refs[:nbr], refs[nbr:2 * nbr], refs[2 * nbr]
        out_ref, outb_ref, lse_ref = refs[2 * nbr + 1:]
        ls = [r[...] for r in l_refs]
        m = ls[0]
        for l in ls[1:]:
            m = jnp.maximum(m, l)
        tot = jnp.exp(ls[0] - m)
        for l in ls[1:]:
            tot = tot + jnp.exp(l - m)
        lse = m + jnp.log(tot)
        acc = None
        for o_r, l in zip(o_refs, ls):
            wt = _dot3(jnp.exp(l - lse), e_ref[...])
            term = wt * o_r[...]
            acc = term if acc is None else acc + term
        out_ref[...] = acc
        outb_ref[...] = acc.astype(BF16)
        lse_ref[...] = lse

    row = lambda i: (i, 0)
    return pl.pallas_call(
        body, name=name, grid=(S // tm,),
        in_specs=[pl.BlockSpec((tm, D_MODEL), row)] * nbr + [pl.BlockSpec((tm, LANES), row)] * nbr
        + [pl.BlockSpec((LANES, D_MODEL), lambda i: (0, 0))],
        out_specs=[pl.BlockSpec((tm, D_MODEL), row), pl.BlockSpec((tm, D_MODEL), row), pl.BlockSpec((tm, LANES), row)],
        out_shape=[jax.ShapeDtypeStruct((S, D_MODEL), F32), jax.ShapeDtypeStruct((S, D_MODEL), BF16),
                   jax.ShapeDtypeStruct((S, LANES), F32)],
        compiler_params=_params(("parallel",)),
    )(*os_, *lses, _head_expand())


def _dil_bwd(qkv, do, lse, dlt, d, name):
    S = qkv.shape[0]
    plan = _DilPlan(S, d)

    def body(q_ref, kc_ref, kp_ref, vc_ref, vp_ref, do_ref, lse_ref, dl_ref,
             dq_ref, dka_ref, dkb_ref, dva_ref, dvb_ref):
        hp = pl.program_id(1)

        def unit(g, r, first):
            valid = _dil_valid(first)
            rows = plan.rows_of(g, r)
            qm = _head_masks(q_ref[rows, :].astype(BF16))
            dom = _head_masks(do_ref[rows, :].astype(BF16))
            kk = plan.keys(kc_ref, kp_ref, g, r)
            vv = plan.keys(vc_ref, vp_ref, g, r)
            km = _head_masks(kk)
            lse_t = lse_ref[rows, :]
            dl_t = dl_ref[rows, :]
            dq2 = dkk = dvv = None
            for h in range(2):
                s = _dot_nt(qm[h], kk)
                p = jnp.where(valid, jnp.exp(s - _lane_pick(lse_t, 2 * hp + h)), 0.0)
                ds = (p * (_dot_nt(dom[h], vv) - _lane_pick(dl_t, 2 * hp + h))).astype(BF16)
                t_q = _dot(ds, km[h])
                t_k = _dot_tn(ds, qm[h])
                t_v = _dot_tn(p.astype(BF16), dom[h])
                dq2 = t_q if dq2 is None else dq2 + t_q
                dkk = t_k if dkk is None else dkk + t_k
                dvv = t_v if dvv is None else dvv + t_v
            dq_ref[rows, :] = dq2
            dkb_ref[rows, :] = dkk[:DIL_BLK]
            dka_ref[rows, :] = dkk[DIL_BLK:]
            dvb_ref[rows, :] = dvv[:DIL_BLK]
            dva_ref[rows, :] = dvv[DIL_BLK:]

        plan.units(unit)

    return pl.pallas_call(
        body, name=name, grid=(plan.steps, HEAD_PAIRS),
        in_specs=[plan.cur(0), plan.cur(HEAD_PAIRS), plan.prev(HEAD_PAIRS), plan.cur(2 * HEAD_PAIRS),
                  plan.prev(2 * HEAD_PAIRS), plan.cur(0), plan.shared(), plan.shared()],
        out_specs=[plan.cur(0)] * 5,
        out_shape=[jax.ShapeDtypeStruct((S, D_MODEL), F32)] * 5,
        compiler_params=_params(("parallel", "arbitrary"), VMEM_BIG),
    )(qkv, qkv, qkv, qkv, qkv, do, lse, dlt)


def _dil_combine(parts, rope, name):
    S = parts[0][0].shape[0]
    tm = DIL_BLK
    nblk = S // tm
    dils = [d for _, d in DILATED_BRANCHES]

    def body(*refs):
        ins = refs[:5 * len(dils)]
        c_ref, s1_ref, s2_ref, o_ref = refs[5 * len(dils):]
        i = pl.program_id(0)
        tabs = (c_ref[...], s1_ref[...], s2_ref[...])
        dq = dk = dv = None
        for b, d in enumerate(dils):
            dq_r, dka_r, dkb_r, dva_r, dvb_r = ins[5 * b:5 * b + 5]
            live = (i + d < nblk).astype(F32)
            tq = dq_r[...]
            tk = dka_r[...] + live * dkb_r[...]
            tv = dva_r[...] + live * dvb_r[...]
            dq = tq if dq is None else dq + tq
            dk = tk if dk is None else dk + tk
            dv = tv if dv is None else dv + tv
        dq = dq * Q_SCALE
        for g in range(HEAD_PAIRS):
            cols = slice(g * LANES, (g + 1) * LANES)
            o_ref[:, g * LANES:(g + 1) * LANES] = _rope_apply(dq[:, cols], *tabs, -1.0).astype(BF16)
            o_ref[:, D_MODEL + g * LANES:D_MODEL + (g + 1) * LANES] = _rope_apply(dk[:, cols], *tabs, -1.0).astype(BF16)
        o_ref[:, 2 * D_MODEL:] = dv.astype(BF16)

    row = pl.BlockSpec((tm, D_MODEL), lambda i: (i, 0))
    in_specs = []
    args = []
    for (dq_b, dka, dkb, dva, dvb), d in zip(parts, dils):
        ahead = pl.BlockSpec((tm, D_MODEL), lambda i, d=d: (jnp.minimum(i + d, nblk - 1), 0))
        in_specs += [row, row, ahead, row, ahead]
        args += [dq_b, dka, dkb, dva, dvb]
    in_specs += [pl.BlockSpec((tm, LANES), lambda i: (i, 0))] * 3
    return pl.pallas_call(
        body, name=name, grid=(nblk,),
        in_specs=in_specs,
        out_specs=pl.BlockSpec((tm, 3 * D_MODEL), lambda i: (i, 0)),
        out_shape=jax.ShapeDtypeStruct((S, 3 * D_MODEL), BF16),
        compiler_params=_params(("parallel",), VMEM_BIG),
    )(*args, *rope)


def _mesh_pos():
    return lax.axis_index("x"), lax.axis_index("y"), lax.axis_index("c")


def _all_gather(shard, name):
    R, C = shard.shape

    def body(x_ref, out_ref, send_sems, recv_sems, local_sem):
        x, y, c = _mesh_pos()
        me, sibling = (x, y, c), (x, y, 1 - c)
        chips = [(1 - x, y), (x, 1 - y), (1 - x, 1 - y)]

        def blk(p):
            return out_ref.at[4 * p[0] + 2 * p[1] + p[2]]

        def copy(k, block, to, src=None):
            return pltpu.make_async_remote_copy(
                src_ref=blk(block) if src is None else src, dst_ref=blk(block),
                send_sem=send_sems.at[k], recv_sem=recv_sems.at[k],
                device_id=to, device_id_type=pl.DeviceIdType.MESH)

        mine = pltpu.make_async_copy(x_ref, blk(me), local_sem)
        mine.start()
        first = [copy(0, me, sibling, src=x_ref)]
        first += [copy(1 + j, me, (*chip, c), src=x_ref) for j, chip in enumerate(chips)]
        for cp in first:
            cp.start()
        passed = [copy(4 + j, (*chip, c), sibling) for j, chip in enumerate(chips)]
        for j, chip in enumerate(chips):
            copy(1 + j, (*chip, c), me).wait_recv()
            passed[j].start()
        copy(0, sibling, me).wait_recv()
        for j, chip in enumerate(chips):
            copy(4 + j, (*chip, 1 - c), me).wait_recv()
        for cp in first + passed:
            cp.wait_send()
        mine.wait()

    return pl.pallas_call(
        body, name=name,
        in_specs=[pl.BlockSpec(memory_space=pl.ANY)],
        out_specs=pl.BlockSpec(memory_space=pl.ANY),
        out_shape=jax.ShapeDtypeStruct((N_DEV, R, C), shard.dtype),
        scratch_shapes=[pltpu.SemaphoreType.DMA((7,)), pltpu.SemaphoreType.DMA((7,)), pltpu.SemaphoreType.DMA],
    )(shard)


def _rs_pair(g, name):
    _, R, C = g.shape

    def body(g_ref, out_ref, send_sems, recv_sems):
        x, y, c = _mesh_pos()
        sibling = (x, y, 1 - c)
        cps = []
        for chip in range(4):
            cps.append(pltpu.make_async_remote_copy(
                src_ref=g_ref.at[2 * chip + (1 - c)], dst_ref=out_ref.at[chip],
                send_sem=send_sems.at[chip], recv_sem=recv_sems.at[chip],
                device_id=sibling, device_id_type=pl.DeviceIdType.MESH))
        for cp in cps:
            cp.start()
        for cp in cps:
            cp.wait_recv()
        for cp in cps:
            cp.wait_send()

    return pl.pallas_call(
        body, name=name,
        in_specs=[pl.BlockSpec(memory_space=pl.ANY)],
        out_specs=pl.BlockSpec(memory_space=pl.ANY),
        out_shape=jax.ShapeDtypeStruct((4, R, C), g.dtype),
        scratch_shapes=[pltpu.SemaphoreType.DMA((4,)), pltpu.SemaphoreType.DMA((4,))],
    )(g)


def _pair_add(g, got, cidx, name):
    _, R, C = g.shape
    tr = 256

    def body(c_ref, g_ref, r_ref, o_ref, ob_ref):
        s = g_ref[...] + r_ref[...].astype(F32)
        o_ref[...] = s
        ob_ref[...] = s.astype(BF16)

    blk = pl.BlockSpec((None, tr, C), lambda k, i, c: (k, i, 0))
    return pl.pallas_call(
        body, name=name,
        grid_spec=pltpu.PrefetchScalarGridSpec(
            num_scalar_prefetch=1, grid=(4, R // tr),
            in_specs=[pl.BlockSpec((None, tr, C), lambda k, i, c: (2 * k + c[0], i, 0)), blk],
            out_specs=[blk, blk]),
        out_shape=[jax.ShapeDtypeStruct((4, R, C), F32), jax.ShapeDtypeStruct((4, R, C), BF16)],
        compiler_params=_params(("parallel", "parallel")),
    )(cidx, g, got)


def _rs_chips(p, name):
    _, R, C = p.shape

    def body(p_ref, out_ref, send_sems, recv_sems):
        x, y, c = _mesh_pos()
        chips = [(1 - x, y), (x, 1 - y), (1 - x, 1 - y)]
        cps = []
        for j, (cx, cy) in enumerate(chips):
            cps.append(pltpu.make_async_remote_copy(
                src_ref=p_ref.at[2 * cx + cy], dst_ref=out_ref.at[j],
                send_sem=send_sems.at[j], recv_sem=recv_sems.at[j],
                device_id=(cx, cy, c), device_id_type=pl.DeviceIdType.MESH))
        for cp in cps:
            cp.start()
        for cp in cps:
            cp.wait_recv()
        for cp in cps:
            cp.wait_send()

    return pl.pallas_call(
        body, name=name,
        in_specs=[pl.BlockSpec(memory_space=pl.ANY)],
        out_specs=pl.BlockSpec(memory_space=pl.ANY),
        out_shape=jax.ShapeDtypeStruct((3, R, C), p.dtype),
        scratch_shapes=[pltpu.SemaphoreType.DMA((3,)), pltpu.SemaphoreType.DMA((3,))],
    )(p)


def _adamw_math(w, g, m, v):
    m2 = ADAM_B1 * m + (1.0 - ADAM_B1) * g
    v2 = ADAM_B2 * v + (1.0 - ADAM_B2) * (g * g)
    m_hat = m2 / (1.0 - ADAM_B1 ** ADAM_STEP)
    v_hat = v2 / (1.0 - ADAM_B2 ** ADAM_STEP)
    delta = -ADAM_LR * (m_hat / (jnp.sqrt(v_hat) + ADAM_EPS) + ADAM_WD * w)
    return delta, m2, v2


def _adamw_shard(p, got, chip_idx, w, m, v, name):
    R, C = w.shape
    tr = 256

    def body(k_ref, p_ref, r_ref, w_ref, m_ref, v_ref, g_out, d_out, m_out, v_out):
        g = ((p_ref[...] + r_ref[0].astype(F32)) + r_ref[1].astype(F32)) + r_ref[2].astype(F32)
        delta, m2, v2 = _adamw_math(w_ref[...], g, m_ref[...], v_ref[...])
        g_out[...] = g
        d_out[...] = delta
        m_out[...] = m2
        v_out[...] = v2

    row = pl.BlockSpec((tr, C), lambda i, k: (i, 0))
    return pl.pallas_call(
        body, name=name,
        grid_spec=pltpu.PrefetchScalarGridSpec(
            num_scalar_prefetch=1, grid=(R // tr,),
            in_specs=[pl.BlockSpec((None, tr, C), lambda i, k: (k[0], i, 0)),
                      pl.BlockSpec((3, tr, C), lambda i, k: (0, i, 0)), row, row, row],
            out_specs=[row] * 4),
        out_shape=[jax.ShapeDtypeStruct((R, C), F32)] * 4,
        compiler_params=_params(("parallel",)),
    )(chip_idx, p, got, w, m, v)


def _adamw_small(gathered, w, m, v, name):
    _, R, C = gathered.shape

    def body(a_ref, w_ref, m_ref, v_ref, g_out, d_out, m_out, v_out):
        g = a_ref[0]
        for k in range(1, N_DEV):
            g = g + a_ref[k]
        delta, m2, v2 = _adamw_math(w_ref[...], g, m_ref[...], v_ref[...])
        g_out[...] = g
        d_out[...] = delta
        m_out[...] = m2
        v_out[...] = v2

    return pl.pallas_call(
        body, name=name, out_shape=[jax.ShapeDtypeStruct((R, C), F32)] * 4,
    )(gathered, w, m, v)


def _rope_tables(S):
    half = ROPE_DIM // 2
    inv_freq = ROPE_THETA ** (-jnp.arange(half, dtype=F32) / half)
    ang = jnp.arange(S, dtype=jnp.int32).astype(F32)[:, None] * inv_freq[None, :]
    cos, sin = jnp.cos(ang), jnp.sin(ang)
    ones = jnp.ones((S, HEAD_DIM - ROPE_DIM), F32)
    zeros = jnp.zeros((S, HEAD_DIM - ROPE_DIM), F32)
    zh = jnp.zeros((S, half), F32)
    c = jnp.concatenate([cos, cos, ones], axis=1)
    s1 = jnp.concatenate([zh, sin, zeros], axis=1)
    s2 = jnp.concatenate([-sin, zh, zeros], axis=1)
    two = lambda t: jnp.concatenate([t, t], axis=1)
    return two(c), two(s1), two(s2)


def _chunk_transposed(a, S):
    return a.reshape(S // SB_CH, SB_CH, HEAD_PAIRS, LANES).transpose(2, 0, 3, 1)


def _flat_shards(ws):
    return jnp.concatenate([w.reshape(-1, D_MODEL) for layer in ws for w in layer], axis=0)


def kernel(x, w_qkv_0, w_o_0, ln1_g_0, ln1_b_0, w_ff1_0, w_ff2_0, ln2_g_0, ln2_b_0, w_qkv_1, w_o_1, ln1_g_1, ln1_b_1, w_ff1_1, w_ff2_1, ln2_g_1, ln2_b_1, loss_target, m_w_qkv_0, m_w_o_0, m_ln1_g_0, m_ln1_b_0, m_w_ff1_0, m_w_ff2_0, m_ln2_g_0, m_ln2_b_0, m_w_qkv_1, m_w_o_1, m_ln1_g_1, m_ln1_b_1, m_w_ff1_1, m_w_ff2_1, m_ln2_g_1, m_ln2_b_1, v_w_qkv_0, v_w_o_0, v_ln1_g_0, v_ln1_b_0, v_w_ff1_0, v_w_ff2_0, v_ln2_g_0, v_ln2_b_0, v_w_qkv_1, v_w_o_1, v_ln1_g_1, v_ln1_b_1, v_w_ff1_1, v_w_ff2_1, v_ln2_g_1, v_ln2_b_1):
    S = x.shape[1]
    x0 = x.reshape(S, D_MODEL)
    target = loss_target.reshape(S, D_MODEL)
    mats = ((w_qkv_0, w_o_0, w_ff1_0, w_ff2_0), (w_qkv_1, w_o_1, w_ff1_1, w_ff2_1))
    mats_m = ((m_w_qkv_0, m_w_o_0, m_w_ff1_0, m_w_ff2_0), (m_w_qkv_1, m_w_o_1, m_w_ff1_1, m_w_ff2_1))
    mats_v = ((v_w_qkv_0, v_w_o_0, v_w_ff1_0, v_w_ff2_0), (v_w_qkv_1, v_w_o_1, v_w_ff1_1, v_w_ff2_1))
    vecs = (ln1_g_0, ln1_b_0, ln2_g_0, ln2_b_0, ln1_g_1, ln1_b_1, ln2_g_1, ln2_b_1)
    vecs_m = (m_ln1_g_0, m_ln1_b_0, m_ln2_g_0, m_ln2_b_0, m_ln1_g_1, m_ln1_b_1, m_ln2_g_1, m_ln2_b_1)
    vecs_v = (v_ln1_g_0, v_ln1_b_0, v_ln2_g_0, v_ln2_b_0, v_ln1_g_1, v_ln1_b_1, v_ln2_g_1, v_ln2_b_1)

    w_flat = _flat_shards(mats)
    w_all = _all_gather(w_flat.astype(BF16), "ag_weights")
    layers = []
    for l in range(N_LAYERS):
        base = l * LAYER_ROWS
        r0, r1, r2, r3 = np.cumsum((0,) + SHARD_ROWS)[:4] + base
        layers.append(dict(
            qkv=w_all[:, r0:r0 + 384].reshape(N_DEV, D_MODEL, 384),
            o=w_all[:, r1:r1 + 128].reshape(D_MODEL, D_MODEL),
            ff1=w_all[:, r2:r2 + 512].reshape(N_DEV, D_MODEL, 512),
            ff2=w_all[:, r3:r3 + 512].reshape(D_FF, D_MODEL),
            g1=vecs[4 * l].reshape(1, D_MODEL), b1=vecs[4 * l + 1].reshape(1, D_MODEL),
            g2=vecs[4 * l + 2].reshape(1, D_MODEL), b2=vecs[4 * l + 3].reshape(1, D_MODEL)))

    rope = _rope_tables(S)
    tmat_later = _sb_tmat(True)
    tmat_upto = _sb_tmat(False)

    saved = []
    xin, xinb = x0, x0.astype(BF16)
    for l, W in enumerate(layers):
        sv = dict(xin=xin, xinb=xinb)
        qkv = _qkv_proj(xinb, W["qkv"], rope if l == 1 else None, Q_SCALE * LOG2E if l == 0 else Q_SCALE,
                        BF16 if l == 0 else F32, f"qkv_proj_{l}")
        sv["qkv"] = qkv
        if l == 0:
            vT3 = _chunk_transposed(qkv[:, 2 * D_MODEL:], S)
            ob, sb_tiles, sb_first = _sb_fwd(qkv, vT3, tmat_later, "sb_fwd")
            o = None
            sv.update(sb_tiles=sb_tiles, sb_first=sb_first)
        else:
            outs = [_dil_fwd(qkv, d, f"dil_fwd_{d}") for _, d in DILATED_BRANCHES]
            o, ob, lse = _dil_merge([t[0] for t in outs], [t[1] for t in outs], "dil_merge")
            sv["lse"] = lse
        sv.update(o=o, ob=ob)
        y1, x1, x1b = _mm_res_ln(ob, xin, W["o"], W["g1"], W["b1"], f"attn_out_ln_{l}")
        hpre, h = _ff1(x1b, W["ff1"], f"ff1_{l}")
        y2, x2, x2b = _mm_res_ln(h, x1, W["ff2"], W["g2"], W["b2"], f"ff2_ln_{l}")
        sv.update(y1=y1, x1=x1, x1b=x1b, hpre=hpre, h=h, y2=y2)
        saved.append(sv)
        xin, xinb = x2, x2b

    dout, loss_parts = _loss_grad(xin, target, "loss_grad")
    loss = lax.psum(jnp.sum(loss_parts), MESH_AXES)

    gmats = [None] * N_LAYERS
    gvecs = [None] * (4 * N_LAYERS)
    for l in reversed(range(N_LAYERS)):
        W, sv = layers[l], saved[l]
        dy2, dy2b, gb2 = _ln_bwd(dout, sv["y2"], W["g2"], f"ln2_bwd_{l}")
        dhp = _dh(dy2b, W["ff2"], sv["hpre"], f"dh_{l}")
        g_ff2 = _mm_tn(sv["h"], dy2b, 512, D_MODEL, False, f"dw_ff2_{l}")
        dx1 = _dx_blk(dy2, dhp, W["ff1"], f"dx_ff1_{l}")
        g_ff1 = _mm_tn(sv["x1b"], dhp, D_MODEL, 512, True, f"dw_ff1_{l}")
        dy1, dy1b, gb1 = _ln_bwd(dx1, sv["y1"], W["g1"], f"ln1_bwd_{l}")
        do = _mm_nt_plain(dy1b, W["o"], f"do_{l}")
        g_o = _mm_tn(sv["ob"], dy1b, 512, D_MODEL, False, f"dw_o_{l}")
        if l == 0:
            kT3 = _chunk_transposed(sv["qkv"][:, D_MODEL:2 * D_MODEL], S)
            dq, dk, dv = _sb_bwd(sv["qkv"], kT3, do, sv["sb_tiles"], sv["sb_first"], tmat_upto, "sb_bwd")
            dqkv = jnp.concatenate([dq, dk, dv], axis=1).astype(BF16)
        else:
            dlt = _head_sums(do, sv["o"], "head_sums")
            parts = [_dil_bwd(sv["qkv"], do, sv["lse"], dlt, d, f"dil_bwd_{d}") for _, d in DILATED_BRANCHES]
            dqkv = _dil_combine(parts, rope, "dil_combine")
        dout = _dx_blk(dy1, dqkv, W["qkv"], f"dx_qkv_{l}")
        g_qkv = _mm_tn(sv["xinb"], dqkv, D_MODEL, 384, True, f"dw_qkv_{l}")
        gmats[l] = (g_qkv.reshape(N_DEV, 384, D_MODEL), g_o.reshape(N_DEV, 128, D_MODEL),
                    g_ff1.reshape(N_DEV, 512, D_MODEL), g_ff2.reshape(N_DEV, 512, D_MODEL))
        gvecs[4 * l:4 * l + 4] = [gb1[0], gb1[1], gb2[0], gb2[1]]
    grad_x = dout.reshape(1, S, D_MODEL)

    cx, cy, cc = _mesh_pos()
    g_all = jnp.concatenate([g for layer in gmats for g in layer], axis=1)
    got_pair = _rs_pair(g_all.astype(BF16), "rs_pair")
    chip_part, chip_part_b = _pair_add(g_all, got_pair, cc.astype(jnp.int32).reshape(1), "rs_pair_add")
    got_chips = _rs_chips(chip_part_b, "rs_chips")
    chip_idx = (2 * cx + cy).astype(jnp.int32).reshape(1)
    g_sh, d_sh, m_sh, v_sh = _adamw_shard(chip_part, got_chips, chip_idx, w_flat, _flat_shards(mats_m),
                                          _flat_shards(mats_v), "adamw_mats")

    def unflat(a):
        out, pos = [], 0
        for layer in mats:
            for w in layer:
                n = w.size // D_MODEL
                out.append(a[pos:pos + n].reshape(w.shape))
                pos += n
        return out

    gv_all = _all_gather(jnp.stack(gvecs), "ag_vec_grads")
    g_v, d_v, m_v, v_v = _adamw_small(gv_all, jnp.stack(vecs), jnp.stack(vecs_m), jnp.stack(vecs_v), "adamw_vecs")

    def interleave(mat_list, vec_arr):
        out = []
        for l in range(N_LAYERS):
            qkv_, o_, ff1_, ff2_ = mat_list[4 * l:4 * l + 4]
            out += [qkv_, o_, vec_arr[4 * l], vec_arr[4 * l + 1], ff1_, ff2_, vec_arr[4 * l + 2], vec_arr[4 * l + 3]]
        return out

    return (loss, grad_x, *interleave(unflat(g_sh), g_v), *interleave(unflat(d_sh), d_v),
            *interleave(unflat(m_sh), m_v), *interleave(unflat(v_sh), v_v))
```

```python
import functools
import math

import jax
import jax.numpy as jnp
import numpy as np
from jax import lax
from jax.experimental import pallas as pl
from jax.experimental.pallas import tpu as pltpu

F32 = jnp.float32
BF16 = jnp.bfloat16

D_MODEL = 1024
N_HEADS = 16
HEAD_DIM = 64
D_FF = 4096
N_DEV = 8
N_LAYERS = 2
ROPE_THETA = 500000.0
ROPE_DIM = 16
DILATED_BRANCHES = ((128, 1), (512, 4), (2048, 16))
ALPHA = (2 * N_LAYERS) ** 0.25
LN_EPS = 1e-5
Q_SCALE = 1.0 / math.sqrt(HEAD_DIM)
LOG2E = math.log2(math.e)
LN2 = math.log(2.0)
ADAM_LR, ADAM_B1, ADAM_B2, ADAM_EPS, ADAM_WD, ADAM_STEP = 0.001, 0.9, 0.999, 1e-08, 0.01, 10

LANES = 128
HEAD_PAIRS = D_MODEL // LANES
SB_TQ = 256
SB_CH = 256
SB_STEPS = 4
SB_SAVE_SLOTS = 2 * SB_STEPS
SB_LOAD_SLOTS = 12
SB_LOAD_AHEAD = SB_LOAD_SLOTS - SB_STEPS - 1
SB_DEAD = -160.0
DIL_BLK = 128
VMEM_BIG = 56 * 2 ** 20
MESH_AXES = ("x", "y", "c")

SHARD_ROWS = (384, 128, 512, 512)
LAYER_ROWS = sum(SHARD_ROWS)
ALL_ROWS = N_LAYERS * LAYER_ROWS


def _params(sem=None, vmem=None):
    kw = {}
    if sem is not None:
        kw["dimension_semantics"] = sem
    if vmem is not None:
        kw["vmem_limit_bytes"] = vmem
    return pltpu.CompilerParams(**kw)


def _dot(a, b):
    return jnp.dot(a, b, preferred_element_type=F32)


def _dot_nt(a, b):
    return lax.dot_general(a, b, (((1,), (1,)), ((), ())), preferred_element_type=F32)


def _dot_tn(a, b):
    return lax.dot_general(a, b, (((0,), (0,)), ((), ())), preferred_element_type=F32)


def _split3(p):
    hi = p.astype(BF16)
    r1 = p - hi.astype(F32)
    mid = r1.astype(BF16)
    lo = (r1 - mid.astype(F32)).astype(BF16)
    return hi, mid, lo


def _dot3(p, e):
    hi, mid, lo = _split3(p)
    return _dot(hi, e) + _dot(mid, e) + _dot(lo, e)


def _rope_apply(a, c, s1, s2, sign):
    return a * c + sign * (pltpu.roll(a, 8, 1) * s1 + pltpu.roll(a, LANES - 8, 1) * s2)


def _qkv_proj(xb, w_blk, rope, q_mult, out_dtype, name):
    S = xb.shape[0]
    tm = 512
    n_rope = 0 if rope is None else 3

    def body(*refs):
        x_ref, w_ref = refs[:2]
        tabs = [r[...] for r in refs[2:2 + n_rope]]
        o_ref = refs[2 + n_rope]
        x = x_ref[...]
        for j in range(N_DEV):
            acc = _dot(x, w_ref[j])
            for g in range(3):
                col = j * 384 + g * LANES
                a = acc[:, g * LANES:(g + 1) * LANES]
                if n_rope and col < 2 * D_MODEL:
                    a = _rope_apply(a, *tabs, 1.0)
                if col < D_MODEL:
                    a = a * q_mult
                o_ref[:, col:col + LANES] = a.astype(out_dtype)

    tab_specs = [pl.BlockSpec((tm, LANES), lambda i: (i, 0))] * n_rope
    return pl.pallas_call(
        body, name=name, grid=(S // tm,),
        in_specs=[pl.BlockSpec((tm, D_MODEL), lambda i: (i, 0)),
                  pl.BlockSpec((N_DEV, D_MODEL, 384), lambda i: (0, 0, 0))] + tab_specs,
        out_specs=pl.BlockSpec((tm, 3 * D_MODEL), lambda i: (i, 0)),
        out_shape=jax.ShapeDtypeStruct((S, 3 * D_MODEL), out_dtype),
        compiler_params=_params(("parallel",), VMEM_BIG),
    )(xb, w_blk, *(rope or ()))


def _layer_norm_rows(y, g, b):
    mu = jnp.mean(y, axis=-1, keepdims=True)
    yc = y - mu
    var = jnp.mean(yc * yc, axis=-1, keepdims=True)
    return yc * lax.rsqrt(var + LN_EPS) * g + b


def _mm_res_ln(a, xres, w, g, b, name):
    S, K = a.shape
    tm = 512 if K <= 1024 else 256

    def body(a_ref, x_ref, w_ref, g_ref, b_ref, y_ref, xn_ref, xb_ref):
        y = ALPHA * x_ref[...] + _dot(a_ref[...], w_ref[...])
        xn = _layer_norm_rows(y, g_ref[...], b_ref[...])
        y_ref[...] = y
        xn_ref[...] = xn
        xb_ref[...] = xn.astype(BF16)

    row = lambda i: (i, 0)
    fix = lambda i: (0, 0)
    return pl.pallas_call(
        body, name=name, grid=(S // tm,),
        in_specs=[pl.BlockSpec((tm, K), row), pl.BlockSpec((tm, D_MODEL), row),
                  pl.BlockSpec((K, D_MODEL), fix), pl.BlockSpec((1, D_MODEL), fix),
                  pl.BlockSpec((1, D_MODEL), fix)],
        out_specs=[pl.BlockSpec((tm, D_MODEL), row)] * 3,
        out_shape=[jax.ShapeDtypeStruct((S, D_MODEL), F32), jax.ShapeDtypeStruct((S, D_MODEL), F32),
                   jax.ShapeDtypeStruct((S, D_MODEL), BF16)],
        compiler_params=_params(("parallel",), VMEM_BIG),
    )(a, xres, w, g, b)


def _ff1(xb, w_blk, name):
    S = xb.shape[0]
    tm = 256

    def body(x_ref, w_ref, hp_ref, h_ref):
        x = x_ref[...]
        for j in range(N_DEV):
            acc = _dot(x, w_ref[j])
            r = jnp.maximum(acc, 0.0)
            hp_ref[:, j * 512:(j + 1) * 512] = acc
            h_ref[:, j * 512:(j + 1) * 512] = (r * r).astype(BF16)

    return pl.pallas_call(
        body, name=name, grid=(S // tm,),
        in_specs=[pl.BlockSpec((tm, D_MODEL), lambda i: (i, 0)),
                  pl.BlockSpec((N_DEV, D_MODEL, 512), lambda i: (0, 0, 0))],
        out_specs=[pl.BlockSpec((tm, D_FF), lambda i: (i, 0))] * 2,
        out_shape=[jax.ShapeDtypeStruct((S, D_FF), F32), jax.ShapeDtypeStruct((S, D_FF), BF16)],
        compiler_params=_params(("parallel",), VMEM_BIG),
    )(xb, w_blk)


def _loss_grad(y, target, name):
    S = y.shape[0]
    tm = 512

    def body(y_ref, t_ref, dy_ref, l_ref):
        @pl.when(pl.program_id(0) == 0)
        def _():
            l_ref[...] = jnp.zeros_like(l_ref)

        err = y_ref[...] - t_ref[...]
        dy_ref[...] = err * (1.0 / D_MODEL)
        sq = err * err
        rows = sq[0:8]
        for r in range(1, tm // 8):
            rows = rows + sq[r * 8:(r + 1) * 8]
        acc = rows[:, 0:LANES]
        for g in range(1, D_MODEL // LANES):
            acc = acc + rows[:, g * LANES:(g + 1) * LANES]
        l_ref[...] += acc * (0.5 / D_MODEL)

    return pl.pallas_call(
        body, name=name, grid=(S // tm,),
        in_specs=[pl.BlockSpec((tm, D_MODEL), lambda i: (i, 0))] * 2,
        out_specs=[pl.BlockSpec((tm, D_MODEL), lambda i: (i, 0)), pl.BlockSpec((8, LANES), lambda i: (0, 0))],
        out_shape=[jax.ShapeDtypeStruct((S, D_MODEL), F32), jax.ShapeDtypeStruct((8, LANES), F32)],
        compiler_params=_params(("arbitrary",)),
    )(y, target)


def _ln_bwd(dout, y, g, w, hpre, name):
    S = y.shape[0]
    N = w.shape[0]
    tm = 512 if hpre is None else 256
    tn = 512
    steps = S // tm

    def body(*refs):
        d_ref, y_ref, g_ref, w_ref = refs[:4]
        hp_ref = None if hpre is None else refs[4]
        dy_ref, dyb_ref, gb_ref, o_ref, acc_g, acc_b = refs[-6:]
        i = pl.program_id(0)

        @pl.when(i == 0)
        def _():
            acc_g[...] = jnp.zeros_like(acc_g)
            acc_b[...] = jnp.zeros_like(acc_b)

        d = d_ref[...]
        yv = y_ref[...]
        mu = jnp.mean(yv, axis=-1, keepdims=True)
        yc = yv - mu
        var = jnp.mean(yc * yc, axis=-1, keepdims=True)
        rstd = lax.rsqrt(var + LN_EPS)
        xhat = yc * rstd
        dxh = d * g_ref[...]
        m1 = jnp.mean(dxh, axis=-1, keepdims=True)
        m2 = jnp.mean(dxh * xhat, axis=-1, keepdims=True)
        dy = rstd * (dxh - m1 - xhat * m2)
        dyb = dy.astype(BF16)
        dy_ref[...] = dy
        dyb_ref[...] = dyb
        for c in range(0, N, tn):
            prod = _dot_nt(dyb, w_ref[c:c + tn, :])
            if hpre is None:
                o_ref[:, c:c + tn] = prod
            else:
                o_ref[:, c:c + tn] = (prod * (2.0 * jnp.maximum(hp_ref[:, c:c + tn], 0.0))).astype(BF16)
        pg = d * xhat
        sg = pg[0:8]
        sb = d[0:8]
        for r in range(1, tm // 8):
            sg = sg + pg[r * 8:(r + 1) * 8]
            sb = sb + d[r * 8:(r + 1) * 8]
        acc_g[...] += sg
        acc_b[...] += sb

        @pl.when(i == steps - 1)
        def _():
            gb_ref[0:1, :] = jnp.sum(acc_g[...], axis=0, keepdims=True)
            gb_ref[1:2, :] = jnp.sum(acc_b[...], axis=0, keepdims=True)

    row = lambda i: (i, 0)
    fix = lambda i: (0, 0)
    extra_in = [] if hpre is None else [pl.BlockSpec((tm, N), row)]
    extra_arg = [] if hpre is None else [hpre]
    return pl.pallas_call(
        body, name=name, grid=(steps,),
        in_specs=[pl.BlockSpec((tm, D_MODEL), row), pl.BlockSpec((tm, D_MODEL), row), pl.BlockSpec((1, D_MODEL), fix),
                  pl.BlockSpec((N, D_MODEL), fix)] + extra_in,
        out_specs=[pl.BlockSpec((tm, D_MODEL), row), pl.BlockSpec((tm, D_MODEL), row), pl.BlockSpec((2, D_MODEL), fix),
                   pl.BlockSpec((tm, N), row)],
        out_shape=[jax.ShapeDtypeStruct((S, D_MODEL), F32), jax.ShapeDtypeStruct((S, D_MODEL), BF16),
                   jax.ShapeDtypeStruct((2, D_MODEL), F32),
                   jax.ShapeDtypeStruct((S, N), F32 if hpre is None else BF16)],
        scratch_shapes=[pltpu.VMEM((8, D_MODEL), F32), pltpu.VMEM((8, D_MODEL), F32)],
        compiler_params=_params(("arbitrary",), VMEM_BIG),
    )(dout, y, g, w, *extra_arg)


def _dx_blk(dres, dz, w_blk, name):
    S, N = dz.shape
    bw = w_blk.shape[2]
    tm = 256

    def body(r_ref, z_ref, w_ref, o_ref):
        acc = ALPHA * r_ref[...]
        for j in range(N_DEV):
            acc = acc + _dot_nt(z_ref[:, j * bw:(j + 1) * bw], w_ref[j])
        o_ref[...] = acc

    return pl.pallas_call(
        body, name=name, grid=(S // tm,),
        in_specs=[pl.BlockSpec((tm, D_MODEL), lambda i: (i, 0)), pl.BlockSpec((tm, N), lambda i: (i, 0)),
                  pl.BlockSpec((N_DEV, D_MODEL, bw), lambda i: (0, 0, 0))],
        out_specs=pl.BlockSpec((tm, D_MODEL), lambda i: (i, 0)),
        out_shape=jax.ShapeDtypeStruct((S, D_MODEL), F32),
        compiler_params=_params(("parallel",), VMEM_BIG),
    )(dres, dz, w_blk)


def _mm_tn(a, b, ta, tb, blocked, name):
    S, Ka = a.shape
    Nb = b.shape[1]
    ts = 2048

    def body(a_ref, b_ref, o_ref):
        @pl.when(pl.program_id(2) == 0)
        def _():
            o_ref[...] = jnp.zeros_like(o_ref)

        o_ref[...] += _dot_tn(a_ref[...], b_ref[...])

    if blocked:
        out_spec = pl.BlockSpec((None, ta, tb), lambda i, j, s: (j, i, 0))
        out_shape = jax.ShapeDtypeStruct((Nb // tb, Ka, tb), F32)
    else:
        out_spec = pl.BlockSpec((ta, tb), lambda i, j, s: (i, j))
        out_shape = jax.ShapeDtypeStruct((Ka, Nb), F32)
    return pl.pallas_call(
        body, name=name, grid=(Ka // ta, Nb // tb, S // ts),
        in_specs=[pl.BlockSpec((ts, ta), lambda i, j, s: (s, i)), pl.BlockSpec((ts, tb), lambda i, j, s: (s, j))],
        out_specs=out_spec, out_shape=out_shape,
        compiler_params=_params(("parallel", "parallel", "arbitrary"), VMEM_BIG),
    )(a, b)


def _head_sums(do, o, name):
    S = do.shape[0]
    tm = 512
    sel = (np.arange(D_MODEL)[:, None] // HEAD_DIM == np.arange(LANES)[None, :]).astype(np.float32)

    def body(d_ref, o_ref, e_ref, out_ref):
        out_ref[...] = _dot3(d_ref[...] * o_ref[...], e_ref[...])

    return pl.pallas_call(
        body, name=name, grid=(S // tm,),
        in_specs=[pl.BlockSpec((tm, D_MODEL), lambda i: (i, 0))] * 2 + [pl.BlockSpec((D_MODEL, LANES), lambda i: (0, 0))],
        out_specs=pl.BlockSpec((tm, LANES), lambda i: (i, 0)),
        out_shape=jax.ShapeDtypeStruct((S, LANES), F32),
        compiler_params=_params(("parallel",)),
    )(do, o, jnp.asarray(sel, BF16))


def _sb_tmat(later):
    r = np.arange(SB_CH)
    t = (r[None, :] > r[:, None]) if later else (r[None, :] <= r[:, None])
    return jnp.asarray(np.concatenate([t.astype(np.float32), np.ones((8, SB_CH), np.float32)], axis=0), BF16)


def _sb_gates(z2):
    neg_abs = lax.bitcast_convert_type(lax.bitcast_convert_type(z2, jnp.uint32) | jnp.uint32(0x80000000), F32)
    l1 = jnp.log2(1.0 + jnp.exp2(neg_abs))
    a = jnp.minimum(z2, 0.0) - l1
    return a, a - z2


def _head_masks(x2):
    lane = lax.broadcasted_iota(jnp.int32, x2.shape, 1)
    zero = jnp.zeros_like(x2)
    return jnp.where(lane < HEAD_DIM, x2, zero), jnp.where(lane >= HEAD_DIM, x2, zero)


def _sb_fwd(qkv, vT3, tmat, name):
    S = qkv.shape[0]
    nq = S // SB_TQ
    nch = S // SB_CH
    ns = SB_SAVE_SLOTS

    def body(q_ref, k_ref, vT_ref, t_ref, o_ref, ws_hbm, first_ref, z_scr, a_scr, cum_scr, oT_scr, stage, sems,
             pending):
        hp = pl.program_id(0)
        i = pl.program_id(1)
        base = (i * (i + 1)) // 2
        qm = _head_masks(q_ref[...])

        def save(src, sem, c):
            return pltpu.make_async_copy(src, ws_hbm.at[hp, base + c], sem)

        causal = (lax.broadcasted_iota(jnp.int32, (SB_CH, SB_TQ), 0)
                  < lax.broadcasted_iota(jnp.int32, (SB_CH, SB_TQ), 1))

        def head_rows(vTc, h):
            return vTc[h * HEAD_DIM:(h + 1) * HEAD_DIM, :]

        @pl.when(jnp.logical_and(hp == 0, i == 0))
        def _():
            z_scr[...] = jnp.zeros_like(z_scr)
            a_scr[...] = jnp.zeros_like(a_scr)
            cum_scr[...] = jnp.zeros_like(cum_scr)

        oT_scr[...] = jnp.zeros_like(oT_scr)

        def c_valid(t):
            return jnp.logical_and(t >= 2, t - 2 <= i)

        def c_chunk(t):
            return jnp.clip(i + 2 - t, 0, nch - 1)

        def step(t, p, slot, R, own_b, own_c, has_b=True, has_c=True):
            cA = jnp.maximum(i - t, 0)
            kA = k_ref[pl.ds(pl.multiple_of(cA * SB_CH, SB_CH), SB_CH), :]
            valid = c_valid(t)
            vC = vT_ref[c_chunk(t)]
            out = []
            for h in range(2):
                z_scr[p, h] = _dot_nt(kA, qm[h])
                if has_b:
                    a, lf = _sb_gates(z_scr[1 - p, h])
                    if own_b:
                        lf = jnp.where(causal, lf, 0.0)
                    a_scr[1 - p, h] = a
                    cum_scr[1 - p, h] = _dot(t_ref[...], lf.astype(BF16))
                if not has_c:
                    out.append(R[h])
                    continue
                a_c = a_scr[p, h]
                w = jnp.exp2(a_c + cum_scr[p, h, :SB_CH, :] + R[h])
                if own_c:
                    w = jnp.where(causal, w, 0.0)
                wb = w.astype(BF16)
                stage[slot, 2 * h] = wb
                stage[slot, 2 * h + 1] = a_c.astype(BF16)
                oT_scr[h] += jnp.where(valid, _dot(head_rows(vC, h), wb), 0.0)
                out.append(R[h] + jnp.where(valid, cum_scr[p, h, SB_CH:SB_CH + 1, :], 0.0))
            return tuple(out)

        step_no = hp * nq + i

        @pl.when(step_no == 0)
        def _():
            for s in range(ns):
                pending[s] = 0

        def settle(slot):
            @pl.when(pending[slot] == 1)
            def _():
                save(stage.at[slot], sems.at[slot], 0).wait()
                pending[slot] = 0

        def trip(tt, R, first):
            half = lax.rem(step_no + tt, 2) * SB_STEPS
            for j in range(SB_STEPS):
                settle(half + j)
            for j in range(SB_STEPS):
                R = step(SB_STEPS * tt + j, j % 2, half + j, R, first and j == 1, first and j == 2,
                         not (first and j == 0), not (first and j < 2))
            for j in range(SB_STEPS):
                t = SB_STEPS * tt + j

                @pl.when(c_valid(t))
                def _():
                    save(stage.at[half + j], sems.at[half + j], c_chunk(t)).start()
                    pending[half + j] = 1

            return R

        z1 = jnp.zeros((1, SB_TQ), F32)
        trips = (i + 3 + SB_STEPS - 1) // SB_STEPS

        def alive(carry):
            tt, R = carry
            return jnp.logical_and(tt < trips, jnp.max(jnp.maximum(R[0], R[1])) > SB_DEAD)

        trips, _ = lax.while_loop(alive, lambda carry: (carry[0] + 1, trip(carry[0], carry[1], False)),
                                  (jnp.int32(1), trip(0, (z1, z1), True)))
        first_ref[hp, i] = jnp.maximum(i - (SB_STEPS * trips - 3), 0)

        @pl.when(step_no == HEAD_PAIRS * nq - 1)
        def _():
            for s in range(ns):
                settle(s)

        o_ref[...] = jnp.concatenate([oT_scr[0], oT_scr[1]], axis=0).T.astype(BF16)

    ntile = nq * (nq + 1) // 2
    return pl.pallas_call(
        body, name=name, grid=(HEAD_PAIRS, nq),
        in_specs=[pl.BlockSpec((SB_TQ, LANES), lambda hp, i: (i, hp)),
                  pl.BlockSpec((S, LANES), lambda hp, i: (0, HEAD_PAIRS + hp)),
                  pl.BlockSpec((None, nch, LANES, SB_CH), lambda hp, i: (hp, 0, 0, 0)),
                  pl.BlockSpec((SB_CH + 8, SB_CH), lambda hp, i: (0, 0))],
        out_specs=[pl.BlockSpec((SB_TQ, LANES), lambda hp, i: (i, hp)), pl.BlockSpec(memory_space=pl.ANY),
                   pl.BlockSpec(memory_space=pltpu.SMEM)],
        out_shape=[jax.ShapeDtypeStruct((S, D_MODEL), BF16),
                   jax.ShapeDtypeStruct((HEAD_PAIRS, ntile, 4, SB_CH, SB_TQ), BF16),
                   jax.ShapeDtypeStruct((HEAD_PAIRS, nq), jnp.int32)],
        scratch_shapes=[pltpu.VMEM((2, 2, SB_CH, SB_TQ), F32), pltpu.VMEM((2, 2, SB_CH, SB_TQ), F32),
                        pltpu.VMEM((2, 2, SB_CH + 8, SB_TQ), F32), pltpu.VMEM((2, HEAD_DIM, SB_TQ), F32),
                        pltpu.VMEM((ns, 4, SB_CH, SB_TQ), BF16), pltpu.SemaphoreType.DMA((ns,)),
                        pltpu.SMEM((ns,), jnp.int32)],
        compiler_params=_params(("arbitrary", "arbitrary"), VMEM_BIG),
    )(qkv, qkv, vT3, tmat)


def _sb_bwd(qkv, kT3, do, ws, first, tmat_g, name):
    S = qkv.shape[0]
    nq = S // SB_TQ
    nch = S // SB_CH
    nl = SB_LOAD_SLOTS
    ahead = SB_LOAD_AHEAD

    def body(first_ref, q_ref, do_ref, v_ref, kT_ref, tg_ref, ws_hbm, dq_ref, dk_hbm, dv_hbm, dk_acc, dv_acc, sems,
             dwv_scr, g_scr, sig_scr, cumg_scr, dqT_scr, ring, ring_sems):
        hp = pl.program_id(0)
        i = pl.program_id(1)
        c0 = first_ref[hp, i]
        n = i - c0
        base = (i * (i + 1)) // 2 + c0

        @pl.when(i == 0)
        def _():
            dk_acc[...] = jnp.zeros_like(dk_acc)
            dv_acc[...] = jnp.zeros_like(dv_acc)

        @pl.when(jnp.logical_and(hp == 0, i == 0))
        def _():
            for scr in (dwv_scr, g_scr, sig_scr, cumg_scr, ring):
                scr[...] = jnp.zeros_like(scr)

        step_no = hp * nq + i
        parity = lax.rem(step_no, 2)

        def slot_of(u, par):
            return jnp.where(u < 2, nl + 1 + 2 * par + u, lax.rem(u, nl))

        def copy_in(hp_, tile, u, par):
            sem = jnp.where(u < 2, nl + 2 * par + u, lax.rem(u, nl))
            return pltpu.make_async_copy(ws_hbm.at[hp_, tile + u], ring.at[slot_of(u, par)], ring_sems.at[sem])

        def load(u):
            return copy_in(hp, base, u, parity)

        for u in range(ahead):
            @pl.when(jnp.logical_and(u <= n, jnp.logical_or(u >= 2, step_no == 0)))
            def _():
                load(u).start()

        nxt = jnp.minimum(step_no + 1, HEAD_PAIRS * nq - 1)
        hp_n, i_n = nxt // nq, lax.rem(nxt, nq)
        c0_n = first_ref[hp_n, i_n]
        for u in range(2):
            @pl.when(jnp.logical_and(step_no + 1 < HEAD_PAIRS * nq, u <= i_n - c0_n))
            def _():
                copy_in(hp_n, (i_n * (i_n + 1)) // 2 + c0_n, u, 1 - parity).start()

        dqT_scr[...] = jnp.zeros_like(dqT_scr)
        qm = _head_masks(q_ref[...])
        dom = _head_masks(do_ref[...].astype(BF16))
        causal = (lax.broadcasted_iota(jnp.int32, (SB_CH, SB_TQ), 0)
                  < lax.broadcasted_iota(jnp.int32, (SB_CH, SB_TQ), 1))

        def rows_of(c):
            return pl.ds(pl.multiple_of(c * SB_CH, SB_CH), SB_CH)

        def head_rows(kTc, h):
            return kTc[h * HEAD_DIM:(h + 1) * HEAD_DIM, :]

        def step(t, p, Gs, has_b=True, has_c=True):
            q = 1 - p
            valid_b = jnp.logical_and(t >= 1, t - 1 <= n)
            valid_c = jnp.logical_and(t >= 2, t - 2 <= n)
            c_b = c0 + jnp.clip(t - 1, 0, n)
            c_c = c0 + jnp.clip(t - 2, 0, n)
            slot = jnp.where(valid_b, slot_of(jnp.maximum(t - 1, 0), parity), nl)
            vA = v_ref[rows_of(c0 + jnp.minimum(t, n)), :]
            kTc = kT_ref[c_c]
            keep = jnp.logical_or(causal, t - 2 != n)
            out = []
            for h in range(2):
                dwv_scr[p, h] = _dot_nt(vA, dom[h])

                if has_b:
                    wb = ring[slot, 2 * h]
                    g = wb.astype(F32) * dwv_scr[q, h]
                    g_scr[q, h] = g
                    sig_scr[q, h] = jnp.exp2(ring[slot, 2 * h + 1].astype(F32))
                    cumg_scr[q, h] = _dot(tg_ref[...], g.astype(BF16))
                    dv_h = _dot(wb, dom[h])
                    dv_c = dv_h if h == 0 else dv_c + dv_h

                if not has_c:
                    out.append(Gs[h])
                    continue
                dz = g_scr[p, h] - sig_scr[p, h] * (Gs[h] + cumg_scr[p, h, :SB_CH, :])
                dzb = jnp.where(keep, dz, 0.0).astype(BF16)
                dk_h = _dot(dzb, qm[h])
                dqT_scr[h] += jnp.where(valid_c, _dot(head_rows(kTc, h), dzb), 0.0)
                out.append(Gs[h] + jnp.where(valid_c, cumg_scr[p, h, SB_CH:SB_CH + 1, :], 0.0))
                dk_c = dk_h if h == 0 else dk_c + dk_h
            if has_b:
                dv_acc[rows_of(c_b), :] += jnp.where(valid_b, dv_c, 0.0)
            if has_c:
                dk_acc[rows_of(c_c), :] += jnp.where(valid_c, dk_c, 0.0)
            return tuple(out)

        def trip(tt, Gs, first):
            for j in range(SB_STEPS):
                t = SB_STEPS * tt + j

                @pl.when(jnp.logical_and(t >= 1, t - 1 <= n))
                def _():
                    load(t - 1).wait()

            for j in range(SB_STEPS):
                t = SB_STEPS * tt + j

                @pl.when(t + ahead <= n)
                def _():
                    load(t + ahead).start()

            for j in range(SB_STEPS):
                Gs = step(SB_STEPS * tt + j, j % 2, Gs, not (first and j == 0), not (first and j < 2))
            return Gs

        z1 = jnp.zeros((1, SB_TQ), F32)
        lax.fori_loop(1, (n + 3 + SB_STEPS - 1) // SB_STEPS, lambda tt, Gs: trip(tt, Gs, False),
                      trip(0, (z1, z1), True))
        dq_ref[...] = jnp.concatenate([dqT_scr[0], dqT_scr[1]], axis=0).T * Q_SCALE

        @pl.when(i == nq - 1)
        def _():
            dk_acc[...] = dk_acc[...] * LN2
            cols = pl.ds(pl.multiple_of(hp * LANES, LANES), LANES)
            ck = pltpu.make_async_copy(dk_acc, dk_hbm.at[:, cols], sems.at[0])
            cv = pltpu.make_async_copy(dv_acc, dv_hbm.at[:, cols], sems.at[1])
            ck.start()
            cv.start()
            ck.wait()
            cv.wait()

    blk = lambda hp, i, first: (i, hp)
    return pl.pallas_call(
        body, name=name,
        grid_spec=pltpu.PrefetchScalarGridSpec(
            num_scalar_prefetch=1, grid=(HEAD_PAIRS, nq),
            in_specs=[pl.BlockSpec((SB_TQ, LANES), blk),
                      pl.BlockSpec((SB_TQ, LANES), blk),
                      pl.BlockSpec((S, LANES), lambda hp, i, first: (0, 2 * HEAD_PAIRS + hp)),
                      pl.BlockSpec((None, nch, LANES, SB_CH), lambda hp, i, first: (hp, 0, 0, 0)),
                      pl.BlockSpec((SB_CH + 8, SB_CH), lambda hp, i, first: (0, 0)),
                      pl.BlockSpec(memory_space=pl.ANY)],
            out_specs=[pl.BlockSpec((SB_TQ, LANES), blk), pl.BlockSpec(memory_space=pl.ANY),
                       pl.BlockSpec(memory_space=pl.ANY)],
            scratch_shapes=[pltpu.VMEM((S, LANES), F32), pltpu.VMEM((S, LANES), F32),
                            pltpu.SemaphoreType.DMA((2,))]
            + [pltpu.VMEM((2, 2, SB_CH, SB_TQ), F32)] * 3
            + [pltpu.VMEM((2, 2, SB_CH + 8, SB_TQ), F32), pltpu.VMEM((2, HEAD_DIM, SB_TQ), F32)]
            + [pltpu.VMEM((nl + 5, 4, SB_CH, SB_TQ), BF16), pltpu.SemaphoreType.DMA((nl + 4,))]),
        out_shape=[jax.ShapeDtypeStruct((S, D_MODEL), F32)] * 3,
        compiler_params=_params(("arbitrary", "arbitrary"), VMEM_BIG),
    )(first, qkv, do, qkv, kT3, tmat_g, ws)


def _dil_valid(first):
    qi = lax.broadcasted_iota(jnp.int32, (DIL_BLK, 2 * DIL_BLK), 0)
    kj = lax.broadcasted_iota(jnp.int32, (DIL_BLK, 2 * DIL_BLK), 1)
    dist = DIL_BLK + qi - kj
    return (dist >= 0) & (dist <= DIL_BLK) & (jnp.logical_not(first) | (kj >= DIL_BLK))


def _lane_pick(tile, idx):
    lane = lax.broadcasted_iota(jnp.int32, tile.shape, 1)
    return jnp.sum(jnp.where(lane == idx, tile, 0.0), axis=-1, keepdims=True)


class _DilPlan:
    def __init__(self, S, d):
        self.d = d
        self.span = DIL_BLK * d
        self.groups = max(1, 1024 // self.span)
        self.rows = self.span * self.groups
        self.steps = S // self.rows

    def cur(self, col0):
        return pl.BlockSpec((self.rows, LANES), lambda n, hp: (n, col0 + hp))

    def prev(self, col0):
        g = self.groups
        return pl.BlockSpec((self.span, LANES), lambda n, hp: (jnp.maximum(n * g - 1, 0), col0 + hp))

    def shared(self):
        return pl.BlockSpec((self.rows, LANES), lambda n, hp: (n, 0))

    def units(self, fn):
        n = pl.program_id(0)
        batch = 8
        if self.d * self.groups <= batch:
            for g in range(self.groups):
                for r in range(self.d):
                    fn(g, r, jnp.logical_and(n == 0, g == 0))
        else:
            assert self.groups == 1 and self.d % batch == 0

            def body(rb, carry):
                for rr in range(batch):
                    fn(0, rb * batch + rr, n == 0)
                return carry

            lax.fori_loop(0, self.d // batch, body, 0)

    def rows_of(self, g, r):
        return pl.ds(g * self.span + r, DIL_BLK, stride=self.d)

    def keys(self, cur_ref, prev_ref, g, r):
        before = prev_ref[pl.ds(r, DIL_BLK, stride=self.d), :] if g == 0 else cur_ref[self.rows_of(g - 1, r), :]
        return jnp.concatenate([before, cur_ref[self.rows_of(g, r), :]], axis=0).astype(BF16)


def _dil_fwd(qkv, d, name):
    S = qkv.shape[0]
    plan = _DilPlan(S, d)

    def body(q_ref, kc_ref, kp_ref, vc_ref, vp_ref, o_ref, lse_ref):
        hp = pl.program_id(1)

        @pl.when(hp == 0)
        def _():
            lse_ref[...] = jnp.zeros_like(lse_ref)

        lane = lax.broadcasted_iota(jnp.int32, (DIL_BLK, LANES), 1)

        def unit(g, r, first):
            valid = _dil_valid(first)
            rows = plan.rows_of(g, r)
            qm = _head_masks(q_ref[rows, :].astype(BF16))
            kk = plan.keys(kc_ref, kp_ref, g, r)
            vm = _head_masks(plan.keys(vc_ref, vp_ref, g, r))
            lse_t = lse_ref[rows, :]
            o2 = None
            for h in range(2):
                s = jnp.where(valid, _dot_nt(qm[h], kk), -1e30)
                m = jnp.max(s, axis=-1, keepdims=True)
                p = jnp.exp(s - m)
                den = jnp.sum(p, axis=-1, keepdims=True)
                oh = _dot(p.astype(BF16), vm[h]) / den
                o2 = oh if o2 is None else o2 + oh
                lse_t = jnp.where(lane == 2 * hp + h, m + jnp.log(den), lse_t)
            o_ref[rows, :] = o2
            lse_ref[rows, :] = lse_t

        plan.units(unit)

    return pl.pallas_call(
        body, name=name, grid=(plan.steps, HEAD_PAIRS),
        in_specs=[plan.cur(0), plan.cur(HEAD_PAIRS), plan.prev(HEAD_PAIRS), plan.cur(2 * HEAD_PAIRS),
                  plan.prev(2 * HEAD_PAIRS)],
        out_specs=[plan.cur(0), plan.shared()],
        out_shape=[jax.ShapeDtypeStruct((S, D_MODEL), F32), jax.ShapeDtypeStruct((S, LANES), F32)],
        compiler_params=_params(("parallel", "arbitrary")),
    )(qkv, qkv, qkv, qkv, qkv)


def _head_expand():
    return jnp.asarray((np.arange(LANES)[:, None] == np.arange(D_MODEL)[None, :] // HEAD_DIM).astype(np.float32), BF16)


def _dil_merge(os_, lses, name):
    S = os_[0].shape[0]
    tm = 256
    nbr = len(os_)

    def body(*refs):
        o_refs, l_refs, e_ref = refs[:nbr], refs[nbr:2 * nbr], refs[2 * nbr]
        out_ref, outb_ref, lse_ref = refs[2 * nbr + 1:]
        ls = [r[...] for r in l_refs]
        m = ls[0]
        for l in ls[1:]:
            m = jnp.maximum(m, l)
        tot = jnp.exp(ls[0] - m)
        for l in ls[1:]:
            tot = tot + jnp.exp(l - m)
        lse = m + jnp.log(tot)
        acc = None
        for o_r, l in zip(o_refs, ls):
            wt = _dot3(jnp.exp(l - lse), e_ref[...])
            term = wt * o_r[...]
            acc = term if acc is None else acc + term
        out_ref[...] = acc
        outb_ref[...] = acc.astype(BF16)
        lse_ref[...] = lse

    row = lambda i: (i, 0)
    return pl.pallas_call(
        body, name=name, grid=(S // tm,),
        in_specs=[pl.BlockSpec((tm, D_MODEL), row)] * nbr + [pl.BlockSpec((tm, LANES), row)] * nbr
        + [pl.BlockSpec((LANES, D_MODEL), lambda i: (0, 0))],
        out_specs=[pl.BlockSpec((tm, D_MODEL), row), pl.BlockSpec((tm, D_MODEL), row), pl.BlockSpec((tm, LANES), row)],
        out_shape=[jax.ShapeDtypeStruct((S, D_MODEL), F32), jax.ShapeDtypeStruct((S, D_MODEL), BF16),
                   jax.ShapeDtypeStruct((S, LANES), F32)],
        compiler_params=_params(("parallel",)),
    )(*os_, *lses, _head_expand())


def _dil_bwd(qkv, do, lse, dlt, d, name):
    S = qkv.shape[0]
    plan = _DilPlan(S, d)

    def body(q_ref, kc_ref, kp_ref, vc_ref, vp_ref, do_ref, lse_ref, dl_ref,
             dq_ref, dka_ref, dkb_ref, dva_ref, dvb_ref):
        hp = pl.program_id(1)

        def unit(g, r, first):
            valid = _dil_valid(first)
            rows = plan.rows_of(g, r)
            qm = _head_masks(q_ref[rows, :].astype(BF16))
            dom = _head_masks(do_ref[rows, :].astype(BF16))
            kk = plan.keys(kc_ref, kp_ref, g, r)
            vv = plan.keys(vc_ref, vp_ref, g, r)
            km = _head_masks(kk)
            lse_t = lse_ref[rows, :]
            dl_t = dl_ref[rows, :]
            dq2 = dkk = dvv = None
            for h in range(2):
                s = _dot_nt(qm[h], kk)
                p = jnp.where(valid, jnp.exp(s - _lane_pick(lse_t, 2 * hp + h)), 0.0)
                ds = (p * (_dot_nt(dom[h], vv) - _lane_pick(dl_t, 2 * hp + h))).astype(BF16)
                t_q = _dot(ds, km[h])
                t_k = _dot_tn(ds, qm[h])
                t_v = _dot_tn(p.astype(BF16), dom[h])
                dq2 = t_q if dq2 is None else dq2 + t_q
                dkk = t_k if dkk is None else dkk + t_k
                dvv = t_v if dvv is None else dvv + t_v
            dq_ref[rows, :] = dq2
            dkb_ref[rows, :] = dkk[:DIL_BLK]
            dka_ref[rows, :] = dkk[DIL_BLK:]
            dvb_ref[rows, :] = dvv[:DIL_BLK]
            dva_ref[rows, :] = dvv[DIL_BLK:]

        plan.units(unit)

    return pl.pallas_call(
        body, name=name, grid=(plan.steps, HEAD_PAIRS),
        in_specs=[plan.cur(0), plan.cur(HEAD_PAIRS), plan.prev(HEAD_PAIRS), plan.cur(2 * HEAD_PAIRS),
                  plan.prev(2 * HEAD_PAIRS), plan.cur(0), plan.shared(), plan.shared()],
        out_specs=[plan.cur(0)] * 5,
        out_shape=[jax.ShapeDtypeStruct((S, D_MODEL), F32)] * 5,
        compiler_params=_params(("parallel", "arbitrary"), VMEM_BIG),
    )(qkv, qkv, qkv, qkv, qkv, do, lse, dlt)


def _dil_combine(parts, rope, name):
    S = parts[0][0].shape[0]
    tm = DIL_BLK
    nblk = S // tm
    dils = [d for _, d in DILATED_BRANCHES]

    def body(*refs):
        ins = refs[:5 * len(dils)]
        c_ref, s1_ref, s2_ref, o_ref = refs[5 * len(dils):]
        i = pl.program_id(0)
        tabs = (c_ref[...], s1_ref[...], s2_ref[...])
        dq = dk = dv = None
        for b, d in enumerate(dils):
            dq_r, dka_r, dkb_r, dva_r, dvb_r = ins[5 * b:5 * b + 5]
            live = (i + d < nblk).astype(F32)
            tq = dq_r[...]
            tk = dka_r[...] + live * dkb_r[...]
            tv = dva_r[...] + live * dvb_r[...]
            dq = tq if dq is None else dq + tq
            dk = tk if dk is None else dk + tk
            dv = tv if dv is None else dv + tv
        dq = dq * Q_SCALE
        for g in range(HEAD_PAIRS):
            cols = slice(g * LANES, (g + 1) * LANES)
            o_ref[:, g * LANES:(g + 1) * LANES] = _rope_apply(dq[:, cols], *tabs, -1.0).astype(BF16)
            o_ref[:, D_MODEL + g * LANES:D_MODEL + (g + 1) * LANES] = _rope_apply(dk[:, cols], *tabs, -1.0).astype(BF16)
        o_ref[:, 2 * D_MODEL:] = dv.astype(BF16)

    row = pl.BlockSpec((tm, D_MODEL), lambda i: (i, 0))
    in_specs = []
    args = []
    for (dq_b, dka, dkb, dva, dvb), d in zip(parts, dils):
        ahead = pl.BlockSpec((tm, D_MODEL), lambda i, d=d: (jnp.minimum(i + d, nblk - 1), 0))
        in_specs += [row, row, ahead, row, ahead]
        args += [dq_b, dka, dkb, dva, dvb]
    in_specs += [pl.BlockSpec((tm, LANES), lambda i: (i, 0))] * 3
    return pl.pallas_call(
        body, name=name, grid=(nblk,),
        in_specs=in_specs,
        out_specs=pl.BlockSpec((tm, 3 * D_MODEL), lambda i: (i, 0)),
        out_shape=jax.ShapeDtypeStruct((S, 3 * D_MODEL), BF16),
        compiler_params=_params(("parallel",), VMEM_BIG),
    )(*args, *rope)


def _mesh_pos():
    return lax.axis_index("x"), lax.axis_index("y"), lax.axis_index("c")


def _all_gather(shard, name):
    R, C = shard.shape

    def body(x_ref, out_ref, send_sems, recv_sems, local_sem):
        x, y, c = _mesh_pos()
        me, sibling = (x, y, c), (x, y, 1 - c)
        chips = [(1 - x, y), (x, 1 - y), (1 - x, 1 - y)]

        def blk(p):
            return out_ref.at[4 * p[0] + 2 * p[1] + p[2]]

        def copy(k, block, to, src=None):
            return pltpu.make_async_remote_copy(
                src_ref=blk(block) if src is None else src, dst_ref=blk(block),
                send_sem=send_sems.at[k], recv_sem=recv_sems.at[k],
                device_id=to, device_id_type=pl.DeviceIdType.MESH)

        mine = pltpu.make_async_copy(x_ref, blk(me), local_sem)
        mine.start()
        first = [copy(0, me, sibling, src=x_ref)]
        first += [copy(1 + j, me, (*chip, c), src=x_ref) for j, chip in enumerate(chips)]
        for cp in first:
            cp.start()
        passed = [copy(4 + j, (*chip, c), sibling) for j, chip in enumerate(chips)]
        for j, chip in enumerate(chips):
            copy(1 + j, (*chip, c), me).wait_recv()
            passed[j].start()
        copy(0, sibling, me).wait_recv()
        for j, chip in enumerate(chips):
            copy(4 + j, (*chip, 1 - c), me).wait_recv()
        for cp in first + passed:
            cp.wait_send()
        mine.wait()

    return pl.pallas_call(
        body, name=name,
        in_specs=[pl.BlockSpec(memory_space=pl.ANY)],
        out_specs=pl.BlockSpec(memory_space=pl.ANY),
        out_shape=jax.ShapeDtypeStruct((N_DEV, R, C), shard.dtype),
        scratch_shapes=[pltpu.SemaphoreType.DMA((7,)), pltpu.SemaphoreType.DMA((7,)), pltpu.SemaphoreType.DMA],
    )(shard)


def _rs_pair(g, name):
    _, R, C = g.shape

    def body(g_ref, out_ref, send_sems, recv_sems):
        x, y, c = _mesh_pos()
        sibling = (x, y, 1 - c)
        cps = []
        for chip in range(4):
            cps.append(pltpu.make_async_remote_copy(
                src_ref=g_ref.at[2 * chip + (1 - c)], dst_ref=out_ref.at[chip],
                send_sem=send_sems.at[chip], recv_sem=recv_sems.at[chip],
                device_id=sibling, device_id_type=pl.DeviceIdType.MESH))
        for cp in cps:
            cp.start()
        for cp in cps:
            cp.wait_recv()
        for cp in cps:
            cp.wait_send()

    return pl.pallas_call(
        body, name=name,
        in_specs=[pl.BlockSpec(memory_space=pl.ANY)],
        out_specs=pl.BlockSpec(memory_space=pl.ANY),
        out_shape=jax.ShapeDtypeStruct((4, R, C), g.dtype),
        scratch_shapes=[pltpu.SemaphoreType.DMA((4,)), pltpu.SemaphoreType.DMA((4,))],
    )(g)


def _pair_add(g, got, cidx, name):
    _, R, C = g.shape
    tr = 256

    def body(c_ref, g_ref, r_ref, o_ref, ob_ref):
        s = g_ref[...] + r_ref[...].astype(F32)
        o_ref[...] = s
        ob_ref[...] = s.astype(BF16)

    blk = pl.BlockSpec((None, tr, C), lambda k, i, c: (k, i, 0))
    return pl.pallas_call(
        body, name=name,
        grid_spec=pltpu.PrefetchScalarGridSpec(
            num_scalar_prefetch=1, grid=(4, R // tr),
            in_specs=[pl.BlockSpec((None, tr, C), lambda k, i, c: (2 * k + c[0], i, 0)), blk],
            out_specs=[blk, blk]),
        out_shape=[jax.ShapeDtypeStruct((4, R, C), F32), jax.ShapeDtypeStruct((4, R, C), BF16)],
        compiler_params=_params(("parallel", "parallel")),
    )(cidx, g, got)


def _rs_chips(p, name):
    _, R, C = p.shape

    def body(p_ref, out_ref, send_sems, recv_sems):
        x, y, c = _mesh_pos()
        chips = [(1 - x, y), (x, 1 - y), (1 - x, 1 - y)]
        cps = []
        for j, (cx, cy) in enumerate(chips):
            cps.append(pltpu.make_async_remote_copy(
                src_ref=p_ref.at[2 * cx + cy], dst_ref=out_ref.at[j],
                send_sem=send_sems.at[j], recv_sem=recv_sems.at[j],
                device_id=(cx, cy, c), device_id_type=pl.DeviceIdType.MESH))
        for cp in cps:
            cp.start()
        for cp in cps:
            cp.wait_recv()
        for cp in cps:
            cp.wait_send()

    return pl.pallas_call(
        body, name=name,
        in_specs=[pl.BlockSpec(memory_space=pl.ANY)],
        out_specs=pl.BlockSpec(memory_space=pl.ANY),
        out_shape=jax.ShapeDtypeStruct((3, R, C), p.dtype),
        scratch_shapes=[pltpu.SemaphoreType.DMA((3,)), pltpu.SemaphoreType.DMA((3,))],
    )(p)


def _adamw_math(w, g, m, v):
    m2 = ADAM_B1 * m + (1.0 - ADAM_B1) * g
    v2 = ADAM_B2 * v + (1.0 - ADAM_B2) * (g * g)
    m_hat = m2 / (1.0 - ADAM_B1 ** ADAM_STEP)
    v_hat = v2 / (1.0 - ADAM_B2 ** ADAM_STEP)
    delta = -ADAM_LR * (m_hat / (jnp.sqrt(v_hat) + ADAM_EPS) + ADAM_WD * w)
    return delta, m2, v2


def _adamw_shard(p, got, chip_idx, w, m, v, name):
    R, C = w.shape
    tr = 256

    def body(k_ref, p_ref, r_ref, w_ref, m_ref, v_ref, g_out, d_out, m_out, v_out):
        g = ((p_ref[...] + r_ref[0].astype(F32)) + r_ref[1].astype(F32)) + r_ref[2].astype(F32)
        delta, m2, v2 = _adamw_math(w_ref[...], g, m_ref[...], v_ref[...])
        g_out[...] = g
        d_out[...] = delta
        m_out[...] = m2
        v_out[...] = v2

    row = pl.BlockSpec((tr, C), lambda i, k: (i, 0))
    return pl.pallas_call(
        body, name=name,
        grid_spec=pltpu.PrefetchScalarGridSpec(
            num_scalar_prefetch=1, grid=(R // tr,),
            in_specs=[pl.BlockSpec((None, tr, C), lambda i, k: (k[0], i, 0)),
                      pl.BlockSpec((3, tr, C), lambda i, k: (0, i, 0)), row, row, row],
            out_specs=[row] * 4),
        out_shape=[jax.ShapeDtypeStruct((R, C), F32)] * 4,
        compiler_params=_params(("parallel",)),
    )(chip_idx, p, got, w, m, v)


def _adamw_small(gathered, w, m, v, name):
    _, R, C = gathered.shape

    def body(a_ref, w_ref, m_ref, v_ref, g_out, d_out, m_out, v_out):
        g = a_ref[0]
        for k in range(1, N_DEV):
            g = g + a_ref[k]
        delta, m2, v2 = _adamw_math(w_ref[...], g, m_ref[...], v_ref[...])
        g_out[...] = g
        d_out[...] = delta
        m_out[...] = m2
        v_out[...] = v2

    return pl.pallas_call(
        body, name=name, out_shape=[jax.ShapeDtypeStruct((R, C), F32)] * 4,
    )(gathered, w, m, v)


def _rope_tables(S):
    half = ROPE_DIM // 2
    inv_freq = ROPE_THETA ** (-jnp.arange(half, dtype=F32) / half)
    ang = jnp.arange(S, dtype=jnp.int32).astype(F32)[:, None] * inv_freq[None, :]
    cos, sin = jnp.cos(ang), jnp.sin(ang)
    ones = jnp.ones((S, HEAD_DIM - ROPE_DIM), F32)
    zeros = jnp.zeros((S, HEAD_DIM - ROPE_DIM), F32)
    zh = jnp.zeros((S, half), F32)
    c = jnp.concatenate([cos, cos, ones], axis=1)
    s1 = jnp.concatenate([zh, sin, zeros], axis=1)
    s2 = jnp.concatenate([-sin, zh, zeros], axis=1)
    two = lambda t: jnp.concatenate([t, t], axis=1)
    return two(c), two(s1), two(s2)


def _chunk_transposed(a, S):
    return a.reshape(S // SB_CH, SB_CH, HEAD_PAIRS, LANES).transpose(2, 0, 3, 1)


def _flat_shards(ws):
    return jnp.concatenate([w.reshape(-1, D_MODEL) for layer in ws for w in layer], axis=0)


def kernel(x, w_qkv_0, w_o_0, ln1_g_0, ln1_b_0, w_ff1_0, w_ff2_0, ln2_g_0, ln2_b_0, w_qkv_1, w_o_1, ln1_g_1, ln1_b_1, w_ff1_1, w_ff2_1, ln2_g_1, ln2_b_1, loss_target, m_w_qkv_0, m_w_o_0, m_ln1_g_0, m_ln1_b_0, m_w_ff1_0, m_w_ff2_0, m_ln2_g_0, m_ln2_b_0, m_w_qkv_1, m_w_o_1, m_ln1_g_1, m_ln1_b_1, m_w_ff1_1, m_w_ff2_1, m_ln2_g_1, m_ln2_b_1, v_w_qkv_0, v_w_o_0, v_ln1_g_0, v_ln1_b_0, v_w_ff1_0, v_w_ff2_0, v_ln2_g_0, v_ln2_b_0, v_w_qkv_1, v_w_o_1, v_ln1_g_1, v_ln1_b_1, v_w_ff1_1, v_w_ff2_1, v_ln2_g_1, v_ln2_b_1):
    S = x.shape[1]
    x0 = x.reshape(S, D_MODEL)
    target = loss_target.reshape(S, D_MODEL)
    mats = ((w_qkv_0, w_o_0, w_ff1_0, w_ff2_0), (w_qkv_1, w_o_1, w_ff1_1, w_ff2_1))
    mats_m = ((m_w_qkv_0, m_w_o_0, m_w_ff1_0, m_w_ff2_0), (m_w_qkv_1, m_w_o_1, m_w_ff1_1, m_w_ff2_1))
    mats_v = ((v_w_qkv_0, v_w_o_0, v_w_ff1_0, v_w_ff2_0), (v_w_qkv_1, v_w_o_1, v_w_ff1_1, v_w_ff2_1))
    vecs = (ln1_g_0, ln1_b_0, ln2_g_0, ln2_b_0, ln1_g_1, ln1_b_1, ln2_g_1, ln2_b_1)
    vecs_m = (m_ln1_g_0, m_ln1_b_0, m_ln2_g_0, m_ln2_b_0, m_ln1_g_1, m_ln1_b_1, m_ln2_g_1, m_ln2_b_1)
    vecs_v = (v_ln1_g_0, v_ln1_b_0, v_ln2_g_0, v_ln2_b_0, v_ln1_g_1, v_ln1_b_1, v_ln2_g_1, v_ln2_b_1)

    w_flat = _flat_shards(mats)
    w_all = _all_gather(w_flat.astype(BF16), "ag_weights")
    layers = []
    for l in range(N_LAYERS):
        base = l * LAYER_ROWS
        r0, r1, r2, r3 = np.cumsum((0,) + SHARD_ROWS)[:4] + base
        layers.append(dict(
            qkv=w_all[:, r0:r0 + 384].reshape(N_DEV, D_MODEL, 384),
            o=w_all[:, r1:r1 + 128].reshape(D_MODEL, D_MODEL),
            ff1=w_all[:, r2:r2 + 512].reshape(N_DEV, D_MODEL, 512),
            ff2=w_all[:, r3:r3 + 512].reshape(D_FF, D_MODEL),
            g1=vecs[4 * l].reshape(1, D_MODEL), b1=vecs[4 * l + 1].reshape(1, D_MODEL),
            g2=vecs[4 * l + 2].reshape(1, D_MODEL), b2=vecs[4 * l + 3].reshape(1, D_MODEL)))

    rope = _rope_tables(S)
    tmat_later = _sb_tmat(True)
    tmat_upto = _sb_tmat(False)

    saved = []
    xin, xinb = x0, x0.astype(BF16)
    for l, W in enumerate(layers):
        sv = dict(xin=xin, xinb=xinb)
        qkv = _qkv_proj(xinb, W["qkv"], rope if l == 1 else None, Q_SCALE * LOG2E if l == 0 else Q_SCALE,
                        BF16 if l == 0 else F32, f"qkv_proj_{l}")
        sv["qkv"] = qkv
        if l == 0:
            vT3 = _chunk_transposed(qkv[:, 2 * D_MODEL:], S)
            ob, sb_tiles, sb_first = _sb_fwd(qkv, vT3, tmat_later, "sb_fwd")
            o = None
            sv.update(sb_tiles=sb_tiles, sb_first=sb_first)
        else:
            outs = [_dil_fwd(qkv, d, f"dil_fwd_{d}") for _, d in DILATED_BRANCHES]
            o, ob, lse = _dil_merge([t[0] for t in outs], [t[1] for t in outs], "dil_merge")
            sv["lse"] = lse
        sv.update(o=o, ob=ob)
        y1, x1, x1b = _mm_res_ln(ob, xin, W["o"], W["g1"], W["b1"], f"attn_out_ln_{l}")
        hpre, h = _ff1(x1b, W["ff1"], f"ff1_{l}")
        y2, x2, x2b = _mm_res_ln(h, x1, W["ff2"], W["g2"], W["b2"], f"ff2_ln_{l}")
        sv.update(y1=y1, x1=x1, x1b=x1b, hpre=hpre, h=h, y2=y2)
        saved.append(sv)
        xin, xinb = x2, x2b

    dout, loss_parts = _loss_grad(xin, target, "loss_grad")
    loss = lax.psum(jnp.sum(loss_parts), MESH_AXES)

    gmats = [None] * N_LAYERS
    gvecs = [None] * (4 * N_LAYERS)
    for l in reversed(range(N_LAYERS)):
        W, sv = layers[l], saved[l]
        dy2, dy2b, gb2, dhp = _ln_bwd(dout, sv["y2"], W["g2"], W["ff2"], sv["hpre"], f"ln2_bwd_dh_{l}")
        g_ff2 = _mm_tn(sv["h"], dy2b, 512, D_MODEL, False, f"dw_ff2_{l}")
        dx1 = _dx_blk(dy2, dhp, W["ff1"], f"dx_ff1_{l}")
        g_ff1 = _mm_tn(sv["x1b"], dhp, D_MODEL, 512, True, f"dw_ff1_{l}")
        dy1, dy1b, gb1, do = _ln_bwd(dx1, sv["y1"], W["g1"], W["o"], None, f"ln1_bwd_do_{l}")
        g_o = _mm_tn(sv["ob"], dy1b, 512, D_MODEL, False, f"dw_o_{l}")
        if l == 0:
            kT3 = _chunk_transposed(sv["qkv"][:, D_MODEL:2 * D_MODEL], S)
            dq, dk, dv = _sb_bwd(sv["qkv"], kT3, do, sv["sb_tiles"], sv["sb_first"], tmat_upto, "sb_bwd")
            dqkv = jnp.concatenate([dq, dk, dv], axis=1).astype(BF16)
        else:
            dlt = _head_sums(do, sv["o"], "head_sums")
            parts = [_dil_bwd(sv["qkv"], do, sv["lse"], dlt, d, f"dil_bwd_{d}") for _, d in DILATED_BRANCHES]
            dqkv = _dil_combine(parts, rope, "dil_combine")
        dout = _dx_blk(dy1, dqkv, W["qkv"], f"dx_qkv_{l}")
        g_qkv = _mm_tn(sv["xinb"], dqkv, D_MODEL, 384, True, f"dw_qkv_{l}")
        gmats[l] = (g_qkv.reshape(N_DEV, 384, D_MODEL), g_o.reshape(N_DEV, 128, D_MODEL),
                    g_ff1.reshape(N_DEV, 512, D_MODEL), g_ff2.reshape(N_DEV, 512, D_MODEL))
        gvecs[4 * l:4 * l + 4] = [gb1[0], gb1[1], gb2[0], gb2[1]]
    grad_x = dout.reshape(1, S, D_MODEL)

    cx, cy, cc = _mesh_pos()
    g_all = jnp.concatenate([g for layer in gmats for g in layer], axis=1)
    got_pair = _rs_pair(g_all.astype(BF16), "rs_pair")
    chip_part, chip_part_b = _pair_add(g_all, got_pair, cc.astype(jnp.int32).reshape(1), "rs_pair_add")
    got_chips = _rs_chips(chip_part_b, "rs_chips")
    chip_idx = (2 * cx + cy).astype(jnp.int32).reshape(1)
    g_sh, d_sh, m_sh, v_sh = _adamw_shard(chip_part, got_chips, chip_idx, w_flat, _flat_shards(mats_m),
                                          _flat_shards(mats_v), "adamw_mats")

    def unflat(a):
        out, pos = [], 0
        for layer in mats:
            for w in layer:
                n = w.size // D_MODEL
                out.append(a[pos:pos + n].reshape(w.shape))
                pos += n
        return out

    gv_all = _all_gather(jnp.stack(gvecs), "ag_vec_grads")
    g_v, d_v, m_v, v_v = _adamw_small(gv_all, jnp.stack(vecs), jnp.stack(vecs_m), jnp.stack(vecs_v), "adamw_vecs")

    def interleave(mat_list, vec_arr):
        out = []
        for l in range(N_LAYERS):
            qkv_, o_, ff1_, ff2_ = mat_list[4 * l:4 * l + 4]
            out += [qkv_, o_, vec_arr[4 * l], vec_arr[4 * l + 1], ff1_, ff2_, vec_arr[4 * l + 2], vec_arr[4 * l + 3]]
        return out

    return (loss, grad_x, *interleave(unflat(g_sh), g_v), *interleave(unflat(d_sh), d_v),
            *interleave(unflat(m_sh), m_v), *interleave(unflat(v_sh), v_v))
```

```python
import functools
import math

import jax
import jax.numpy as jnp
import numpy as np
from jax import lax
from jax.experimental import pallas as pl
from jax.experimental.pallas import tpu as pltpu

F32 = jnp.float32
BF16 = jnp.bfloat16

D_MODEL = 1024
N_HEADS = 16
HEAD_DIM = 64
D_FF = 4096
N_DEV = 8
N_LAYERS = 2
ROPE_THETA = 500000.0
ROPE_DIM = 16
DILATED_BRANCHES = ((128, 1), (512, 4), (2048, 16))
ALPHA = (2 * N_LAYERS) ** 0.25
LN_EPS = 1e-5
Q_SCALE = 1.0 / math.sqrt(HEAD_DIM)
LOG2E = math.log2(math.e)
LN2 = math.log(2.0)
ADAM_LR, ADAM_B1, ADAM_B2, ADAM_EPS, ADAM_WD, ADAM_STEP = 0.001, 0.9, 0.999, 1e-08, 0.01, 10

LANES = 128
HEAD_PAIRS = D_MODEL // LANES
SB_TQ = 256
SB_CH = 256
SB_STEPS = 4
SB_SAVE_SLOTS = 2 * SB_STEPS
SB_LOAD_SLOTS = 12
SB_LOAD_AHEAD = SB_LOAD_SLOTS - SB_STEPS - 1
SB_DEAD = -160.0
DIL_BLK = 128
VMEM_BIG = 56 * 2 ** 20
MESH_AXES = ("x", "y", "c")

SHARD_ROWS = (384, 128, 512, 512)
LAYER_ROWS = sum(SHARD_ROWS)
ALL_ROWS = N_LAYERS * LAYER_ROWS


def _params(sem=None, vmem=None):
    kw = {}
    if sem is not None:
        kw["dimension_semantics"] = sem
    if vmem is not None:
        kw["vmem_limit_bytes"] = vmem
    return pltpu.CompilerParams(**kw)


def _dot(a, b):
    return jnp.dot(a, b, preferred_element_type=F32)


def _dot_nt(a, b):
    return lax.dot_general(a, b, (((1,), (1,)), ((), ())), preferred_element_type=F32)


def _dot_tn(a, b):
    return lax.dot_general(a, b, (((0,), (0,)), ((), ())), preferred_element_type=F32)


def _split3(p):
    hi = p.astype(BF16)
    r1 = p - hi.astype(F32)
    mid = r1.astype(BF16)
    lo = (r1 - mid.astype(F32)).astype(BF16)
    return hi, mid, lo


def _dot3(p, e):
    hi, mid, lo = _split3(p)
    return _dot(hi, e) + _dot(mid, e) + _dot(lo, e)


def _rope_apply(a, c, s1, s2, sign):
    return a * c + sign * (pltpu.roll(a, 8, 1) * s1 + pltpu.roll(a, LANES - 8, 1) * s2)


def _qkv_proj(xb, w_blk, rope, q_mult, out_dtype, name):
    S = xb.shape[0]
    tm = 512
    n_rope = 0 if rope is None else 3

    def body(*refs):
        x_ref, w_ref = refs[:2]
        tabs = [r[...] for r in refs[2:2 + n_rope]]
        o_ref = refs[2 + n_rope]
        x = x_ref[...]
        for j in range(N_DEV):
            acc = _dot(x, w_ref[j])
            for g in range(3):
                col = j * 384 + g * LANES
                a = acc[:, g * LANES:(g + 1) * LANES]
                if n_rope and col < 2 * D_MODEL:
                    a = _rope_apply(a, *tabs, 1.0)
                if col < D_MODEL:
                    a = a * q_mult
                o_ref[:, col:col + LANES] = a.astype(out_dtype)

    tab_specs = [pl.BlockSpec((tm, LANES), lambda i: (i, 0))] * n_rope
    return pl.pallas_call(
        body, name=name, grid=(S // tm,),
        in_specs=[pl.BlockSpec((tm, D_MODEL), lambda i: (i, 0)),
                  pl.BlockSpec((N_DEV, D_MODEL, 384), lambda i: (0, 0, 0))] + tab_specs,
        out_specs=pl.BlockSpec((tm, 3 * D_MODEL), lambda i: (i, 0)),
        out_shape=jax.ShapeDtypeStruct((S, 3 * D_MODEL), out_dtype),
        compiler_params=_params(("parallel",), VMEM_BIG),
    )(xb, w_blk, *(rope or ()))


def _layer_norm_rows(y, g, b):
    mu = jnp.mean(y, axis=-1, keepdims=True)
    yc = y - mu
    var = jnp.mean(yc * yc, axis=-1, keepdims=True)
    return yc * lax.rsqrt(var + LN_EPS) * g + b


def _mm_res_ln(a, xres, w, g, b, name):
    S, K = a.shape
    tm = 512 if K <= 1024 else 256

    def body(a_ref, x_ref, w_ref, g_ref, b_ref, y_ref, xn_ref, xb_ref):
        y = ALPHA * x_ref[...] + _dot(a_ref[...], w_ref[...])
        xn = _layer_norm_rows(y, g_ref[...], b_ref[...])
        y_ref[...] = y
        xn_ref[...] = xn
        xb_ref[...] = xn.astype(BF16)

    row = lambda i: (i, 0)
    fix = lambda i: (0, 0)
    return pl.pallas_call(
        body, name=name, grid=(S // tm,),
        in_specs=[pl.BlockSpec((tm, K), row), pl.BlockSpec((tm, D_MODEL), row),
                  pl.BlockSpec((K, D_MODEL), fix), pl.BlockSpec((1, D_MODEL), fix),
                  pl.BlockSpec((1, D_MODEL), fix)],
        out_specs=[pl.BlockSpec((tm, D_MODEL), row)] * 3,
        out_shape=[jax.ShapeDtypeStruct((S, D_MODEL), F32), jax.ShapeDtypeStruct((S, D_MODEL), F32),
                   jax.ShapeDtypeStruct((S, D_MODEL), BF16)],
        compiler_params=_params(("parallel",), VMEM_BIG),
    )(a, xres, w, g, b)


def _ff1(xb, w_blk, name):
    S = xb.shape[0]
    tm = 256

    def body(x_ref, w_ref, hp_ref, h_ref):
        x = x_ref[...]
        for j in range(N_DEV):
            acc = _dot(x, w_ref[j])
            r = jnp.maximum(acc, 0.0)
            hp_ref[:, j * 512:(j + 1) * 512] = acc
            h_ref[:, j * 512:(j + 1) * 512] = (r * r).astype(BF16)

    return pl.pallas_call(
        body, name=name, grid=(S // tm,),
        in_specs=[pl.BlockSpec((tm, D_MODEL), lambda i: (i, 0)),
                  pl.BlockSpec((N_DEV, D_MODEL, 512), lambda i: (0, 0, 0))],
        out_specs=[pl.BlockSpec((tm, D_FF), lambda i: (i, 0))] * 2,
        out_shape=[jax.ShapeDtypeStruct((S, D_FF), F32), jax.ShapeDtypeStruct((S, D_FF), BF16)],
        compiler_params=_params(("parallel",), VMEM_BIG),
    )(xb, w_blk)


def _loss_grad(y, target, name):
    S = y.shape[0]
    tm = 512

    def body(y_ref, t_ref, dy_ref, l_ref):
        @pl.when(pl.program_id(0) == 0)
        def _():
            l_ref[...] = jnp.zeros_like(l_ref)

        err = y_ref[...] - t_ref[...]
        dy_ref[...] = err * (1.0 / D_MODEL)
        sq = err * err
        rows = sq[0:8]
        for r in range(1, tm // 8):
            rows = rows + sq[r * 8:(r + 1) * 8]
        acc = rows[:, 0:LANES]
        for g in range(1, D_MODEL // LANES):
            acc = acc + rows[:, g * LANES:(g + 1) * LANES]
        l_ref[...] += acc * (0.5 / D_MODEL)

    return pl.pallas_call(
        body, name=name, grid=(S // tm,),
        in_specs=[pl.BlockSpec((tm, D_MODEL), lambda i: (i, 0))] * 2,
        out_specs=[pl.BlockSpec((tm, D_MODEL), lambda i: (i, 0)), pl.BlockSpec((8, LANES), lambda i: (0, 0))],
        out_shape=[jax.ShapeDtypeStruct((S, D_MODEL), F32), jax.ShapeDtypeStruct((8, LANES), F32)],
        compiler_params=_params(("arbitrary",)),
    )(y, target)


def _ln_bwd(dout, y, g, w, hpre, name):
    S = y.shape[0]
    N = w.shape[0]
    tm = 512 if hpre is None else 256
    tn = 512
    steps = S // tm

    def body(*refs):
        d_ref, y_ref, g_ref, w_ref = refs[:4]
        hp_ref = None if hpre is None else refs[4]
        dy_ref, dyb_ref, gb_ref, o_ref, acc_g, acc_b = refs[-6:]
        i = pl.program_id(0)

        @pl.when(i == 0)
        def _():
            acc_g[...] = jnp.zeros_like(acc_g)
            acc_b[...] = jnp.zeros_like(acc_b)

        d = d_ref[...]
        yv = y_ref[...]
        mu = jnp.mean(yv, axis=-1, keepdims=True)
        yc = yv - mu
        var = jnp.mean(yc * yc, axis=-1, keepdims=True)
        rstd = lax.rsqrt(var + LN_EPS)
        xhat = yc * rstd
        dxh = d * g_ref[...]
        m1 = jnp.mean(dxh, axis=-1, keepdims=True)
        m2 = jnp.mean(dxh * xhat, axis=-1, keepdims=True)
        dy = rstd * (dxh - m1 - xhat * m2)
        dyb = dy.astype(BF16)
        dy_ref[...] = dy
        dyb_ref[...] = dyb
        for c in range(0, N, tn):
            prod = _dot_nt(dyb, w_ref[c:c + tn, :])
            if hpre is None:
                o_ref[:, c:c + tn] = prod
            else:
                o_ref[:, c:c + tn] = (prod * (2.0 * jnp.maximum(hp_ref[:, c:c + tn], 0.0))).astype(BF16)
        pg = d * xhat
        sg = pg[0:8]
        sb = d[0:8]
        for r in range(1, tm // 8):
            sg = sg + pg[r * 8:(r + 1) * 8]
            sb = sb + d[r * 8:(r + 1) * 8]
        acc_g[...] += sg
        acc_b[...] += sb

        @pl.when(i == steps - 1)
        def _():
            gb_ref[0:1, :] = jnp.sum(acc_g[...], axis=0, keepdims=True)
            gb_ref[1:2, :] = jnp.sum(acc_b[...], axis=0, keepdims=True)

    row = lambda i: (i, 0)
    fix = lambda i: (0, 0)
    extra_in = [] if hpre is None else [pl.BlockSpec((tm, N), row)]
    extra_arg = [] if hpre is None else [hpre]
    return pl.pallas_call(
        body, name=name, grid=(steps,),
        in_specs=[pl.BlockSpec((tm, D_MODEL), row), pl.BlockSpec((tm, D_MODEL), row), pl.BlockSpec((1, D_MODEL), fix),
                  pl.BlockSpec((N, D_MODEL), fix)] + extra_in,
        out_specs=[pl.BlockSpec((tm, D_MODEL), row), pl.BlockSpec((tm, D_MODEL), row), pl.BlockSpec((2, D_MODEL), fix),
                   pl.BlockSpec((tm, N), row)],
        out_shape=[jax.ShapeDtypeStruct((S, D_MODEL), F32), jax.ShapeDtypeStruct((S, D_MODEL), BF16),
                   jax.ShapeDtypeStruct((2, D_MODEL), F32),
                   jax.ShapeDtypeStruct((S, N), F32 if hpre is None else BF16)],
        scratch_shapes=[pltpu.VMEM((8, D_MODEL), F32), pltpu.VMEM((8, D_MODEL), F32)],
        compiler_params=_params(("arbitrary",), VMEM_BIG),
    )(dout, y, g, w, *extra_arg)


def _dx_blk(dres, dz, w_blk, name):
    S, N = dz.shape
    bw = w_blk.shape[2]
    tm = 256

    def body(r_ref, z_ref, w_ref, o_ref):
        acc = ALPHA * r_ref[...]
        for j in range(N_DEV):
            acc = acc + _dot_nt(z_ref[:, j * bw:(j + 1) * bw], w_ref[j])
        o_ref[...] = acc

    return pl.pallas_call(
        body, name=name, grid=(S // tm,),
        in_specs=[pl.BlockSpec((tm, D_MODEL), lambda i: (i, 0)), pl.BlockSpec((tm, N), lambda i: (i, 0)),
                  pl.BlockSpec((N_DEV, D_MODEL, bw), lambda i: (0, 0, 0))],
        out_specs=pl.BlockSpec((tm, D_MODEL), lambda i: (i, 0)),
        out_shape=jax.ShapeDtypeStruct((S, D_MODEL), F32),
        compiler_params=_params(("parallel",), VMEM_BIG),
    )(dres, dz, w_blk)


def _mm_tn(a, b, ta, tb, blocked, name):
    S, Ka = a.shape
    Nb = b.shape[1]
    ts = 2048

    def body(a_ref, b_ref, o_ref):
        @pl.when(pl.program_id(2) == 0)
        def _():
            o_ref[...] = jnp.zeros_like(o_ref)

        o_ref[...] += _dot_tn(a_ref[...], b_ref[...])

    if blocked:
        out_spec = pl.BlockSpec((None, ta, tb), lambda i, j, s: (j, i, 0))
        out_shape = jax.ShapeDtypeStruct((Nb // tb, Ka, tb), F32)
    else:
        out_spec = pl.BlockSpec((ta, tb), lambda i, j, s: (i, j))
        out_shape = jax.ShapeDtypeStruct((Ka, Nb), F32)
    return pl.pallas_call(
        body, name=name, grid=(Ka // ta, Nb // tb, S // ts),
        in_specs=[pl.BlockSpec((ts, ta), lambda i, j, s: (s, i)), pl.BlockSpec((ts, tb), lambda i, j, s: (s, j))],
        out_specs=out_spec, out_shape=out_shape,
        compiler_params=_params(("parallel", "parallel", "arbitrary"), VMEM_BIG),
    )(a, b)


def _head_sums(do, o, name):
    S = do.shape[0]
    tm = 512
    sel = (np.arange(D_MODEL)[:, None] // HEAD_DIM == np.arange(LANES)[None, :]).astype(np.float32)

    def body(d_ref, o_ref, e_ref, out_ref):
        out_ref[...] = _dot3(d_ref[...] * o_ref[...], e_ref[...])

    return pl.pallas_call(
        body, name=name, grid=(S // tm,),
        in_specs=[pl.BlockSpec((tm, D_MODEL), lambda i: (i, 0))] * 2 + [pl.BlockSpec((D_MODEL, LANES), lambda i: (0, 0))],
        out_specs=pl.BlockSpec((tm, LANES), lambda i: (i, 0)),
        out_shape=jax.ShapeDtypeStruct((S, LANES), F32),
        compiler_params=_params(("parallel",)),
    )(do, o, jnp.asarray(sel, BF16))


def _sb_tmat(later):
    r = np.arange(SB_CH)
    t = (r[None, :] > r[:, None]) if later else (r[None, :] <= r[:, None])
    return jnp.asarray(np.concatenate([t.astype(np.float32), np.ones((8, SB_CH), np.float32)], axis=0), BF16)


def _sb_gates(z2):
    neg_abs = lax.bitcast_convert_type(lax.bitcast_convert_type(z2, jnp.uint32) | jnp.uint32(0x80000000), F32)
    l1 = jnp.log2(1.0 + jnp.exp2(neg_abs))
    a = jnp.minimum(z2, 0.0) - l1
    return a, a - z2


def _head_masks(x2):
    lane = lax.broadcasted_iota(jnp.int32, x2.shape, 1)
    zero = jnp.zeros_like(x2)
    return jnp.where(lane < HEAD_DIM, x2, zero), jnp.where(lane >= HEAD_DIM, x2, zero)


def _sb_fwd(qkv, vT3, tmat, name):
    S = qkv.shape[0]
    nq = S // SB_TQ
    nch = S // SB_CH
    ns = SB_SAVE_SLOTS

    def body(q_ref, k_ref, vT_ref, t_ref, o_ref, ws_hbm, first_ref, z_scr, a_scr, cum_scr, oT_scr, stage, sems,
             pending):
        hp = pl.program_id(0)
        i = pl.program_id(1)
        base = (i * (i + 1)) // 2
        qm = _head_masks(q_ref[...])

        def save(src, sem, c):
            return pltpu.make_async_copy(src, ws_hbm.at[hp, base + c], sem)

        causal = (lax.broadcasted_iota(jnp.int32, (SB_CH, SB_TQ), 0)
                  < lax.broadcasted_iota(jnp.int32, (SB_CH, SB_TQ), 1))

        def head_rows(vTc, h):
            return vTc[h * HEAD_DIM:(h + 1) * HEAD_DIM, :]

        @pl.when(jnp.logical_and(hp == 0, i == 0))
        def _():
            z_scr[...] = jnp.zeros_like(z_scr)
            a_scr[...] = jnp.zeros_like(a_scr)
            cum_scr[...] = jnp.zeros_like(cum_scr)

        oT_scr[...] = jnp.zeros_like(oT_scr)

        def c_valid(t):
            return jnp.logical_and(t >= 2, t - 2 <= i)

        def c_chunk(t):
            return jnp.clip(i + 2 - t, 0, nch - 1)

        def step(t, p, slot, R, own_b, own_c, has_a, has_b, has_c, t_first_c):
            cA = jnp.maximum(i - t, 0)
            kA = k_ref[pl.ds(pl.multiple_of(cA * SB_CH, SB_CH), SB_CH), :]
            valid = jnp.logical_and(c_valid(t), t >= t_first_c)
            vC = vT_ref[c_chunk(t)]
            out = []
            for h in range(2):
                if has_a:
                    z_scr[p, h] = _dot_nt(kA, qm[h])
                if has_b:
                    a, lf = _sb_gates(z_scr[1 - p, h])
                    if own_b:
                        lf = jnp.where(causal, lf, 0.0)
                    a_scr[1 - p, h] = a
                    cum_scr[1 - p, h] = _dot(t_ref[...], lf.astype(BF16))
                if not has_c:
                    out.append(R[h])
                    continue
                a_c = a_scr[p, h]
                w = jnp.exp2(a_c + cum_scr[p, h, :SB_CH, :] + R[h])
                if own_c:
                    w = jnp.where(causal, w, 0.0)
                wb = w.astype(BF16)
                stage[slot, 2 * h] = wb
                stage[slot, 2 * h + 1] = a_c.astype(BF16)
                oT_scr[h] += jnp.where(valid, _dot(head_rows(vC, h), wb), 0.0)
                out.append(R[h] + jnp.where(valid, cum_scr[p, h, SB_CH:SB_CH + 1, :], 0.0))
            return tuple(out)

        step_no = hp * nq + i

        @pl.when(step_no == 0)
        def _():
            for s in range(ns):
                pending[s] = 0

        def settle(slot):
            @pl.when(pending[slot] == 1)
            def _():
                save(stage.at[slot], sems.at[slot], 0).wait()
                pending[slot] = 0

        def trip(tt, R, first):
            half = lax.rem(step_no + tt, 2) * SB_STEPS
            for j in range(SB_STEPS):
                settle(half + j)
            for j in range(SB_STEPS):
                if first:
                    R = step(j, j % 2, half + j, R, j == 1, j == 2, j < 2, j in (1, 2), j >= 2, 2)
                else:
                    R = step(SB_STEPS * (tt - 1) + j + 2, j % 2, half + j, R, False, False, True, True, True, 4)
            for j in range(SB_STEPS):
                t = j if first else SB_STEPS * (tt - 1) + j + 2

                @pl.when(jnp.logical_and(c_valid(t), t >= (2 if first else 4)))
                def _():
                    save(stage.at[half + j], sems.at[half + j], c_chunk(t)).start()
                    pending[half + j] = 1

            return R

        z1 = jnp.zeros((1, SB_TQ), F32)
        trips = 1 + jnp.where(i >= 2, (i + 1 + SB_STEPS - 1) // SB_STEPS, 0)

        def alive(carry):
            tt, R = carry
            return jnp.logical_and(tt < trips, jnp.max(jnp.maximum(R[0], R[1])) > SB_DEAD)

        trips, _ = lax.while_loop(alive, lambda carry: (carry[0] + 1, trip(carry[0], carry[1], False)),
                                  (jnp.int32(1), trip(0, (z1, z1), True)))
        first_ref[hp, i] = jnp.maximum(jnp.where(trips == 1, i - 1, i - (SB_STEPS * (trips - 1) - 1)), 0)

        @pl.when(step_no == HEAD_PAIRS * nq - 1)
        def _():
            for s in range(ns):
                settle(s)

        o_ref[...] = jnp.concatenate([oT_scr[0], oT_scr[1]], axis=0).T.astype(BF16)

    ntile = nq * (nq + 1) // 2
    return pl.pallas_call(
        body, name=name, grid=(HEAD_PAIRS, nq),
        in_specs=[pl.BlockSpec((SB_TQ, LANES), lambda hp, i: (i, hp)),
                  pl.BlockSpec((S, LANES), lambda hp, i: (0, HEAD_PAIRS + hp)),
                  pl.BlockSpec((None, nch, LANES, SB_CH), lambda hp, i: (hp, 0, 0, 0)),
                  pl.BlockSpec((SB_CH + 8, SB_CH), lambda hp, i: (0, 0))],
        out_specs=[pl.BlockSpec((SB_TQ, LANES), lambda hp, i: (i, hp)), pl.BlockSpec(memory_space=pl.ANY),
                   pl.BlockSpec(memory_space=pltpu.SMEM)],
        out_shape=[jax.ShapeDtypeStruct((S, D_MODEL), BF16),
                   jax.ShapeDtypeStruct((HEAD_PAIRS, ntile, 4, SB_CH, SB_TQ), BF16),
                   jax.ShapeDtypeStruct((HEAD_PAIRS, nq), jnp.int32)],
        scratch_shapes=[pltpu.VMEM((2, 2, SB_CH, SB_TQ), F32), pltpu.VMEM((2, 2, SB_CH, SB_TQ), F32),
                        pltpu.VMEM((2, 2, SB_CH + 8, SB_TQ), F32), pltpu.VMEM((2, HEAD_DIM, SB_TQ), F32),
                        pltpu.VMEM((ns, 4, SB_CH, SB_TQ), BF16), pltpu.SemaphoreType.DMA((ns,)),
                        pltpu.SMEM((ns,), jnp.int32)],
        compiler_params=_params(("arbitrary", "arbitrary"), VMEM_BIG),
    )(qkv, qkv, vT3, tmat)


def _sb_bwd(qkv, kT3, do, ws, first, tmat_g, name):
    S = qkv.shape[0]
    nq = S // SB_TQ
    nch = S // SB_CH
    nl = SB_LOAD_SLOTS
    ahead = SB_LOAD_AHEAD

    def body(first_ref, q_ref, do_ref, v_ref, kT_ref, tg_ref, ws_hbm, dq_ref, dk_hbm, dv_hbm, dk_acc, dv_acc, sems,
             dwv_scr, g_scr, sig_scr, cumg_scr, dqT_scr, ring, ring_sems):
        hp = pl.program_id(0)
        i = pl.program_id(1)
        c0 = first_ref[hp, i]
        n = i - c0
        base = (i * (i + 1)) // 2 + c0

        @pl.when(i == 0)
        def _():
            dk_acc[...] = jnp.zeros_like(dk_acc)
            dv_acc[...] = jnp.zeros_like(dv_acc)

        @pl.when(jnp.logical_and(hp == 0, i == 0))
        def _():
            for scr in (dwv_scr, g_scr, sig_scr, cumg_scr, ring):
                scr[...] = jnp.zeros_like(scr)

        step_no = hp * nq + i
        parity = lax.rem(step_no, 2)

        def slot_of(u, par):
            return jnp.where(u < 2, nl + 1 + 2 * par + u, lax.rem(u, nl))

        def copy_in(hp_, tile, u, par):
            sem = jnp.where(u < 2, nl + 2 * par + u, lax.rem(u, nl))
            return pltpu.make_async_copy(ws_hbm.at[hp_, tile + u], ring.at[slot_of(u, par)], ring_sems.at[sem])

        def load(u):
            return copy_in(hp, base, u, parity)

        for u in range(ahead):
            @pl.when(jnp.logical_and(u <= n, jnp.logical_or(u >= 2, step_no == 0)))
            def _():
                load(u).start()

        nxt = jnp.minimum(step_no + 1, HEAD_PAIRS * nq - 1)
        hp_n, i_n = nxt // nq, lax.rem(nxt, nq)
        c0_n = first_ref[hp_n, i_n]
        for u in range(2):
            @pl.when(jnp.logical_and(step_no + 1 < HEAD_PAIRS * nq, u <= i_n - c0_n))
            def _():
                copy_in(hp_n, (i_n * (i_n + 1)) // 2 + c0_n, u, 1 - parity).start()

        dqT_scr[...] = jnp.zeros_like(dqT_scr)
        qm = _head_masks(q_ref[...])
        dom = _head_masks(do_ref[...].astype(BF16))
        causal = (lax.broadcasted_iota(jnp.int32, (SB_CH, SB_TQ), 0)
                  < lax.broadcasted_iota(jnp.int32, (SB_CH, SB_TQ), 1))

        def rows_of(c):
            return pl.ds(pl.multiple_of(c * SB_CH, SB_CH), SB_CH)

        def head_rows(kTc, h):
            return kTc[h * HEAD_DIM:(h + 1) * HEAD_DIM, :]

        def step(t, p, Gs, has_b=True, has_c=True):
            q = 1 - p
            valid_b = jnp.logical_and(t >= 1, t - 1 <= n)
            valid_c = jnp.logical_and(t >= 2, t - 2 <= n)
            c_b = c0 + jnp.clip(t - 1, 0, n)
            c_c = c0 + jnp.clip(t - 2, 0, n)
            slot = jnp.where(valid_b, slot_of(jnp.maximum(t - 1, 0), parity), nl)
            vA = v_ref[rows_of(c0 + jnp.minimum(t, n)), :]
            kTc = kT_ref[c_c]
            keep = jnp.logical_or(causal, t - 2 != n)
            out = []
            for h in range(2):
                dwv_scr[p, h] = _dot_nt(vA, dom[h])

                if has_b:
                    wb = ring[slot, 2 * h]
                    g = wb.astype(F32) * dwv_scr[q, h]
                    g_scr[q, h] = g
                    sig_scr[q, h] = jnp.exp2(ring[slot, 2 * h + 1].astype(F32))
                    cumg_scr[q, h] = _dot(tg_ref[...], g.astype(BF16))
                    dv_h = _dot(wb, dom[h])
                    dv_c = dv_h if h == 0 else dv_c + dv_h

                if not has_c:
                    out.append(Gs[h])
                    continue
                dz = g_scr[p, h] - sig_scr[p, h] * (Gs[h] + cumg_scr[p, h, :SB_CH, :])
                dzb = jnp.where(keep, dz, 0.0).astype(BF16)
                dk_h = _dot(dzb, qm[h])
                dqT_scr[h] += jnp.where(valid_c, _dot(head_rows(kTc, h), dzb), 0.0)
                out.append(Gs[h] + jnp.where(valid_c, cumg_scr[p, h, SB_CH:SB_CH + 1, :], 0.0))
                dk_c = dk_h if h == 0 else dk_c + dk_h
            if has_b:
                dv_acc[rows_of(c_b), :] += jnp.where(valid_b, dv_c, 0.0)
            if has_c:
                dk_acc[rows_of(c_c), :] += jnp.where(valid_c, dk_c, 0.0)
            return tuple(out)

        def trip(tt, Gs, first):
            for j in range(SB_STEPS):
                t = SB_STEPS * tt + j

                @pl.when(jnp.logical_and(t >= 1, t - 1 <= n))
                def _():
                    load(t - 1).wait()

            for j in range(SB_STEPS):
                t = SB_STEPS * tt + j

                @pl.when(t + ahead <= n)
                def _():
                    load(t + ahead).start()

            for j in range(SB_STEPS):
                Gs = step(SB_STEPS * tt + j, j % 2, Gs, not (first and j == 0), not (first and j < 2))
            return Gs

        z1 = jnp.zeros((1, SB_TQ), F32)
        lax.fori_loop(1, (n + 3 + SB_STEPS - 1) // SB_STEPS, lambda tt, Gs: trip(tt, Gs, False),
                      trip(0, (z1, z1), True))
        dq_ref[...] = jnp.concatenate([dqT_scr[0], dqT_scr[1]], axis=0).T * Q_SCALE

        @pl.when(i == nq - 1)
        def _():
            dk_acc[...] = dk_acc[...] * LN2
            cols = pl.ds(pl.multiple_of(hp * LANES, LANES), LANES)
            ck = pltpu.make_async_copy(dk_acc, dk_hbm.at[:, cols], sems.at[0])
            cv = pltpu.make_async_copy(dv_acc, dv_hbm.at[:, cols], sems.at[1])
            ck.start()
            cv.start()
            ck.wait()
            cv.wait()

    blk = lambda hp, i, first: (i, hp)
    return pl.pallas_call(
        body, name=name,
        grid_spec=pltpu.PrefetchScalarGridSpec(
            num_scalar_prefetch=1, grid=(HEAD_PAIRS, nq),
            in_specs=[pl.BlockSpec((SB_TQ, LANES), blk),
                      pl.BlockSpec((SB_TQ, LANES), blk),
                      pl.BlockSpec((S, LANES), lambda hp, i, first: (0, 2 * HEAD_PAIRS + hp)),
                      pl.BlockSpec((None, nch, LANES, SB_CH), lambda hp, i, first: (hp, 0, 0, 0)),
                      pl.BlockSpec((SB_CH + 8, SB_CH), lambda hp, i, first: (0, 0)),
                      pl.BlockSpec(memory_space=pl.ANY)],
            out_specs=[pl.BlockSpec((SB_TQ, LANES), blk), pl.BlockSpec(memory_space=pl.ANY),
                       pl.BlockSpec(memory_space=pl.ANY)],
            scratch_shapes=[pltpu.VMEM((S, LANES), F32), pltpu.VMEM((S, LANES), F32),
                            pltpu.SemaphoreType.DMA((2,))]
            + [pltpu.VMEM((2, 2, SB_CH, SB_TQ), F32)] * 3
            + [pltpu.VMEM((2, 2, SB_CH + 8, SB_TQ), F32), pltpu.VMEM((2, HEAD_DIM, SB_TQ), F32)]
            + [pltpu.VMEM((nl + 5, 4, SB_CH, SB_TQ), BF16), pltpu.SemaphoreType.DMA((nl + 4,))]),
        out_shape=[jax.ShapeDtypeStruct((S, D_MODEL), F32)] * 3,
        compiler_params=_params(("arbitrary", "arbitrary"), VMEM_BIG),
    )(first, qkv, do, qkv, kT3, tmat_g, ws)


def _dil_valid(first):
    qi = lax.broadcasted_iota(jnp.int32, (DIL_BLK, 2 * DIL_BLK), 0)
    kj = lax.broadcasted_iota(jnp.int32, (DIL_BLK, 2 * DIL_BLK), 1)
    dist = DIL_BLK + qi - kj
    return (dist >= 0) & (dist <= DIL_BLK) & (jnp.logical_not(first) | (kj >= DIL_BLK))


def _lane_pick(tile, idx):
    lane = lax.broadcasted_iota(jnp.int32, tile.shape, 1)
    return jnp.sum(jnp.where(lane == idx, tile, 0.0), axis=-1, keepdims=True)


class _DilPlan:
    def __init__(self, S, d):
        self.d = d
        self.span = DIL_BLK * d
        self.groups = max(1, 1024 // self.span)
        self.rows = self.span * self.groups
        self.steps = S // self.rows

    def cur(self, col0):
        return pl.BlockSpec((self.rows, LANES), lambda n, hp: (n, col0 + hp))

    def prev(self, col0):
        g = self.groups
        return pl.BlockSpec((self.span, LANES), lambda n, hp: (jnp.maximum(n * g - 1, 0), col0 + hp))

    def shared(self):
        return pl.BlockSpec((self.rows, LANES), lambda n, hp: (n, 0))

    def units(self, fn):
        n = pl.program_id(0)
        batch = 8
        if self.d * self.groups <= batch:
            for g in range(self.groups):
                for r in range(self.d):
                    fn(g, r, jnp.logical_and(n == 0, g == 0))
        else:
            assert self.groups == 1 and self.d % batch == 0

            def body(rb, carry):
                for rr in range(batch):
                    fn(0, rb * batch + rr, n == 0)
                return carry

            lax.fori_loop(0, self.d // batch, body, 0)

    def rows_of(self, g, r):
        return pl.ds(g * self.span + r, DIL_BLK, stride=self.d)

    def keys(self, cur_ref, prev_ref, g, r):
        before = prev_ref[pl.ds(r, DIL_BLK, stride=self.d), :] if g == 0 else cur_ref[self.rows_of(g - 1, r), :]
        return jnp.concatenate([before, cur_ref[self.rows_of(g, r), :]], axis=0).astype(BF16)


def _dil_fwd(qkv, d, name):
    S = qkv.shape[0]
    plan = _DilPlan(S, d)

    def body(q_ref, kc_ref, kp_ref, vc_ref, vp_ref, o_ref, lse_ref):
        hp = pl.program_id(1)

        @pl.when(hp == 0)
        def _():
            lse_ref[...] = jnp.zeros_like(lse_ref)

        lane = lax.broadcasted_iota(jnp.int32, (DIL_BLK, LANES), 1)

        def unit(g, r, first):
            valid = _dil_valid(first)
            rows = plan.rows_of(g, r)
            qm = _head_masks(q_ref[rows, :].astype(BF16))
            kk = plan.keys(kc_ref, kp_ref, g, r)
            vm = _head_masks(plan.keys(vc_ref, vp_ref, g, r))
            lse_t = lse_ref[rows, :]
            o2 = None
            for h in range(2):
                s = jnp.where(valid, _dot_nt(qm[h], kk), -1e30)
                m = jnp.max(s, axis=-1, keepdims=True)
                p = jnp.exp(s - m)
                den = jnp.sum(p, axis=-1, keepdims=True)
                oh = _dot(p.astype(BF16), vm[h]) / den
                o2 = oh if o2 is None else o2 + oh
                lse_t = jnp.where(lane == 2 * hp + h, m + jnp.log(den), lse_t)
            o_ref[rows, :] = o2
            lse_ref[rows, :] = lse_t

        plan.units(unit)

    return pl.pallas_call(
        body, name=name, grid=(plan.steps, HEAD_PAIRS),
        in_specs=[plan.cur(0), plan.cur(HEAD_PAIRS), plan.prev(HEAD_PAIRS), plan.cur(2 * HEAD_PAIRS),
                  plan.prev(2 * HEAD_PAIRS)],
        out_specs=[plan.cur(0), plan.shared()],
        out_shape=[jax.ShapeDtypeStruct((S, D_MODEL), F32), jax.ShapeDtypeStruct((S, LANES), F32)],
        compiler_params=_params(("parallel", "arbitrary")),
    )(qkv, qkv, qkv, qkv, qkv)


def _head_expand():
    return jnp.asarray((np.arange(LANES)[:, None] == np.arange(D_MODEL)[None, :] // HEAD_DIM).astype(np.float32), BF16)


def _dil_merge(os_, lses, name):
    S = os_[0].shape[0]
    tm = 256
    nbr = len(os_)

    def body(*refs):
        o_refs, l_refs, e_ref = refs[:nbr], refs[nbr:2 * nbr], refs[2 * nbr]
        out_ref, outb_ref, lse_ref = refs[2 * nbr + 1:]
        ls = [r[...] for r in l_refs]
        m = ls[0]
        for l in ls[1:]:
            m = jnp.maximum(m, l)
        tot = jnp.exp(ls[0] - m)
        for l in ls[1:]:
            tot = tot + jnp.exp(l - m)
        lse = m + jnp.log(tot)
        acc = None
        for o_r, l in zip(o_refs, ls):
            wt = _dot3(jnp.exp(l - lse), e_ref[...])
            term = wt * o_r[...]
            acc = term if acc is None else acc + term
        out_ref[...] = acc
        outb_ref[...] = acc.astype(BF16)
        lse_ref[...] = lse

    row = lambda i: (i, 0)
    return pl.pallas_call(
        body, name=name, grid=(S // tm,),
        in_specs=[pl.BlockSpec((tm, D_MODEL), row)] * nbr + [pl.BlockSpec((tm, LANES), row)] * nbr
        + [pl.BlockSpec((LANES, D_MODEL), lambda i: (0, 0))],
        out_specs=[pl.BlockSpec((tm, D_MODEL), row), pl.BlockSpec((tm, D_MODEL), row), pl.BlockSpec((tm, LANES), row)],
        out_shape=[jax.ShapeDtypeStruct((S, D_MODEL), F32), jax.ShapeDtypeStruct((S, D_MODEL), BF16),
                   jax.ShapeDtypeStruct((S, LANES), F32)],
        compiler_params=_params(("parallel",)),
    )(*os_, *lses, _head_expand())


def _dil_bwd(qkv, do, lse, dlt, d, name):
    S = qkv.shape[0]
    plan = _DilPlan(S, d)

    def body(q_ref, kc_ref, kp_ref, vc_ref, vp_ref, do_ref, lse_ref, dl_ref,
             dq_ref, dka_ref, dkb_ref, dva_ref, dvb_ref):
        hp = pl.program_id(1)

        def unit(g, r, first):
            valid = _dil_valid(first)
            rows = plan.rows_of(g, r)
            qm = _head_masks(q_ref[rows, :].astype(BF16))
            dom = _head_masks(do_ref[rows, :].astype(BF16))
            kk = plan.keys(kc_ref, kp_ref, g, r)
            vv = plan.keys(vc_ref, vp_ref, g, r)
            km = _head_masks(kk)
            lse_t = lse_ref[rows, :]
            dl_t = dl_ref[rows, :]
            dq2 = dkk = dvv = None
            for h in range(2):
                s = _dot_nt(qm[h], kk)
                p = jnp.where(valid, jnp.exp(s - _lane_pick(lse_t, 2 * hp + h)), 0.0)
                ds = (p * (_dot_nt(dom[h], vv) - _lane_pick(dl_t, 2 * hp + h))).astype(BF16)
                t_q = _dot(ds, km[h])
                t_k = _dot_tn(ds, qm[h])
                t_v = _dot_tn(p.astype(BF16), dom[h])
                dq2 = t_q if dq2 is None else dq2 + t_q
                dkk = t_k if dkk is None else dkk + t_k
                dvv = t_v if dvv is None else dvv + t_v
            dq_ref[rows, :] = dq2
            dkb_ref[rows, :] = dkk[:DIL_BLK]
            dka_ref[rows, :] = dkk[DIL_BLK:]
            dvb_ref[rows, :] = dvv[:DIL_BLK]
            dva_ref[rows, :] = dvv[DIL_BLK:]

        plan.units(unit)

    return pl.pallas_call(
        body, name=name, grid=(plan.steps, HEAD_PAIRS),
        in_specs=[plan.cur(0), plan.cur(HEAD_PAIRS), plan.prev(HEAD_PAIRS), plan.cur(2 * HEAD_PAIRS),
                  plan.prev(2 * HEAD_PAIRS), plan.cur(0), plan.shared(), plan.shared()],
        out_specs=[plan.cur(0)] * 5,
        out_shape=[jax.ShapeDtypeStruct((S, D_MODEL), F32)] * 5,
        compiler_params=_params(("parallel", "arbitrary"), VMEM_BIG),
    )(qkv, qkv, qkv, qkv, qkv, do, lse, dlt)


def _dil_combine(parts, rope, name):
    S = parts[0][0].shape[0]
    tm = DIL_BLK
    nblk = S // tm
    dils = [d for _, d in DILATED_BRANCHES]

    def body(*refs):
        ins = refs[:5 * len(dils)]
        c_ref, s1_ref, s2_ref, o_ref = refs[5 * len(dils):]
        i = pl.program_id(0)
        tabs = (c_ref[...], s1_ref[...], s2_ref[...])
        dq = dk = dv = None
        for b, d in enumerate(dils):
            dq_r, dka_r, dkb_r, dva_r, dvb_r = ins[5 * b:5 * b + 5]
            live = (i + d < nblk).astype(F32)
            tq = dq_r[...]
            tk = dka_r[...] + live * dkb_r[...]
            tv = dva_r[...] + live * dvb_r[...]
            dq = tq if dq is None else dq + tq
            dk = tk if dk is None else dk + tk
            dv = tv if dv is None else dv + tv
        dq = dq * Q_SCALE
        for g in range(HEAD_PAIRS):
            cols = slice(g * LANES, (g + 1) * LANES)
            o_ref[:, g * LANES:(g + 1) * LANES] = _rope_apply(dq[:, cols], *tabs, -1.0).astype(BF16)
            o_ref[:, D_MODEL + g * LANES:D_MODEL + (g + 1) * LANES] = _rope_apply(dk[:, cols], *tabs, -1.0).astype(BF16)
        o_ref[:, 2 * D_MODEL:] = dv.astype(BF16)

    row = pl.BlockSpec((tm, D_MODEL), lambda i: (i, 0))
    in_specs = []
    args = []
    for (dq_b, dka, dkb, dva, dvb), d in zip(parts, dils):
        ahead = pl.BlockSpec((tm, D_MODEL), lambda i, d=d: (jnp.minimum(i + d, nblk - 1), 0))
        in_specs += [row, row, ahead, row, ahead]
        args += [dq_b, dka, dkb, dva, dvb]
    in_specs += [pl.BlockSpec((tm, LANES), lambda i: (i, 0))] * 3
    return pl.pallas_call(
        body, name=name, grid=(nblk,),
        in_specs=in_specs,
        out_specs=pl.BlockSpec((tm, 3 * D_MODEL), lambda i: (i, 0)),
        out_shape=jax.ShapeDtypeStruct((S, 3 * D_MODEL), BF16),
        compiler_params=_params(("parallel",), VMEM_BIG),
    )(*args, *rope)


def _mesh_pos():
    return lax.axis_index("x"), lax.axis_index("y"), lax.axis_index("c")


def _all_gather(shard, name):
    R, C = shard.shape

    def body(x_ref, out_ref, send_sems, recv_sems, local_sem):
        x, y, c = _mesh_pos()
        me, sibling = (x, y, c), (x, y, 1 - c)
        chips = [(1 - x, y), (x, 1 - y), (1 - x, 1 - y)]

        def blk(p):
            return out_ref.at[4 * p[0] + 2 * p[1] + p[2]]

        def copy(k, block, to, src=None):
            return pltpu.make_async_remote_copy(
                src_ref=blk(block) if src is None else src, dst_ref=blk(block),
                send_sem=send_sems.at[k], recv_sem=recv_sems.at[k],
                device_id=to, device_id_type=pl.DeviceIdType.MESH)

        mine = pltpu.make_async_copy(x_ref, blk(me), local_sem)
        mine.start()
        first = [copy(0, me, sibling, src=x_ref)]
        first += [copy(1 + j, me, (*chip, c), src=x_ref) for j, chip in enumerate(chips)]
        for cp in first:
            cp.start()
        passed = [copy(4 + j, (*chip, c), sibling) for j, chip in enumerate(chips)]
        for j, chip in enumerate(chips):
            copy(1 + j, (*chip, c), me).wait_recv()
            passed[j].start()
        copy(0, sibling, me).wait_recv()
        for j, chip in enumerate(chips):
            copy(4 + j, (*chip, 1 - c), me).wait_recv()
        for cp in first + passed:
            cp.wait_send()
        mine.wait()

    return pl.pallas_call(
        body, name=name,
        in_specs=[pl.BlockSpec(memory_space=pl.ANY)],
        out_specs=pl.BlockSpec(memory_space=pl.ANY),
        out_shape=jax.ShapeDtypeStruct((N_DEV, R, C), shard.dtype),
        scratch_shapes=[pltpu.SemaphoreType.DMA((7,)), pltpu.SemaphoreType.DMA((7,)), pltpu.SemaphoreType.DMA],
    )(shard)


def _rs_pair(g, name):
    _, R, C = g.shape

    def body(g_ref, out_ref, send_sems, recv_sems):
        x, y, c = _mesh_pos()
        sibling = (x, y, 1 - c)
        cps = []
        for chip in range(4):
            cps.append(pltpu.make_async_remote_copy(
                src_ref=g_ref.at[2 * chip + (1 - c)], dst_ref=out_ref.at[chip],
                send_sem=send_sems.at[chip], recv_sem=recv_sems.at[chip],
                device_id=sibling, device_id_type=pl.DeviceIdType.MESH))
        for cp in cps:
            cp.start()
        for cp in cps:
            cp.wait_recv()
        for cp in cps:
            cp.wait_send()

    return pl.pallas_call(
        body, name=name,
        in_specs=[pl.BlockSpec(memory_space=pl.ANY)],
        out_specs=pl.BlockSpec(memory_space=pl.ANY),
        out_shape=jax.ShapeDtypeStruct((4, R, C), g.dtype),
        scratch_shapes=[pltpu.SemaphoreType.DMA((4,)), pltpu.SemaphoreType.DMA((4,))],
    )(g)


def _pair_add(g, got, cidx, name):
    _, R, C = g.shape
    tr = 256

    def body(c_ref, g_ref, r_ref, o_ref, ob_ref):
        s = g_ref[...] + r_ref[...].astype(F32)
        o_ref[...] = s
        ob_ref[...] = s.astype(BF16)

    blk = pl.BlockSpec((None, tr, C), lambda k, i, c: (k, i, 0))
    return pl.pallas_call(
        body, name=name,
        grid_spec=pltpu.PrefetchScalarGridSpec(
            num_scalar_prefetch=1, grid=(4, R // tr),
            in_specs=[pl.BlockSpec((None, tr, C), lambda k, i, c: (2 * k + c[0], i, 0)), blk],
            out_specs=[blk, blk]),
        out_shape=[jax.ShapeDtypeStruct((4, R, C), F32), jax.ShapeDtypeStruct((4, R, C), BF16)],
        compiler_params=_params(("parallel", "parallel")),
    )(cidx, g, got)


def _rs_chips(p, name):
    _, R, C = p.shape

    def body(p_ref, out_ref, send_sems, recv_sems):
        x, y, c = _mesh_pos()
        chips = [(1 - x, y), (x, 1 - y), (1 - x, 1 - y)]
        cps = []
        for j, (cx, cy) in enumerate(chips):
            cps.append(pltpu.make_async_remote_copy(
                src_ref=p_ref.at[2 * cx + cy], dst_ref=out_ref.at[j],
                send_sem=send_sems.at[j], recv_sem=recv_sems.at[j],
                device_id=(cx, cy, c), device_id_type=pl.DeviceIdType.MESH))
        for cp in cps:
            cp.start()
        for cp in cps:
            cp.wait_recv()
        for cp in cps:
            cp.wait_send()

    return pl.pallas_call(
        body, name=name,
        in_specs=[pl.BlockSpec(memory_space=pl.ANY)],
        out_specs=pl.BlockSpec(memory_space=pl.ANY),
        out_shape=jax.ShapeDtypeStruct((3, R, C), p.dtype),
        scratch_shapes=[pltpu.SemaphoreType.DMA((3,)), pltpu.SemaphoreType.DMA((3,))],
    )(p)


def _adamw_math(w, g, m, v):
    m2 = ADAM_B1 * m + (1.0 - ADAM_B1) * g
    v2 = ADAM_B2 * v + (1.0 - ADAM_B2) * (g * g)
    m_hat = m2 / (1.0 - ADAM_B1 ** ADAM_STEP)
    v_hat = v2 / (1.0 - ADAM_B2 ** ADAM_STEP)
    delta = -ADAM_LR * (m_hat / (jnp.sqrt(v_hat) + ADAM_EPS) + ADAM_WD * w)
    return delta, m2, v2


def _adamw_shard(p, got, chip_idx, w, m, v, name):
    R, C = w.shape
    tr = 256

    def body(k_ref, p_ref, r_ref, w_ref, m_ref, v_ref, g_out, d_out, m_out, v_out):
        g = ((p_ref[...] + r_ref[0].astype(F32)) + r_ref[1].astype(F32)) + r_ref[2].astype(F32)
        delta, m2, v2 = _adamw_math(w_ref[...], g, m_ref[...], v_ref[...])
        g_out[...] = g
        d_out[...] = delta
        m_out[...] = m2
        v_out[...] = v2

    row = pl.BlockSpec((tr, C), lambda i, k: (i, 0))
    return pl.pallas_call(
        body, name=name,
        grid_spec=pltpu.PrefetchScalarGridSpec(
            num_scalar_prefetch=1, grid=(R // tr,),
            in_specs=[pl.BlockSpec((None, tr, C), lambda i, k: (k[0], i, 0)),
                      pl.BlockSpec((3, tr, C), lambda i, k: (0, i, 0)), row, row, row],
            out_specs=[row] * 4),
        out_shape=[jax.ShapeDtypeStruct((R, C), F32)] * 4,
        compiler_params=_params(("parallel",)),
    )(chip_idx, p, got, w, m, v)


def _adamw_small(gathered, w, m, v, name):
    _, R, C = gathered.shape

    def body(a_ref, w_ref, m_ref, v_ref, g_out, d_out, m_out, v_out):
        g = a_ref[0]
        for k in range(1, N_DEV):
            g = g + a_ref[k]
        delta, m2, v2 = _adamw_math(w_ref[...], g, m_ref[...], v_ref[...])
        g_out[...] = g
        d_out[...] = delta
        m_out[...] = m2
        v_out[...] = v2

    return pl.pallas_call(
        body, name=name, out_shape=[jax.ShapeDtypeStruct((R, C), F32)] * 4,
    )(gathered, w, m, v)


def _rope_tables(S):
    half = ROPE_DIM // 2
    inv_freq = ROPE_THETA ** (-jnp.arange(half, dtype=F32) / half)
    ang = jnp.arange(S, dtype=jnp.int32).astype(F32)[:, None] * inv_freq[None, :]
    cos, sin = jnp.cos(ang), jnp.sin(ang)
    ones = jnp.ones((S, HEAD_DIM - ROPE_DIM), F32)
    zeros = jnp.zeros((S, HEAD_DIM - ROPE_DIM), F32)
    zh = jnp.zeros((S, half), F32)
    c = jnp.concatenate([cos, cos, ones], axis=1)
    s1 = jnp.concatenate([zh, sin, zeros], axis=1)
    s2 = jnp.concatenate([-sin, zh, zeros], axis=1)
    two = lambda t: jnp.concatenate([t, t], axis=1)
    return two(c), two(s1), two(s2)


def _chunk_transposed(a, S):
    return a.reshape(S // SB_CH, SB_CH, HEAD_PAIRS, LANES).transpose(2, 0, 3, 1)


def _flat_shards(ws):
    return jnp.concatenate([w.reshape(-1, D_MODEL) for layer in ws for w in layer], axis=0)


def kernel(x, w_qkv_0, w_o_0, ln1_g_0, ln1_b_0, w_ff1_0, w_ff2_0, ln2_g_0, ln2_b_0, w_qkv_1, w_o_1, ln1_g_1, ln1_b_1, w_ff1_1, w_ff2_1, ln2_g_1, ln2_b_1, loss_target, m_w_qkv_0, m_w_o_0, m_ln1_g_0, m_ln1_b_0, m_w_ff1_0, m_w_ff2_0, m_ln2_g_0, m_ln2_b_0, m_w_qkv_1, m_w_o_1, m_ln1_g_1, m_ln1_b_1, m_w_ff1_1, m_w_ff2_1, m_ln2_g_1, m_ln2_b_1, v_w_qkv_0, v_w_o_0, v_ln1_g_0, v_ln1_b_0, v_w_ff1_0, v_w_ff2_0, v_ln2_g_0, v_ln2_b_0, v_w_qkv_1, v_w_o_1, v_ln1_g_1, v_ln1_b_1, v_w_ff1_1, v_w_ff2_1, v_ln2_g_1, v_ln2_b_1):
    S = x.shape[1]
    x0 = x.reshape(S, D_MODEL)
    target = loss_target.reshape(S, D_MODEL)
    mats = ((w_qkv_0, w_o_0, w_ff1_0, w_ff2_0), (w_qkv_1, w_o_1, w_ff1_1, w_ff2_1))
    mats_m = ((m_w_qkv_0, m_w_o_0, m_w_ff1_0, m_w_ff2_0), (m_w_qkv_1, m_w_o_1, m_w_ff1_1, m_w_ff2_1))
    mats_v = ((v_w_qkv_0, v_w_o_0, v_w_ff1_0, v_w_ff2_0), (v_w_qkv_1, v_w_o_1, v_w_ff1_1, v_w_ff2_1))
    vecs = (ln1_g_0, ln1_b_0, ln2_g_0, ln2_b_0, ln1_g_1, ln1_b_1, ln2_g_1, ln2_b_1)
    vecs_m = (m_ln1_g_0, m_ln1_b_0, m_ln2_g_0, m_ln2_b_0, m_ln1_g_1, m_ln1_b_1, m_ln2_g_1, m_ln2_b_1)
    vecs_v = (v_ln1_g_0, v_ln1_b_0, v_ln2_g_0, v_ln2_b_0, v_ln1_g_1, v_ln1_b_1, v_ln2_g_1, v_ln2_b_1)

    w_flat = _flat_shards(mats)
    w_all = _all_gather(w_flat.astype(BF16), "ag_weights")
    layers = []
    for l in range(N_LAYERS):
        base = l * LAYER_ROWS
        r0, r1, r2, r3 = np.cumsum((0,) + SHARD_ROWS)[:4] + base
        layers.append(dict(
            qkv=w_all[:, r0:r0 + 384].reshape(N_DEV, D_MODEL, 384),
            o=w_all[:, r1:r1 + 128].reshape(D_MODEL, D_MODEL),
            ff1=w_all[:, r2:r2 + 512].reshape(N_DEV, D_MODEL, 512),
            ff2=w_all[:, r3:r3 + 512].reshape(D_FF, D_MODEL),
            g1=vecs[4 * l].reshape(1, D_MODEL), b1=vecs[4 * l + 1].reshape(1, D_MODEL),
            g2=vecs[4 * l + 2].reshape(1, D_MODEL), b2=vecs[4 * l + 3].reshape(1, D_MODEL)))

    rope = _rope_tables(S)
    tmat_later = _sb_tmat(True)
    tmat_upto = _sb_tmat(False)

    saved = []
    xin, xinb = x0, x0.astype(BF16)
    for l, W in enumerate(layers):
        sv = dict(xin=xin, xinb=xinb)
        qkv = _qkv_proj(xinb, W["qkv"], rope if l == 1 else None, Q_SCALE * LOG2E if l == 0 else Q_SCALE,
                        BF16 if l == 0 else F32, f"qkv_proj_{l}")
        sv["qkv"] = qkv
        if l == 0:
            vT3 = _chunk_transposed(qkv[:, 2 * D_MODEL:], S)
            ob, sb_tiles, sb_first = _sb_fwd(qkv, vT3, tmat_later, "sb_fwd")
            o = None
            sv.update(sb_tiles=sb_tiles, sb_first=sb_first)
        else:
            outs = [_dil_fwd(qkv, d, f"dil_fwd_{d}") for _, d in DILATED_BRANCHES]
            o, ob, lse = _dil_merge([t[0] for t in outs], [t[1] for t in outs], "dil_merge")
            sv["lse"] = lse
        sv.update(o=o, ob=ob)
        y1, x1, x1b = _mm_res_ln(ob, xin, W["o"], W["g1"], W["b1"], f"attn_out_ln_{l}")
        hpre, h = _ff1(x1b, W["ff1"], f"ff1_{l}")
        y2, x2, x2b = _mm_res_ln(h, x1, W["ff2"], W["g2"], W["b2"], f"ff2_ln_{l}")
        sv.update(y1=y1, x1=x1, x1b=x1b, hpre=hpre, h=h, y2=y2)
        saved.append(sv)
        xin, xinb = x2, x2b

    dout, loss_parts = _loss_grad(xin, target, "loss_grad")
    loss = lax.psum(jnp.sum(loss_parts), MESH_AXES)

    gmats = [None] * N_LAYERS
    gvecs = [None] * (4 * N_LAYERS)
    for l in reversed(range(N_LAYERS)):
        W, sv = layers[l], saved[l]
        dy2, dy2b, gb2, dhp = _ln_bwd(dout, sv["y2"], W["g2"], W["ff2"], sv["hpre"], f"ln2_bwd_dh_{l}")
        g_ff2 = _mm_tn(sv["h"], dy2b, 512, D_MODEL, False, f"dw_ff2_{l}")
        dx1 = _dx_blk(dy2, dhp, W["ff1"], f"dx_ff1_{l}")
        g_ff1 = _mm_tn(sv["x1b"], dhp, D_MODEL, 512, True, f"dw_ff1_{l}")
        dy1, dy1b, gb1, do = _ln_bwd(dx1, sv["y1"], W["g1"], W["o"], None, f"ln1_bwd_do_{l}")
        g_o = _mm_tn(sv["ob"], dy1b, 512, D_MODEL, False, f"dw_o_{l}")
        if l == 0:
            kT3 = _chunk_transposed(sv["qkv"][:, D_MODEL:2 * D_MODEL], S)
            dq, dk, dv = _sb_bwd(sv["qkv"], kT3, do, sv["sb_tiles"], sv["sb_first"], tmat_upto, "sb_bwd")
            dqkv = jnp.concatenate([dq, dk, dv], axis=1).astype(BF16)
        else:
            dlt = _head_sums(do, sv["o"], "head_sums")
            parts = [_dil_bwd(sv["qkv"], do, sv["lse"], dlt, d, f"dil_bwd_{d}") for _, d in DILATED_BRANCHES]
            dqkv = _dil_combine(parts, rope, "dil_combine")
        dout = _dx_blk(dy1, dqkv, W["qkv"], f"dx_qkv_{l}")
        g_qkv = _mm_tn(sv["xinb"], dqkv, D_MODEL, 384, True, f"dw_qkv_{l}")
        gmats[l] = (g_qkv.reshape(N_DEV, 384, D_MODEL), g_o.reshape(N_DEV, 128, D_MODEL),
                    g_ff1.reshape(N_DEV, 512, D_MODEL), g_ff2.reshape(N_DEV, 512, D_MODEL))
        gvecs[4 * l:4 * l + 4] = [gb1[0], gb1[1], gb2[0], gb2[1]]
    grad_x = dout.reshape(1, S, D_MODEL)

    cx, cy, cc = _mesh_pos()
    g_all = jnp.concatenate([g for layer in gmats for g in layer], axis=1)
    got_pair = _rs_pair(g_all.astype(BF16), "rs_pair")
    chip_part, chip_part_b = _pair_add(g_all, got_pair, cc.astype(jnp.int32).reshape(1), "rs_pair_add")
    got_chips = _rs_chips(chip_part_b, "rs_chips")
    chip_idx = (2 * cx + cy).astype(jnp.int32).reshape(1)
    g_sh, d_sh, m_sh, v_sh = _adamw_shard(chip_part, got_chips, chip_idx, w_flat, _flat_shards(mats_m),
                                          _flat_shards(mats_v), "adamw_mats")

    def unflat(a):
        out, pos = [], 0
        for layer in mats:
            for w in layer:
                n = w.size // D_MODEL
                out.append(a[pos:pos + n].reshape(w.shape))
                pos += n
        return out

    gv_all = _all_gather(jnp.stack(gvecs), "ag_vec_grads")
    g_v, d_v, m_v, v_v = _adamw_small(gv_all, jnp.stack(vecs), jnp.stack(vecs_m), jnp.stack(vecs_v), "adamw_vecs")

    def interleave(mat_list, vec_arr):
        out = []
        for l in range(N_LAYERS):
            qkv_, o_, ff1_, ff2_ = mat_list[4 * l:4 * l + 4]
            out += [qkv_, o_, vec_arr[4 * l], vec_arr[4 * l + 1], ff1_, ff2_, vec_arr[4 * l + 2], vec_arr[4 * l + 3]]
        return out

    return (loss, grad_x, *interleave(unflat(g_sh), g_v), *interleave(unflat(d_sh), d_v),
            *interleave(unflat(m_sh), m_v), *interleave(unflat(v_sh), v_v))
```

```python
import functools
import math

import jax
import jax.numpy as jnp
import numpy as np
from jax import lax
from jax.experimental import pallas as pl
from jax.experimental.pallas import tpu as pltpu

F32 = jnp.float32
BF16 = jnp.bfloat16

D_MODEL = 1024
N_HEADS = 16
HEAD_DIM = 64
D_FF = 4096
N_DEV = 8
N_LAYERS = 2
ROPE_THETA = 500000.0
ROPE_DIM = 16
DILATED_BRANCHES = ((128, 1), (512, 4), (2048, 16))
ALPHA = (2 * N_LAYERS) ** 0.25
LN_EPS = 1e-5
Q_SCALE = 1.0 / math.sqrt(HEAD_DIM)
LOG2E = math.log2(math.e)
LN2 = math.log(2.0)
ADAM_LR, ADAM_B1, ADAM_B2, ADAM_EPS, ADAM_WD, ADAM_STEP = 0.001, 0.9, 0.999, 1e-08, 0.01, 10

LANES = 128
HEAD_PAIRS = D_MODEL // LANES
SB_TQ = 256
SB_CH = 256
SB_STEPS = 4
SB_SAVE_SLOTS = 2 * SB_STEPS
SB_LOAD_SLOTS = 12
SB_LOAD_AHEAD = SB_LOAD_SLOTS - SB_STEPS - 1
SB_DEAD = -160.0
DIL_BLK = 128
VMEM_BIG = 56 * 2 ** 20
MESH_AXES = ("x", "y", "c")

SHARD_ROWS = (384, 128, 512, 512)
LAYER_ROWS = sum(SHARD_ROWS)
ALL_ROWS = N_LAYERS * LAYER_ROWS


def _params(sem=None, vmem=None):
    kw = {}
    if sem is not None:
        kw["dimension_semantics"] = sem
    if vmem is not None:
        kw["vmem_limit_bytes"] = vmem
    return pltpu.CompilerParams(**kw)


def _dot(a, b):
    return jnp.dot(a, b, preferred_element_type=F32)


def _dot_nt(a, b):
    return lax.dot_general(a, b, (((1,), (1,)), ((), ())), preferred_element_type=F32)


def _dot_tn(a, b):
    return lax.dot_general(a, b, (((0,), (0,)), ((), ())), preferred_element_type=F32)


def _split3(p):
    hi = p.astype(BF16)
    r1 = p - hi.astype(F32)
    mid = r1.astype(BF16)
    lo = (r1 - mid.astype(F32)).astype(BF16)
    return hi, mid, lo


def _dot3(p, e):
    hi, mid, lo = _split3(p)
    return _dot(hi, e) + _dot(mid, e) + _dot(lo, e)


def _rope_apply(a, c, s1, s2, sign):
    return a * c + sign * (pltpu.roll(a, 8, 1) * s1 + pltpu.roll(a, LANES - 8, 1) * s2)


def _qkv_proj(xb, w_blk, rope, q_mult, out_dtype, name):
    S = xb.shape[0]
    tm = 512
    n_rope = 0 if rope is None else 3

    def body(*refs):
        x_ref, w_ref = refs[:2]
        tabs = [r[...] for r in refs[2:2 + n_rope]]
        o_ref = refs[2 + n_rope]
        x = x_ref[...]
        for j in range(N_DEV):
            acc = _dot(x, w_ref[j])
            for g in range(3):
                col = j * 384 + g * LANES
                a = acc[:, g * LANES:(g + 1) * LANES]
                if n_rope and col < 2 * D_MODEL:
                    a = _rope_apply(a, *tabs, 1.0)
                if col < D_MODEL:
                    a = a * q_mult
                o_ref[:, col:col + LANES] = a.astype(out_dtype)

    tab_specs = [pl.BlockSpec((tm, LANES), lambda i: (i, 0))] * n_rope
    return pl.pallas_call(
        body, name=name, grid=(S // tm,),
        in_specs=[pl.BlockSpec((tm, D_MODEL), lambda i: (i, 0)),
                  pl.BlockSpec((N_DEV, D_MODEL, 384), lambda i: (0, 0, 0))] + tab_specs,
        out_specs=pl.BlockSpec((tm, 3 * D_MODEL), lambda i: (i, 0)),
        out_shape=jax.ShapeDtypeStruct((S, 3 * D_MODEL), out_dtype),
        compiler_params=_params(("parallel",), VMEM_BIG),
    )(xb, w_blk, *(rope or ()))


def _layer_norm_rows(y, g, b):
    mu = jnp.mean(y, axis=-1, keepdims=True)
    yc = y - mu
    var = jnp.mean(yc * yc, axis=-1, keepdims=True)
    return yc * lax.rsqrt(var + LN_EPS) * g + b


def _mm_res_ln(a, xres, w, g, b, target, name):
    S, K = a.shape
    tm = 512 if K <= 1024 else 256
    head = target is not None

    def body(*refs):
        a_ref, x_ref, w_ref, g_ref, b_ref = refs[:5]
        y_ref, xn_ref, xb_ref = refs[5 + head:8 + head]
        y = ALPHA * x_ref[...] + _dot(a_ref[...], w_ref[...])
        xn = _layer_norm_rows(y, g_ref[...], b_ref[...])
        y_ref[...] = y
        xn_ref[...] = xn
        xb_ref[...] = xn.astype(BF16)
        if head:
            t_ref, dy_ref, l_ref = refs[5], refs[9], refs[10]

            @pl.when(pl.program_id(0) == 0)
            def _():
                l_ref[...] = jnp.zeros_like(l_ref)

            err = xn - t_ref[...]
            dy_ref[...] = err * (1.0 / D_MODEL)
            sq = err * err
            rows = sq[0:8]
            for r in range(1, tm // 8):
                rows = rows + sq[r * 8:(r + 1) * 8]
            acc = rows[:, 0:LANES]
            for c in range(1, D_MODEL // LANES):
                acc = acc + rows[:, c * LANES:(c + 1) * LANES]
            l_ref[...] += acc * (0.5 / D_MODEL)

    row = lambda i: (i, 0)
    fix = lambda i: (0, 0)
    tile = pl.BlockSpec((tm, D_MODEL), row)
    return pl.pallas_call(
        body, name=name, grid=(S // tm,),
        in_specs=[pl.BlockSpec((tm, K), row), tile, pl.BlockSpec((K, D_MODEL), fix), pl.BlockSpec((1, D_MODEL), fix),
                  pl.BlockSpec((1, D_MODEL), fix)] + [tile] * head,
        out_specs=[tile] * 3 + [tile, pl.BlockSpec((8, LANES), fix)] * head,
        out_shape=[jax.ShapeDtypeStruct((S, D_MODEL), F32), jax.ShapeDtypeStruct((S, D_MODEL), F32),
                   jax.ShapeDtypeStruct((S, D_MODEL), BF16)]
        + [jax.ShapeDtypeStruct((S, D_MODEL), F32), jax.ShapeDtypeStruct((8, LANES), F32)] * head,
        compiler_params=_params(("arbitrary",) if head else ("parallel",), VMEM_BIG),
    )(a, xres, w, g, b, *([target] if head else []))


def _ff1(xb, w_blk, name):
    S = xb.shape[0]
    tm = 256

    def body(x_ref, w_ref, hp_ref, h_ref):
        x = x_ref[...]
        for j in range(N_DEV):
            acc = _dot(x, w_ref[j])
            r = jnp.maximum(acc, 0.0)
            hp_ref[:, j * 512:(j + 1) * 512] = acc
            h_ref[:, j * 512:(j + 1) * 512] = (r * r).astype(BF16)

    return pl.pallas_call(
        body, name=name, grid=(S // tm,),
        in_specs=[pl.BlockSpec((tm, D_MODEL), lambda i: (i, 0)),
                  pl.BlockSpec((N_DEV, D_MODEL, 512), lambda i: (0, 0, 0))],
        out_specs=[pl.BlockSpec((tm, D_FF), lambda i: (i, 0))] * 2,
        out_shape=[jax.ShapeDtypeStruct((S, D_FF), F32), jax.ShapeDtypeStruct((S, D_FF), BF16)],
        compiler_params=_params(("parallel",), VMEM_BIG),
    )(xb, w_blk)


def _ln_bwd(dout, y, g, w, hpre, name):
    S = y.shape[0]
    N = w.shape[0]
    tm = 512 if hpre is None else 256
    tn = 512
    steps = S // tm

    def body(*refs):
        d_ref, y_ref, g_ref, w_ref = refs[:4]
        hp_ref = None if hpre is None else refs[4]
        dy_ref, dyb_ref, gb_ref, o_ref, acc_g, acc_b = refs[-6:]
        i = pl.program_id(0)

        @pl.when(i == 0)
        def _():
            acc_g[...] = jnp.zeros_like(acc_g)
            acc_b[...] = jnp.zeros_like(acc_b)

        d = d_ref[...]
        yv = y_ref[...]
        mu = jnp.mean(yv, axis=-1, keepdims=True)
        yc = yv - mu
        var = jnp.mean(yc * yc, axis=-1, keepdims=True)
        rstd = lax.rsqrt(var + LN_EPS)
        xhat = yc * rstd
        dxh = d * g_ref[...]
        m1 = jnp.mean(dxh, axis=-1, keepdims=True)
        m2 = jnp.mean(dxh * xhat, axis=-1, keepdims=True)
        dy = rstd * (dxh - m1 - xhat * m2)
        dyb = dy.astype(BF16)
        dy_ref[...] = dy
        dyb_ref[...] = dyb
        for c in range(0, N, tn):
            prod = _dot_nt(dyb, w_ref[c:c + tn, :])
            if hpre is None:
                o_ref[:, c:c + tn] = prod
            else:
                o_ref[:, c:c + tn] = (prod * (2.0 * jnp.maximum(hp_ref[:, c:c + tn], 0.0))).astype(BF16)
        pg = d * xhat
        sg = pg[0:8]
        sb = d[0:8]
        for r in range(1, tm // 8):
            sg = sg + pg[r * 8:(r + 1) * 8]
            sb = sb + d[r * 8:(r + 1) * 8]
        acc_g[...] += sg
        acc_b[...] += sb

        @pl.when(i == steps - 1)
        def _():
            gb_ref[0:1, :] = jnp.sum(acc_g[...], axis=0, keepdims=True)
            gb_ref[1:2, :] = jnp.sum(acc_b[...], axis=0, keepdims=True)

    row = lambda i: (i, 0)
    fix = lambda i: (0, 0)
    extra_in = [] if hpre is None else [pl.BlockSpec((tm, N), row)]
    extra_arg = [] if hpre is None else [hpre]
    return pl.pallas_call(
        body, name=name, grid=(steps,),
        in_specs=[pl.BlockSpec((tm, D_MODEL), row), pl.BlockSpec((tm, D_MODEL), row), pl.BlockSpec((1, D_MODEL), fix),
                  pl.BlockSpec((N, D_MODEL), fix)] + extra_in,
        out_specs=[pl.BlockSpec((tm, D_MODEL), row), pl.BlockSpec((tm, D_MODEL), row), pl.BlockSpec((2, D_MODEL), fix),
                   pl.BlockSpec((tm, N), row)],
        out_shape=[jax.ShapeDtypeStruct((S, D_MODEL), F32), jax.ShapeDtypeStruct((S, D_MODEL), BF16),
                   jax.ShapeDtypeStruct((2, D_MODEL), F32),
                   jax.ShapeDtypeStruct((S, N), F32 if hpre is None else BF16)],
        scratch_shapes=[pltpu.VMEM((8, D_MODEL), F32), pltpu.VMEM((8, D_MODEL), F32)],
        compiler_params=_params(("arbitrary",), VMEM_BIG),
    )(dout, y, g, w, *extra_arg)


def _dx_blk(dres, dz, w_blk, name):
    S, N = dz.shape
    bw = w_blk.shape[2]
    tm = 256

    def body(r_ref, z_ref, w_ref, o_ref):
        acc = ALPHA * r_ref[...]
        for j in range(N_DEV):
            acc = acc + _dot_nt(z_ref[:, j * bw:(j + 1) * bw], w_ref[j])
        o_ref[...] = acc

    return pl.pallas_call(
        body, name=name, grid=(S // tm,),
        in_specs=[pl.BlockSpec((tm, D_MODEL), lambda i: (i, 0)), pl.BlockSpec((tm, N), lambda i: (i, 0)),
                  pl.BlockSpec((N_DEV, D_MODEL, bw), lambda i: (0, 0, 0))],
        out_specs=pl.BlockSpec((tm, D_MODEL), lambda i: (i, 0)),
        out_shape=jax.ShapeDtypeStruct((S, D_MODEL), F32),
        compiler_params=_params(("parallel",), VMEM_BIG),
    )(dres, dz, w_blk)


def _mm_tn(a, b, ta, tb, blocked, name):
    S, Ka = a.shape
    Nb = b.shape[1]
    ts = 2048

    def body(a_ref, b_ref, o_ref):
        @pl.when(pl.program_id(2) == 0)
        def _():
            o_ref[...] = jnp.zeros_like(o_ref)

        o_ref[...] += _dot_tn(a_ref[...], b_ref[...])

    if blocked:
        out_spec = pl.BlockSpec((None, ta, tb), lambda i, j, s: (j, i, 0))
        out_shape = jax.ShapeDtypeStruct((Nb // tb, Ka, tb), F32)
    else:
        out_spec = pl.BlockSpec((ta, tb), lambda i, j, s: (i, j))
        out_shape = jax.ShapeDtypeStruct((Ka, Nb), F32)
    return pl.pallas_call(
        body, name=name, grid=(Ka // ta, Nb // tb, S // ts),
        in_specs=[pl.BlockSpec((ts, ta), lambda i, j, s: (s, i)), pl.BlockSpec((ts, tb), lambda i, j, s: (s, j))],
        out_specs=out_spec, out_shape=out_shape,
        compiler_params=_params(("parallel", "parallel", "arbitrary"), VMEM_BIG),
    )(a, b)


def _head_sums(do, o, name):
    S = do.shape[0]
    tm = 512
    sel = (np.arange(D_MODEL)[:, None] // HEAD_DIM == np.arange(LANES)[None, :]).astype(np.float32)

    def body(d_ref, o_ref, e_ref, out_ref):
        out_ref[...] = _dot3(d_ref[...] * o_ref[...], e_ref[...])

    return pl.pallas_call(
        body, name=name, grid=(S // tm,),
        in_specs=[pl.BlockSpec((tm, D_MODEL), lambda i: (i, 0))] * 2 + [pl.BlockSpec((D_MODEL, LANES), lambda i: (0, 0))],
        out_specs=pl.BlockSpec((tm, LANES), lambda i: (i, 0)),
        out_shape=jax.ShapeDtypeStruct((S, LANES), F32),
        compiler_params=_params(("parallel",)),
    )(do, o, jnp.asarray(sel, BF16))


def _sb_tmat(later):
    r = np.arange(SB_CH)
    t = (r[None, :] > r[:, None]) if later else (r[None, :] <= r[:, None])
    return jnp.asarray(np.concatenate([t.astype(np.float32), np.ones((8, SB_CH), np.float32)], axis=0), BF16)


def _sb_gates(z2):
    neg_abs = lax.bitcast_convert_type(lax.bitcast_convert_type(z2, jnp.uint32) | jnp.uint32(0x80000000), F32)
    l1 = jnp.log2(1.0 + jnp.exp2(neg_abs))
    a = jnp.minimum(z2, 0.0) - l1
    return a, a - z2


def _head_masks(x2):
    lane = lax.broadcasted_iota(jnp.int32, x2.shape, 1)
    zero = jnp.zeros_like(x2)
    return jnp.where(lane < HEAD_DIM, x2, zero), jnp.where(lane >= HEAD_DIM, x2, zero)


def _sb_fwd(qkv, vT3, tmat, name):
    S = qkv.shape[0]
    nq = S // SB_TQ
    nch = S // SB_CH
    ns = SB_SAVE_SLOTS

    def body(q_ref, k_ref, vT_ref, t_ref, o_ref, ws_hbm, first_ref, z_scr, a_scr, cum_scr, oT_scr, stage, sems,
             pending):
        hp = pl.program_id(0)
        i = pl.program_id(1)
        base = (i * (i + 1)) // 2
        qm = _head_masks(q_ref[...])

        def save(src, sem, c):
            return pltpu.make_async_copy(src, ws_hbm.at[hp, base + c], sem)

        causal = (lax.broadcasted_iota(jnp.int32, (SB_CH, SB_TQ), 0)
                  < lax.broadcasted_iota(jnp.int32, (SB_CH, SB_TQ), 1))

        def head_rows(vTc, h):
            return vTc[h * HEAD_DIM:(h + 1) * HEAD_DIM, :]

        @pl.when(jnp.logical_and(hp == 0, i == 0))
        def _():
            z_scr[...] = jnp.zeros_like(z_scr)
            a_scr[...] = jnp.zeros_like(a_scr)
            cum_scr[...] = jnp.zeros_like(cum_scr)

        oT_scr[...] = jnp.zeros_like(oT_scr)

        def c_valid(t):
            return jnp.logical_and(t >= 2, t - 2 <= i)

        def c_chunk(t):
            return jnp.clip(i + 2 - t, 0, nch - 1)

        def step(t, p, slot, R, own_b, own_c, has_a, has_b, has_c, t_first_c):
            cA = jnp.maximum(i - t, 0)
            kA = k_ref[pl.ds(pl.multiple_of(cA * SB_CH, SB_CH), SB_CH), :]
            valid = jnp.logical_and(c_valid(t), t >= t_first_c)
            vC = vT_ref[c_chunk(t)]
            out = []
            for h in range(2):
                if has_a:
                    z_scr[p, h] = _dot_nt(kA, qm[h])
                if has_b:
                    a, lf = _sb_gates(z_scr[1 - p, h])
                    if own_b:
                        lf = jnp.where(causal, lf, 0.0)
                    a_scr[1 - p, h] = a
                    cum_scr[1 - p, h] = _dot(t_ref[...], lf.astype(BF16))
                if not has_c:
                    out.append(R[h])
                    continue
                a_c = a_scr[p, h]
                w = jnp.exp2(a_c + cum_scr[p, h, :SB_CH, :] + R[h])
                if own_c:
                    w = jnp.where(causal, w, 0.0)
                wb = w.astype(BF16)
                stage[slot, 2 * h] = wb
                stage[slot, 2 * h + 1] = a_c.astype(BF16)
                oT_scr[h] += jnp.where(valid, _dot(head_rows(vC, h), wb), 0.0)
                out.append(R[h] + jnp.where(valid, cum_scr[p, h, SB_CH:SB_CH + 1, :], 0.0))
            return tuple(out)

        step_no = hp * nq + i

        @pl.when(step_no == 0)
        def _():
            for s in range(ns):
                pending[s] = 0

        def settle(slot):
            @pl.when(pending[slot] == 1)
            def _():
                save(stage.at[slot], sems.at[slot], 0).wait()
                pending[slot] = 0

        def trip(tt, R, first):
            half = lax.rem(step_no + tt, 2) * SB_STEPS
            for j in range(SB_STEPS):
                settle(half + j)
            for j in range(SB_STEPS):
                if first:
                    R = step(j, j % 2, half + j, R, j == 1, j == 2, j < 2, j in (1, 2), j >= 2, 2)
                else:
                    R = step(SB_STEPS * (tt - 1) + j + 2, j % 2, half + j, R, False, False, True, True, True, 4)
            for j in range(SB_STEPS):
                t = j if first else SB_STEPS * (tt - 1) + j + 2

                @pl.when(jnp.logical_and(c_valid(t), t >= (2 if first else 4)))
                def _():
                    save(stage.at[half + j], sems.at[half + j], c_chunk(t)).start()
                    pending[half + j] = 1

            return R

        z1 = jnp.zeros((1, SB_TQ), F32)
        trips = 1 + jnp.where(i >= 2, (i + 1 + SB_STEPS - 1) // SB_STEPS, 0)

        def alive(carry):
            tt, R = carry
            return jnp.logical_and(tt < trips, jnp.max(jnp.maximum(R[0], R[1])) > SB_DEAD)

        trips, _ = lax.while_loop(alive, lambda carry: (carry[0] + 1, trip(carry[0], carry[1], False)),
                                  (jnp.int32(1), trip(0, (z1, z1), True)))
        first_ref[hp, i] = jnp.maximum(jnp.where(trips == 1, i - 1, i - (SB_STEPS * (trips - 1) - 1)), 0)

        @pl.when(step_no == HEAD_PAIRS * nq - 1)
        def _():
            for s in range(ns):
                settle(s)

        o_ref[...] = jnp.concatenate([oT_scr[0], oT_scr[1]], axis=0).T.astype(BF16)

    ntile = nq * (nq + 1) // 2
    return pl.pallas_call(
        body, name=name, grid=(HEAD_PAIRS, nq),
        in_specs=[pl.BlockSpec((SB_TQ, LANES), lambda hp, i: (i, hp)),
                  pl.BlockSpec((S, LANES), lambda hp, i: (0, HEAD_PAIRS + hp)),
                  pl.BlockSpec((None, nch, LANES, SB_CH), lambda hp, i: (hp, 0, 0, 0)),
                  pl.BlockSpec((SB_CH + 8, SB_CH), lambda hp, i: (0, 0))],
        out_specs=[pl.BlockSpec((SB_TQ, LANES), lambda hp, i: (i, hp)), pl.BlockSpec(memory_space=pl.ANY),
                   pl.BlockSpec(memory_space=pltpu.SMEM)],
        out_shape=[jax.ShapeDtypeStruct((S, D_MODEL), BF16),
                   jax.ShapeDtypeStruct((HEAD_PAIRS, ntile, 4, SB_CH, SB_TQ), BF16),
                   jax.ShapeDtypeStruct((HEAD_PAIRS, nq), jnp.int32)],
        scratch_shapes=[pltpu.VMEM((2, 2, SB_CH, SB_TQ), F32), pltpu.VMEM((2, 2, SB_CH, SB_TQ), F32),
                        pltpu.VMEM((2, 2, SB_CH + 8, SB_TQ), F32), pltpu.VMEM((2, HEAD_DIM, SB_TQ), F32),
                        pltpu.VMEM((ns, 4, SB_CH, SB_TQ), BF16), pltpu.SemaphoreType.DMA((ns,)),
                        pltpu.SMEM((ns,), jnp.int32)],
        compiler_params=_params(("arbitrary", "arbitrary"), VMEM_BIG),
    )(qkv, qkv, vT3, tmat)


def _sb_bwd(qkv, kT3, do, ws, first, tmat_g, name):
    S = qkv.shape[0]
    nq = S // SB_TQ
    nch = S // SB_CH
    nl = SB_LOAD_SLOTS
    ahead = SB_LOAD_AHEAD

    def body(first_ref, q_ref, do_ref, v_ref, kT_ref, tg_ref, ws_hbm, dq_ref, dk_hbm, dv_hbm, dk_acc, dv_acc, sems,
             dwv_scr, g_scr, sig_scr, cumg_scr, dqT_scr, ring, ring_sems):
        hp = pl.program_id(0)
        i = pl.program_id(1)
        c0 = first_ref[hp, i]
        n = i - c0
        base = (i * (i + 1)) // 2 + c0

        @pl.when(i == 0)
        def _():
            dk_acc[...] = jnp.zeros_like(dk_acc)
            dv_acc[...] = jnp.zeros_like(dv_acc)

        @pl.when(jnp.logical_and(hp == 0, i == 0))
        def _():
            for scr in (dwv_scr, g_scr, sig_scr, cumg_scr, ring):
                scr[...] = jnp.zeros_like(scr)

        step_no = hp * nq + i
        parity = lax.rem(step_no, 2)

        def slot_of(u, par):
            return jnp.where(u < 2, nl + 1 + 2 * par + u, lax.rem(u, nl))

        def copy_in(hp_, tile, u, par):
            sem = jnp.where(u < 2, nl + 2 * par + u, lax.rem(u, nl))
            return pltpu.make_async_copy(ws_hbm.at[hp_, tile + u], ring.at[slot_of(u, par)], ring_sems.at[sem])

        def load(u):
            return copy_in(hp, base, u, parity)

        for u in range(ahead):
            @pl.when(jnp.logical_and(u <= n, jnp.logical_or(u >= 2, step_no == 0)))
            def _():
                load(u).start()

        nxt = jnp.minimum(step_no + 1, HEAD_PAIRS * nq - 1)
        hp_n, i_n = nxt // nq, lax.rem(nxt, nq)
        c0_n = first_ref[hp_n, i_n]
        for u in range(2):
            @pl.when(jnp.logical_and(step_no + 1 < HEAD_PAIRS * nq, u <= i_n - c0_n))
            def _():
                copy_in(hp_n, (i_n * (i_n + 1)) // 2 + c0_n, u, 1 - parity).start()

        dqT_scr[...] = jnp.zeros_like(dqT_scr)
        qm = _head_masks(q_ref[...])
        dom = _head_masks(do_ref[...].astype(BF16))
        causal = (lax.broadcasted_iota(jnp.int32, (SB_CH, SB_TQ), 0)
                  < lax.broadcasted_iota(jnp.int32, (SB_CH, SB_TQ), 1))

        def rows_of(c):
            return pl.ds(pl.multiple_of(c * SB_CH, SB_CH), SB_CH)

        def head_rows(kTc, h):
            return kTc[h * HEAD_DIM:(h + 1) * HEAD_DIM, :]

        def step(t, p, Gs, has_b=True, has_c=True):
            q = 1 - p
            valid_b = jnp.logical_and(t >= 1, t - 1 <= n)
            valid_c = jnp.logical_and(t >= 2, t - 2 <= n)
            c_b = c0 + jnp.clip(t - 1, 0, n)
            c_c = c0 + jnp.clip(t - 2, 0, n)
            slot = jnp.where(valid_b, slot_of(jnp.maximum(t - 1, 0), parity), nl)
            vA = v_ref[rows_of(c0 + jnp.minimum(t, n)), :]
            kTc = kT_ref[c_c]
            keep = jnp.logical_or(causal, t - 2 != n)
            out = []
            for h in range(2):
                dwv_scr[p, h] = _dot_nt(vA, dom[h])

                if has_b:
                    wb = ring[slot, 2 * h]
                    g = wb.astype(F32) * dwv_scr[q, h]
                    g_scr[q, h] = g
                    sig_scr[q, h] = jnp.exp2(ring[slot, 2 * h + 1].astype(F32))
                    cumg_scr[q, h] = _dot(tg_ref[...], g.astype(BF16))
                    dv_h = _dot(wb, dom[h])
                    dv_c = dv_h if h == 0 else dv_c + dv_h

                if not has_c:
                    out.append(Gs[h])
                    continue
                dz = g_scr[p, h] - sig_scr[p, h] * (Gs[h] + cumg_scr[p, h, :SB_CH, :])
                dzb = jnp.where(keep, dz, 0.0).astype(BF16)
                dk_h = _dot(dzb, qm[h])
                dqT_scr[h] += jnp.where(valid_c, _dot(head_rows(kTc, h), dzb), 0.0)
                out.append(Gs[h] + jnp.where(valid_c, cumg_scr[p, h, SB_CH:SB_CH + 1, :], 0.0))
                dk_c = dk_h if h == 0 else dk_c + dk_h
            if has_b:
                dv_acc[rows_of(c_b), :] += jnp.where(valid_b, dv_c, 0.0)
            if has_c:
                dk_acc[rows_of(c_c), :] += jnp.where(valid_c, dk_c, 0.0)
            return tuple(out)

        def trip(tt, Gs, first):
            for j in range(SB_STEPS):
                t = SB_STEPS * tt + j

                @pl.when(jnp.logical_and(t >= 1, t - 1 <= n))
                def _():
                    load(t - 1).wait()

            for j in range(SB_STEPS):
                t = SB_STEPS * tt + j

                @pl.when(t + ahead <= n)
                def _():
                    load(t + ahead).start()

            for j in range(SB_STEPS):
                Gs = step(SB_STEPS * tt + j, j % 2, Gs, not (first and j == 0), not (first and j < 2))
            return Gs

        z1 = jnp.zeros((1, SB_TQ), F32)
        lax.fori_loop(1, (n + 3 + SB_STEPS - 1) // SB_STEPS, lambda tt, Gs: trip(tt, Gs, False),
                      trip(0, (z1, z1), True))
        dq_ref[...] = jnp.concatenate([dqT_scr[0], dqT_scr[1]], axis=0).T * Q_SCALE

        @pl.when(i == nq - 1)
        def _():
            dk_acc[...] = dk_acc[...] * LN2
            cols = pl.ds(pl.multiple_of(hp * LANES, LANES), LANES)
            ck = pltpu.make_async_copy(dk_acc, dk_hbm.at[:, cols], sems.at[0])
            cv = pltpu.make_async_copy(dv_acc, dv_hbm.at[:, cols], sems.at[1])
            ck.start()
            cv.start()
            ck.wait()
            cv.wait()

    blk = lambda hp, i, first: (i, hp)
    return pl.pallas_call(
        body, name=name,
        grid_spec=pltpu.PrefetchScalarGridSpec(
            num_scalar_prefetch=1, grid=(HEAD_PAIRS, nq),
            in_specs=[pl.BlockSpec((SB_TQ, LANES), blk),
                      pl.BlockSpec((SB_TQ, LANES), blk),
                      pl.BlockSpec((S, LANES), lambda hp, i, first: (0, 2 * HEAD_PAIRS + hp)),
                      pl.BlockSpec((None, nch, LANES, SB_CH), lambda hp, i, first: (hp, 0, 0, 0)),
                      pl.BlockSpec((SB_CH + 8, SB_CH), lambda hp, i, first: (0, 0)),
                      pl.BlockSpec(memory_space=pl.ANY)],
            out_specs=[pl.BlockSpec((SB_TQ, LANES), blk), pl.BlockSpec(memory_space=pl.ANY),
                       pl.BlockSpec(memory_space=pl.ANY)],
            scratch_shapes=[pltpu.VMEM((S, LANES), F32), pltpu.VMEM((S, LANES), F32),
                            pltpu.SemaphoreType.DMA((2,))]
            + [pltpu.VMEM((2, 2, SB_CH, SB_TQ), F32)] * 3
            + [pltpu.VMEM((2, 2, SB_CH + 8, SB_TQ), F32), pltpu.VMEM((2, HEAD_DIM, SB_TQ), F32)]
            + [pltpu.VMEM((nl + 5, 4, SB_CH, SB_TQ), BF16), pltpu.SemaphoreType.DMA((nl + 4,))]),
        out_shape=[jax.ShapeDtypeStruct((S, D_MODEL), F32)] * 3,
        compiler_params=_params(("arbitrary", "arbitrary"), VMEM_BIG),
    )(first, qkv, do, qkv, kT3, tmat_g, ws)


def _dil_valid(first):
    qi = lax.broadcasted_iota(jnp.int32, (DIL_BLK, 2 * DIL_BLK), 0)
    kj = lax.broadcasted_iota(jnp.int32, (DIL_BLK, 2 * DIL_BLK), 1)
    dist = DIL_BLK + qi - kj
    return (dist >= 0) & (dist <= DIL_BLK) & (jnp.logical_not(first) | (kj >= DIL_BLK))


def _lane_pick(tile, idx):
    lane = lax.broadcasted_iota(jnp.int32, tile.shape, 1)
    return jnp.sum(jnp.where(lane == idx, tile, 0.0), axis=-1, keepdims=True)


class _DilPlan:
    def __init__(self, S, d):
        self.d = d
        self.span = DIL_BLK * d
        self.groups = max(1, 1024 // self.span)
        self.rows = self.span * self.groups
        self.steps = S // self.rows

    def cur(self, col0):
        return pl.BlockSpec((self.rows, LANES), lambda n, hp: (n, col0 + hp))

    def prev(self, col0):
        g = self.groups
        return pl.BlockSpec((self.span, LANES), lambda n, hp: (jnp.maximum(n * g - 1, 0), col0 + hp))

    def shared(self):
        return pl.BlockSpec((self.rows, LANES), lambda n, hp: (n, 0))

    def units(self, fn):
        n = pl.program_id(0)
        batch = 8
        if self.d * self.groups <= batch:
            for g in range(self.groups):
                for r in range(self.d):
                    fn(g, r, jnp.logical_and(n == 0, g == 0))
        else:
            assert self.groups == 1 and self.d % batch == 0

            def body(rb, carry):
                for rr in range(batch):
                    fn(0, rb * batch + rr, n == 0)
                return carry

            lax.fori_loop(0, self.d // batch, body, 0)

    def rows_of(self, g, r):
        return pl.ds(g * self.span + r, DIL_BLK, stride=self.d)

    def keys(self, cur_ref, prev_ref, g, r):
        before = prev_ref[pl.ds(r, DIL_BLK, stride=self.d), :] if g == 0 else cur_ref[self.rows_of(g - 1, r), :]
        return jnp.concatenate([before, cur_ref[self.rows_of(g, r), :]], axis=0).astype(BF16)


def _dil_fwd(qkv, d, name):
    S = qkv.shape[0]
    plan = _DilPlan(S, d)

    def body(q_ref, kc_ref, kp_ref, vc_ref, vp_ref, o_ref, lse_ref):
        hp = pl.program_id(1)

        @pl.when(hp == 0)
        def _():
            lse_ref[...] = jnp.zeros_like(lse_ref)

        lane = lax.broadcasted_iota(jnp.int32, (DIL_BLK, LANES), 1)

        def unit(g, r, first):
            valid = _dil_valid(first)
            rows = plan.rows_of(g, r)
            qm = _head_masks(q_ref[rows, :].astype(BF16))
            kk = plan.keys(kc_ref, kp_ref, g, r)
            vm = _head_masks(plan.keys(vc_ref, vp_ref, g, r))
            lse_t = lse_ref[rows, :]
            o2 = None
            for h in range(2):
                s = jnp.where(valid, _dot_nt(qm[h], kk), -1e30)
                m = jnp.max(s, axis=-1, keepdims=True)
                p = jnp.exp(s - m)
                den = jnp.sum(p, axis=-1, keepdims=True)
                oh = _dot(p.astype(BF16), vm[h]) / den
                o2 = oh if o2 is None else o2 + oh
                lse_t = jnp.where(lane == 2 * hp + h, m + jnp.log(den), lse_t)
            o_ref[rows, :] = o2
            lse_ref[rows, :] = lse_t

        plan.units(unit)

    return pl.pallas_call(
        body, name=name, grid=(plan.steps, HEAD_PAIRS),
        in_specs=[plan.cur(0), plan.cur(HEAD_PAIRS), plan.prev(HEAD_PAIRS), plan.cur(2 * HEAD_PAIRS),
                  plan.prev(2 * HEAD_PAIRS)],
        out_specs=[plan.cur(0), plan.shared()],
        out_shape=[jax.ShapeDtypeStruct((S, D_MODEL), F32), jax.ShapeDtypeStruct((S, LANES), F32)],
        compiler_params=_params(("parallel", "arbitrary")),
    )(qkv, qkv, qkv, qkv, qkv)


def _head_expand():
    return jnp.asarray((np.arange(LANES)[:, None] == np.arange(D_MODEL)[None, :] // HEAD_DIM).astype(np.float32), BF16)


def _dil_merge(os_, lses, name):
    S = os_[0].shape[0]
    tm = 256
    nbr = len(os_)

    def body(*refs):
        o_refs, l_refs, e_ref = refs[:nbr], refs[nbr:2 * nbr], refs[2 * nbr]
        out_ref, outb_ref, lse_ref = refs[2 * nbr + 1:]
        ls = [r[...] for r in l_refs]
        m = ls[0]
        for l in ls[1:]:
            m = jnp.maximum(m, l)
        tot = jnp.exp(ls[0] - m)
        for l in ls[1:]:
            tot = tot + jnp.exp(l - m)
        lse = m + jnp.log(tot)
        acc = None
        for o_r, l in zip(o_refs, ls):
            wt = _dot3(jnp.exp(l - lse), e_ref[...])
            term = wt * o_r[...]
            acc = term if acc is None else acc + term
        out_ref[...] = acc
        outb_ref[...] = acc.astype(BF16)
        lse_ref[...] = lse

    row = lambda i: (i, 0)
    return pl.pallas_call(
        body, name=name, grid=(S // tm,),
        in_specs=[pl.BlockSpec((tm, D_MODEL), row)] * nbr + [pl.BlockSpec((tm, LANES), row)] * nbr
        + [pl.BlockSpec((LANES, D_MODEL), lambda i: (0, 0))],
        out_specs=[pl.BlockSpec((tm, D_MODEL), row), pl.BlockSpec((tm, D_MODEL), row), pl.BlockSpec((tm, LANES), row)],
        out_shape=[jax.ShapeDtypeStruct((S, D_MODEL), F32), jax.ShapeDtypeStruct((S, D_MODEL), BF16),
                   jax.ShapeDtypeStruct((S, LANES), F32)],
        compiler_params=_params(("parallel",)),
    )(*os_, *lses, _head_expand())


def _dil_bwd(qkv, do, lse, dlt, d, name):
    S = qkv.shape[0]
    plan = _DilPlan(S, d)

    def body(q_ref, kc_ref, kp_ref, vc_ref, vp_ref, do_ref, lse_ref, dl_ref,
             dq_ref, dka_ref, dkb_ref, dva_ref, dvb_ref):
        hp = pl.program_id(1)

        def unit(g, r, first):
            valid = _dil_valid(first)
            rows = plan.rows_of(g, r)
            qm = _head_masks(q_ref[rows, :].astype(BF16))
            dom = _head_masks(do_ref[rows, :].astype(BF16))
            kk = plan.keys(kc_ref, kp_ref, g, r)
            vv = plan.keys(vc_ref, vp_ref, g, r)
            km = _head_masks(kk)
            lse_t = lse_ref[rows, :]
            dl_t = dl_ref[rows, :]
            dq2 = dkk = dvv = None
            for h in range(2):
                s = _dot_nt(qm[h], kk)
                p = jnp.where(valid, jnp.exp(s - _lane_pick(lse_t, 2 * hp + h)), 0.0)
                ds = (p * (_dot_nt(dom[h], vv) - _lane_pick(dl_t, 2 * hp + h))).astype(BF16)
                t_q = _dot(ds, km[h])
                t_k = _dot_tn(ds, qm[h])
                t_v = _dot_tn(p.astype(BF16), dom[h])
                dq2 = t_q if dq2 is None else dq2 + t_q
                dkk = t_k if dkk is None else dkk + t_k
                dvv = t_v if dvv is None else dvv + t_v
            dq_ref[rows, :] = dq2
            dkb_ref[rows, :] = dkk[:DIL_BLK]
            dka_ref[rows, :] = dkk[DIL_BLK:]
            dvb_ref[rows, :] = dvv[:DIL_BLK]
            dva_ref[rows, :] = dvv[DIL_BLK:]

        plan.units(unit)

    return pl.pallas_call(
        body, name=name, grid=(plan.steps, HEAD_PAIRS),
        in_specs=[plan.cur(0), plan.cur(HEAD_PAIRS), plan.prev(HEAD_PAIRS), plan.cur(2 * HEAD_PAIRS),
                  plan.prev(2 * HEAD_PAIRS), plan.cur(0), plan.shared(), plan.shared()],
        out_specs=[plan.cur(0)] * 5,
        out_shape=[jax.ShapeDtypeStruct((S, D_MODEL), F32)] * 5,
        compiler_params=_params(("parallel", "arbitrary"), VMEM_BIG),
    )(qkv, qkv, qkv, qkv, qkv, do, lse, dlt)


def _dil_combine(parts, rope, name):
    S = parts[0][0].shape[0]
    tm = DIL_BLK
    nblk = S // tm
    dils = [d for _, d in DILATED_BRANCHES]

    def body(*refs):
        ins = refs[:5 * len(dils)]
        c_ref, s1_ref, s2_ref, o_ref = refs[5 * len(dils):]
        i = pl.program_id(0)
        tabs = (c_ref[...], s1_ref[...], s2_ref[...])
        dq = dk = dv = None
        for b, d in enumerate(dils):
            dq_r, dka_r, dkb_r, dva_r, dvb_r = ins[5 * b:5 * b + 5]
            live = (i + d < nblk).astype(F32)
            tq = dq_r[...]
            tk = dka_r[...] + live * dkb_r[...]
            tv = dva_r[...] + live * dvb_r[...]
            dq = tq if dq is None else dq + tq
            dk = tk if dk is None else dk + tk
            dv = tv if dv is None else dv + tv
        dq = dq * Q_SCALE
        for g in range(HEAD_PAIRS):
            cols = slice(g * LANES, (g + 1) * LANES)
            o_ref[:, g * LANES:(g + 1) * LANES] = _rope_apply(dq[:, cols], *tabs, -1.0).astype(BF16)
            o_ref[:, D_MODEL + g * LANES:D_MODEL + (g + 1) * LANES] = _rope_apply(dk[:, cols], *tabs, -1.0).astype(BF16)
        o_ref[:, 2 * D_MODEL:] = dv.astype(BF16)

    row = pl.BlockSpec((tm, D_MODEL), lambda i: (i, 0))
    in_specs = []
    args = []
    for (dq_b, dka, dkb, dva, dvb), d in zip(parts, dils):
        ahead = pl.BlockSpec((tm, D_MODEL), lambda i, d=d: (jnp.minimum(i + d, nblk - 1), 0))
        in_specs += [row, row, ahead, row, ahead]
        args += [dq_b, dka, dkb, dva, dvb]
    in_specs += [pl.BlockSpec((tm, LANES), lambda i: (i, 0))] * 3
    return pl.pallas_call(
        body, name=name, grid=(nblk,),
        in_specs=in_specs,
        out_specs=pl.BlockSpec((tm, 3 * D_MODEL), lambda i: (i, 0)),
        out_shape=jax.ShapeDtypeStruct((S, 3 * D_MODEL), BF16),
        compiler_params=_params(("parallel",), VMEM_BIG),
    )(*args, *rope)


def _mesh_pos():
    return lax.axis_index("x"), lax.axis_index("y"), lax.axis_index("c")


def _all_gather(shard, name):
    R, C = shard.shape

    def body(x_ref, out_ref, send_sems, recv_sems, local_sem):
        x, y, c = _mesh_pos()
        me, sibling = (x, y, c), (x, y, 1 - c)
        chips = [(1 - x, y), (x, 1 - y), (1 - x, 1 - y)]

        def blk(p):
            return out_ref.at[4 * p[0] + 2 * p[1] + p[2]]

        def copy(k, block, to, src=None):
            return pltpu.make_async_remote_copy(
                src_ref=blk(block) if src is None else src, dst_ref=blk(block),
                send_sem=send_sems.at[k], recv_sem=recv_sems.at[k],
                device_id=to, device_id_type=pl.DeviceIdType.MESH)

        mine = pltpu.make_async_copy(x_ref, blk(me), local_sem)
        mine.start()
        first = [copy(0, me, sibling, src=x_ref)]
        first += [copy(1 + j, me, (*chip, c), src=x_ref) for j, chip in enumerate(chips)]
        for cp in first:
            cp.start()
        passed = [copy(4 + j, (*chip, c), sibling) for j, chip in enumerate(chips)]
        for j, chip in enumerate(chips):
            copy(1 + j, (*chip, c), me).wait_recv()
            passed[j].start()
        copy(0, sibling, me).wait_recv()
        for j, chip in enumerate(chips):
            copy(4 + j, (*chip, 1 - c), me).wait_recv()
        for cp in first + passed:
            cp.wait_send()
        mine.wait()

    return pl.pallas_call(
        body, name=name,
        in_specs=[pl.BlockSpec(memory_space=pl.ANY)],
        out_specs=pl.BlockSpec(memory_space=pl.ANY),
        out_shape=jax.ShapeDtypeStruct((N_DEV, R, C), shard.dtype),
        scratch_shapes=[pltpu.SemaphoreType.DMA((7,)), pltpu.SemaphoreType.DMA((7,)), pltpu.SemaphoreType.DMA],
    )(shard)


def _rs_pair(g, name):
    _, R, C = g.shape

    def body(g_ref, out_ref, send_sems, recv_sems):
        x, y, c = _mesh_pos()
        sibling = (x, y, 1 - c)
        cps = []
        for chip in range(4):
            cps.append(pltpu.make_async_remote_copy(
                src_ref=g_ref.at[2 * chip + (1 - c)], dst_ref=out_ref.at[chip],
                send_sem=send_sems.at[chip], recv_sem=recv_sems.at[chip],
                device_id=sibling, device_id_type=pl.DeviceIdType.MESH))
        for cp in cps:
            cp.start()
        for cp in cps:
            cp.wait_recv()
        for cp in cps:
            cp.wait_send()

    return pl.pallas_call(
        body, name=name,
        in_specs=[pl.BlockSpec(memory_space=pl.ANY)],
        out_specs=pl.BlockSpec(memory_space=pl.ANY),
        out_shape=jax.ShapeDtypeStruct((4, R, C), g.dtype),
        scratch_shapes=[pltpu.SemaphoreType.DMA((4,)), pltpu.SemaphoreType.DMA((4,))],
    )(g)


def _pair_add(g, got, cidx, name):
    _, R, C = g.shape
    tr = 256

    def body(c_ref, g_ref, r_ref, o_ref, ob_ref):
        s = g_ref[...] + r_ref[...].astype(F32)
        o_ref[...] = s
        ob_ref[...] = s.astype(BF16)

    blk = pl.BlockSpec((None, tr, C), lambda k, i, c: (k, i, 0))
    return pl.pallas_call(
        body, name=name,
        grid_spec=pltpu.PrefetchScalarGridSpec(
            num_scalar_prefetch=1, grid=(4, R // tr),
            in_specs=[pl.BlockSpec((None, tr, C), lambda k, i, c: (2 * k + c[0], i, 0)), blk],
            out_specs=[blk, blk]),
        out_shape=[jax.ShapeDtypeStruct((4, R, C), F32), jax.ShapeDtypeStruct((4, R, C), BF16)],
        compiler_params=_params(("parallel", "parallel")),
    )(cidx, g, got)


def _rs_chips(p, name):
    _, R, C = p.shape

    def body(p_ref, out_ref, send_sems, recv_sems):
        x, y, c = _mesh_pos()
        chips = [(1 - x, y), (x, 1 - y), (1 - x, 1 - y)]
        cps = []
        for j, (cx, cy) in enumerate(chips):
            cps.append(pltpu.make_async_remote_copy(
                src_ref=p_ref.at[2 * cx + cy], dst_ref=out_ref.at[j],
                send_sem=send_sems.at[j], recv_sem=recv_sems.at[j],
                device_id=(cx, cy, c), device_id_type=pl.DeviceIdType.MESH))
        for cp in cps:
            cp.start()
        for cp in cps:
            cp.wait_recv()
        for cp in cps:
            cp.wait_send()

    return pl.pallas_call(
        body, name=name,
        in_specs=[pl.BlockSpec(memory_space=pl.ANY)],
        out_specs=pl.BlockSpec(memory_space=pl.ANY),
        out_shape=jax.ShapeDtypeStruct((3, R, C), p.dtype),
        scratch_shapes=[pltpu.SemaphoreType.DMA((3,)), pltpu.SemaphoreType.DMA((3,))],
    )(p)


def _adamw_math(w, g, m, v):
    m2 = ADAM_B1 * m + (1.0 - ADAM_B1) * g
    v2 = ADAM_B2 * v + (1.0 - ADAM_B2) * (g * g)
    m_hat = m2 / (1.0 - ADAM_B1 ** ADAM_STEP)
    v_hat = v2 / (1.0 - ADAM_B2 ** ADAM_STEP)
    delta = -ADAM_LR * (m_hat / (jnp.sqrt(v_hat) + ADAM_EPS) + ADAM_WD * w)
    return delta, m2, v2


def _adamw_shard(p, got, chip_idx, w, m, v, name):
    R, C = w.shape
    tr = 256

    def body(k_ref, p_ref, r_ref, w_ref, m_ref, v_ref, g_out, d_out, m_out, v_out):
        g = ((p_ref[...] + r_ref[0].astype(F32)) + r_ref[1].astype(F32)) + r_ref[2].astype(F32)
        delta, m2, v2 = _adamw_math(w_ref[...], g, m_ref[...], v_ref[...])
        g_out[...] = g
        d_out[...] = delta
        m_out[...] = m2
        v_out[...] = v2

    row = pl.BlockSpec((tr, C), lambda i, k: (i, 0))
    return pl.pallas_call(
        body, name=name,
        grid_spec=pltpu.PrefetchScalarGridSpec(
            num_scalar_prefetch=1, grid=(R // tr,),
            in_specs=[pl.BlockSpec((None, tr, C), lambda i, k: (k[0], i, 0)),
                      pl.BlockSpec((3, tr, C), lambda i, k: (0, i, 0)), row, row, row],
            out_specs=[row] * 4),
        out_shape=[jax.ShapeDtypeStruct((R, C), F32)] * 4,
        compiler_params=_params(("parallel",)),
    )(chip_idx, p, got, w, m, v)


def _adamw_small(gathered, w, m, v, name):
    _, R, C = gathered.shape

    def body(a_ref, w_ref, m_ref, v_ref, g_out, d_out, m_out, v_out):
        g = a_ref[0]
        for k in range(1, N_DEV):
            g = g + a_ref[k]
        delta, m2, v2 = _adamw_math(w_ref[...], g, m_ref[...], v_ref[...])
        g_out[...] = g
        d_out[...] = delta
        m_out[...] = m2
        v_out[...] = v2

    return pl.pallas_call(
        body, name=name, out_shape=[jax.ShapeDtypeStruct((R, C), F32)] * 4,
    )(gathered, w, m, v)


def _rope_tables(S):
    half = ROPE_DIM // 2
    inv_freq = ROPE_THETA ** (-jnp.arange(half, dtype=F32) / half)
    ang = jnp.arange(S, dtype=jnp.int32).astype(F32)[:, None] * inv_freq[None, :]
    cos, sin = jnp.cos(ang), jnp.sin(ang)
    ones = jnp.ones((S, HEAD_DIM - ROPE_DIM), F32)
    zeros = jnp.zeros((S, HEAD_DIM - ROPE_DIM), F32)
    zh = jnp.zeros((S, half), F32)
    c = jnp.concatenate([cos, cos, ones], axis=1)
    s1 = jnp.concatenate([zh, sin, zeros], axis=1)
    s2 = jnp.concatenate([-sin, zh, zeros], axis=1)
    two = lambda t: jnp.concatenate([t, t], axis=1)
    return two(c), two(s1), two(s2)


def _chunk_transposed(a, S):
    return a.reshape(S // SB_CH, SB_CH, HEAD_PAIRS, LANES).transpose(2, 0, 3, 1)


def _flat_shards(ws):
    return jnp.concatenate([w.reshape(-1, D_MODEL) for layer in ws for w in layer], axis=0)


def kernel(x, w_qkv_0, w_o_0, ln1_g_0, ln1_b_0, w_ff1_0, w_ff2_0, ln2_g_0, ln2_b_0, w_qkv_1, w_o_1, ln1_g_1, ln1_b_1, w_ff1_1, w_ff2_1, ln2_g_1, ln2_b_1, loss_target, m_w_qkv_0, m_w_o_0, m_ln1_g_0, m_ln1_b_0, m_w_ff1_0, m_w_ff2_0, m_ln2_g_0, m_ln2_b_0, m_w_qkv_1, m_w_o_1, m_ln1_g_1, m_ln1_b_1, m_w_ff1_1, m_w_ff2_1, m_ln2_g_1, m_ln2_b_1, v_w_qkv_0, v_w_o_0, v_ln1_g_0, v_ln1_b_0, v_w_ff1_0, v_w_ff2_0, v_ln2_g_0, v_ln2_b_0, v_w_qkv_1, v_w_o_1, v_ln1_g_1, v_ln1_b_1, v_w_ff1_1, v_w_ff2_1, v_ln2_g_1, v_ln2_b_1):
    S = x.shape[1]
    x0 = x.reshape(S, D_MODEL)
    target = loss_target.reshape(S, D_MODEL)
    mats = ((w_qkv_0, w_o_0, w_ff1_0, w_ff2_0), (w_qkv_1, w_o_1, w_ff1_1, w_ff2_1))
    mats_m = ((m_w_qkv_0, m_w_o_0, m_w_ff1_0, m_w_ff2_0), (m_w_qkv_1, m_w_o_1, m_w_ff1_1, m_w_ff2_1))
    mats_v = ((v_w_qkv_0, v_w_o_0, v_w_ff1_0, v_w_ff2_0), (v_w_qkv_1, v_w_o_1, v_w_ff1_1, v_w_ff2_1))
    vecs = (ln1_g_0, ln1_b_0, ln2_g_0, ln2_b_0, ln1_g_1, ln1_b_1, ln2_g_1, ln2_b_1)
    vecs_m = (m_ln1_g_0, m_ln1_b_0, m_ln2_g_0, m_ln2_b_0, m_ln1_g_1, m_ln1_b_1, m_ln2_g_1, m_ln2_b_1)
    vecs_v = (v_ln1_g_0, v_ln1_b_0, v_ln2_g_0, v_ln2_b_0, v_ln1_g_1, v_ln1_b_1, v_ln2_g_1, v_ln2_b_1)

    w_flat = _flat_shards(mats)
    w_all = _all_gather(w_flat.astype(BF16), "ag_weights")
    layers = []
    for l in range(N_LAYERS):
        base = l * LAYER_ROWS
        r0, r1, r2, r3 = np.cumsum((0,) + SHARD_ROWS)[:4] + base
        layers.append(dict(
            qkv=w_all[:, r0:r0 + 384].reshape(N_DEV, D_MODEL, 384),
            o=w_all[:, r1:r1 + 128].reshape(D_MODEL, D_MODEL),
            ff1=w_all[:, r2:r2 + 512].reshape(N_DEV, D_MODEL, 512),
            ff2=w_all[:, r3:r3 + 512].reshape(D_FF, D_MODEL),
            g1=vecs[4 * l].reshape(1, D_MODEL), b1=vecs[4 * l + 1].reshape(1, D_MODEL),
            g2=vecs[4 * l + 2].reshape(1, D_MODEL), b2=vecs[4 * l + 3].reshape(1, D_MODEL)))

    rope = _rope_tables(S)
    tmat_later = _sb_tmat(True)
    tmat_upto = _sb_tmat(False)

    saved = []
    xin, xinb = x0, x0.astype(BF16)
    for l, W in enumerate(layers):
        sv = dict(xin=xin, xinb=xinb)
        qkv = _qkv_proj(xinb, W["qkv"], rope if l == 1 else None, Q_SCALE * LOG2E if l == 0 else Q_SCALE,
                        BF16 if l == 0 else F32, f"qkv_proj_{l}")
        sv["qkv"] = qkv
        if l == 0:
            vT3 = _chunk_transposed(qkv[:, 2 * D_MODEL:], S)
            ob, sb_tiles, sb_first = _sb_fwd(qkv, vT3, tmat_later, "sb_fwd")
            o = None
            sv.update(sb_tiles=sb_tiles, sb_first=sb_first)
        else:
            outs = [_dil_fwd(qkv, d, f"dil_fwd_{d}") for _, d in DILATED_BRANCHES]
            o, ob, lse = _dil_merge([t[0] for t in outs], [t[1] for t in outs], "dil_merge")
            sv["lse"] = lse
        sv.update(o=o, ob=ob)
        y1, x1, x1b = _mm_res_ln(ob, xin, W["o"], W["g1"], W["b1"], None, f"attn_out_ln_{l}")
        hpre, h = _ff1(x1b, W["ff1"], f"ff1_{l}")
        if l < N_LAYERS - 1:
            y2, x2, x2b = _mm_res_ln(h, x1, W["ff2"], W["g2"], W["b2"], None, f"ff2_ln_{l}")
        else:
            y2, x2, x2b, dout, loss_parts = _mm_res_ln(h, x1, W["ff2"], W["g2"], W["b2"], target, "ff2_ln_loss")
        sv.update(y1=y1, x1=x1, x1b=x1b, hpre=hpre, h=h, y2=y2)
        saved.append(sv)
        xin, xinb = x2, x2b

    loss = lax.psum(jnp.sum(loss_parts), MESH_AXES)

    gmats = [None] * N_LAYERS
    gvecs = [None] * (4 * N_LAYERS)
    for l in reversed(range(N_LAYERS)):
        W, sv = layers[l], saved[l]
        dy2, dy2b, gb2, dhp = _ln_bwd(dout, sv["y2"], W["g2"], W["ff2"], sv["hpre"], f"ln2_bwd_dh_{l}")
        g_ff2 = _mm_tn(sv["h"], dy2b, 512, D_MODEL, False, f"dw_ff2_{l}")
        dx1 = _dx_blk(dy2, dhp, W["ff1"], f"dx_ff1_{l}")
        g_ff1 = _mm_tn(sv["x1b"], dhp, D_MODEL, 512, True, f"dw_ff1_{l}")
        dy1, dy1b, gb1, do = _ln_bwd(dx1, sv["y1"], W["g1"], W["o"], None, f"ln1_bwd_do_{l}")
        g_o = _mm_tn(sv["ob"], dy1b, 512, D_MODEL, False, f"dw_o_{l}")
        if l == 0:
            kT3 = _chunk_transposed(sv["qkv"][:, D_MODEL:2 * D_MODEL], S)
            dq, dk, dv = _sb_bwd(sv["qkv"], kT3, do, sv["sb_tiles"], sv["sb_first"], tmat_upto, "sb_bwd")
            dqkv = jnp.concatenate([dq, dk, dv], axis=1).astype(BF16)
        else:
            dlt = _head_sums(do, sv["o"], "head_sums")
            parts = [_dil_bwd(sv["qkv"], do, sv["lse"], dlt, d, f"dil_bwd_{d}") for _, d in DILATED_BRANCHES]
            dqkv = _dil_combine(parts, rope, "dil_combine")
        dout = _dx_blk(dy1, dqkv, W["qkv"], f"dx_qkv_{l}")
        g_qkv = _mm_tn(sv["xinb"], dqkv, D_MODEL, 384, True, f"dw_qkv_{l}")
        gmats[l] = (g_qkv.reshape(N_DEV, 384, D_MODEL), g_o.reshape(N_DEV, 128, D_MODEL),
                    g_ff1.reshape(N_DEV, 512, D_MODEL), g_ff2.reshape(N_DEV, 512, D_MODEL))
        gvecs[4 * l:4 * l + 4] = [gb1[0], gb1[1], gb2[0], gb2[1]]
    grad_x = dout.reshape(1, S, D_MODEL)

    cx, cy, cc = _mesh_pos()
    g_all = jnp.concatenate([g for layer in gmats for g in layer], axis=1)
    got_pair = _rs_pair(g_all.astype(BF16), "rs_pair")
    chip_part, chip_part_b = _pair_add(g_all, got_pair, cc.astype(jnp.int32).reshape(1), "rs_pair_add")
    got_chips = _rs_chips(chip_part_b, "rs_chips")
    chip_idx = (2 * cx + cy).astype(jnp.int32).reshape(1)
    g_sh, d_sh, m_sh, v_sh = _adamw_shard(chip_part, got_chips, chip_idx, w_flat, _flat_shards(mats_m),
                                          _flat_shards(mats_v), "adamw_mats")

    def unflat(a):
        out, pos = [], 0
        for layer in mats:
            for w in layer:
                n = w.size // D_MODEL
                out.append(a[pos:pos + n].reshape(w.shape))
                pos += n
        return out

    gv_all = _all_gather(jnp.stack(gvecs), "ag_vec_grads")
    g_v, d_v, m_v, v_v = _adamw_small(gv_all, jnp.stack(vecs), jnp.stack(vecs_m), jnp.stack(vecs_v), "adamw_vecs")

    def interleave(mat_list, vec_arr):
        out = []
        for l in range(N_LAYERS):
            qkv_, o_, ff1_, ff2_ = mat_list[4 * l:4 * l + 4]
            out += [qkv_, o_, vec_arr[4 * l], vec_arr[4 * l + 1], ff1_, ff2_, vec_arr[4 * l + 2], vec_arr[4 * l + 3]]
        return out

    return (loss, grad_x, *interleave(unflat(g_sh), g_v), *interleave(unflat(d_sh), d_v),
            *interleave(unflat(m_sh), m_v), *interleave(unflat(v_sh), v_v))
```
